```python
import math
import jax, jax.numpy as jnp
from jax import lax
import numpy as np

D_MODEL = 1024
BATCH = 8
SEQ = 16384
DEPTH = 1

CHUNK = 64
D_A = D_MODEL
D_B = D_MODEL
GROUP = 64
CONV_A_WIDTH = 31
CONV_B_WIDTH = 3
D_FF = 4 * D_MODEL
N_ADA = 6
D_IN = 2 * D_A + 3 * D_B + 2 * D_MODEL
LN_EPS = 1e-5
DEEPNORM_ALPHA = (2.0 * DEPTH) ** 0.25
DEEPNORM_BETA = (8.0 * DEPTH) ** -0.25

kernel_name = "hybrid_conformer_shortconv_gated_block"


def _layernorm(x, g=None, b=None):
    xf = x.astype(jnp.float32)
    mu = jnp.mean(xf, axis=-1, keepdims=True)
    var = jnp.mean(jnp.square(xf - mu), axis=-1, keepdims=True)
    y = (xf - mu) * lax.rsqrt(var + LN_EPS)
    if g is not None:
        y = y * g.astype(jnp.float32) + b.astype(jnp.float32)
    return y.astype(x.dtype)


def _causal_depthwise_conv(u, w):
    k = w.shape[0]
    u_pad = jnp.pad(u, ((0, 0), (k - 1, 0), (0, 0)))
    return lax.conv_general_dilated(
        u_pad, w[:, None, :].astype(u.dtype), window_strides=(1,), padding="VALID",
        dimension_numbers=("NWC", "WIO", "NWC"), feature_group_count=u.shape[-1])


def _fwd_setup_inputs(seed: int = 0) -> dict:
    key = jax.random.key(seed)
    ks = jax.random.split(key, 24)
    f32 = jnp.float32
    L, D = DEPTH, D_MODEL

    def nrm(k, shape, scale):
        return jax.random.normal(k, shape, f32) * scale

    return {
        "x": nrm(ks[0], (BATCH, SEQ, D), 1.0),
        "c": nrm(ks[1], (BATCH, D), 1.0),
        "w_ada": nrm(ks[2], (L, D, N_ADA * D), 0.2 * D ** -0.5),
        "b_ada": nrm(ks[3], (L, N_ADA * D), 0.02),
        "w_in": nrm(ks[4], (L, D, D_IN), D ** -0.5),
        "b_in": nrm(ks[5], (L, D_IN), 0.02),
        "conv_a_w": nrm(ks[6], (L, CONV_A_WIDTH, D_A), CONV_A_WIDTH ** -0.5),
        "conv_a_b": nrm(ks[7], (L, D_A), 0.02),
        "ln_a_g": 1.0 + nrm(ks[8], (L, D_A), 0.02),
        "ln_a_b": nrm(ks[9], (L, D_A), 0.02),
        "w_a_out": nrm(ks[10], (L, D_A, D), D_A ** -0.5),
        "b_a_out": nrm(ks[11], (L, D), 0.02),
        "conv_b_w": nrm(ks[12], (L, CONV_B_WIDTH, D_B), CONV_B_WIDTH ** -0.5),
        "w_b_out": nrm(ks[13], (L, D_B, D), D_B ** -0.5),
        "w_o": nrm(ks[14], (L, D, D), DEEPNORM_BETA * D ** -0.5),
        "b_o": nrm(ks[15], (L, D), 0.02),
        "ln1_g": 1.0 + nrm(ks[16], (L, D), 0.02),
        "ln1_b": nrm(ks[17], (L, D), 0.02),
        "w_up": nrm(ks[18], (L, D, D_FF), D ** -0.5),
        "b_up": nrm(ks[19], (L, D_FF), 0.02),
        "w_down": nrm(ks[20], (L, D_FF, D), DEEPNORM_BETA * D_FF ** -0.5),
        "b_down": nrm(ks[21], (L, D), 0.02),
        "ln2_g": 1.0 + nrm(ks[22], (L, D), 0.02),
        "ln2_b": nrm(ks[23], (L, D), 0.02),
    }


def _fwd_reference(x, c, w_ada, b_ada, w_in, b_in, conv_a_w, conv_a_b, ln_a_g, ln_a_b,
              w_a_out, b_a_out, conv_b_w, w_b_out, w_o, b_o, ln1_g, ln1_b,
              w_up, b_up, w_down, b_down, ln2_g, ln2_b):
    split_idx = np.cumsum([D_A, D_A, D_B, D_B, D_B, D_MODEL]).tolist()
    c_act = jax.nn.silu(c)
    for l in range(DEPTH):
        mod = c_act @ w_ada[l] + b_ada[l]
        shift1, scale1, gate1, shift2, scale2, gate2 = [
            m[:, None, :] for m in jnp.split(mod, N_ADA, axis=-1)]

        h = _layernorm(x) * (1.0 + scale1) + shift1
        z = jnp.einsum("bsd,de->bse", h, w_in[l]) + b_in[l]
        a_val, a_gate, b_gb, b_gc, b_x, g_a, g_b = jnp.split(z, split_idx, axis=-1)

        u = a_val * jax.nn.sigmoid(a_gate)
        u = _causal_depthwise_conv(u, conv_a_w[l]) + conv_a_b[l]
        u = jax.nn.silu(_layernorm(u, ln_a_g[l], ln_a_b[l]))
        y_a = jnp.einsum("bsc,cd->bsd", u, w_a_out[l]) + b_a_out[l]

        v = b_gb * _causal_depthwise_conv(b_gc * b_x, conv_b_w[l])
        y_b = jnp.einsum("bsc,cd->bsd", v, w_b_out[l])

        merged = jax.nn.sigmoid(g_a) * y_a + jax.nn.sigmoid(g_b) * y_b
        out = jnp.einsum("bsd,de->bse", merged, w_o[l]) + b_o[l]
        x = _layernorm(DEEPNORM_ALPHA * x + (1.0 + gate1) * out, ln1_g[l], ln1_b[l])

        h = _layernorm(x) * (1.0 + scale2) + shift2
        f = jnp.square(jax.nn.relu(jnp.einsum("bsd,df->bsf", h, w_up[l]) + b_up[l]))
        out = jnp.einsum("bsf,fd->bsd", f, w_down[l]) + b_down[l]
        x = _layernorm(DEEPNORM_ALPHA * x + (1.0 + gate2) * out, ln2_g[l], ln2_b[l])
    return x


import jax as _jax
import jax.numpy as _jnp

TWIN_FORMAT = 'train_step'
FWD_PARAMS = ['x', 'c', 'w_ada', 'b_ada', 'w_in', 'b_in', 'conv_a_w', 'conv_a_b', 'ln_a_g', 'ln_a_b', 'w_a_out', 'b_a_out', 'conv_b_w', 'w_b_out', 'w_o', 'b_o', 'ln1_g', 'ln1_b', 'w_up', 'b_up', 'w_down', 'b_down', 'ln2_g', 'ln2_b']
TWIN_WEIGHTS = ['w_ada', 'b_ada', 'w_in', 'b_in', 'conv_a_w', 'conv_a_b', 'ln_a_g', 'ln_a_b', 'w_a_out', 'b_a_out', 'conv_b_w', 'w_b_out', 'w_o', 'b_o', 'ln1_g', 'ln1_b', 'w_up', 'b_up', 'w_down', 'b_down', 'ln2_g', 'ln2_b']
TWIN_DIFF_INPUT = 'x'
TWIN_INPUTS = ['x', 'c', 'w_ada', 'b_ada', 'w_in', 'b_in', 'conv_a_w', 'conv_a_b', 'ln_a_g', 'ln_a_b', 'w_a_out', 'b_a_out', 'conv_b_w', 'w_b_out', 'w_o', 'b_o', 'ln1_g', 'ln1_b', 'w_up', 'b_up', 'w_down', 'b_down', 'ln2_g', 'ln2_b', 'loss_target', 'm_w_ada', 'm_b_ada', 'm_w_in', 'm_b_in', 'm_conv_a_w', 'm_conv_a_b', 'm_ln_a_g', 'm_ln_a_b', 'm_w_a_out', 'm_b_a_out', 'm_conv_b_w', 'm_w_b_out', 'm_w_o', 'm_b_o', 'm_ln1_g', 'm_ln1_b', 'm_w_up', 'm_b_up', 'm_w_down', 'm_b_down', 'm_ln2_g', 'm_ln2_b', 'v_w_ada', 'v_b_ada', 'v_w_in', 'v_b_in', 'v_conv_a_w', 'v_conv_a_b', 'v_ln_a_g', 'v_ln_a_b', 'v_w_a_out', 'v_b_a_out', 'v_conv_b_w', 'v_w_b_out', 'v_w_o', 'v_b_o', 'v_ln1_g', 'v_ln1_b', 'v_w_up', 'v_b_up', 'v_w_down', 'v_b_down', 'v_ln2_g', 'v_ln2_b']
TWIN_OUTPUTS = ['loss', 'grad_x', 'grad_w_ada', 'grad_b_ada', 'grad_w_in', 'grad_b_in', 'grad_conv_a_w', 'grad_conv_a_b', 'grad_ln_a_g', 'grad_ln_a_b', 'grad_w_a_out', 'grad_b_a_out', 'grad_conv_b_w', 'grad_w_b_out', 'grad_w_o', 'grad_b_o', 'grad_ln1_g', 'grad_ln1_b', 'grad_w_up', 'grad_b_up', 'grad_w_down', 'grad_b_down', 'grad_ln2_g', 'grad_ln2_b', 'delta_w_ada', 'delta_b_ada', 'delta_w_in', 'delta_b_in', 'delta_conv_a_w', 'delta_conv_a_b', 'delta_ln_a_g', 'delta_ln_a_b', 'delta_w_a_out', 'delta_b_a_out', 'delta_conv_b_w', 'delta_w_b_out', 'delta_w_o', 'delta_b_o', 'delta_ln1_g', 'delta_ln1_b', 'delta_w_up', 'delta_b_up', 'delta_w_down', 'delta_b_down', 'delta_ln2_g', 'delta_ln2_b', 'new_m_w_ada', 'new_m_b_ada', 'new_m_w_in', 'new_m_b_in', 'new_m_conv_a_w', 'new_m_conv_a_b', 'new_m_ln_a_g', 'new_m_ln_a_b', 'new_m_w_a_out', 'new_m_b_a_out', 'new_m_conv_b_w', 'new_m_w_b_out', 'new_m_w_o', 'new_m_b_o', 'new_m_ln1_g', 'new_m_ln1_b', 'new_m_w_up', 'new_m_b_up', 'new_m_w_down', 'new_m_b_down', 'new_m_ln2_g', 'new_m_ln2_b', 'new_v_w_ada', 'new_v_b_ada', 'new_v_w_in', 'new_v_b_in', 'new_v_conv_a_w', 'new_v_conv_a_b', 'new_v_ln_a_g', 'new_v_ln_a_b', 'new_v_w_a_out', 'new_v_b_a_out', 'new_v_conv_b_w', 'new_v_w_b_out', 'new_v_w_o', 'new_v_b_o', 'new_v_ln1_g', 'new_v_ln1_b', 'new_v_w_up', 'new_v_b_up', 'new_v_w_down', 'new_v_b_down', 'new_v_ln2_g', 'new_v_ln2_b']
TWIN_LEAF_KINDS = {'loss': 'loss', 'grad_x': 'grad_x', 'grad_w_ada': 'grad_w', 'grad_b_ada': 'grad_w', 'grad_w_in': 'grad_w', 'grad_b_in': 'grad_w', 'grad_conv_a_w': 'grad_w', 'grad_conv_a_b': 'grad_w', 'grad_ln_a_g': 'grad_w', 'grad_ln_a_b': 'grad_w', 'grad_w_a_out': 'grad_w', 'grad_b_a_out': 'grad_w', 'grad_conv_b_w': 'grad_w', 'grad_w_b_out': 'grad_w', 'grad_w_o': 'grad_w', 'grad_b_o': 'grad_w', 'grad_ln1_g': 'grad_w', 'grad_ln1_b': 'grad_w', 'grad_w_up': 'grad_w', 'grad_b_up': 'grad_w', 'grad_w_down': 'grad_w', 'grad_b_down': 'grad_w', 'grad_ln2_g': 'grad_w', 'grad_ln2_b': 'grad_w', 'delta_w_ada': 'delta_w', 'delta_b_ada': 'delta_w', 'delta_w_in': 'delta_w', 'delta_b_in': 'delta_w', 'delta_conv_a_w': 'delta_w', 'delta_conv_a_b': 'delta_w', 'delta_ln_a_g': 'delta_w', 'delta_ln_a_b': 'delta_w', 'delta_w_a_out': 'delta_w', 'delta_b_a_out': 'delta_w', 'delta_conv_b_w': 'delta_w', 'delta_w_b_out': 'delta_w', 'delta_w_o': 'delta_w', 'delta_b_o': 'delta_w', 'delta_ln1_g': 'delta_w', 'delta_ln1_b': 'delta_w', 'delta_w_up': 'delta_w', 'delta_b_up': 'delta_w', 'delta_w_down': 'delta_w', 'delta_b_down': 'delta_w', 'delta_ln2_g': 'delta_w', 'delta_ln2_b': 'delta_w', 'new_m_w_ada': 'new_m', 'new_m_b_ada': 'new_m', 'new_m_w_in': 'new_m', 'new_m_b_in': 'new_m', 'new_m_conv_a_w': 'new_m', 'new_m_conv_a_b': 'new_m', 'new_m_ln_a_g': 'new_m', 'new_m_ln_a_b': 'new_m', 'new_m_w_a_out': 'new_m', 'new_m_b_a_out': 'new_m', 'new_m_conv_b_w': 'new_m', 'new_m_w_b_out': 'new_m', 'new_m_w_o': 'new_m', 'new_m_b_o': 'new_m', 'new_m_ln1_g': 'new_m', 'new_m_ln1_b': 'new_m', 'new_m_w_up': 'new_m', 'new_m_b_up': 'new_m', 'new_m_w_down': 'new_m', 'new_m_b_down': 'new_m', 'new_m_ln2_g': 'new_m', 'new_m_ln2_b': 'new_m', 'new_v_w_ada': 'new_v', 'new_v_b_ada': 'new_v', 'new_v_w_in': 'new_v', 'new_v_b_in': 'new_v', 'new_v_conv_a_w': 'new_v', 'new_v_conv_a_b': 'new_v', 'new_v_ln_a_g': 'new_v', 'new_v_ln_a_b': 'new_v', 'new_v_w_a_out': 'new_v', 'new_v_b_a_out': 'new_v', 'new_v_conv_b_w': 'new_v', 'new_v_w_b_out': 'new_v', 'new_v_w_o': 'new_v', 'new_v_b_o': 'new_v', 'new_v_ln1_g': 'new_v', 'new_v_ln1_b': 'new_v', 'new_v_w_up': 'new_v', 'new_v_b_up': 'new_v', 'new_v_w_down': 'new_v', 'new_v_b_down': 'new_v', 'new_v_ln2_g': 'new_v', 'new_v_ln2_b': 'new_v'}


def _forward(args):
    return _fwd_reference(*[args[k] for k in FWD_PARAMS])


def _output_shape():
    def fwd():
        inp = _fwd_setup_inputs(0)
        return _fwd_reference(*[inp[k] for k in FWD_PARAMS])
    out = _jax.eval_shape(fwd)
    return out.shape, out.dtype

N_MICROBATCH = 1
ADAM_LR = 0.001
ADAM_B1 = 0.9
ADAM_B2 = 0.999
ADAM_EPS = 1e-08
ADAM_WD = 0.01
ADAM_STEP = 10
PER_EXAMPLE_BATCH_AXIS = {'x': 0, 'c': 0, 'loss_target': 0}
SHARED_INPUTS = []
_WEIGHT_DTYPES = {'w_ada': _jnp.float32, 'b_ada': _jnp.float32, 'w_in': _jnp.float32, 'b_in': _jnp.float32, 'conv_a_w': _jnp.float32, 'conv_a_b': _jnp.float32, 'ln_a_g': _jnp.float32, 'ln_a_b': _jnp.float32, 'w_a_out': _jnp.float32, 'b_a_out': _jnp.float32, 'conv_b_w': _jnp.float32, 'w_b_out': _jnp.float32, 'w_o': _jnp.float32, 'b_o': _jnp.float32, 'ln1_g': _jnp.float32, 'ln1_b': _jnp.float32, 'w_up': _jnp.float32, 'b_up': _jnp.float32, 'w_down': _jnp.float32, 'b_down': _jnp.float32, 'ln2_g': _jnp.float32, 'ln2_b': _jnp.float32}
MOMENT_SCALE = {'w_ada': 1.495373e-01, 'b_ada': 4.602021e-01, 'w_in': 7.178978e-02, 'b_in': 8.288334e-02, 'conv_a_w': 6.281137e-02, 'conv_a_b': 2.836384e-01, 'ln_a_g': 1.134234e-01, 'ln_a_b': 1.883946e-01, 'w_a_out': 8.332386e-02, 'b_a_out': 4.261474e-01, 'conv_b_w': 1.016625e-01, 'w_b_out': 9.934052e-02, 'w_o': 2.131733e-01, 'b_o': 1.478891e+00, 'ln1_g': 2.839806e+00, 'ln1_b': 1.823008e+00, 'w_up': 1.117124e-01, 'b_up': 3.080471e-01, 'w_down': 6.121642e-01, 'b_down': 1.480668e+00, 'ln2_g': 1.285460e+02, 'ln2_b': 2.856600e+01}


def _to_microbatches(a, axis):
    t = _jnp.moveaxis(a, axis, 0)
    t = t.reshape((N_MICROBATCH, t.shape[0] // N_MICROBATCH) + t.shape[1:])
    return _jnp.moveaxis(t, 1, axis + 1)


def setup_inputs(seed: int = 0) -> dict:
    inp = _fwd_setup_inputs(seed)
    key = _jax.random.fold_in(_jax.random.key(seed), 7919)
    shape, _ = _output_shape()
    out = dict(inp)
    out["loss_target"] = _jax.random.normal(_jax.random.fold_in(key, 0), shape, _jnp.float32)
    for i, name in enumerate(TWIN_WEIGHTS):
        w = inp[name].astype(_jnp.float32)
        if MOMENT_SCALE is None:
            s = _jnp.sqrt(_jnp.mean(_jnp.square(w)) + 1e-30)
        else:
            s = MOMENT_SCALE[name]
        km, kv = _jax.random.split(_jax.random.fold_in(key, i + 1))
        out[name] = w
        out["m_" + name] = s * _jax.random.normal(km, w.shape, _jnp.float32)
        out["v_" + name] = (s * s) * _jax.random.uniform(kv, w.shape, _jnp.float32, 0.5, 1.5)
    if N_MICROBATCH > 1:
        for name, axis in PER_EXAMPLE_BATCH_AXIS.items():
            out[name] = _to_microbatches(out[name], axis)
    return {'x': out['x'], 'c': out['c'], 'w_ada': out['w_ada'], 'b_ada': out['b_ada'], 'w_in': out['w_in'], 'b_in': out['b_in'], 'conv_a_w': out['conv_a_w'], 'conv_a_b': out['conv_a_b'], 'ln_a_g': out['ln_a_g'], 'ln_a_b': out['ln_a_b'], 'w_a_out': out['w_a_out'], 'b_a_out': out['b_a_out'], 'conv_b_w': out['conv_b_w'], 'w_b_out': out['w_b_out'], 'w_o': out['w_o'], 'b_o': out['b_o'], 'ln1_g': out['ln1_g'], 'ln1_b': out['ln1_b'], 'w_up': out['w_up'], 'b_up': out['b_up'], 'w_down': out['w_down'], 'b_down': out['b_down'], 'ln2_g': out['ln2_g'], 'ln2_b': out['ln2_b'], 'loss_target': out['loss_target'], 'm_w_ada': out['m_w_ada'], 'm_b_ada': out['m_b_ada'], 'm_w_in': out['m_w_in'], 'm_b_in': out['m_b_in'], 'm_conv_a_w': out['m_conv_a_w'], 'm_conv_a_b': out['m_conv_a_b'], 'm_ln_a_g': out['m_ln_a_g'], 'm_ln_a_b': out['m_ln_a_b'], 'm_w_a_out': out['m_w_a_out'], 'm_b_a_out': out['m_b_a_out'], 'm_conv_b_w': out['m_conv_b_w'], 'm_w_b_out': out['m_w_b_out'], 'm_w_o': out['m_w_o'], 'm_b_o': out['m_b_o'], 'm_ln1_g': out['m_ln1_g'], 'm_ln1_b': out['m_ln1_b'], 'm_w_up': out['m_w_up'], 'm_b_up': out['m_b_up'], 'm_w_down': out['m_w_down'], 'm_b_down': out['m_b_down'], 'm_ln2_g': out['m_ln2_g'], 'm_ln2_b': out['m_ln2_b'], 'v_w_ada': out['v_w_ada'], 'v_b_ada': out['v_b_ada'], 'v_w_in': out['v_w_in'], 'v_b_in': out['v_b_in'], 'v_conv_a_w': out['v_conv_a_w'], 'v_conv_a_b': out['v_conv_a_b'], 'v_ln_a_g': out['v_ln_a_g'], 'v_ln_a_b': out['v_ln_a_b'], 'v_w_a_out': out['v_w_a_out'], 'v_b_a_out': out['v_b_a_out'], 'v_conv_b_w': out['v_conv_b_w'], 'v_w_b_out': out['v_w_b_out'], 'v_w_o': out['v_w_o'], 'v_b_o': out['v_b_o'], 'v_ln1_g': out['v_ln1_g'], 'v_ln1_b': out['v_ln1_b'], 'v_w_up': out['v_w_up'], 'v_b_up': out['v_b_up'], 'v_w_down': out['v_w_down'], 'v_b_down': out['v_b_down'], 'v_ln2_g': out['v_ln2_g'], 'v_ln2_b': out['v_ln2_b']}


def _loss(weights, diff, rest, loss_target):
    with _jax.named_scope("forward"):
        args = {**rest, TWIN_DIFF_INPUT: diff, **{k: w.astype(_WEIGHT_DTYPES[k]) for k, w in weights.items()}}
        y = _forward(args)
    with _jax.named_scope("loss_head"):
        err = _jnp.square(y.astype(_jnp.float32) - loss_target)
        return 0.5 * _jnp.sum(_jnp.mean(err, axis=-1)) if err.ndim else 0.5 * err


def _adamw(w, g, m, v):
    m = ADAM_B1 * m + (1.0 - ADAM_B1) * g
    v = ADAM_B2 * v + (1.0 - ADAM_B2) * _jnp.square(g)
    m_hat = m / (1.0 - ADAM_B1 ** ADAM_STEP)
    v_hat = v / (1.0 - ADAM_B2 ** ADAM_STEP)
    delta = -ADAM_LR * (m_hat / (_jnp.sqrt(v_hat) + ADAM_EPS) + ADAM_WD * w)
    return delta, m, v


def reference(x, c, w_ada, b_ada, w_in, b_in, conv_a_w, conv_a_b, ln_a_g, ln_a_b, w_a_out, b_a_out, conv_b_w, w_b_out, w_o, b_o, ln1_g, ln1_b, w_up, b_up, w_down, b_down, ln2_g, ln2_b, loss_target, m_w_ada, m_b_ada, m_w_in, m_b_in, m_conv_a_w, m_conv_a_b, m_ln_a_g, m_ln_a_b, m_w_a_out, m_b_a_out, m_conv_b_w, m_w_b_out, m_w_o, m_b_o, m_ln1_g, m_ln1_b, m_w_up, m_b_up, m_w_down, m_b_down, m_ln2_g, m_ln2_b, v_w_ada, v_b_ada, v_w_in, v_b_in, v_conv_a_w, v_conv_a_b, v_ln_a_g, v_ln_a_b, v_w_a_out, v_b_a_out, v_conv_b_w, v_w_b_out, v_w_o, v_b_o, v_ln1_g, v_ln1_b, v_w_up, v_b_up, v_w_down, v_b_down, v_ln2_g, v_ln2_b):
    given = dict(x=x, c=c, w_ada=w_ada, b_ada=b_ada, w_in=w_in, b_in=b_in, conv_a_w=conv_a_w, conv_a_b=conv_a_b, ln_a_g=ln_a_g, ln_a_b=ln_a_b, w_a_out=w_a_out, b_a_out=b_a_out, conv_b_w=conv_b_w, w_b_out=w_b_out, w_o=w_o, b_o=b_o, ln1_g=ln1_g, ln1_b=ln1_b, w_up=w_up, b_up=b_up, w_down=w_down, b_down=b_down, ln2_g=ln2_g, ln2_b=ln2_b, loss_target=loss_target, m_w_ada=m_w_ada, m_b_ada=m_b_ada, m_w_in=m_w_in, m_b_in=m_b_in, m_conv_a_w=m_conv_a_w, m_conv_a_b=m_conv_a_b, m_ln_a_g=m_ln_a_g, m_ln_a_b=m_ln_a_b, m_w_a_out=m_w_a_out, m_b_a_out=m_b_a_out, m_conv_b_w=m_conv_b_w, m_w_b_out=m_w_b_out, m_w_o=m_w_o, m_b_o=m_b_o, m_ln1_g=m_ln1_g, m_ln1_b=m_ln1_b, m_w_up=m_w_up, m_b_up=m_b_up, m_w_down=m_w_down, m_b_down=m_b_down, m_ln2_g=m_ln2_g, m_ln2_b=m_ln2_b, v_w_ada=v_w_ada, v_b_ada=v_b_ada, v_w_in=v_w_in, v_b_in=v_b_in, v_conv_a_w=v_conv_a_w, v_conv_a_b=v_conv_a_b, v_ln_a_g=v_ln_a_g, v_ln_a_b=v_ln_a_b, v_w_a_out=v_w_a_out, v_b_a_out=v_b_a_out, v_conv_b_w=v_conv_b_w, v_w_b_out=v_w_b_out, v_w_o=v_w_o, v_b_o=v_b_o, v_ln1_g=v_ln1_g, v_ln1_b=v_ln1_b, v_w_up=v_w_up, v_b_up=v_b_up, v_w_down=v_w_down, v_b_down=v_b_down, v_ln2_g=v_ln2_g, v_ln2_b=v_ln2_b)
    weights = {n: given[n] for n in TWIN_WEIGHTS}
    shared = {n: given[n] for n in SHARED_INPUTS}
    per_example = {n: given[n] for n in ['x', 'c']}
    grad_fn = _jax.value_and_grad(_loss, argnums=(0, 1))

    def one_microbatch(ex, loss_target):
        ex = dict(ex)
        diff = ex.pop(TWIN_DIFF_INPUT)
        return grad_fn(weights, diff, {**shared, **ex}, loss_target)

    if N_MICROBATCH == 1:
        loss, (grad_w, grad_x) = one_microbatch(per_example, given["loss_target"])
    else:
        def body(carry, xs):
            loss_sum, grad_sum = carry
            l_k, (gw_k, gx_k) = one_microbatch(xs[0], xs[1])
            with _jax.named_scope("update"):
                return (loss_sum + l_k, _jax.tree.map(_jnp.add, grad_sum, gw_k)), gx_k

        init = (_jnp.zeros((), _jnp.float32), _jax.tree.map(_jnp.zeros_like, weights))
        (loss, grad_w), grad_x = _jax.lax.scan(body, init, (per_example, given["loss_target"]))
    with _jax.named_scope("update"):
        delta_w, new_m, new_v = {}, {}, {}
        for n in TWIN_WEIGHTS:
            delta_w[n], new_m[n], new_v[n] = _adamw(weights[n], grad_w[n], given["m_" + n], given["v_" + n])
    return (loss, grad_x, *[grad_w[n] for n in TWIN_WEIGHTS], *[delta_w[n] for n in TWIN_WEIGHTS],
            *[new_m[n] for n in TWIN_WEIGHTS], *[new_v[n] for n in TWIN_WEIGHTS])
```

```python
import functools

import jax
import jax.numpy as jnp
from jax import lax
from jax.experimental import pallas as pl
from jax.experimental.pallas import tpu as pltpu

F32 = jnp.float32
BF16 = jnp.bfloat16
MESH = pl.DeviceIdType.MESH

LN_EPS = 1e-5
DEPTH = 1
ALPHA = (2.0 * DEPTH) ** 0.25
CONV_A = 31
CONV_B = 3
HALO_A = 32
HALO_B = 8
N_CHIPS = 4
N_DEV = 8
ADAM_LR = 0.001
ADAM_B1 = 0.9
ADAM_B2 = 0.999
ADAM_EPS = 1e-08
ADAM_WD = 0.01
ADAM_STEP = 10
VMEM_LIMIT = 56 * 1024 * 1024

V_SHIFT1, V_SCALE1, V_GATE1, V_SHIFT2, V_SCALE2, V_GATE2 = 0, 1, 2, 3, 4, 5
V_CAB, V_LNAG, V_LNAB, V_BAO, V_BO, V_LN1G, V_LN1B, V_BDN, V_LN2G, V_LN2B = 6, 7, 8, 9, 10, 11, 12, 13, 14, 15

R_DMOD = 0
R_BIN = 6
R_CAW = 13
R_CAB, R_LNAG, R_LNAB, R_BAO = 44, 45, 46, 47
R_CBW = 48
R_BO, R_LN1G, R_LN1B = 51, 52, 53
R_BUP = 54
R_BDN, R_LN2G, R_LN2B, R_LOSS = 58, 59, 60, 61
R_TOTAL = 64


def _pcall(body, **kw):
    return pl.pallas_call(body, **kw)


def _cparams(**kw):
    return pltpu.CompilerParams(vmem_limit_bytes=VMEM_LIMIT, **kw)


def _seq(n):
    return _cparams(dimension_semantics=("arbitrary",) * n)


def _full(shape, single=False):
    nd = len(shape)
    if single:
        return pl.BlockSpec(shape, lambda *_: (0,) * nd, pipeline_mode=pl.Buffered(1))
    return pl.BlockSpec(shape, lambda *_: (0,) * nd)


def _rows(tm, width):
    return pl.BlockSpec((tm, width), lambda i: (i, 0))


def _sig(x):
    return jax.nn.sigmoid(x)


def _ln(x):
    mu = jnp.mean(x, axis=-1, keepdims=True)
    xc = x - mu
    var = jnp.mean(xc * xc, axis=-1, keepdims=True)
    rstd = lax.rsqrt(var + LN_EPS)
    return xc * rstd, rstd


def _ln_bwd(dxh, xh, rstd):
    m1 = jnp.mean(dxh, axis=-1, keepdims=True)
    m2 = jnp.mean(dxh * xh, axis=-1, keepdims=True)
    return rstd * (dxh - m1 - xh * m2)


def _rsum(v):
    return jnp.sum(v, axis=0, keepdims=True)


def _dot(a, b):
    return jnp.dot(a, b, preferred_element_type=F32)


def _dot_nt(a, b):
    return lax.dot_general(a, b, (((1,), (1,)), ((), ())), preferred_element_type=F32)


def _dot_tn(a, b):
    return lax.dot_general(a, b, (((0,), (0,)), ((), ())), preferred_element_type=F32)


def _my_pos():
    return lax.axis_index("x"), lax.axis_index("y"), lax.axis_index("c")


def _all_gather_small(v, name):
    r, c = v.shape

    def body(v_ref, out_ref, send_sems, recv_sems, local_sem):
        x, y, cc = _my_pos()
        me = 4 * x + 2 * y + cc
        mine = pltpu.make_async_copy(v_ref, out_ref.at[me], local_sem)
        mine.start()
        sends = []
        for rel in range(1, N_DEV):
            rx, ry, rc = (rel >> 2) & 1, (rel >> 1) & 1, rel & 1
            peer = (1 - x if rx else x, 1 - y if ry else y, 1 - cc if rc else cc)
            cp = pltpu.make_async_remote_copy(
                src_ref=v_ref, dst_ref=out_ref.at[me], send_sem=send_sems.at[rel - 1], recv_sem=recv_sems.at[rel - 1],
                device_id=peer, device_id_type=MESH)
            cp.start()
            sends.append(cp)
        for rel in range(1, N_DEV):
            rx, ry, rc = (rel >> 2) & 1, (rel >> 1) & 1, rel & 1
            peer = (1 - x if rx else x, 1 - y if ry else y, 1 - cc if rc else cc)
            slot = 4 * peer[0] + 2 * peer[1] + peer[2]
            pltpu.make_async_remote_copy(
                src_ref=v_ref, dst_ref=out_ref.at[slot], send_sem=send_sems.at[rel - 1], recv_sem=recv_sems.at[rel - 1],
                device_id=peer, device_id_type=MESH).wait_recv()
        for cp in sends:
            cp.wait_send()
        mine.wait()

    return _pcall(
        body, name=name,
        out_shape=jax.ShapeDtypeStruct((N_DEV, r, c), v.dtype),
        in_specs=[pl.BlockSpec(memory_space=pltpu.VMEM)],
        out_specs=pl.BlockSpec(memory_space=pltpu.VMEM),
        scratch_shapes=[pltpu.SemaphoreType.DMA((N_DEV - 1,)), pltpu.SemaphoreType.DMA((N_DEV - 1,)),
                        pltpu.SemaphoreType.DMA],
        compiler_params=_cparams(),
    )(v)


def _other_chips(x, y):
    return [(1 - x, y), (x, 1 - y), (1 - x, 1 - y)]


def _all_gather_weights(wa_sh, wb_sh):
    bufs = (wa_sh, wb_sh)
    nb = len(bufs)

    def body(a_ref, b_ref, ao_ref, bo_ref, send_sems, recv_sems, local_sems):
        x, y, c = _my_pos()
        q = 2 * x + y
        srcs, outs = (a_ref, b_ref), (ao_ref, bo_ref)
        sibling = (x, y, 1 - c)
        chips = _other_chips(x, y)

        def half(ref, h):
            rows = ref.shape[0] // 2
            return ref.at[pl.ds(h * rows, rows)]

        local = [pltpu.make_async_copy(srcs[b], outs[b].at[q], local_sems.at[b]) for b in range(nb)]
        for cp in local:
            cp.start()
        first = []
        for j, chip in enumerate(chips):
            for b in range(nb):
                k = j * nb + b
                cp = pltpu.make_async_remote_copy(
                    src_ref=half(srcs[b], c), dst_ref=half(outs[b].at[q], c),
                    send_sem=send_sems.at[k], recv_sem=recv_sems.at[k], device_id=(*chip, c), device_id_type=MESH)
                cp.start()
                first.append(cp)
        passed = []
        for j, chip in enumerate(chips):
            qj = 2 * chip[0] + chip[1]
            for b in range(nb):
                k = j * nb + b
                landed = half(outs[b].at[qj], c)
                pltpu.make_async_remote_copy(
                    src_ref=landed, dst_ref=landed, send_sem=send_sems.at[k], recv_sem=recv_sems.at[k],
                    device_id=(*chip, c), device_id_type=MESH).wait_recv()
                k2 = 3 * nb + k
                cp = pltpu.make_async_remote_copy(
                    src_ref=landed, dst_ref=landed, send_sem=send_sems.at[k2], recv_sem=recv_sems.at[k2],
                    device_id=sibling, device_id_type=MESH)
                cp.start()
                passed.append(cp)
        for j, chip in enumerate(chips):
            qj = 2 * chip[0] + chip[1]
            for b in range(nb):
                k2 = 3 * nb + j * nb + b
                other = half(outs[b].at[qj], 1 - c)
                pltpu.make_async_remote_copy(
                    src_ref=other, dst_ref=other, send_sem=send_sems.at[k2], recv_sem=recv_sems.at[k2],
                    device_id=sibling, device_id_type=MESH).wait_recv()
        for cp in first + passed:
            cp.wait_send()
        for cp in local:
            cp.wait()

    any_spec = pl.BlockSpec(memory_space=pl.ANY)
    return _pcall(
        body, name="all_gather_weights",
        out_shape=[jax.ShapeDtypeStruct((N_CHIPS,) + b.shape, b.dtype) for b in bufs],
        in_specs=[any_spec] * nb, out_specs=[any_spec] * nb,
        scratch_shapes=[pltpu.SemaphoreType.DMA((6 * nb,)), pltpu.SemaphoreType.DMA((6 * nb,)),
                        pltpu.SemaphoreType.DMA((nb,))],
        compiler_params=_cparams(),
    )(*bufs)


def _rs_to_sibling(ga, gb):
    bufs = (ga, gb)
    nb = len(bufs)

    def body(a_ref, b_ref, ao_ref, bo_ref, send_sems, recv_sems):
        x, y, c = _my_pos()
        srcs, outs = (a_ref, b_ref), (ao_ref, bo_ref)
        cps = []
        for b in range(nb):
            rows = outs[b].shape[1]
            cp = pltpu.make_async_remote_copy(
                src_ref=srcs[b].at[:, pl.ds((1 - c) * rows, rows)], dst_ref=outs[b],
                send_sem=send_sems.at[b], recv_sem=recv_sems.at[b], device_id=(x, y, 1 - c), device_id_type=MESH)
            cp.start()
            cps.append(cp)
        for cp in cps:
            cp.wait()

    any_spec = pl.BlockSpec(memory_space=pl.ANY)
    return _pcall(
        body, name="rs_to_sibling",
        out_shape=[jax.ShapeDtypeStruct((b.shape[0], b.shape[1] // 2, b.shape[2]), b.dtype) for b in bufs],
        in_specs=[any_spec] * nb, out_specs=[any_spec] * nb,
        scratch_shapes=[pltpu.SemaphoreType.DMA((nb,)), pltpu.SemaphoreType.DMA((nb,))],
        compiler_params=_cparams(),
    )(*bufs)


def _rs_to_chips(pa, pb):
    bufs = (pa, pb)
    nb = len(bufs)

    def body(a_ref, b_ref, ao_ref, bo_ref, send_sems, recv_sems):
        x, y, c = _my_pos()
        srcs, outs = (a_ref, b_ref), (ao_ref, bo_ref)
        cps = []
        for j, chip in enumerate(_other_chips(x, y)):
            qj = 2 * chip[0] + chip[1]
            for b in range(nb):
                k = j * nb + b
                cp = pltpu.make_async_remote_copy(
                    src_ref=srcs[b].at[qj], dst_ref=outs[b].at[j],
                    send_sem=send_sems.at[k], recv_sem=recv_sems.at[k], device_id=(*chip, c), device_id_type=MESH)
                cp.start()
                cps.append(cp)
        for cp in cps:
            cp.wait()

    any_spec = pl.BlockSpec(memory_space=pl.ANY)
    return _pcall(
        body, name="rs_to_chips",
        out_shape=[jax.ShapeDtypeStruct((3,) + b.shape[1:], b.dtype) for b in bufs],
        in_specs=[any_spec] * nb, out_specs=[any_spec] * nb,
        scratch_shapes=[pltpu.SemaphoreType.DMA((3 * nb,)), pltpu.SemaphoreType.DMA((3 * nb,))],
        compiler_params=_cparams(),
    )(*bufs)


def _rs_join_halves(sa, sb):
    bufs = (sa, sb)
    nb = len(bufs)

    def body(a_ref, b_ref, ao_ref, bo_ref, send_sems, recv_sems, local_sems):
        x, y, c = _my_pos()
        srcs, outs = (a_ref, b_ref), (ao_ref, bo_ref)
        cps, local = [], []
        for b in range(nb):
            rows = srcs[b].shape[0]
            mine = outs[b].at[pl.ds(c * rows, rows)]
            lc = pltpu.make_async_copy(srcs[b], mine, local_sems.at[b])
            lc.start()
            local.append(lc)
            cp = pltpu.make_async_remote_copy(
                src_ref=srcs[b], dst_ref=mine, send_sem=send_sems.at[b], recv_sem=recv_sems.at[b],
                device_id=(x, y, 1 - c), device_id_type=MESH)
            cp.start()
            cps.append(cp)
        for b in range(nb):
            rows = srcs[b].shape[0]
            theirs = outs[b].at[pl.ds((1 - c) * rows, rows)]
            pltpu.make_async_remote_copy(
                src_ref=srcs[b], dst_ref=theirs, send_sem=send_sems.at[b], recv_sem=recv_sems.at[b],
                device_id=(x, y, 1 - c), device_id_type=MESH).wait_recv()
        for cp in cps:
            cp.wait_send()
        for lc in local:
            lc.wait()

    any_spec = pl.BlockSpec(memory_space=pl.ANY)
    return _pcall(
        body, name="rs_join_halves",
        out_shape=[jax.ShapeDtypeStruct((2 * b.shape[0], b.shape[1]), b.dtype) for b in bufs],
        in_specs=[any_spec] * nb, out_specs=[any_spec] * nb,
        scratch_shapes=[pltpu.SemaphoreType.DMA((nb,)), pltpu.SemaphoreType.DMA((nb,)), pltpu.SemaphoreType.DMA((nb,))],
        compiler_params=_cparams(),
    )(*bufs)


def _add_halves(g, r, c_idx, tr):
    nq, rows, w = r.shape
    nt = rows // tr

    def body(c_ref, g_ref, r_ref, o_ref):
        o_ref[...] = g_ref[...] + r_ref[...]

    grid_spec = pltpu.PrefetchScalarGridSpec(
        num_scalar_prefetch=1, grid=(nq, nt),
        in_specs=[pl.BlockSpec((None, tr, w), lambda q, i, c: (q, c[0] * nt + i, 0)),
                  pl.BlockSpec((None, tr, w), lambda q, i, c: (q, i, 0))],
        out_specs=pl.BlockSpec((None, tr, w), lambda q, i, c: (q, i, 0)))
    return _pcall(body, name="rs_add_halves", grid_spec=grid_spec,
                  out_shape=jax.ShapeDtypeStruct(r.shape, F32), compiler_params=_seq(2))(c_idx, g, r)


def _add_chips(p, r3, q_idx, tr):
    _, rows, w = p.shape
    nt = rows // tr

    def body(q_ref, p_ref, a_ref, b_ref, c_ref, o_ref):
        o_ref[...] = ((p_ref[...] + a_ref[...]) + b_ref[...]) + c_ref[...]

    def other(j):
        return pl.BlockSpec((None, tr, w), lambda i, q: (j, i, 0))

    grid_spec = pltpu.PrefetchScalarGridSpec(
        num_scalar_prefetch=1, grid=(nt,),
        in_specs=[pl.BlockSpec((None, tr, w), lambda i, q: (q[0], i, 0)), other(0), other(1), other(2)],
        out_specs=pl.BlockSpec((tr, w), lambda i, q: (i, 0)))
    return _pcall(body, name="rs_add_chips", grid_spec=grid_spec,
                  out_shape=jax.ShapeDtypeStruct((rows, w), F32), compiler_params=_seq(1))(q_idx, p, r3, r3, r3)


def _ada_fwd(c_all, w_sh, b_sh):
    def body(c_ref, w_ref, b_ref, o_ref):
        cv = c_ref[...]
        ca = cv * _sig(cv)
        o_ref[...] = jnp.dot(ca, w_ref[...], preferred_element_type=F32, precision=lax.Precision.HIGHEST) + b_ref[...]

    return _pcall(body, name="ada_fwd", out_shape=jax.ShapeDtypeStruct((c_all.shape[0], w_sh.shape[1]), F32),
                  compiler_params=_cparams())(c_all, w_sh, b_sh)


def _ada_bwd(c_all_t, dmod_sh):
    def body(c_ref, d_ref, o_ref):
        cv = c_ref[...]
        ca = cv * _sig(cv)
        o_ref[...] = jnp.dot(ca, d_ref[...], preferred_element_type=F32, precision=lax.Precision.HIGHEST)

    return _pcall(body, name="ada_bwd", out_shape=jax.ShapeDtypeStruct((c_all_t.shape[0], dmod_sh.shape[1]), F32),
                  compiler_params=_cparams())(c_all_t, dmod_sh)


def _fwd_in(x, vecs, wa, b_in, tm):
    s, d = x.shape
    nq, _, nw = wa.shape

    def body(x_ref, v_ref, w_ref, b_ref, h_ref, z_ref):
        xh, _ = _ln(x_ref[...])
        h = (xh * (1.0 + v_ref[V_SCALE1:V_SCALE1 + 1, :]) + v_ref[V_SHIFT1:V_SHIFT1 + 1, :]).astype(BF16)
        h_ref[...] = h
        for q in range(nq):
            z_ref[:, q * nw:(q + 1) * nw] = _dot(h, w_ref[q]) + b_ref[:, q * nw:(q + 1) * nw]

    return _pcall(
        body, name="fwd_in", grid=(s // tm,),
        in_specs=[_rows(tm, d), _full(vecs.shape), _full(wa.shape, single=True), _full(b_in.shape)],
        out_specs=[_rows(tm, d), _rows(tm, nq * nw)],
        out_shape=[jax.ShapeDtypeStruct((s, d), BF16), jax.ShapeDtypeStruct((s, nq * nw), F32)],
        compiler_params=_seq(1),
    )(x, vecs, wa, b_in)


def _conv_causal(ext_ref, w_ref, ntaps, halo, tm, d, rc, bias, out_ref):
    off = halo - (ntaps - 1)
    for r0 in range(0, tm, rc):
        acc = jnp.zeros((rc, d), F32) if bias is None else jnp.broadcast_to(bias, (rc, d))
        for k in range(ntaps):
            acc = acc + w_ref[k:k + 1, :] * ext_ref[r0 + off + k:r0 + off + k + rc, :]
        out_ref[r0:r0 + rc, :] = acc


def _conv_anticausal(ext_ref, w_ref, ntaps, tm, d, rc, out_ref):
    for r0 in range(0, tm, rc):
        acc = jnp.zeros((rc, d), F32)
        for k in range(ntaps):
            o = ntaps - 1 - k
            acc = acc + w_ref[k:k + 1, :] * ext_ref[r0 + o:r0 + o + rc, :]
        out_ref[r0:r0 + rc, :] = acc


def _fwd_mix(z, vecs, caw, cbw, wb, tm, rc):
    s = z.shape[0]
    d = vecs.shape[1]
    nq = wb.shape[0]
    kq = d // nq
    base = 2 * d // kq

    def body(z_ref, v_ref, caw_ref, cbw_ref, wao_ref, wbo_ref, wo_ref,
             u1_ref, ya_ref, yb_ref, o1_ref, u3_ref, vv_ref, mg_ref, ext_ref, pext_ref, q_ref):
        @pl.when(pl.program_id(0) == 0)
        def _():
            ext_ref[0:HALO_A, :] = jnp.zeros((HALO_A, d), F32)
            pext_ref[0:HALO_B, :] = jnp.zeros((HALO_B, d), F32)

        ext_ref[HALO_A:HALO_A + tm, :] = z_ref[:, 0:d] * _sig(z_ref[:, d:2 * d])
        _conv_causal(ext_ref, caw_ref, CONV_A, HALO_A, tm, d, rc, v_ref[V_CAB:V_CAB + 1, :], u1_ref)
        ext_ref[0:HALO_A, :] = ext_ref[tm:tm + HALO_A, :]
        xa, _ = _ln(u1_ref[...])
        u2 = xa * v_ref[V_LNAG:V_LNAG + 1, :] + v_ref[V_LNAB:V_LNAB + 1, :]
        u3 = (u2 * _sig(u2)).astype(BF16)
        u3_ref[...] = u3
        ya = jnp.broadcast_to(v_ref[V_BAO:V_BAO + 1, :], (tm, d))
        for q in range(nq):
            ya = ya + _dot(u3[:, q * kq:(q + 1) * kq], wao_ref[q])
        ya_ref[...] = ya

        pext_ref[HALO_B:HALO_B + tm, :] = z_ref[:, 3 * d:4 * d] * z_ref[:, 4 * d:5 * d]
        _conv_causal(pext_ref, cbw_ref, CONV_B, HALO_B, tm, d, rc, None, q_ref)
        pext_ref[0:HALO_B, :] = pext_ref[tm:tm + HALO_B, :]
        vv = (z_ref[:, 2 * d:3 * d] * q_ref[...]).astype(BF16)
        vv_ref[...] = vv
        yb = jnp.zeros((tm, d), F32)
        for q in range(nq):
            yb = yb + _dot(vv[:, q * kq:(q + 1) * kq], wbo_ref[q])
        yb_ref[...] = yb

        mg = (_sig(z_ref[:, 5 * d:6 * d]) * ya + _sig(z_ref[:, 6 * d:7 * d]) * yb).astype(BF16)
        mg_ref[...] = mg
        o1 = jnp.broadcast_to(v_ref[V_BO:V_BO + 1, :], (tm, d))
        for q in range(nq):
            o1 = o1 + _dot(mg[:, q * kq:(q + 1) * kq], wo_ref[q])
        o1_ref[...] = o1

    def wspec(j):
        return pl.BlockSpec((nq, kq, d), lambda i: (0, base + j, 0))

    f32o = jax.ShapeDtypeStruct((s, d), F32)
    b16o = jax.ShapeDtypeStruct((s, d), BF16)
    return _pcall(
        body, name="fwd_mix", grid=(s // tm,),
        in_specs=[_rows(tm, 7 * d), _full(vecs.shape), _full(caw.shape), _full(cbw.shape), wspec(0), wspec(1), wspec(2)],
        out_specs=[_rows(tm, d)] * 7,
        out_shape=[f32o, f32o, f32o, f32o, b16o, b16o, b16o],
        scratch_shapes=[pltpu.VMEM((HALO_A + tm, d), F32), pltpu.VMEM((HALO_B + tm, d), F32), pltpu.VMEM((tm, d), F32)],
        compiler_params=_seq(1),
    )(z, vecs, caw, cbw, wb, wb, wb)


M_LN2G, M_LN2B, M_GATE2, M_BDN, M_SHIFT2, M_SCALE2, M_LN1G, M_LN1B, M_GATE1, M_BO, M_LOSS = range(11)


def _mlp_fwd_bwd(x, out1, tgt, vecs, b_up, wb, tm):
    s, d = x.shape
    nq = wb.shape[0]
    dff = nq * d

    def body(x_ref, o1_ref, t_ref, v_ref, bup_ref, wup_ref, wdn_ref,
             h2_ref, f_ref, df0_ref, do2_ref, do1_ref, dxp_ref, acc_ref, dbup_ref, f0_ref):
        @pl.when(pl.program_id(0) == 0)
        def _():
            acc_ref[...] = jnp.zeros(acc_ref.shape, F32)
            dbup_ref[...] = jnp.zeros(dbup_ref.shape, F32)

        def vec(r):
            return v_ref[r:r + 1, :]

        def accum(r, val):
            acc_ref[r:r + 1, :] += _rsum(val)

        out1v = o1_ref[...]
        r1 = ALPHA * x_ref[...] + (1.0 + vec(V_GATE1)) * out1v
        xh1, rstd1 = _ln(r1)
        x1 = xh1 * vec(V_LN1G) + vec(V_LN1B)
        xn1, rstdn = _ln(x1)
        h2 = (xn1 * (1.0 + vec(V_SCALE2)) + vec(V_SHIFT2)).astype(BF16)
        h2_ref[...] = h2
        out2 = jnp.broadcast_to(vec(V_BDN), (tm, d))
        for q in range(nq):
            f0 = _dot(h2, wup_ref[q]) + bup_ref[:, q * d:(q + 1) * d]
            rl = jnp.maximum(f0, 0.0)
            f0_ref[:, q * d:(q + 1) * d] = rl
            fb = (rl * rl).astype(BF16)
            f_ref[:, q * d:(q + 1) * d] = fb
            out2 = out2 + _dot(fb, wdn_ref[q])
        r2 = ALPHA * x1 + (1.0 + vec(V_GATE2)) * out2
        xh2, rstd2 = _ln(r2)
        yv = xh2 * vec(V_LN2G) + vec(V_LN2B)
        err = yv - t_ref[...]
        accum(M_LOSS, err * err)
        dy = err * (1.0 / d)
        accum(M_LN2G, dy * xh2)
        accum(M_LN2B, dy)
        dr2 = _ln_bwd(dy * vec(V_LN2G), xh2, rstd2)
        accum(M_GATE2, dr2 * out2)
        dout2 = (1.0 + vec(V_GATE2)) * dr2
        accum(M_BDN, dout2)
        do2b = dout2.astype(BF16)
        do2_ref[...] = do2b
        dh2 = jnp.zeros((tm, d), F32)
        for q in range(nq):
            df0 = _dot_nt(do2b, wdn_ref[q]) * (2.0 * f0_ref[:, q * d:(q + 1) * d])
            dbup_ref[:, q * d:(q + 1) * d] += _rsum(df0)
            df0b = df0.astype(BF16)
            df0_ref[:, q * d:(q + 1) * d] = df0b
            dh2 = dh2 + _dot_nt(df0b, wup_ref[q])
        accum(M_SHIFT2, dh2)
        accum(M_SCALE2, dh2 * xn1)
        dx1 = ALPHA * dr2 + _ln_bwd(dh2 * (1.0 + vec(V_SCALE2)), xn1, rstdn)
        accum(M_LN1G, dx1 * xh1)
        accum(M_LN1B, dx1)
        dr1 = _ln_bwd(dx1 * vec(V_LN1G), xh1, rstd1)
        accum(M_GATE1, dr1 * out1v)
        dout1 = (1.0 + vec(V_GATE1)) * dr1
        accum(M_BO, dout1)
        do1_ref[...] = dout1.astype(BF16)
        dxp_ref[...] = ALPHA * dr1

    def wspec(j):
        return pl.BlockSpec((nq, d, d), lambda i: (0, j, 0), pipeline_mode=pl.Buffered(1))

    b16 = lambda w: jax.ShapeDtypeStruct((s, w), BF16)
    return _pcall(
        body, name="mlp_fwd_bwd", grid=(s // tm,),
        in_specs=[_rows(tm, d), _rows(tm, d), _rows(tm, d), _full(vecs.shape), _full(b_up.shape), wspec(0), wspec(1)],
        out_specs=[_rows(tm, d), _rows(tm, dff), _rows(tm, dff), _rows(tm, d), _rows(tm, d), _rows(tm, d),
                   _full((16, d)), _full((1, dff))],
        out_shape=[b16(d), b16(dff), b16(dff), b16(d), b16(d), jax.ShapeDtypeStruct((s, d), F32),
                   jax.ShapeDtypeStruct((16, d), F32), jax.ShapeDtypeStruct((1, dff), F32)],
        scratch_shapes=[pltpu.VMEM((tm, dff), F32)],
        compiler_params=_seq(1),
    )(x, out1, tgt, vecs, b_up, wb, wb)


X_BAO, X_LNAG, X_LNAB, X_CAB = range(4)


def _mix_bwd(dout1, z, u1, ya, yb, vecs, caw, cbw, wb, tm, rc):
    s = z.shape[0]
    d = vecs.shape[1]
    nq = wb.shape[0]
    kq = d // nq
    base = 2 * d // kq
    nt = s // tm
    hb = tm // HALO_A

    def body(do1_ref, z_ref, zh_ref, u1_ref, ya_ref, yb_ref, v_ref, caw_ref, cbw_ref, wao_ref, wbo_ref, wo_ref,
             dz_ref, dya_ref, dyb_ref, acc_ref, dcaw_ref, dcbw_ref, dbin_ref,
             ext_ref, du1e_ref, pext_ref, dqe_ref, tmp_ref):
        i = pl.program_id(0)

        @pl.when(i == 0)
        def _():
            acc_ref[...] = jnp.zeros(acc_ref.shape, F32)
            dcaw_ref[...] = jnp.zeros(dcaw_ref.shape, F32)
            dcbw_ref[...] = jnp.zeros(dcbw_ref.shape, F32)
            dbin_ref[...] = jnp.zeros(dbin_ref.shape, F32)
            du1e_ref[tm:tm + HALO_A, :] = jnp.zeros((HALO_A, d), F32)
            dqe_ref[tm:tm + HALO_B, :] = jnp.zeros((HALO_B, d), F32)

        def vec(r):
            return v_ref[r:r + 1, :]

        def accum(r, val):
            acc_ref[r:r + 1, :] += _rsum(val)

        def put_dz(j, val):
            dbin_ref[:, j * d:(j + 1) * d] += _rsum(val)
            dz_ref[:, j * d:(j + 1) * d] = val.astype(BF16)

        has_history = i < nt - 1

        do1 = do1_ref[...]
        dmg = jnp.concatenate([_dot_nt(do1, wo_ref[q]) for q in range(nq)], axis=1)
        sga = _sig(z_ref[:, 5 * d:6 * d])
        sgb = _sig(z_ref[:, 6 * d:7 * d])
        dya = dmg * sga
        dyb = dmg * sgb
        accum(X_BAO, dya)
        put_dz(5, dya * ya_ref[...] * (1.0 - sga))
        put_dz(6, dyb * yb_ref[...] * (1.0 - sgb))
        dyab = dya.astype(BF16)
        dybb = dyb.astype(BF16)
        dya_ref[...] = dyab
        dyb_ref[...] = dybb

        du3 = jnp.concatenate([_dot_nt(dyab, wao_ref[q]) for q in range(nq)], axis=1)
        xa, rstda = _ln(u1_ref[...])
        u2 = xa * vec(V_LNAG) + vec(V_LNAB)
        s2 = _sig(u2)
        du2 = du3 * (s2 * (1.0 + u2 * (1.0 - s2)))
        accum(X_LNAG, du2 * xa)
        accum(X_LNAB, du2)
        du1 = _ln_bwd(du2 * vec(V_LNAG), xa, rstda)
        accum(X_CAB, du1)
        du1e_ref[0:tm, :] = du1
        sg = _sig(z_ref[:, d:2 * d])
        aval = z_ref[:, 0:d]
        ext_ref[HALO_A:HALO_A + tm, :] = aval * sg
        ext_ref[0:HALO_A, :] = jnp.where(has_history, zh_ref[:, 0:d] * _sig(zh_ref[:, d:2 * d]), 0.0)
        off = HALO_A - (CONV_A - 1)
        for k in range(CONV_A):
            dcaw_ref[k:k + 1, :] += _rsum(du1 * ext_ref[off + k:off + k + tm, :])
        _conv_anticausal(du1e_ref, caw_ref, CONV_A, tm, d, rc, tmp_ref)
        du1e_ref[tm:tm + HALO_A, :] = du1e_ref[0:HALO_A, :]
        du0 = tmp_ref[...]
        put_dz(0, du0 * sg)
        put_dz(1, du0 * aval * sg * (1.0 - sg))

        dv = jnp.concatenate([_dot_nt(dybb, wbo_ref[q]) for q in range(nq)], axis=1)
        bgc = z_ref[:, 3 * d:4 * d]
        bx = z_ref[:, 4 * d:5 * d]
        pext_ref[HALO_B:HALO_B + tm, :] = bgc * bx
        pext_ref[0:HALO_B, :] = jnp.where(
            has_history, zh_ref[HALO_A - HALO_B:HALO_A, 3 * d:4 * d] * zh_ref[HALO_A - HALO_B:HALO_A, 4 * d:5 * d], 0.0)
        _conv_causal(pext_ref, cbw_ref, CONV_B, HALO_B, tm, d, rc, None, tmp_ref)
        put_dz(2, dv * tmp_ref[...])
        dq = dv * z_ref[:, 2 * d:3 * d]
        dqe_ref[0:tm, :] = dq
        offb = HALO_B - (CONV_B - 1)
        for k in range(CONV_B):
            dcbw_ref[k:k + 1, :] += _rsum(dq * pext_ref[offb + k:offb + k + tm, :])
        _conv_anticausal(dqe_ref, cbw_ref, CONV_B, tm, d, rc, tmp_ref)
        dqe_ref[tm:tm + HALO_B, :] = dqe_ref[0:HALO_B, :]
        dp = tmp_ref[...]
        put_dz(3, dp * bx)
        put_dz(4, dp * bgc)

    def rev(width):
        return pl.BlockSpec((tm, width), lambda i: (nt - 1 - i, 0))

    def wspec(j):
        return pl.BlockSpec((nq, kq, d), lambda i: (0, base + j, 0))

    halo = pl.BlockSpec((HALO_A, 7 * d), lambda i: (jnp.maximum((nt - 1 - i) * hb - 1, 0), 0))
    b16 = jax.ShapeDtypeStruct((s, d), BF16)
    return _pcall(
        body, name="mix_bwd", grid=(nt,),
        in_specs=[rev(d), rev(7 * d), halo, rev(d), rev(d), rev(d), _full(vecs.shape), _full(caw.shape), _full(cbw.shape),
                  wspec(0), wspec(1), wspec(2)],
        out_specs=[rev(7 * d), rev(d), rev(d), _full((8, d)), _full((HALO_A, d)), _full((HALO_B, d)), _full((1, 7 * d))],
        out_shape=[jax.ShapeDtypeStruct((s, 7 * d), BF16), b16, b16, jax.ShapeDtypeStruct((8, d), F32),
                   jax.ShapeDtypeStruct((HALO_A, d), F32), jax.ShapeDtypeStruct((HALO_B, d), F32),
                   jax.ShapeDtypeStruct((1, 7 * d), F32)],
        scratch_shapes=[pltpu.VMEM((HALO_A + tm, d), F32), pltpu.VMEM((tm + HALO_A, d), F32),
                        pltpu.VMEM((HALO_B + tm, d), F32), pltpu.VMEM((tm + HALO_B, d), F32), pltpu.VMEM((tm, d), F32)],
        compiler_params=_seq(1),
    )(dout1, z, z, u1, ya, yb, vecs, caw, cbw, wb, wb, wb)


def _in_bwd(dz, x, dxp, vecs, wa, tm):
    s, d = x.shape
    nq, _, nw = wa.shape

    def body(dz_ref, x_ref, dxp_ref, v_ref, w_ref, gx_ref, acc_ref):
        @pl.when(pl.program_id(0) == 0)
        def _():
            acc_ref[...] = jnp.zeros(acc_ref.shape, F32)

        dh1 = jnp.zeros((tm, d), F32)
        for q in range(nq):
            dh1 = dh1 + _dot_nt(dz_ref[:, q * nw:(q + 1) * nw], w_ref[q])
        xh, rstd = _ln(x_ref[...])
        acc_ref[0:1, :] += _rsum(dh1)
        acc_ref[1:2, :] += _rsum(dh1 * xh)
        gx_ref[...] = dxp_ref[...] + _ln_bwd(dh1 * (1.0 + v_ref[V_SCALE1:V_SCALE1 + 1, :]), xh, rstd)

    return _pcall(
        body, name="in_bwd", grid=(s // tm,),
        in_specs=[_rows(tm, nq * nw), _rows(tm, d), _rows(tm, d), _full(vecs.shape), _full(wa.shape, single=True)],
        out_specs=[_rows(tm, d), _full((8, d))],
        out_shape=[jax.ShapeDtypeStruct((s, d), F32), jax.ShapeDtypeStruct((8, d), F32)],
        compiler_params=_seq(1),
    )(dz, x, dxp, vecs, wa)


def _dw(a, b, nblk, split_a, ts, name):
    s = a.shape[0]
    ka = a.shape[1] // nblk if split_a else a.shape[1]
    nb = b.shape[1] if split_a else b.shape[1] // nblk

    def body(a_ref, b_ref, o_ref):
        @pl.when(pl.program_id(1) == 0)
        def _():
            o_ref[...] = jnp.zeros(o_ref.shape, F32)

        o_ref[...] += _dot_tn(a_ref[...], b_ref[...])

    a_spec = pl.BlockSpec((ts, ka), (lambda q, i: (i, q)) if split_a else (lambda q, i: (i, 0)))
    b_spec = pl.BlockSpec((ts, nb), (lambda q, i: (i, 0)) if split_a else (lambda q, i: (i, q)))
    return _pcall(
        body, name=name, grid=(nblk, s // ts),
        in_specs=[a_spec, b_spec], out_specs=pl.BlockSpec((None, ka, nb), lambda q, i: (q, 0, 0)),
        out_shape=jax.ShapeDtypeStruct((nblk, ka, nb), F32),
        compiler_params=_seq(2),
    )(a, b)


def _adam_math(w, g, m, v):
    m2 = ADAM_B1 * m + (1.0 - ADAM_B1) * g
    v2 = ADAM_B2 * v + (1.0 - ADAM_B2) * (g * g)
    m_hat = m2 / (1.0 - ADAM_B1 ** ADAM_STEP)
    v_hat = v2 / (1.0 - ADAM_B2 ** ADAM_STEP)
    delta = -ADAM_LR * (m_hat / (jnp.sqrt(v_hat) + ADAM_EPS) + ADAM_WD * w)
    return delta, m2, v2


def _adam(w, g, m, v, tr, name):
    r, c = w.shape

    def body(w_ref, g_ref, m_ref, v_ref, go_ref, d_ref, mo_ref, vo_ref):
        gv = g_ref[...]
        go_ref[...] = gv
        d_ref[...], mo_ref[...], vo_ref[...] = _adam_math(w_ref[...], gv, m_ref[...], v_ref[...])

    spec = _rows(tr, c)
    o = jax.ShapeDtypeStruct((r, c), F32)
    return _pcall(body, name=name, grid=(r // tr,), in_specs=[spec] * 4, out_specs=[spec] * 4,
                  out_shape=[o, o, o, o], compiler_params=_seq(1))(w, g, m, v)


_SMALL = (("b_ada", R_DMOD, 6), ("b_in", R_BIN, 7), ("conv_a_b", R_CAB, 1), ("ln_a_g", R_LNAG, 1), ("ln_a_b", R_LNAB, 1),
          ("b_a_out", R_BAO, 1), ("b_o", R_BO, 1), ("ln1_g", R_LN1G, 1), ("ln1_b", R_LN1B, 1), ("b_up", R_BUP, 4),
          ("b_down", R_BDN, 1), ("ln2_g", R_LN2G, 1), ("ln2_b", R_LN2B, 1))


def _small_update(gathered, q_idx, small_w, small_m, small_v, conv_w, conv_m, conv_v):
    d = gathered.shape[2]
    ns = len(_SMALL)
    cw = conv_w[0].shape[1]
    conv_rows = ((R_CAW, CONV_A), (R_CBW, CONV_B))

    def body(q_ref, g_ref, *refs):
        ins, outs = refs[:3 * (ns + 2)], refs[3 * (ns + 2):]
        tot_ref, loss_ref = outs[0], outs[1]
        outs = outs[2:]
        tot = g_ref[0]
        for dev in range(1, N_DEV):
            tot = tot + g_ref[dev]
        tot_ref[...] = tot
        loss_ref[...] = (0.5 / d) * jnp.sum(tot[R_LOSS:R_LOSS + 1, :], axis=1, keepdims=True)
        for p, (_, row, n) in enumerate(_SMALL):
            w_ref, m_ref, v_ref = ins[p], ins[ns + 2 + p], ins[2 * (ns + 2) + p]
            go, do, mo, vo = outs[4 * p:4 * p + 4]
            for j in range(n):
                sl = slice(j * d, (j + 1) * d)
                gv = tot[row + j:row + j + 1, :]
                go[:, sl] = gv
                do[:, sl], mo[:, sl], vo[:, sl] = _adam_math(w_ref[:, sl], gv, m_ref[:, sl], v_ref[:, sl])
        for p, (row, taps) in enumerate(conv_rows):
            w_ref, m_ref, v_ref = ins[ns + p], ins[ns + 2 + ns + p], ins[2 * (ns + 2) + ns + p]
            go, do, mo, vo = outs[4 * (ns + p):4 * (ns + p) + 4]
            full = tot[row:row + taps, :]
            gv = full[:, 0:cw]
            for qq in range(1, N_CHIPS):
                gv = jnp.where(q_ref[0] == qq, full[:, qq * cw:(qq + 1) * cw], gv)
            go[...] = gv
            do[...], mo[...], vo[...] = _adam_math(w_ref[...], gv, m_ref[...], v_ref[...])

    params = list(small_w) + list(conv_w) + list(small_m) + list(conv_m) + list(small_v) + list(conv_v)
    out_shape = [jax.ShapeDtypeStruct((R_TOTAL, d), F32), jax.ShapeDtypeStruct((1, 1), F32)]
    for w in list(small_w) + list(conv_w):
        out_shape += [jax.ShapeDtypeStruct(w.shape, F32)] * 4
    vm = pl.BlockSpec(memory_space=pltpu.VMEM)
    return _pcall(
        body, name="small_update", out_shape=out_shape,
        in_specs=[pl.BlockSpec(memory_space=pltpu.SMEM), vm] + [vm] * len(params),
        out_specs=[vm] * len(out_shape), compiler_params=_cparams(),
    )(q_idx, gathered, *params)


def kernel(x, c, w_ada, b_ada, w_in, b_in, conv_a_w, conv_a_b, ln_a_g, ln_a_b, w_a_out, b_a_out, conv_b_w, w_b_out, w_o, b_o, ln1_g, ln1_b, w_up, b_up, w_down, b_down, ln2_g, ln2_b, loss_target, m_w_ada, m_b_ada, m_w_in, m_b_in, m_conv_a_w, m_conv_a_b, m_ln_a_g, m_ln_a_b, m_w_a_out, m_b_a_out, m_conv_b_w, m_w_b_out, m_w_o, m_b_o, m_ln1_g, m_ln1_b, m_w_up, m_b_up, m_w_down, m_b_down, m_ln2_g, m_ln2_b, v_w_ada, v_b_ada, v_w_in, v_b_in, v_conv_a_w, v_conv_a_b, v_ln_a_g, v_ln_a_b, v_w_a_out, v_b_a_out, v_conv_b_w, v_w_b_out, v_w_o, v_b_o, v_ln1_g, v_ln1_b, v_w_up, v_b_up, v_w_down, v_b_down, v_ln2_g, v_ln2_b):
    given = dict(locals())
    s, d = x.shape[1], x.shape[2]
    xi, yi, ci = _my_pos()
    q = 2 * xi + yi
    me = 4 * xi + 2 * yi + ci
    q_arr = jnp.reshape(q, (1,)).astype(jnp.int32)
    c_arr = jnp.reshape(ci, (1,)).astype(jnp.int32)
    kq = d // N_CHIPS
    tm = min(256, s)
    rc = min(32, tm)

    def sq(a):
        return a.reshape(a.shape[1:])

    x2, tgt = sq(x), sq(loss_target)

    n_ada = w_ada.shape[2]
    pre = jnp.concatenate([c, sq(conv_a_w).reshape(1, -1), sq(conv_b_w).reshape(1, -1)], axis=1)
    pre_all = _all_gather_small(jnp.broadcast_to(pre, (8, pre.shape[1])), "gather_c_conv")[:, 0, :]
    c_all = pre_all[:, :d]
    taps = pre_all[0::2, d:]
    caw = jnp.concatenate([taps[p, :CONV_A * kq].reshape(CONV_A, kq) for p in range(N_CHIPS)], axis=1)
    cbw = jnp.concatenate([taps[p, CONV_A * kq:].reshape(CONV_B, kq) for p in range(N_CHIPS)], axis=1)
    caw = jnp.pad(caw, ((0, HALO_A - CONV_A), (0, 0)))
    cbw = jnp.pad(cbw, ((0, HALO_B - CONV_B), (0, 0)))
    b_ada_sh = lax.dynamic_slice(b_ada, (0, q * n_ada), (1, n_ada))
    mod_part = _ada_fwd(c_all, sq(w_ada), b_ada_sh)
    mod_all = _all_gather_small(mod_part, "gather_mod")
    mod_rows = lax.dynamic_slice(mod_all, (0, me, 0), (N_DEV, 1, n_ada))[0::2, 0, :]
    mod = mod_rows.reshape(6, d)
    vecs = jnp.concatenate([mod, conv_a_b, ln_a_g, ln_a_b, b_a_out, b_o, ln1_g, ln1_b, b_down, ln2_g, ln2_b], axis=0)

    wa_sh = sq(w_in).astype(BF16)
    wb_sh = jnp.concatenate([sq(w_up), sq(w_down), sq(w_a_out), sq(w_b_out), sq(w_o)], axis=0).astype(BF16)
    wa, wb = _all_gather_weights(wa_sh, wb_sh)

    h1, z = _fwd_in(x2, vecs, wa, b_in, tm)
    u1, ya, yb, out1, u3, vv, mg = _fwd_mix(z, vecs, caw, cbw, wb, tm, rc)

    h2, fb, df0, do2, do1, dxp, macc, dbup = _mlp_fwd_bwd(x2, out1, tgt, vecs, b_up, wb, tm)
    dz, dya, dyb, xacc, dcaw, dcbw, dbin = _mix_bwd(do1, z, u1, ya, yb, vecs, caw, cbw, wb, tm, rc)
    gx, iacc = _in_bwd(dz, x2, dxp, vecs, wa, tm)

    ts = min(512, s)
    ga = _dw(h1, dz, N_CHIPS, False, ts, "dw_in")
    gb = jnp.concatenate([
        _dw(h2, df0, N_CHIPS, False, ts, "dw_up"),
        _dw(fb, do2, N_CHIPS, True, ts, "dw_down"),
        _dw(u3, dya, N_CHIPS, True, ts, "dw_a_out"),
        _dw(vv, dyb, N_CHIPS, True, ts, "dw_b_out"),
        _dw(mg, do1, N_CHIPS, True, ts, "dw_o"),
    ], axis=1)

    ra, rb = _rs_to_sibling(ga, gb)
    pa = _add_halves(ga, ra, c_arr, ra.shape[1] // 4)
    pb = _add_halves(gb, rb, c_arr, rb.shape[1] // 4)
    r3a, r3b = _rs_to_chips(pa, pb)
    sa = _add_chips(pa, r3a, q_arr, pa.shape[1] // 4)
    sb = _add_chips(pb, r3b, q_arr, pb.shape[1] // 4)
    g_in, g_b = _rs_join_halves(sa, sb)

    table = jnp.concatenate([
        iacc[0:2], macc[M_GATE1:M_GATE1 + 1], macc[M_SHIFT2:M_SCALE2 + 1], macc[M_GATE2:M_GATE2 + 1],
        dbin.reshape(7, d), dcaw[:CONV_A], xacc[X_CAB:X_CAB + 1], xacc[X_LNAG:X_LNAB + 1], xacc[X_BAO:X_BAO + 1],
        dcbw[:CONV_B], macc[M_BO:M_BO + 1], macc[M_LN1G:M_LN1B + 1], dbup.reshape(4, d), macc[M_BDN:M_BDN + 1],
        macc[M_LN2G:M_LN2B + 1], macc[M_LOSS:M_LOSS + 1], jnp.zeros((R_TOTAL - R_LOSS - 1, d), F32)], axis=0)
    gathered = _all_gather_small(table, "gather_small_grads")

    names = [n for n, _, _ in _SMALL]
    res = _small_update(
        gathered, q_arr,
        [given[n] for n in names], [given["m_" + n] for n in names], [given["v_" + n] for n in names],
        [sq(conv_a_w), sq(conv_b_w)], [sq(m_conv_a_w), sq(m_conv_b_w)], [sq(v_conv_a_w), sq(v_conv_b_w)])
    loss = res[1].reshape(())
    upd = {}
    for p, n in enumerate(names + ["conv_a_w", "conv_b_w"]):
        upd[n] = res[2 + 4 * p:6 + 4 * p]

    dmod_all = gathered[:, R_DMOD:R_DMOD + 6, :].reshape(N_DEV, 6 * d)
    dmod_sh = lax.dynamic_slice(dmod_all, (0, q * n_ada), (N_DEV, n_ada))
    g_ada = _ada_bwd(c_all.T, dmod_sh)
    upd["w_ada"] = _adam(sq(w_ada), g_ada, sq(m_w_ada), sq(v_w_ada), 256, "adam_w_ada")

    upd["w_in"] = _adam(sq(w_in), g_in, sq(m_w_in), sq(v_w_in), 256, "adam_w_in")
    r0 = 0
    for n in ("w_up", "w_down", "w_a_out", "w_b_out", "w_o"):
        w = sq(given[n])
        rows = w.shape[0]
        upd[n] = _adam(w, g_b[r0:r0 + rows], sq(given["m_" + n]), sq(given["v_" + n]), min(256, rows), "adam_" + n)
        r0 += rows

    order = ["w_ada", "b_ada", "w_in", "b_in", "conv_a_w", "conv_a_b", "ln_a_g", "ln_a_b", "w_a_out", "b_a_out", "conv_b_w",
             "w_b_out", "w_o", "b_o", "ln1_g", "ln1_b", "w_up", "b_up", "w_down", "b_down", "ln2_g", "ln2_b"]
    outs = [loss, gx.reshape(x.shape)]
    for k in range(4):
        outs += [upd[n][k].reshape(given[n].shape) for n in order]
    return tuple(outs)
```

```python
import jax
import jax.numpy as jnp
from jax import lax
from jax.experimental import pallas as pl
from jax.experimental.pallas import tpu as pltpu

F32 = jnp.float32
BF16 = jnp.bfloat16
MESH = pl.DeviceIdType.MESH

LN_EPS = 1e-5
DEPTH = 1
ALPHA = (2.0 * DEPTH) ** 0.25
CONV_A = 31
CONV_B = 3
SUBLANES = 8
HALO_A = 32
HALO_B = 8
N_CHIPS = 4
N_DEV = 8
ADAM_LR = 0.001
ADAM_B1 = 0.9
ADAM_B2 = 0.999
ADAM_EPS = 1e-08
ADAM_WD = 0.01
ADAM_STEP = 10
VMEM_LIMIT = 56 * 1024 * 1024

V_SHIFT1, V_SCALE1, V_GATE1, V_SHIFT2, V_SCALE2, V_GATE2 = 0, 1, 2, 3, 4, 5
V_CAB, V_LNAG, V_LNAB, V_BAO, V_BO, V_LN1G, V_LN1B, V_BDN, V_LN2G, V_LN2B = 6, 7, 8, 9, 10, 11, 12, 13, 14, 15

M_LN2G, M_LN2B, M_GATE2, M_BDN, M_SHIFT2, M_SCALE2, M_LN1G, M_LN1B, M_GATE1, M_BO, M_LOSS = range(11)
X_BAO, X_LNAG, X_LNAB, X_CAB = range(4)
I_SHIFT1, I_SCALE1 = 0, 1

T_I, T_M, T_BIN, T_CAW, T_X, T_CBW, T_BUP, T_ROWS = 0, 8, 24, 32, 64, 72, 80, 88
T_LOSS = T_M + M_LOSS
_SMALL = (
    ("b_ada", (T_I + I_SHIFT1, T_I + I_SCALE1, T_M + M_GATE1, T_M + M_SHIFT2, T_M + M_SCALE2, T_M + M_GATE2)),
    ("b_in", tuple(T_BIN + j for j in range(7))),
    ("conv_a_b", (T_X + X_CAB,)), ("ln_a_g", (T_X + X_LNAG,)), ("ln_a_b", (T_X + X_LNAB,)), ("b_a_out", (T_X + X_BAO,)),
    ("b_o", (T_M + M_BO,)), ("ln1_g", (T_M + M_LN1G,)), ("ln1_b", (T_M + M_LN1B,)),
    ("b_up", tuple(T_BUP + j for j in range(4))),
    ("b_down", (T_M + M_BDN,)), ("ln2_g", (T_M + M_LN2G,)), ("ln2_b", (T_M + M_LN2B,)),
)


def _pcall(body, **kw):
    return pl.pallas_call(body, **kw)


def _cparams(**kw):
    return pltpu.CompilerParams(vmem_limit_bytes=VMEM_LIMIT, **kw)


def _seq(n):
    return _cparams(dimension_semantics=("arbitrary",) * n)


def _full(shape, single=False):
    nd = len(shape)
    if single:
        return pl.BlockSpec(shape, lambda *_: (0,) * nd, pipeline_mode=pl.Buffered(1))
    return pl.BlockSpec(shape, lambda *_: (0,) * nd)


def _rows(tm, width):
    return pl.BlockSpec((tm, width), lambda i: (i, 0))


def _sig(x):
    return jax.nn.sigmoid(x)


def _ln(x):
    mu = jnp.mean(x, axis=-1, keepdims=True)
    xc = x - mu
    var = jnp.mean(xc * xc, axis=-1, keepdims=True)
    rstd = lax.rsqrt(var + LN_EPS)
    return xc * rstd, rstd


def _ln_bwd(dxh, xh, rstd):
    m1 = jnp.mean(dxh, axis=-1, keepdims=True)
    m2 = jnp.mean(dxh * xh, axis=-1, keepdims=True)
    return rstd * (dxh - m1 - xh * m2)


def _rsum(v):
    return jnp.sum(v, axis=0, keepdims=True)


def _dot(a, b):
    return jnp.dot(a, b, preferred_element_type=F32)


def _dot_nt(a, b):
    return lax.dot_general(a, b, (((1,), (1,)), ((), ())), preferred_element_type=F32)


def _dot_tn(a, b):
    return lax.dot_general(a, b, (((0,), (0,)), ((), ())), preferred_element_type=F32)


def _my_pos():
    return lax.axis_index("x"), lax.axis_index("y"), lax.axis_index("c")


def _other_chips(x, y):
    return [(1 - x, y), (x, 1 - y), (1 - x, 1 - y)]


def _all_gather_small(v, name):
    r, c = v.shape

    def body(v_ref, out_ref, send_sems, recv_sems, local_sem):
        x, y, cc = _my_pos()
        me = 4 * x + 2 * y + cc
        mine = pltpu.make_async_copy(v_ref, out_ref.at[me], local_sem)
        mine.start()
        sends = []
        for rel in range(1, N_DEV):
            rx, ry, rc = (rel >> 2) & 1, (rel >> 1) & 1, rel & 1
            peer = (1 - x if rx else x, 1 - y if ry else y, 1 - cc if rc else cc)
            cp = pltpu.make_async_remote_copy(
                src_ref=v_ref, dst_ref=out_ref.at[me], send_sem=send_sems.at[rel - 1], recv_sem=recv_sems.at[rel - 1],
                device_id=peer, device_id_type=MESH)
            cp.start()
            sends.append(cp)
        for rel in range(1, N_DEV):
            rx, ry, rc = (rel >> 2) & 1, (rel >> 1) & 1, rel & 1
            peer = (1 - x if rx else x, 1 - y if ry else y, 1 - cc if rc else cc)
            slot = 4 * peer[0] + 2 * peer[1] + peer[2]
            pltpu.make_async_remote_copy(
                src_ref=v_ref, dst_ref=out_ref.at[slot], send_sem=send_sems.at[rel - 1], recv_sem=recv_sems.at[rel - 1],
                device_id=peer, device_id_type=MESH).wait_recv()
        for cp in sends:
            cp.wait_send()
        mine.wait()

    return _pcall(
        body, name=name,
        out_shape=jax.ShapeDtypeStruct((N_DEV, r, c), v.dtype),
        in_specs=[pl.BlockSpec(memory_space=pltpu.VMEM)],
        out_specs=pl.BlockSpec(memory_space=pltpu.VMEM),
        scratch_shapes=[pltpu.SemaphoreType.DMA((N_DEV - 1,)), pltpu.SemaphoreType.DMA((N_DEV - 1,)),
                        pltpu.SemaphoreType.DMA],
        compiler_params=_cparams(),
    )(v)


def _place_shard(parts, q_idx, name):
    rows = sum(p.shape[0] for p in parts)
    w = parts[0].shape[1]

    def body(q_ref, *refs):
        o_ref = refs[-1]
        r0 = 0
        for p_ref in refs[:-1]:
            n = p_ref.shape[0]
            o_ref[r0:r0 + n, :] = p_ref[...].astype(BF16)
            r0 += n

    grid_spec = pltpu.PrefetchScalarGridSpec(
        num_scalar_prefetch=1, grid=(1,),
        in_specs=[pl.BlockSpec(p.shape, lambda i, q: (0, 0)) for p in parts],
        out_specs=pl.BlockSpec((None, rows, w), lambda i, q: (q[0], 0, 0)))
    return _pcall(body, name=name, grid_spec=grid_spec, out_shape=jax.ShapeDtypeStruct((N_CHIPS, rows, w), BF16),
                  compiler_params=_seq(1))(q_idx, *parts)


def _all_gather_weights(wa, wb):
    bufs = (wa, wb)
    nb = len(bufs)

    def body(a_ref, b_ref, ao_ref, bo_ref, send_sems, recv_sems):
        x, y, c = _my_pos()
        q = 2 * x + y
        srcs, outs = (a_ref, b_ref), (ao_ref, bo_ref)
        sibling = (x, y, 1 - c)
        chips = _other_chips(x, y)

        def half(ref, slot, h):
            rows = ref.shape[1] // 2
            return ref.at[slot, pl.ds(h * rows, rows)]

        first = []
        for j, chip in enumerate(chips):
            for b in range(nb):
                k = j * nb + b
                cp = pltpu.make_async_remote_copy(
                    src_ref=half(srcs[b], q, c), dst_ref=half(outs[b], q, c),
                    send_sem=send_sems.at[k], recv_sem=recv_sems.at[k], device_id=(*chip, c), device_id_type=MESH)
                cp.start()
                first.append(cp)
        passed = []
        for j, chip in enumerate(chips):
            qj = 2 * chip[0] + chip[1]
            for b in range(nb):
                k = j * nb + b
                landed = half(outs[b], qj, c)
                pltpu.make_async_remote_copy(
                    src_ref=landed, dst_ref=landed, send_sem=send_sems.at[k], recv_sem=recv_sems.at[k],
                    device_id=(*chip, c), device_id_type=MESH).wait_recv()
                k2 = 3 * nb + k
                cp = pltpu.make_async_remote_copy(
                    src_ref=landed, dst_ref=landed, send_sem=send_sems.at[k2], recv_sem=recv_sems.at[k2],
                    device_id=sibling, device_id_type=MESH)
                cp.start()
                passed.append(cp)
        for j, chip in enumerate(chips):
            qj = 2 * chip[0] + chip[1]
            for b in range(nb):
                k2 = 3 * nb + j * nb + b
                other = half(outs[b], qj, 1 - c)
                pltpu.make_async_remote_copy(
                    src_ref=other, dst_ref=other, send_sem=send_sems.at[k2], recv_sem=recv_sems.at[k2],
                    device_id=sibling, device_id_type=MESH).wait_recv()
        for cp in first + passed:
            cp.wait_send()

    any_spec = pl.BlockSpec(memory_space=pl.ANY)
    return _pcall(
        body, name="all_gather_weights",
        out_shape=[jax.ShapeDtypeStruct(b.shape, b.dtype) for b in bufs],
        in_specs=[any_spec] * nb, out_specs=[any_spec] * nb, input_output_aliases={0: 0, 1: 1},
        scratch_shapes=[pltpu.SemaphoreType.DMA((6 * nb,)), pltpu.SemaphoreType.DMA((6 * nb,))],
        compiler_params=_cparams(),
    )(*bufs)


def _rs_to_sibling(ga, gb):
    bufs = (ga, gb)
    nb = len(bufs)

    def body(a_ref, b_ref, ao_ref, bo_ref, send_sems, recv_sems):
        x, y, c = _my_pos()
        srcs, outs = (a_ref, b_ref), (ao_ref, bo_ref)
        cps = []
        for b in range(nb):
            rows = outs[b].shape[1]
            cp = pltpu.make_async_remote_copy(
                src_ref=srcs[b].at[:, pl.ds((1 - c) * rows, rows)], dst_ref=outs[b],
                send_sem=send_sems.at[b], recv_sem=recv_sems.at[b], device_id=(x, y, 1 - c), device_id_type=MESH)
            cp.start()
            cps.append(cp)
        for cp in cps:
            cp.wait()

    any_spec = pl.BlockSpec(memory_space=pl.ANY)
    return _pcall(
        body, name="rs_to_sibling",
        out_shape=[jax.ShapeDtypeStruct((b.shape[0], b.shape[1] // 2, b.shape[2]), b.dtype) for b in bufs],
        in_specs=[any_spec] * nb, out_specs=[any_spec] * nb,
        scratch_shapes=[pltpu.SemaphoreType.DMA((nb,)), pltpu.SemaphoreType.DMA((nb,))],
        compiler_params=_cparams(),
    )(*bufs)


def _rs_to_chips(pa, pb):
    bufs = (pa, pb)
    nb = len(bufs)

    def body(a_ref, b_ref, ao_ref, bo_ref, send_sems, recv_sems):
        x, y, c = _my_pos()
        srcs, outs = (a_ref, b_ref), (ao_ref, bo_ref)
        cps = []
        for j, chip in enumerate(_other_chips(x, y)):
            for b in range(nb):
                k = j * nb + b
                cp = pltpu.make_async_remote_copy(
                    src_ref=srcs[b].at[j], dst_ref=outs[b].at[j],
                    send_sem=send_sems.at[k], recv_sem=recv_sems.at[k], device_id=(*chip, c), device_id_type=MESH)
                cp.start()
                cps.append(cp)
        for cp in cps:
            cp.wait()

    any_spec = pl.BlockSpec(memory_space=pl.ANY)
    return _pcall(
        body, name="rs_to_chips",
        out_shape=[jax.ShapeDtypeStruct(b.shape, b.dtype) for b in bufs],
        in_specs=[any_spec] * nb, out_specs=[any_spec] * nb,
        scratch_shapes=[pltpu.SemaphoreType.DMA((3 * nb,)), pltpu.SemaphoreType.DMA((3 * nb,))],
        compiler_params=_cparams(),
    )(*bufs)


def _rs_join_halves(fa, fb):
    bufs = (fa, fb)
    nb = len(bufs)

    def body(a_ref, b_ref, ao_ref, bo_ref, send_sems, recv_sems):
        x, y, c = _my_pos()
        srcs, outs = (a_ref, b_ref), (ao_ref, bo_ref)
        cps = []
        for b in range(nb):
            rows = srcs[b].shape[0] // 2
            cp = pltpu.make_async_remote_copy(
                src_ref=srcs[b].at[pl.ds(c * rows, rows)], dst_ref=outs[b].at[pl.ds(c * rows, rows)],
                send_sem=send_sems.at[b], recv_sem=recv_sems.at[b], device_id=(x, y, 1 - c), device_id_type=MESH)
            cp.start()
            cps.append(cp)
        for b in range(nb):
            rows = srcs[b].shape[0] // 2
            theirs = outs[b].at[pl.ds((1 - c) * rows, rows)]
            pltpu.make_async_remote_copy(
                src_ref=theirs, dst_ref=theirs, send_sem=send_sems.at[b], recv_sem=recv_sems.at[b],
                device_id=(x, y, 1 - c), device_id_type=MESH).wait_recv()
        for cp in cps:
            cp.wait_send()

    any_spec = pl.BlockSpec(memory_space=pl.ANY)
    return _pcall(
        body, name="rs_join_halves",
        out_shape=[jax.ShapeDtypeStruct(b.shape, b.dtype) for b in bufs],
        in_specs=[any_spec] * nb, out_specs=[any_spec] * nb, input_output_aliases={0: 0, 1: 1},
        scratch_shapes=[pltpu.SemaphoreType.DMA((nb,)), pltpu.SemaphoreType.DMA((nb,))],
        compiler_params=_cparams(),
    )(*bufs)


def _add_halves(g, r, idx, tr, name):
    _, rows, w = r.shape
    nt = rows // tr

    def body(i_ref, g_ref, r_ref, o_ref):
        o_ref[...] = (g_ref[...] + r_ref[...]).astype(BF16)

    grid_spec = pltpu.PrefetchScalarGridSpec(
        num_scalar_prefetch=1, grid=(3, nt),
        in_specs=[pl.BlockSpec((None, tr, w), lambda j, i, ix: (ix[1 + j], ix[0] * nt + i, 0)),
                  pl.BlockSpec((None, tr, w), lambda j, i, ix: (ix[1 + j], i, 0))],
        out_specs=pl.BlockSpec((None, tr, w), lambda j, i, ix: (j, i, 0)))
    return _pcall(body, name=name, grid_spec=grid_spec,
                  out_shape=jax.ShapeDtypeStruct((3, rows, w), BF16), compiler_params=_seq(2))(idx, g, r)


def _add_chips(g, r, r3, idx, tr, name):
    _, rows, w = r.shape
    nt = rows // tr

    def body(i_ref, g_ref, r_ref, a_ref, b_ref, c_ref, o_ref):
        own = g_ref[...] + r_ref[...]
        o_ref[...] = ((own + a_ref[...].astype(F32)) + b_ref[...].astype(F32)) + c_ref[...].astype(F32)

    def other(j):
        return pl.BlockSpec((None, tr, w), lambda i, ix: (j, i, 0))

    grid_spec = pltpu.PrefetchScalarGridSpec(
        num_scalar_prefetch=1, grid=(nt,),
        in_specs=[pl.BlockSpec((None, tr, w), lambda i, ix: (ix[0], ix[1] * nt + i, 0)),
                  pl.BlockSpec((None, tr, w), lambda i, ix: (ix[0], i, 0)), other(0), other(1), other(2)],
        out_specs=pl.BlockSpec((tr, w), lambda i, ix: (ix[1] * nt + i, 0)))
    return _pcall(body, name=name, grid_spec=grid_spec,
                  out_shape=jax.ShapeDtypeStruct((2 * rows, w), F32), compiler_params=_seq(1))(idx, g, r, r3, r3, r3)


def _ada_fwd(c_all, w_sh, b_sh):
    def body(c_ref, w_ref, b_ref, o_ref):
        cv = c_ref[...]
        ca = cv * _sig(cv)
        o_ref[...] = jnp.dot(ca, w_ref[...], preferred_element_type=F32, precision=lax.Precision.HIGHEST) + b_ref[...]

    return _pcall(body, name="ada_fwd", out_shape=jax.ShapeDtypeStruct((c_all.shape[0], w_sh.shape[1]), F32),
                  compiler_params=_cparams())(c_all, w_sh, b_sh)


def _ada_bwd(c_all_t, dmod_sh):
    def body(c_ref, d_ref, o_ref):
        cv = c_ref[...]
        ca = cv * _sig(cv)
        o_ref[...] = jnp.dot(ca, d_ref[...], preferred_element_type=F32, precision=lax.Precision.HIGHEST)

    return _pcall(body, name="ada_bwd", out_shape=jax.ShapeDtypeStruct((c_all_t.shape[0], dmod_sh.shape[1]), F32),
                  compiler_params=_cparams())(c_all_t, dmod_sh)


def _conv31_causal(ext_ref, sh_ref, w_ref, bias, out_ref, tm, d, rc):
    off = HALO_A - (CONV_A - 1)
    n = tm + HALO_A - SUBLANES
    for s in range(SUBLANES):
        if s == 0:
            src = ext_ref
        else:
            sh_ref[0:n, :] = ext_ref[s:s + n, :]
            src = sh_ref
        taps = [k for k in range(CONV_A) if (off + k) % SUBLANES == s]
        for r0 in range(0, tm, rc):
            acc = jnp.broadcast_to(bias, (rc, d)) if s == 0 else out_ref[r0:r0 + rc, :]
            for k in taps:
                a = r0 + (off + k) // SUBLANES * SUBLANES
                acc = acc + w_ref[k:k + 1, :] * src[a:a + rc, :]
            out_ref[r0:r0 + rc, :] = acc


def _conv31_adjoint(dp_ref, sh_ref, ext_ref, w_ref, dx_ref, dw_ref, tm, d, rc):
    off = HALO_A - (CONV_A - 1)
    lead = SUBLANES + CONV_A - 1
    n = tm + HALO_A
    row = lax.broadcasted_iota(jnp.int32, (SUBLANES, d), 0)
    for s in range(SUBLANES):
        if s == 0:
            src = dp_ref
        else:
            sh_ref[0:n, :] = dp_ref[s:s + n, :]
            src = sh_ref
        taps = [k for k in range(CONV_A) if (lead - k) % SUBLANES == s]
        for r0 in range(0, tm, rc):
            acc = jnp.zeros((rc, d), F32) if s == 0 else dx_ref[r0:r0 + rc, :]
            for k in taps:
                a = r0 + (lead - k) // SUBLANES * SUBLANES
                acc = acc + w_ref[k:k + 1, :] * src[a:a + rc, :]
            dx_ref[r0:r0 + rc, :] = acc
        for k in range(CONV_A):
            if (-(off + k)) % SUBLANES != s:
                continue
            e = s + off + k - SUBLANES
            tot = _rsum(src[0:tm, :] * ext_ref[e:e + tm, :])
            tail = src[tm:tm + SUBLANES, :] * ext_ref[tm + e:tm + e + SUBLANES, :]
            tot = tot + _rsum(jnp.where(row < SUBLANES - s, tail, 0.0))
            dw_ref[k:k + 1, :] += tot


def _conv3_causal(ext_ref, w_ref, out_ref, tm, d, rc):
    off = HALO_B - (CONV_B - 1)
    for r0 in range(0, tm, rc):
        acc = jnp.zeros((rc, d), F32)
        for k in range(CONV_B):
            acc = acc + w_ref[k:k + 1, :] * ext_ref[r0 + off + k:r0 + off + k + rc, :]
        out_ref[r0:r0 + rc, :] = acc


def _conv3_anticausal(ext_ref, w_ref, out_ref, tm, d, rc):
    for r0 in range(0, tm, rc):
        acc = jnp.zeros((rc, d), F32)
        for k in range(CONV_B):
            o = CONV_B - 1 - k
            acc = acc + w_ref[k:k + 1, :] * ext_ref[r0 + o:r0 + o + rc, :]
        out_ref[r0:r0 + rc, :] = acc


def _fwd_in(x, vecs, wa, b_in, tm):
    s, d = x.shape
    nq, _, nw = wa.shape

    def body(x_ref, v_ref, w_ref, b_ref, h_ref, z_ref):
        xh, _ = _ln(x_ref[...])
        h = (xh * (1.0 + v_ref[V_SCALE1:V_SCALE1 + 1, :]) + v_ref[V_SHIFT1:V_SHIFT1 + 1, :]).astype(BF16)
        h_ref[...] = h
        for q in range(nq):
            z_ref[:, q * nw:(q + 1) * nw] = _dot(h, w_ref[q]) + b_ref[:, q * nw:(q + 1) * nw]

    return _pcall(
        body, name="fwd_in", grid=(s // tm,),
        in_specs=[_rows(tm, d), _full(vecs.shape), _full(wa.shape, single=True), _full(b_in.shape)],
        out_specs=[_rows(tm, d), _rows(tm, nq * nw)],
        out_shape=[jax.ShapeDtypeStruct((s, d), BF16), jax.ShapeDtypeStruct((s, nq * nw), F32)],
        compiler_params=_seq(1),
    )(x, vecs, wa, b_in)


def _fwd_mix(z, vecs, caw, cbw, wb, tm, rc):
    s = z.shape[0]
    d = vecs.shape[1]
    nq = wb.shape[0]
    kq = d // nq
    base = 2 * d // kq

    def body(z_ref, v_ref, caw_ref, cbw_ref, wao_ref, wbo_ref, wo_ref,
             u1_ref, ya_ref, yb_ref, o1_ref, u3_ref, vv_ref, mg_ref, ext_ref, sh_ref, pext_ref, q_ref):
        @pl.when(pl.program_id(0) == 0)
        def _():
            ext_ref[0:HALO_A, :] = jnp.zeros((HALO_A, d), F32)
            pext_ref[0:HALO_B, :] = jnp.zeros((HALO_B, d), F32)

        ext_ref[HALO_A:HALO_A + tm, :] = z_ref[:, 0:d] * _sig(z_ref[:, d:2 * d])
        _conv31_causal(ext_ref, sh_ref, caw_ref, v_ref[V_CAB:V_CAB + 1, :], u1_ref, tm, d, rc)
        ext_ref[0:HALO_A, :] = ext_ref[tm:tm + HALO_A, :]
        xa, _ = _ln(u1_ref[...])
        u2 = xa * v_ref[V_LNAG:V_LNAG + 1, :] + v_ref[V_LNAB:V_LNAB + 1, :]
        u3 = (u2 * _sig(u2)).astype(BF16)
        u3_ref[...] = u3
        ya = jnp.broadcast_to(v_ref[V_BAO:V_BAO + 1, :], (tm, d))
        for q in range(nq):
            ya = ya + _dot(u3[:, q * kq:(q + 1) * kq], wao_ref[q])
        ya_ref[...] = ya

        pext_ref[HALO_B:HALO_B + tm, :] = z_ref[:, 3 * d:4 * d] * z_ref[:, 4 * d:5 * d]
        _conv3_causal(pext_ref, cbw_ref, q_ref, tm, d, rc)
        pext_ref[0:HALO_B, :] = pext_ref[tm:tm + HALO_B, :]
        vv = (z_ref[:, 2 * d:3 * d] * q_ref[...]).astype(BF16)
        vv_ref[...] = vv
        yb = jnp.zeros((tm, d), F32)
        for q in range(nq):
            yb = yb + _dot(vv[:, q * kq:(q + 1) * kq], wbo_ref[q])
        yb_ref[...] = yb

        mg = (_sig(z_ref[:, 5 * d:6 * d]) * ya + _sig(z_ref[:, 6 * d:7 * d]) * yb).astype(BF16)
        mg_ref[...] = mg
        o1 = jnp.broadcast_to(v_ref[V_BO:V_BO + 1, :], (tm, d))
        for q in range(nq):
            o1 = o1 + _dot(mg[:, q * kq:(q + 1) * kq], wo_ref[q])
        o1_ref[...] = o1

    def wspec(j):
        return pl.BlockSpec((nq, kq, d), lambda i: (0, base + j, 0), pipeline_mode=pl.Buffered(1))

    f32o = jax.ShapeDtypeStruct((s, d), F32)
    b16o = jax.ShapeDtypeStruct((s, d), BF16)
    return _pcall(
        body, name="fwd_mix", grid=(s // tm,),
        in_specs=[_rows(tm, 7 * d), _full(vecs.shape), _full(caw.shape), _full(cbw.shape), wspec(0), wspec(1), wspec(2)],
        out_specs=[_rows(tm, d)] * 7,
        out_shape=[f32o, f32o, f32o, f32o, b16o, b16o, b16o],
        scratch_shapes=[pltpu.VMEM((HALO_A + tm, d), F32), pltpu.VMEM((HALO_A + tm, d), F32),
                        pltpu.VMEM((HALO_B + tm, d), F32), pltpu.VMEM((tm, d), F32)],
        compiler_params=_seq(1),
    )(z, vecs, caw, cbw, wb, wb, wb)


def _mlp_fwd_bwd(x, out1, tgt, vecs, b_up, wb, tm):
    s, d = x.shape
    nq = wb.shape[0]
    dff = nq * d

    def body(x_ref, o1_ref, t_ref, v_ref, bup_ref, wup_ref, wdn_ref,
             h2_ref, f_ref, df0_ref, do2_ref, do1_ref, dxp_ref, acc_ref, dbup_ref, f0_ref):
        @pl.when(pl.program_id(0) == 0)
        def _():
            acc_ref[...] = jnp.zeros(acc_ref.shape, F32)
            dbup_ref[...] = jnp.zeros(dbup_ref.shape, F32)

        def vec(r):
            return v_ref[r:r + 1, :]

        def accum(r, val):
            acc_ref[r:r + 1, :] += _rsum(val)

        out1v = o1_ref[...]
        r1 = ALPHA * x_ref[...] + (1.0 + vec(V_GATE1)) * out1v
        xh1, rstd1 = _ln(r1)
        x1 = xh1 * vec(V_LN1G) + vec(V_LN1B)
        xn1, rstdn = _ln(x1)
        h2 = (xn1 * (1.0 + vec(V_SCALE2)) + vec(V_SHIFT2)).astype(BF16)
        h2_ref[...] = h2
        out2 = jnp.broadcast_to(vec(V_BDN), (tm, d))
        for q in range(nq):
            f0 = _dot(h2, wup_ref[q]) + bup_ref[:, q * d:(q + 1) * d]
            rl = jnp.maximum(f0, 0.0)
            f0_ref[:, q * d:(q + 1) * d] = rl
            fb = (rl * rl).astype(BF16)
            f_ref[:, q * d:(q + 1) * d] = fb
            out2 = out2 + _dot(fb, wdn_ref[q])
        r2 = ALPHA * x1 + (1.0 + vec(V_GATE2)) * out2
        xh2, rstd2 = _ln(r2)
        yv = xh2 * vec(V_LN2G) + vec(V_LN2B)
        err = yv - t_ref[...]
        accum(M_LOSS, err * err)
        dy = err * (1.0 / d)
        accum(M_LN2G, dy * xh2)
        accum(M_LN2B, dy)
        dr2 = _ln_bwd(dy * vec(V_LN2G), xh2, rstd2)
        accum(M_GATE2, dr2 * out2)
        dout2 = (1.0 + vec(V_GATE2)) * dr2
        accum(M_BDN, dout2)
        do2b = dout2.astype(BF16)
        do2_ref[...] = do2b
        dh2 = jnp.zeros((tm, d), F32)
        for q in range(nq):
            df0 = _dot_nt(do2b, wdn_ref[q]) * (2.0 * f0_ref[:, q * d:(q + 1) * d])
            dbup_ref[q:q + 1, :] += _rsum(df0)
            df0b = df0.astype(BF16)
            df0_ref[:, q * d:(q + 1) * d] = df0b
            dh2 = dh2 + _dot_nt(df0b, wup_ref[q])
        accum(M_SHIFT2, dh2)
        accum(M_SCALE2, dh2 * xn1)
        dx1 = ALPHA * dr2 + _ln_bwd(dh2 * (1.0 + vec(V_SCALE2)), xn1, rstdn)
        accum(M_LN1G, dx1 * xh1)
        accum(M_LN1B, dx1)
        dr1 = _ln_bwd(dx1 * vec(V_LN1G), xh1, rstd1)
        accum(M_GATE1, dr1 * out1v)
        dout1 = (1.0 + vec(V_GATE1)) * dr1
        accum(M_BO, dout1)
        do1_ref[...] = dout1.astype(BF16)
        dxp_ref[...] = ALPHA * dr1

    def wspec(j):
        return pl.BlockSpec((nq, d, d), lambda i: (0, j, 0), pipeline_mode=pl.Buffered(1))

    b16 = lambda w: jax.ShapeDtypeStruct((s, w), BF16)
    return _pcall(
        body, name="mlp_fwd_bwd", grid=(s // tm,),
        in_specs=[_rows(tm, d), _rows(tm, d), _rows(tm, d), _full(vecs.shape), _full(b_up.shape), wspec(0), wspec(1)],
        out_specs=[_rows(tm, d), _rows(tm, dff), _rows(tm, dff), _rows(tm, d), _rows(tm, d), _rows(tm, d),
                   _full((16, d)), _full((SUBLANES, d))],
        out_shape=[b16(d), b16(dff), b16(dff), b16(d), b16(d), jax.ShapeDtypeStruct((s, d), F32),
                   jax.ShapeDtypeStruct((16, d), F32), jax.ShapeDtypeStruct((SUBLANES, d), F32)],
        scratch_shapes=[pltpu.VMEM((tm, dff), F32)],
        compiler_params=_seq(1),
    )(x, out1, tgt, vecs, b_up, wb, wb)


def _mix_bwd(dout1, z, u1, ya, yb, vecs, caw, cbw, wb, tm, rc):
    s = z.shape[0]
    d = vecs.shape[1]
    nq = wb.shape[0]
    kq = d // nq
    base = 2 * d // kq
    nt = s // tm
    hb = tm // HALO_A

    def body(do1_ref, z_ref, zh_ref, u1_ref, ya_ref, yb_ref, v_ref, caw_ref, cbw_ref, wao_ref, wbo_ref, wo_ref,
             dz_ref, dya_ref, dyb_ref, acc_ref, dcaw_ref, dcbw_ref, dbin_ref,
             ext_ref, du1p_ref, sh_ref, pext_ref, dqe_ref, tmp_ref):
        i = pl.program_id(0)

        @pl.when(i == 0)
        def _():
            acc_ref[...] = jnp.zeros(acc_ref.shape, F32)
            dcaw_ref[...] = jnp.zeros(dcaw_ref.shape, F32)
            dcbw_ref[...] = jnp.zeros(dcbw_ref.shape, F32)
            dbin_ref[...] = jnp.zeros(dbin_ref.shape, F32)
            du1p_ref[0:SUBLANES, :] = jnp.zeros((SUBLANES, d), F32)
            du1p_ref[SUBLANES + tm:SUBLANES + tm + HALO_A, :] = jnp.zeros((HALO_A, d), F32)
            dqe_ref[tm:tm + HALO_B, :] = jnp.zeros((HALO_B, d), F32)

        def vec(r):
            return v_ref[r:r + 1, :]

        def accum(r, val):
            acc_ref[r:r + 1, :] += _rsum(val)

        def put_dz(j, val):
            dbin_ref[j:j + 1, :] += _rsum(val)
            dz_ref[:, j * d:(j + 1) * d] = val.astype(BF16)

        has_history = i < nt - 1

        do1 = do1_ref[...]
        dmg = jnp.concatenate([_dot_nt(do1, wo_ref[q]) for q in range(nq)], axis=1)
        sga = _sig(z_ref[:, 5 * d:6 * d])
        sgb = _sig(z_ref[:, 6 * d:7 * d])
        dya = dmg * sga
        dyb = dmg * sgb
        accum(X_BAO, dya)
        put_dz(5, dya * ya_ref[...] * (1.0 - sga))
        put_dz(6, dyb * yb_ref[...] * (1.0 - sgb))
        dyab = dya.astype(BF16)
        dybb = dyb.astype(BF16)
        dya_ref[...] = dyab
        dyb_ref[...] = dybb

        du3 = jnp.concatenate([_dot_nt(dyab, wao_ref[q]) for q in range(nq)], axis=1)
        xa, rstda = _ln(u1_ref[...])
        u2 = xa * vec(V_LNAG) + vec(V_LNAB)
        s2 = _sig(u2)
        du2 = du3 * (s2 * (1.0 + u2 * (1.0 - s2)))
        accum(X_LNAG, du2 * xa)
        accum(X_LNAB, du2)
        du1 = _ln_bwd(du2 * vec(V_LNAG), xa, rstda)
        accum(X_CAB, du1)
        du1p_ref[SUBLANES:SUBLANES + tm, :] = du1
        sg = _sig(z_ref[:, d:2 * d])
        aval = z_ref[:, 0:d]
        ext_ref[HALO_A:HALO_A + tm, :] = aval * sg
        ext_ref[0:HALO_A, :] = jnp.where(has_history, zh_ref[:, 0:d] * _sig(zh_ref[:, d:2 * d]), 0.0)
        _conv31_adjoint(du1p_ref, sh_ref, ext_ref, caw_ref, tmp_ref, dcaw_ref, tm, d, rc)
        du1p_ref[SUBLANES + tm:SUBLANES + tm + HALO_A, :] = du1p_ref[SUBLANES:SUBLANES + HALO_A, :]
        du0 = tmp_ref[...]
        put_dz(0, du0 * sg)
        put_dz(1, du0 * aval * sg * (1.0 - sg))

        dv = jnp.concatenate([_dot_nt(dybb, wbo_ref[q]) for q in range(nq)], axis=1)
        bgc = z_ref[:, 3 * d:4 * d]
        bx = z_ref[:, 4 * d:5 * d]
        pext_ref[HALO_B:HALO_B + tm, :] = bgc * bx
        pext_ref[0:HALO_B, :] = jnp.where(
            has_history, zh_ref[HALO_A - HALO_B:HALO_A, 3 * d:4 * d] * zh_ref[HALO_A - HALO_B:HALO_A, 4 * d:5 * d], 0.0)
        _conv3_causal(pext_ref, cbw_ref, tmp_ref, tm, d, rc)
        put_dz(2, dv * tmp_ref[...])
        dq = dv * z_ref[:, 2 * d:3 * d]
        dqe_ref[0:tm, :] = dq
        offb = HALO_B - (CONV_B - 1)
        for k in range(CONV_B):
            dcbw_ref[k:k + 1, :] += _rsum(dq * pext_ref[offb + k:offb + k + tm, :])
        _conv3_anticausal(dqe_ref, cbw_ref, tmp_ref, tm, d, rc)
        dqe_ref[tm:tm + HALO_B, :] = dqe_ref[0:HALO_B, :]
        dp = tmp_ref[...]
        put_dz(3, dp * bx)
        put_dz(4, dp * bgc)

    def rev(width):
        return pl.BlockSpec((tm, width), lambda i: (nt - 1 - i, 0))

    def wspec(j):
        return pl.BlockSpec((nq, kq, d), lambda i: (0, base + j, 0), pipeline_mode=pl.Buffered(1))

    halo = pl.BlockSpec((HALO_A, 7 * d), lambda i: (jnp.maximum((nt - 1 - i) * hb - 1, 0), 0))
    b16 = jax.ShapeDtypeStruct((s, d), BF16)
    return _pcall(
        body, name="mix_bwd", grid=(nt,),
        in_specs=[rev(d), rev(7 * d), halo, rev(d), rev(d), rev(d), _full(vecs.shape), _full(caw.shape), _full(cbw.shape),
                  wspec(0), wspec(1), wspec(2)],
        out_specs=[rev(7 * d), rev(d), rev(d), _full((SUBLANES, d)), _full((HALO_A, d)), _full((HALO_B, d)),
                   _full((SUBLANES, d))],
        out_shape=[jax.ShapeDtypeStruct((s, 7 * d), BF16), b16, b16, jax.ShapeDtypeStruct((SUBLANES, d), F32),
                   jax.ShapeDtypeStruct((HALO_A, d), F32), jax.ShapeDtypeStruct((HALO_B, d), F32),
                   jax.ShapeDtypeStruct((SUBLANES, d), F32)],
        scratch_shapes=[pltpu.VMEM((HALO_A + tm, d), F32), pltpu.VMEM((SUBLANES + tm + HALO_A, d), F32),
                        pltpu.VMEM((tm + HALO_A, d), F32), pltpu.VMEM((HALO_B + tm, d), F32),
                        pltpu.VMEM((tm + HALO_B, d), F32), pltpu.VMEM((tm, d), F32)],
        compiler_params=_seq(1),
    )(dout1, z, z, u1, ya, yb, vecs, caw, cbw, wb, wb, wb)


def _in_bwd(dz, x, dxp, vecs, wa, tm):
    s, d = x.shape
    nq, _, nw = wa.shape

    def body(dz_ref, x_ref, dxp_ref, v_ref, w_ref, gx_ref, acc_ref):
        @pl.when(pl.program_id(0) == 0)
        def _():
            acc_ref[...] = jnp.zeros(acc_ref.shape, F32)

        dh1 = jnp.zeros((tm, d), F32)
        for q in range(nq):
            dh1 = dh1 + _dot_nt(dz_ref[:, q * nw:(q + 1) * nw], w_ref[q])
        xh, rstd = _ln(x_ref[...])
        acc_ref[I_SHIFT1:I_SHIFT1 + 1, :] += _rsum(dh1)
        acc_ref[I_SCALE1:I_SCALE1 + 1, :] += _rsum(dh1 * xh)
        gx_ref[...] = dxp_ref[...] + _ln_bwd(dh1 * (1.0 + v_ref[V_SCALE1:V_SCALE1 + 1, :]), xh, rstd)

    return _pcall(
        body, name="in_bwd", grid=(s // tm,),
        in_specs=[_rows(tm, nq * nw), _rows(tm, d), _rows(tm, d), _full(vecs.shape), _full(wa.shape, single=True)],
        out_specs=[_rows(tm, d), _full((SUBLANES, d))],
        out_shape=[jax.ShapeDtypeStruct((s, d), F32), jax.ShapeDtypeStruct((SUBLANES, d), F32)],
        compiler_params=_seq(1),
    )(dz, x, dxp, vecs, wa)


def _dw(a, b, nblk, split_a, ts, name):
    s = a.shape[0]
    ka = a.shape[1] // nblk if split_a else a.shape[1]
    nb = b.shape[1] if split_a else b.shape[1] // nblk

    def body(a_ref, b_ref, o_ref):
        @pl.when(pl.program_id(1) == 0)
        def _():
            o_ref[...] = jnp.zeros(o_ref.shape, F32)

        o_ref[...] += _dot_tn(a_ref[...], b_ref[...])

    a_spec = pl.BlockSpec((ts, ka), (lambda q, i: (i, q)) if split_a else (lambda q, i: (i, 0)))
    b_spec = pl.BlockSpec((ts, nb), (lambda q, i: (i, 0)) if split_a else (lambda q, i: (i, q)))
    return _pcall(
        body, name=name, grid=(nblk, s // ts),
        in_specs=[a_spec, b_spec], out_specs=pl.BlockSpec((None, ka, nb), lambda q, i: (q, 0, 0)),
        out_shape=jax.ShapeDtypeStruct((nblk, ka, nb), F32),
        compiler_params=_seq(2),
    )(a, b)


def _adam_math(w, g, m, v):
    m2 = ADAM_B1 * m + (1.0 - ADAM_B1) * g
    v2 = ADAM_B2 * v + (1.0 - ADAM_B2) * (g * g)
    m_hat = m2 / (1.0 - ADAM_B1 ** ADAM_STEP)
    v_hat = v2 / (1.0 - ADAM_B2 ** ADAM_STEP)
    delta = -ADAM_LR * (m_hat / (jnp.sqrt(v_hat) + ADAM_EPS) + ADAM_WD * w)
    return delta, m2, v2


def _adam(w, g, m, v, g_row0, tr, name):
    r, c = w.shape
    blk0 = g_row0 // tr

    def body(w_ref, g_ref, m_ref, v_ref, go_ref, d_ref, mo_ref, vo_ref):
        gv = g_ref[...]
        go_ref[...] = gv
        d_ref[...], mo_ref[...], vo_ref[...] = _adam_math(w_ref[...], gv, m_ref[...], v_ref[...])

    spec = _rows(tr, c)
    g_spec = pl.BlockSpec((tr, c), lambda i: (blk0 + i, 0))
    o = jax.ShapeDtypeStruct((r, c), F32)
    return _pcall(body, name=name, grid=(r // tr,), in_specs=[spec, g_spec, spec, spec], out_specs=[spec] * 4,
                  out_shape=[o, o, o, o], compiler_params=_seq(1))(w, g, m, v)


def _small_update(gathered, q_idx, small_w, small_m, small_v, conv_w, conv_m, conv_v):
    d = gathered.shape[2]
    ns = len(_SMALL)
    cw = conv_w[0].shape[1]
    conv_rows = ((T_CAW, CONV_A), (T_CBW, CONV_B))

    def body(q_ref, g_ref, *refs):
        ins, outs = refs[:3 * (ns + 2)], refs[3 * (ns + 2):]
        tot_ref, loss_ref = outs[0], outs[1]
        outs = outs[2:]
        tot = g_ref[0]
        for dev in range(1, N_DEV):
            tot = tot + g_ref[dev]
        tot_ref[...] = tot
        loss_ref[...] = (0.5 / d) * jnp.sum(tot_ref[T_LOSS:T_LOSS + 1, :], axis=1, keepdims=True)
        for p, (_, rows) in enumerate(_SMALL):
            w_ref, m_ref, v_ref = ins[p], ins[ns + 2 + p], ins[2 * (ns + 2) + p]
            go, do, mo, vo = outs[4 * p:4 * p + 4]
            for j, row in enumerate(rows):
                sl = slice(j * d, (j + 1) * d)
                gv = tot_ref[row:row + 1, :]
                go[:, sl] = gv
                do[:, sl], mo[:, sl], vo[:, sl] = _adam_math(w_ref[:, sl], gv, m_ref[:, sl], v_ref[:, sl])
        for p, (row, taps) in enumerate(conv_rows):
            w_ref, m_ref, v_ref = ins[ns + p], ins[ns + 2 + ns + p], ins[2 * (ns + 2) + ns + p]
            go, do, mo, vo = outs[4 * (ns + p):4 * (ns + p) + 4]
            gv = tot_ref[row:row + taps, 0:cw]
            for qq in range(1, N_CHIPS):
                gv = jnp.where(q_ref[0] == qq, tot_ref[row:row + taps, qq * cw:(qq + 1) * cw], gv)
            go[...] = gv
            do[...], mo[...], vo[...] = _adam_math(w_ref[...], gv, m_ref[...], v_ref[...])

    params = list(small_w) + list(conv_w) + list(small_m) + list(conv_m) + list(small_v) + list(conv_v)
    out_shape = [jax.ShapeDtypeStruct((T_ROWS, d), F32), jax.ShapeDtypeStruct((1, 1), F32)]
    for w in list(small_w) + list(conv_w):
        out_shape += [jax.ShapeDtypeStruct(w.shape, F32)] * 4
    vm = pl.BlockSpec(memory_space=pltpu.VMEM)
    return _pcall(
        body, name="small_update", out_shape=out_shape,
        in_specs=[pl.BlockSpec(memory_space=pltpu.SMEM), vm] + [vm] * len(params),
        out_specs=[vm] * len(out_shape), compiler_params=_cparams(),
    )(q_idx, gathered, *params)


def kernel(x, c, w_ada, b_ada, w_in, b_in, conv_a_w, conv_a_b, ln_a_g, ln_a_b, w_a_out, b_a_out, conv_b_w, w_b_out, w_o, b_o, ln1_g, ln1_b, w_up, b_up, w_down, b_down, ln2_g, ln2_b, loss_target, m_w_ada, m_b_ada, m_w_in, m_b_in, m_conv_a_w, m_conv_a_b, m_ln_a_g, m_ln_a_b, m_w_a_out, m_b_a_out, m_conv_b_w, m_w_b_out, m_w_o, m_b_o, m_ln1_g, m_ln1_b, m_w_up, m_b_up, m_w_down, m_b_down, m_ln2_g, m_ln2_b, v_w_ada, v_b_ada, v_w_in, v_b_in, v_conv_a_w, v_conv_a_b, v_ln_a_g, v_ln_a_b, v_w_a_out, v_b_a_out, v_conv_b_w, v_w_b_out, v_w_o, v_b_o, v_ln1_g, v_ln1_b, v_w_up, v_b_up, v_w_down, v_b_down, v_ln2_g, v_ln2_b):
    given = dict(locals())
    s, d = x.shape[1], x.shape[2]
    xi, yi, ci = _my_pos()
    q = 2 * xi + yi
    me = 4 * xi + 2 * yi + ci
    i32 = jnp.int32
    q_arr = jnp.reshape(q, (1,)).astype(i32)
    others = [2 * ox + oy for ox, oy in _other_chips(xi, yi)]
    halves_idx = jnp.stack([ci] + others).astype(i32)
    chips_idx = jnp.stack([q, ci]).astype(i32)
    kq = d // N_CHIPS
    tm = min(256, s)
    rc = min(32, tm)

    def sq(a):
        return a.reshape(a.shape[1:])

    x2, tgt = sq(x), sq(loss_target)

    n_ada = w_ada.shape[2]
    pre = jnp.concatenate([c, sq(conv_a_w).reshape(1, -1), sq(conv_b_w).reshape(1, -1)], axis=1)
    pre_all = _all_gather_small(jnp.broadcast_to(pre, (SUBLANES, pre.shape[1])), "gather_c_conv")[:, 0, :]
    c_all = pre_all[:, :d]
    taps = pre_all[0::2, d:]
    caw = jnp.concatenate([taps[p, :CONV_A * kq].reshape(CONV_A, kq) for p in range(N_CHIPS)], axis=1)
    cbw = jnp.concatenate([taps[p, CONV_A * kq:].reshape(CONV_B, kq) for p in range(N_CHIPS)], axis=1)
    caw = jnp.pad(caw, ((0, HALO_A - CONV_A), (0, 0)))
    cbw = jnp.pad(cbw, ((0, HALO_B - CONV_B), (0, 0)))
    b_ada_sh = lax.dynamic_slice(b_ada, (0, q * n_ada), (1, n_ada))
    mod_part = _ada_fwd(c_all, sq(w_ada), b_ada_sh)
    mod_all = _all_gather_small(mod_part, "gather_mod")
    mod_rows = lax.dynamic_slice(mod_all, (0, me, 0), (N_DEV, 1, n_ada))[0::2, 0, :]
    mod = mod_rows.reshape(6, d)
    vecs = jnp.concatenate([mod, conv_a_b, ln_a_g, ln_a_b, b_a_out, b_o, ln1_g, ln1_b, b_down, ln2_g, ln2_b], axis=0)

    wa = _place_shard([sq(w_in)], q_arr, "place_w_in")
    wb = _place_shard([sq(w_up), sq(w_down), sq(w_a_out), sq(w_b_out), sq(w_o)], q_arr, "place_w_rest")
    wa, wb = _all_gather_weights(wa, wb)

    h1, z = _fwd_in(x2, vecs, wa, b_in, tm)
    u1, ya, yb, out1, u3, vv, mg = _fwd_mix(z, vecs, caw, cbw, wb, tm, rc)

    h2, fb, df0, do2, do1, dxp, macc, dbup = _mlp_fwd_bwd(x2, out1, tgt, vecs, b_up, wb, tm)
    dz, dya, dyb, xacc, dcaw, dcbw, dbin = _mix_bwd(do1, z, u1, ya, yb, vecs, caw, cbw, wb, tm, rc)
    gx, iacc = _in_bwd(dz, x2, dxp, vecs, wa, tm)

    ts = min(2048, s)
    ga = _dw(h1, dz, N_CHIPS, False, ts, "dw_in")
    gb = jnp.concatenate([
        _dw(h2, df0, N_CHIPS, False, ts, "dw_up"),
        _dw(fb, do2, N_CHIPS, True, ts, "dw_down"),
        _dw(u3, dya, N_CHIPS, True, ts, "dw_a_out"),
        _dw(vv, dyb, N_CHIPS, True, ts, "dw_b_out"),
        _dw(mg, do1, N_CHIPS, True, ts, "dw_o"),
    ], axis=1)

    ra, rb = _rs_to_sibling(ga, gb)
    tra, trb = ra.shape[1] // 4, rb.shape[1] // 4
    pa = _add_halves(ga, ra, halves_idx, tra, "rs_add_halves_in")
    pb = _add_halves(gb, rb, halves_idx, trb, "rs_add_halves_rest")
    r3a, r3b = _rs_to_chips(pa, pb)
    fa = _add_chips(ga, ra, r3a, chips_idx, tra, "rs_add_chips_in")
    fb_ = _add_chips(gb, rb, r3b, chips_idx, trb, "rs_add_chips_rest")
    g_in, g_b = _rs_join_halves(fa, fb_)

    table = jnp.concatenate([iacc, macc, dbin, dcaw, xacc, dcbw, dbup], axis=0)
    gathered = _all_gather_small(table, "gather_small_grads")

    names = [n for n, _ in _SMALL]
    res = _small_update(
        gathered, q_arr,
        [given[n] for n in names], [given["m_" + n] for n in names], [given["v_" + n] for n in names],
        [sq(conv_a_w), sq(conv_b_w)], [sq(m_conv_a_w), sq(m_conv_b_w)], [sq(v_conv_a_w), sq(v_conv_b_w)])
    loss = res[1].reshape(())
    upd = {}
    for p, n in enumerate(names + ["conv_a_w", "conv_b_w"]):
        upd[n] = res[2 + 4 * p:6 + 4 * p]

    dmod_all = jnp.stack([gathered[:, r, :] for r in _SMALL[0][1]], axis=1).reshape(N_DEV, 6 * d)
    dmod_sh = lax.dynamic_slice(dmod_all, (0, q * n_ada), (N_DEV, n_ada))
    g_ada = _ada_bwd(c_all.T, dmod_sh)
    upd["w_ada"] = _adam(sq(w_ada), g_ada, sq(m_w_ada), sq(v_w_ada), 0, min(256, d), "adam_w_ada")

    upd["w_in"] = _adam(sq(w_in), g_in, sq(m_w_in), sq(v_w_in), 0, min(256, d), "adam_w_in")
    r0 = 0
    for n in ("w_up", "w_down", "w_a_out", "w_b_out", "w_o"):
        w = sq(given[n])
        upd[n] = _adam(w, g_b, sq(given["m_" + n]), sq(given["v_" + n]), r0, min(256, w.shape[0]), "adam_" + n)
        r0 += w.shape[0]

    order = ["w_ada", "b_ada", "w_in", "b_in", "conv_a_w", "conv_a_b", "ln_a_g", "ln_a_b", "w_a_out", "b_a_out", "conv_b_w",
             "w_b_out", "w_o", "b_o", "ln1_g", "ln1_b", "w_up", "b_up", "w_down", "b_down", "ln2_g", "ln2_b"]
    outs = [loss, gx.reshape(x.shape)]
    for k in range(4):
        outs += [upd[n][k].reshape(given[n].shape) for n in order]
    return tuple(outs)
```

```python
import jax
import jax.numpy as jnp
from jax import lax
from jax.experimental import pallas as pl
from jax.experimental.pallas import tpu as pltpu

F32 = jnp.float32
BF16 = jnp.bfloat16
MESH = pl.DeviceIdType.MESH

LN_EPS = 1e-5
DEPTH = 1
ALPHA = (2.0 * DEPTH) ** 0.25
CONV_A = 31
CONV_B = 3
SUBLANES = 8
HALO_A = 32
HALO_B = 8
N_CHIPS = 4
N_DEV = 8
ADAM_LR = 0.001
ADAM_B1 = 0.9
ADAM_B2 = 0.999
ADAM_EPS = 1e-08
ADAM_WD = 0.01
ADAM_STEP = 10
VMEM_LIMIT = 56 * 1024 * 1024

V_SHIFT1, V_SCALE1, V_GATE1, V_SHIFT2, V_SCALE2, V_GATE2 = 0, 1, 2, 3, 4, 5
V_CAB, V_LNAG, V_LNAB, V_BAO, V_BO, V_LN1G, V_LN1B, V_BDN, V_LN2G, V_LN2B = 6, 7, 8, 9, 10, 11, 12, 13, 14, 15

M_LN2G, M_LN2B, M_GATE2, M_BDN, M_SHIFT2, M_SCALE2, M_LN1G, M_LN1B, M_GATE1, M_BO, M_LOSS = range(11)
X_BAO, X_LNAG, X_LNAB, X_CAB = range(4)
I_SHIFT1, I_SCALE1 = 0, 1

T_I, T_M, T_BIN, T_CAW, T_X, T_CBW, T_BUP, T_ROWS = 0, 8, 24, 32, 64, 72, 80, 88
T_LOSS = T_M + M_LOSS
_SMALL = (
    ("b_ada", (T_I + I_SHIFT1, T_I + I_SCALE1, T_M + M_GATE1, T_M + M_SHIFT2, T_M + M_SCALE2, T_M + M_GATE2)),
    ("b_in", tuple(T_BIN + j for j in range(7))),
    ("conv_a_b", (T_X + X_CAB,)), ("ln_a_g", (T_X + X_LNAG,)), ("ln_a_b", (T_X + X_LNAB,)), ("b_a_out", (T_X + X_BAO,)),
    ("b_o", (T_M + M_BO,)), ("ln1_g", (T_M + M_LN1G,)), ("ln1_b", (T_M + M_LN1B,)),
    ("b_up", tuple(T_BUP + j for j in range(4))),
    ("b_down", (T_M + M_BDN,)), ("ln2_g", (T_M + M_LN2G,)), ("ln2_b", (T_M + M_LN2B,)),
)


def _pcall(body, **kw):
    return pl.pallas_call(body, **kw)


def _cparams(**kw):
    return pltpu.CompilerParams(vmem_limit_bytes=VMEM_LIMIT, **kw)


def _seq(n):
    return _cparams(dimension_semantics=("arbitrary",) * n)


def _full(shape, single=False):
    nd = len(shape)
    if single:
        return pl.BlockSpec(shape, lambda *_: (0,) * nd, pipeline_mode=pl.Buffered(1))
    return pl.BlockSpec(shape, lambda *_: (0,) * nd)


def _rows(tm, width):
    return pl.BlockSpec((tm, width), lambda i: (i, 0))


def _sig(x):
    return jax.nn.sigmoid(x)


def _ln(x):
    mu = jnp.mean(x, axis=-1, keepdims=True)
    xc = x - mu
    var = jnp.mean(xc * xc, axis=-1, keepdims=True)
    rstd = lax.rsqrt(var + LN_EPS)
    return xc * rstd, rstd


def _ln_bwd(dxh, xh, rstd):
    m1 = jnp.mean(dxh, axis=-1, keepdims=True)
    m2 = jnp.mean(dxh * xh, axis=-1, keepdims=True)
    return rstd * (dxh - m1 - xh * m2)


def _rsum(v):
    return jnp.sum(v, axis=0, keepdims=True)


def _dot(a, b):
    return jnp.dot(a, b, preferred_element_type=F32)


def _dot_nt(a, b):
    return lax.dot_general(a, b, (((1,), (1,)), ((), ())), preferred_element_type=F32)


def _dot_tn(a, b):
    return lax.dot_general(a, b, (((0,), (0,)), ((), ())), preferred_element_type=F32)


def _my_pos():
    return lax.axis_index("x"), lax.axis_index("y"), lax.axis_index("c")


def _other_chips(x, y):
    return [(1 - x, y), (x, 1 - y), (1 - x, 1 - y)]


def _all_gather_small(v, name):
    r, c = v.shape

    def body(v_ref, out_ref, send_sems, recv_sems, local_sem):
        x, y, cc = _my_pos()
        me = 4 * x + 2 * y + cc
        mine = pltpu.make_async_copy(v_ref, out_ref.at[me], local_sem)
        mine.start()
        sends = []
        for rel in range(1, N_DEV):
            rx, ry, rc = (rel >> 2) & 1, (rel >> 1) & 1, rel & 1
            peer = (1 - x if rx else x, 1 - y if ry else y, 1 - cc if rc else cc)
            cp = pltpu.make_async_remote_copy(
                src_ref=v_ref, dst_ref=out_ref.at[me], send_sem=send_sems.at[rel - 1], recv_sem=recv_sems.at[rel - 1],
                device_id=peer, device_id_type=MESH)
            cp.start()
            sends.append(cp)
        for rel in range(1, N_DEV):
            rx, ry, rc = (rel >> 2) & 1, (rel >> 1) & 1, rel & 1
            peer = (1 - x if rx else x, 1 - y if ry else y, 1 - cc if rc else cc)
            slot = 4 * peer[0] + 2 * peer[1] + peer[2]
            pltpu.make_async_remote_copy(
                src_ref=v_ref, dst_ref=out_ref.at[slot], send_sem=send_sems.at[rel - 1], recv_sem=recv_sems.at[rel - 1],
                device_id=peer, device_id_type=MESH).wait_recv()
        for cp in sends:
            cp.wait_send()
        mine.wait()

    return _pcall(
        body, name=name,
        out_shape=jax.ShapeDtypeStruct((N_DEV, r, c), v.dtype),
        in_specs=[pl.BlockSpec(memory_space=pltpu.VMEM)],
        out_specs=pl.BlockSpec(memory_space=pltpu.VMEM),
        scratch_shapes=[pltpu.SemaphoreType.DMA((N_DEV - 1,)), pltpu.SemaphoreType.DMA((N_DEV - 1,)),
                        pltpu.SemaphoreType.DMA],
        compiler_params=_cparams(),
    )(v)


def _place_shard(parts, q_idx, name):
    rows = sum(p.shape[0] for p in parts)
    w = parts[0].shape[1]

    def body(q_ref, *refs):
        o_ref = refs[-1]
        r0 = 0
        for p_ref in refs[:-1]:
            n = p_ref.shape[0]
            o_ref[r0:r0 + n, :] = p_ref[...].astype(BF16)
            r0 += n

    grid_spec = pltpu.PrefetchScalarGridSpec(
        num_scalar_prefetch=1, grid=(1,),
        in_specs=[pl.BlockSpec(p.shape, lambda i, q: (0, 0)) for p in parts],
        out_specs=pl.BlockSpec((None, rows, w), lambda i, q: (q[0], 0, 0)))
    return _pcall(body, name=name, grid_spec=grid_spec, out_shape=jax.ShapeDtypeStruct((N_CHIPS, rows, w), BF16),
                  compiler_params=_seq(1))(q_idx, *parts)


def _all_gather_weights(wa, wb):
    bufs = (wa, wb)
    nb = len(bufs)

    def body(a_ref, b_ref, ao_ref, bo_ref, send_sems, recv_sems):
        x, y, c = _my_pos()
        q = 2 * x + y
        srcs, outs = (a_ref, b_ref), (ao_ref, bo_ref)
        sibling = (x, y, 1 - c)
        chips = _other_chips(x, y)

        def half(ref, slot, h):
            rows = ref.shape[1] // 2
            return ref.at[slot, pl.ds(h * rows, rows)]

        first = []
        for j, chip in enumerate(chips):
            for b in range(nb):
                k = j * nb + b
                cp = pltpu.make_async_remote_copy(
                    src_ref=half(srcs[b], q, c), dst_ref=half(outs[b], q, c),
                    send_sem=send_sems.at[k], recv_sem=recv_sems.at[k], device_id=(*chip, c), device_id_type=MESH)
                cp.start()
                first.append(cp)
        passed = []
        for j, chip in enumerate(chips):
            qj = 2 * chip[0] + chip[1]
            for b in range(nb):
                k = j * nb + b
                landed = half(outs[b], qj, c)
                pltpu.make_async_remote_copy(
                    src_ref=landed, dst_ref=landed, send_sem=send_sems.at[k], recv_sem=recv_sems.at[k],
                    device_id=(*chip, c), device_id_type=MESH).wait_recv()
                k2 = 3 * nb + k
                cp = pltpu.make_async_remote_copy(
                    src_ref=landed, dst_ref=landed, send_sem=send_sems.at[k2], recv_sem=recv_sems.at[k2],
                    device_id=sibling, device_id_type=MESH)
                cp.start()
                passed.append(cp)
        for j, chip in enumerate(chips):
            qj = 2 * chip[0] + chip[1]
            for b in range(nb):
                k2 = 3 * nb + j * nb + b
                other = half(outs[b], qj, 1 - c)
                pltpu.make_async_remote_copy(
                    src_ref=other, dst_ref=other, send_sem=send_sems.at[k2], recv_sem=recv_sems.at[k2],
                    device_id=sibling, device_id_type=MESH).wait_recv()
        for cp in first + passed:
            cp.wait_send()

    any_spec = pl.BlockSpec(memory_space=pl.ANY)
    return _pcall(
        body, name="all_gather_weights",
        out_shape=[jax.ShapeDtypeStruct(b.shape, b.dtype) for b in bufs],
        in_specs=[any_spec] * nb, out_specs=[any_spec] * nb, input_output_aliases={0: 0, 1: 1},
        scratch_shapes=[pltpu.SemaphoreType.DMA((6 * nb,)), pltpu.SemaphoreType.DMA((6 * nb,))],
        compiler_params=_cparams(),
    )(*bufs)


def _rs_to_sibling(ga, gb):
    bufs = (ga, gb)
    nb = len(bufs)

    def body(a_ref, b_ref, ao_ref, bo_ref, send_sems, recv_sems):
        x, y, c = _my_pos()
        srcs, outs = (a_ref, b_ref), (ao_ref, bo_ref)
        cps = []
        for b in range(nb):
            rows = outs[b].shape[1]
            cp = pltpu.make_async_remote_copy(
                src_ref=srcs[b].at[:, pl.ds((1 - c) * rows, rows)], dst_ref=outs[b],
                send_sem=send_sems.at[b], recv_sem=recv_sems.at[b], device_id=(x, y, 1 - c), device_id_type=MESH)
            cp.start()
            cps.append(cp)
        for cp in cps:
            cp.wait()

    any_spec = pl.BlockSpec(memory_space=pl.ANY)
    return _pcall(
        body, name="rs_to_sibling",
        out_shape=[jax.ShapeDtypeStruct((b.shape[0], b.shape[1] // 2, b.shape[2]), b.dtype) for b in bufs],
        in_specs=[any_spec] * nb, out_specs=[any_spec] * nb,
        scratch_shapes=[pltpu.SemaphoreType.DMA((nb,)), pltpu.SemaphoreType.DMA((nb,))],
        compiler_params=_cparams(),
    )(*bufs)


def _rs_to_chips(pa, pb):
    bufs = (pa, pb)
    nb = len(bufs)

    def body(a_ref, b_ref, ao_ref, bo_ref, send_sems, recv_sems):
        x, y, c = _my_pos()
        srcs, outs = (a_ref, b_ref), (ao_ref, bo_ref)
        cps = []
        for j, chip in enumerate(_other_chips(x, y)):
            for b in range(nb):
                k = j * nb + b
                cp = pltpu.make_async_remote_copy(
                    src_ref=srcs[b].at[j], dst_ref=outs[b].at[j],
                    send_sem=send_sems.at[k], recv_sem=recv_sems.at[k], device_id=(*chip, c), device_id_type=MESH)
                cp.start()
                cps.append(cp)
        for cp in cps:
            cp.wait()

    any_spec = pl.BlockSpec(memory_space=pl.ANY)
    return _pcall(
        body, name="rs_to_chips",
        out_shape=[jax.ShapeDtypeStruct(b.shape, b.dtype) for b in bufs],
        in_specs=[any_spec] * nb, out_specs=[any_spec] * nb,
        scratch_shapes=[pltpu.SemaphoreType.DMA((3 * nb,)), pltpu.SemaphoreType.DMA((3 * nb,))],
        compiler_params=_cparams(),
    )(*bufs)


def _rs_join_halves(fa, fb):
    bufs = (fa, fb)
    nb = len(bufs)

    def body(a_ref, b_ref, ao_ref, bo_ref, send_sems, recv_sems):
        x, y, c = _my_pos()
        srcs, outs = (a_ref, b_ref), (ao_ref, bo_ref)
        cps = []
        for b in range(nb):
            rows = srcs[b].shape[0] // 2
            cp = pltpu.make_async_remote_copy(
                src_ref=srcs[b].at[pl.ds(c * rows, rows)], dst_ref=outs[b].at[pl.ds(c * rows, rows)],
                send_sem=send_sems.at[b], recv_sem=recv_sems.at[b], device_id=(x, y, 1 - c), device_id_type=MESH)
            cp.start()
            cps.append(cp)
        for b in range(nb):
            rows = srcs[b].shape[0] // 2
            theirs = outs[b].at[pl.ds((1 - c) * rows, rows)]
            pltpu.make_async_remote_copy(
                src_ref=theirs, dst_ref=theirs, send_sem=send_sems.at[b], recv_sem=recv_sems.at[b],
                device_id=(x, y, 1 - c), device_id_type=MESH).wait_recv()
        for cp in cps:
            cp.wait_send()

    any_spec = pl.BlockSpec(memory_space=pl.ANY)
    return _pcall(
        body, name="rs_join_halves",
        out_shape=[jax.ShapeDtypeStruct(b.shape, b.dtype) for b in bufs],
        in_specs=[any_spec] * nb, out_specs=[any_spec] * nb, input_output_aliases={0: 0, 1: 1},
        scratch_shapes=[pltpu.SemaphoreType.DMA((nb,)), pltpu.SemaphoreType.DMA((nb,))],
        compiler_params=_cparams(),
    )(*bufs)


def _add_halves(g, r, idx, tr, name):
    _, rows, w = r.shape
    nt = rows // tr

    def body(i_ref, g_ref, r_ref, o_ref):
        o_ref[...] = (g_ref[...] + r_ref[...]).astype(BF16)

    grid_spec = pltpu.PrefetchScalarGridSpec(
        num_scalar_prefetch=1, grid=(3, nt),
        in_specs=[pl.BlockSpec((None, tr, w), lambda j, i, ix: (ix[1 + j], ix[0] * nt + i, 0)),
                  pl.BlockSpec((None, tr, w), lambda j, i, ix: (ix[1 + j], i, 0))],
        out_specs=pl.BlockSpec((None, tr, w), lambda j, i, ix: (j, i, 0)))
    return _pcall(body, name=name, grid_spec=grid_spec,
                  out_shape=jax.ShapeDtypeStruct((3, rows, w), BF16), compiler_params=_seq(2))(idx, g, r)


def _add_chips(g, r, r3, idx, tr, name):
    _, rows, w = r.shape
    nt = rows // tr

    def body(i_ref, g_ref, r_ref, a_ref, b_ref, c_ref, o_ref):
        own = g_ref[...] + r_ref[...]
        o_ref[...] = ((own + a_ref[...].astype(F32)) + b_ref[...].astype(F32)) + c_ref[...].astype(F32)

    def other(j):
        return pl.BlockSpec((None, tr, w), lambda i, ix: (j, i, 0))

    grid_spec = pltpu.PrefetchScalarGridSpec(
        num_scalar_prefetch=1, grid=(nt,),
        in_specs=[pl.BlockSpec((None, tr, w), lambda i, ix: (ix[0], ix[1] * nt + i, 0)),
                  pl.BlockSpec((None, tr, w), lambda i, ix: (ix[0], i, 0)), other(0), other(1), other(2)],
        out_specs=pl.BlockSpec((tr, w), lambda i, ix: (ix[1] * nt + i, 0)))
    return _pcall(body, name=name, grid_spec=grid_spec,
                  out_shape=jax.ShapeDtypeStruct((2 * rows, w), F32), compiler_params=_seq(1))(idx, g, r, r3, r3, r3)


def _ada_fwd(c_all, w_sh, b_sh):
    def body(c_ref, w_ref, b_ref, o_ref):
        cv = c_ref[...]
        ca = cv * _sig(cv)
        o_ref[...] = jnp.dot(ca, w_ref[...], preferred_element_type=F32, precision=lax.Precision.HIGHEST) + b_ref[...]

    return _pcall(body, name="ada_fwd", out_shape=jax.ShapeDtypeStruct((c_all.shape[0], w_sh.shape[1]), F32),
                  compiler_params=_cparams())(c_all, w_sh, b_sh)


def _ada_bwd(c_all_t, dmod_sh):
    def body(c_ref, d_ref, o_ref):
        cv = c_ref[...]
        ca = cv * _sig(cv)
        o_ref[...] = jnp.dot(ca, d_ref[...], preferred_element_type=F32, precision=lax.Precision.HIGHEST)

    return _pcall(body, name="ada_bwd", out_shape=jax.ShapeDtypeStruct((c_all_t.shape[0], dmod_sh.shape[1]), F32),
                  compiler_params=_cparams())(c_all_t, dmod_sh)


def _conv31_causal(ext_ref, sh_ref, w_ref, bias, out_ref, tm, d, rc):
    off = HALO_A - (CONV_A - 1)
    n = tm + HALO_A - SUBLANES
    for s in range(SUBLANES):
        if s == 0:
            src = ext_ref
        else:
            sh_ref[0:n, :] = ext_ref[s:s + n, :]
            src = sh_ref
        taps = [k for k in range(CONV_A) if (off + k) % SUBLANES == s]
        for r0 in range(0, tm, rc):
            acc = jnp.broadcast_to(bias, (rc, d)) if s == 0 else out_ref[r0:r0 + rc, :]
            for k in taps:
                a = r0 + (off + k) // SUBLANES * SUBLANES
                acc = acc + w_ref[k:k + 1, :] * src[a:a + rc, :]
            out_ref[r0:r0 + rc, :] = acc


def _conv31_adjoint(dp_ref, sh_ref, ext_ref, w_ref, dx_ref, dw_ref, tm, d, rc):
    off = HALO_A - (CONV_A - 1)
    lead = SUBLANES + CONV_A - 1
    n = tm + HALO_A
    row = lax.broadcasted_iota(jnp.int32, (SUBLANES, d), 0)
    for s in range(SUBLANES):
        if s == 0:
            src = dp_ref
        else:
            sh_ref[0:n, :] = dp_ref[s:s + n, :]
            src = sh_ref
        taps = [k for k in range(CONV_A) if (lead - k) % SUBLANES == s]
        for r0 in range(0, tm, rc):
            acc = jnp.zeros((rc, d), F32) if s == 0 else dx_ref[r0:r0 + rc, :]
            for k in taps:
                a = r0 + (lead - k) // SUBLANES * SUBLANES
                acc = acc + w_ref[k:k + 1, :] * src[a:a + rc, :]
            dx_ref[r0:r0 + rc, :] = acc
        for k in range(CONV_A):
            if (-(off + k)) % SUBLANES != s:
                continue
            e = s + off + k - SUBLANES
            tot = _rsum(src[0:tm, :] * ext_ref[e:e + tm, :])
            tail = src[tm:tm + SUBLANES, :] * ext_ref[tm + e:tm + e + SUBLANES, :]
            tot = tot + _rsum(jnp.where(row < SUBLANES - s, tail, 0.0))
            dw_ref[k:k + 1, :] += tot


def _conv3_causal(ext_ref, w_ref, out_ref, tm, d, rc):
    off = HALO_B - (CONV_B - 1)
    for r0 in range(0, tm, rc):
        acc = jnp.zeros((rc, d), F32)
        for k in range(CONV_B):
            acc = acc + w_ref[k:k + 1, :] * ext_ref[r0 + off + k:r0 + off + k + rc, :]
        out_ref[r0:r0 + rc, :] = acc


def _conv3_anticausal(ext_ref, w_ref, out_ref, tm, d, rc):
    for r0 in range(0, tm, rc):
        acc = jnp.zeros((rc, d), F32)
        for k in range(CONV_B):
            o = CONV_B - 1 - k
            acc = acc + w_ref[k:k + 1, :] * ext_ref[r0 + o:r0 + o + rc, :]
        out_ref[r0:r0 + rc, :] = acc


def _fwd_token_mixing(x, vecs, wa, b_in, caw, cbw, wb, tm, rc):
    s, d = x.shape
    nq, _, nw = wa.shape
    kq = d // nq
    base = 2 * d // kq

    def body(x_ref, v_ref, w_ref, b_ref, caw_ref, cbw_ref, wao_ref, wbo_ref, wo_ref,
             h_ref, z_ref, u1_ref, ya_ref, yb_ref, o1_ref, u3_ref, vv_ref, mg_ref, ext_ref, sh_ref, pext_ref, q_ref):
        @pl.when(pl.program_id(0) == 0)
        def _():
            ext_ref[0:HALO_A, :] = jnp.zeros((HALO_A, d), F32)
            pext_ref[0:HALO_B, :] = jnp.zeros((HALO_B, d), F32)

        xh, _ = _ln(x_ref[...])
        h = (xh * (1.0 + v_ref[V_SCALE1:V_SCALE1 + 1, :]) + v_ref[V_SHIFT1:V_SHIFT1 + 1, :]).astype(BF16)
        h_ref[...] = h
        for q in range(nq):
            z_ref[:, q * nw:(q + 1) * nw] = _dot(h, w_ref[q]) + b_ref[:, q * nw:(q + 1) * nw]

        ext_ref[HALO_A:HALO_A + tm, :] = z_ref[:, 0:d] * _sig(z_ref[:, d:2 * d])
        _conv31_causal(ext_ref, sh_ref, caw_ref, v_ref[V_CAB:V_CAB + 1, :], u1_ref, tm, d, rc)
        ext_ref[0:HALO_A, :] = ext_ref[tm:tm + HALO_A, :]
        xa, _ = _ln(u1_ref[...])
        u2 = xa * v_ref[V_LNAG:V_LNAG + 1, :] + v_ref[V_LNAB:V_LNAB + 1, :]
        u3 = (u2 * _sig(u2)).astype(BF16)
        u3_ref[...] = u3
        ya = jnp.broadcast_to(v_ref[V_BAO:V_BAO + 1, :], (tm, d))
        for q in range(nq):
            ya = ya + _dot(u3[:, q * kq:(q + 1) * kq], wao_ref[q])
        ya_ref[...] = ya

        pext_ref[HALO_B:HALO_B + tm, :] = z_ref[:, 3 * d:4 * d] * z_ref[:, 4 * d:5 * d]
        _conv3_causal(pext_ref, cbw_ref, q_ref, tm, d, rc)
        pext_ref[0:HALO_B, :] = pext_ref[tm:tm + HALO_B, :]
        vv = (z_ref[:, 2 * d:3 * d] * q_ref[...]).astype(BF16)
        vv_ref[...] = vv
        yb = jnp.zeros((tm, d), F32)
        for q in range(nq):
            yb = yb + _dot(vv[:, q * kq:(q + 1) * kq], wbo_ref[q])
        yb_ref[...] = yb

        mg = (_sig(z_ref[:, 5 * d:6 * d]) * ya + _sig(z_ref[:, 6 * d:7 * d]) * yb).astype(BF16)
        mg_ref[...] = mg
        o1 = jnp.broadcast_to(v_ref[V_BO:V_BO + 1, :], (tm, d))
        for q in range(nq):
            o1 = o1 + _dot(mg[:, q * kq:(q + 1) * kq], wo_ref[q])
        o1_ref[...] = o1

    def wspec(j):
        return pl.BlockSpec((nq, kq, d), lambda i: (0, base + j, 0), pipeline_mode=pl.Buffered(1))

    f32o = jax.ShapeDtypeStruct((s, d), F32)
    b16o = jax.ShapeDtypeStruct((s, d), BF16)
    return _pcall(
        body, name="fwd_token_mixing", grid=(s // tm,),
        in_specs=[_rows(tm, d), _full(vecs.shape), _full(wa.shape, single=True), _full(b_in.shape),
                  _full(caw.shape), _full(cbw.shape), wspec(0), wspec(1), wspec(2)],
        out_specs=[_rows(tm, d), _rows(tm, nq * nw)] + [_rows(tm, d)] * 7,
        out_shape=[b16o, jax.ShapeDtypeStruct((s, nq * nw), F32), f32o, f32o, f32o, f32o, b16o, b16o, b16o],
        scratch_shapes=[pltpu.VMEM((HALO_A + tm, d), F32), pltpu.VMEM((HALO_A + tm, d), F32),
                        pltpu.VMEM((HALO_B + tm, d), F32), pltpu.VMEM((tm, d), F32)],
        compiler_params=_seq(1),
    )(x, vecs, wa, b_in, caw, cbw, wb, wb, wb)


def _mlp_fwd_bwd(x, out1, tgt, vecs, b_up, wb, tm):
    s, d = x.shape
    nq = wb.shape[0]
    dff = nq * d

    def body(x_ref, o1_ref, t_ref, v_ref, bup_ref, wup_ref, wdn_ref,
             h2_ref, f_ref, df0_ref, do2_ref, do1_ref, dxp_ref, acc_ref, dbup_ref, f0_ref):
        @pl.when(pl.program_id(0) == 0)
        def _():
            acc_ref[...] = jnp.zeros(acc_ref.shape, F32)
            dbup_ref[...] = jnp.zeros(dbup_ref.shape, F32)

        def vec(r):
            return v_ref[r:r + 1, :]

        def accum(r, val):
            acc_ref[r:r + 1, :] += _rsum(val)

        out1v = o1_ref[...]
        r1 = ALPHA * x_ref[...] + (1.0 + vec(V_GATE1)) * out1v
        xh1, rstd1 = _ln(r1)
        x1 = xh1 * vec(V_LN1G) + vec(V_LN1B)
        xn1, rstdn = _ln(x1)
        h2 = (xn1 * (1.0 + vec(V_SCALE2)) + vec(V_SHIFT2)).astype(BF16)
        h2_ref[...] = h2
        out2 = jnp.broadcast_to(vec(V_BDN), (tm, d))
        for q in range(nq):
            f0 = _dot(h2, wup_ref[q]) + bup_ref[:, q * d:(q + 1) * d]
            rl = jnp.maximum(f0, 0.0)
            f0_ref[:, q * d:(q + 1) * d] = rl
            fb = (rl * rl).astype(BF16)
            f_ref[:, q * d:(q + 1) * d] = fb
            out2 = out2 + _dot(fb, wdn_ref[q])
        r2 = ALPHA * x1 + (1.0 + vec(V_GATE2)) * out2
        xh2, rstd2 = _ln(r2)
        yv = xh2 * vec(V_LN2G) + vec(V_LN2B)
        err = yv - t_ref[...]
        accum(M_LOSS, err * err)
        dy = err * (1.0 / d)
        accum(M_LN2G, dy * xh2)
        accum(M_LN2B, dy)
        dr2 = _ln_bwd(dy * vec(V_LN2G), xh2, rstd2)
        accum(M_GATE2, dr2 * out2)
        dout2 = (1.0 + vec(V_GATE2)) * dr2
        accum(M_BDN, dout2)
        do2b = dout2.astype(BF16)
        do2_ref[...] = do2b
        dh2 = jnp.zeros((tm, d), F32)
        for q in range(nq):
            df0 = _dot_nt(do2b, wdn_ref[q]) * (2.0 * f0_ref[:, q * d:(q + 1) * d])
            dbup_ref[q:q + 1, :] += _rsum(df0)
            df0b = df0.astype(BF16)
            df0_ref[:, q * d:(q + 1) * d] = df0b
            dh2 = dh2 + _dot_nt(df0b, wup_ref[q])
        accum(M_SHIFT2, dh2)
        accum(M_SCALE2, dh2 * xn1)
        dx1 = ALPHA * dr2 + _ln_bwd(dh2 * (1.0 + vec(V_SCALE2)), xn1, rstdn)
        accum(M_LN1G, dx1 * xh1)
        accum(M_LN1B, dx1)
        dr1 = _ln_bwd(dx1 * vec(V_LN1G), xh1, rstd1)
        accum(M_GATE1, dr1 * out1v)
        dout1 = (1.0 + vec(V_GATE1)) * dr1
        accum(M_BO, dout1)
        do1_ref[...] = dout1.astype(BF16)
        dxp_ref[...] = ALPHA * dr1

    def wspec(j):
        return pl.BlockSpec((nq, d, d), lambda i: (0, j, 0), pipeline_mode=pl.Buffered(1))

    b16 = lambda w: jax.ShapeDtypeStruct((s, w), BF16)
    return _pcall(
        body, name="mlp_fwd_bwd", grid=(s // tm,),
        in_specs=[_rows(tm, d), _rows(tm, d), _rows(tm, d), _full(vecs.shape), _full(b_up.shape), wspec(0), wspec(1)],
        out_specs=[_rows(tm, d), _rows(tm, dff), _rows(tm, dff), _rows(tm, d), _rows(tm, d), _rows(tm, d),
                   _full((16, d)), _full((SUBLANES, d))],
        out_shape=[b16(d), b16(dff), b16(dff), b16(d), b16(d), jax.ShapeDtypeStruct((s, d), F32),
                   jax.ShapeDtypeStruct((16, d), F32), jax.ShapeDtypeStruct((SUBLANES, d), F32)],
        scratch_shapes=[pltpu.VMEM((tm, dff), F32)],
        compiler_params=_seq(1),
    )(x, out1, tgt, vecs, b_up, wb, wb)


def _bwd_token_mixing(dout1, z, u1, ya, yb, x, dxp, vecs, caw, cbw, wa, wb, tm, rc):
    s = z.shape[0]
    d = vecs.shape[1]
    nq, _, nw = wa.shape
    kq = d // nq
    base = 2 * d // kq
    nt = s // tm
    hb = tm // HALO_A

    def body(do1_ref, z_ref, zh_ref, u1_ref, ya_ref, yb_ref, x_ref, dxp_ref, v_ref, caw_ref, cbw_ref,
             win_ref, wao_ref, wbo_ref, wo_ref,
             dz_ref, dya_ref, dyb_ref, gx_ref, acc_ref, dcaw_ref, dcbw_ref, dbin_ref, iacc_ref,
             ext_ref, du1p_ref, sh_ref, pext_ref, dqe_ref, tmp_ref):
        i = pl.program_id(0)

        @pl.when(i == 0)
        def _():
            acc_ref[...] = jnp.zeros(acc_ref.shape, F32)
            iacc_ref[...] = jnp.zeros(iacc_ref.shape, F32)
            dcaw_ref[...] = jnp.zeros(dcaw_ref.shape, F32)
            dcbw_ref[...] = jnp.zeros(dcbw_ref.shape, F32)
            dbin_ref[...] = jnp.zeros(dbin_ref.shape, F32)
            du1p_ref[0:SUBLANES, :] = jnp.zeros((SUBLANES, d), F32)
            du1p_ref[SUBLANES + tm:SUBLANES + tm + HALO_A, :] = jnp.zeros((HALO_A, d), F32)
            dqe_ref[tm:tm + HALO_B, :] = jnp.zeros((HALO_B, d), F32)

        def vec(r):
            return v_ref[r:r + 1, :]

        def accum(r, val):
            acc_ref[r:r + 1, :] += _rsum(val)

        def put_dz(j, val):
            dbin_ref[j:j + 1, :] += _rsum(val)
            dz_ref[:, j * d:(j + 1) * d] = val.astype(BF16)

        has_history = i < nt - 1

        do1 = do1_ref[...]
        dmg = jnp.concatenate([_dot_nt(do1, wo_ref[q]) for q in range(nq)], axis=1)
        sga = _sig(z_ref[:, 5 * d:6 * d])
        sgb = _sig(z_ref[:, 6 * d:7 * d])
        dya = dmg * sga
        dyb = dmg * sgb
        accum(X_BAO, dya)
        put_dz(5, dya * ya_ref[...] * (1.0 - sga))
        put_dz(6, dyb * yb_ref[...] * (1.0 - sgb))
        dyab = dya.astype(BF16)
        dybb = dyb.astype(BF16)
        dya_ref[...] = dyab
        dyb_ref[...] = dybb

        du3 = jnp.concatenate([_dot_nt(dyab, wao_ref[q]) for q in range(nq)], axis=1)
        xa, rstda = _ln(u1_ref[...])
        u2 = xa * vec(V_LNAG) + vec(V_LNAB)
        s2 = _sig(u2)
        du2 = du3 * (s2 * (1.0 + u2 * (1.0 - s2)))
        accum(X_LNAG, du2 * xa)
        accum(X_LNAB, du2)
        du1 = _ln_bwd(du2 * vec(V_LNAG), xa, rstda)
        accum(X_CAB, du1)
        du1p_ref[SUBLANES:SUBLANES + tm, :] = du1
        sg = _sig(z_ref[:, d:2 * d])
        aval = z_ref[:, 0:d]
        ext_ref[HALO_A:HALO_A + tm, :] = aval * sg
        ext_ref[0:HALO_A, :] = jnp.where(has_history, zh_ref[:, 0:d] * _sig(zh_ref[:, d:2 * d]), 0.0)
        _conv31_adjoint(du1p_ref, sh_ref, ext_ref, caw_ref, tmp_ref, dcaw_ref, tm, d, rc)
        du1p_ref[SUBLANES + tm:SUBLANES + tm + HALO_A, :] = du1p_ref[SUBLANES:SUBLANES + HALO_A, :]
        du0 = tmp_ref[...]
        put_dz(0, du0 * sg)
        put_dz(1, du0 * aval * sg * (1.0 - sg))

        dv = jnp.concatenate([_dot_nt(dybb, wbo_ref[q]) for q in range(nq)], axis=1)
        bgc = z_ref[:, 3 * d:4 * d]
        bx = z_ref[:, 4 * d:5 * d]
        pext_ref[HALO_B:HALO_B + tm, :] = bgc * bx
        pext_ref[0:HALO_B, :] = jnp.where(
            has_history, zh_ref[HALO_A - HALO_B:HALO_A, 3 * d:4 * d] * zh_ref[HALO_A - HALO_B:HALO_A, 4 * d:5 * d], 0.0)
        _conv3_causal(pext_ref, cbw_ref, tmp_ref, tm, d, rc)
        put_dz(2, dv * tmp_ref[...])
        dq = dv * z_ref[:, 2 * d:3 * d]
        dqe_ref[0:tm, :] = dq
        offb = HALO_B - (CONV_B - 1)
        for k in range(CONV_B):
            dcbw_ref[k:k + 1, :] += _rsum(dq * pext_ref[offb + k:offb + k + tm, :])
        _conv3_anticausal(dqe_ref, cbw_ref, tmp_ref, tm, d, rc)
        dqe_ref[tm:tm + HALO_B, :] = dqe_ref[0:HALO_B, :]
        dp = tmp_ref[...]
        put_dz(3, dp * bx)
        put_dz(4, dp * bgc)

        dh1 = jnp.zeros((tm, d), F32)
        for q in range(nq):
            dh1 = dh1 + _dot_nt(dz_ref[:, q * nw:(q + 1) * nw], win_ref[q])
        xh, rstd = _ln(x_ref[...])
        iacc_ref[I_SHIFT1:I_SHIFT1 + 1, :] += _rsum(dh1)
        iacc_ref[I_SCALE1:I_SCALE1 + 1, :] += _rsum(dh1 * xh)
        gx_ref[...] = dxp_ref[...] + _ln_bwd(dh1 * (1.0 + vec(V_SCALE1)), xh, rstd)

    def rev(width):
        return pl.BlockSpec((tm, width), lambda i: (nt - 1 - i, 0))

    def wspec(j):
        return pl.BlockSpec((nq, kq, d), lambda i: (0, base + j, 0), pipeline_mode=pl.Buffered(1))

    halo = pl.BlockSpec((HALO_A, 7 * d), lambda i: (jnp.maximum((nt - 1 - i) * hb - 1, 0), 0))
    b16 = jax.ShapeDtypeStruct((s, d), BF16)
    acc8 = jax.ShapeDtypeStruct((SUBLANES, d), F32)
    return _pcall(
        body, name="bwd_token_mixing", grid=(nt,),
        in_specs=[rev(d), rev(7 * d), halo, rev(d), rev(d), rev(d), rev(d), rev(d), _full(vecs.shape), _full(caw.shape),
                  _full(cbw.shape), _full(wa.shape, single=True), wspec(0), wspec(1), wspec(2)],
        out_specs=[rev(7 * d), rev(d), rev(d), rev(d), _full((SUBLANES, d)), _full((HALO_A, d)), _full((HALO_B, d)),
                   _full((SUBLANES, d)), _full((SUBLANES, d))],
        out_shape=[jax.ShapeDtypeStruct((s, 7 * d), BF16), b16, b16, jax.ShapeDtypeStruct((s, d), F32), acc8,
                   jax.ShapeDtypeStruct((HALO_A, d), F32), jax.ShapeDtypeStruct((HALO_B, d), F32), acc8, acc8],
        scratch_shapes=[pltpu.VMEM((HALO_A + tm, d), F32), pltpu.VMEM((SUBLANES + tm + HALO_A, d), F32),
                        pltpu.VMEM((tm + HALO_A, d), F32), pltpu.VMEM((HALO_B + tm, d), F32),
                        pltpu.VMEM((tm + HALO_B, d), F32), pltpu.VMEM((tm, d), F32)],
        compiler_params=_seq(1),
    )(dout1, z, z, u1, ya, yb, x, dxp, vecs, caw, cbw, wa, wb, wb, wb)


def _dw(a, b, split_a, ts, name, into=None, rows_total=None, row_block=0):
    s = a.shape[0]
    ka = a.shape[1] // N_CHIPS if split_a else a.shape[1]
    nb = b.shape[1] if split_a else b.shape[1] // N_CHIPS
    rows_total = ka if rows_total is None else rows_total

    def body(a_ref, b_ref, *rest):
        o_ref = rest[-1]

        @pl.when(pl.program_id(1) == 0)
        def _():
            o_ref[...] = jnp.zeros(o_ref.shape, F32)

        o_ref[...] += _dot_tn(a_ref[...], b_ref[...])

    a_spec = pl.BlockSpec((ts, ka), (lambda q, i: (i, q)) if split_a else (lambda q, i: (i, 0)))
    b_spec = pl.BlockSpec((ts, nb), (lambda q, i: (i, 0)) if split_a else (lambda q, i: (i, q)))
    extra = {} if into is None else dict(input_output_aliases={2: 0})
    return _pcall(
        body, name=name, grid=(N_CHIPS, s // ts),
        in_specs=[a_spec, b_spec] + ([] if into is None else [pl.BlockSpec(memory_space=pl.ANY)]),
        out_specs=pl.BlockSpec((None, ka, nb), lambda q, i: (q, row_block, 0)),
        out_shape=jax.ShapeDtypeStruct((N_CHIPS, rows_total, nb), F32),
        compiler_params=_seq(2), **extra,
    )(*((a, b) if into is None else (a, b, into)))


def _dw_rows(a, b, ts, name, into, row_block):
    s, k = a.shape
    n = b.shape[1]
    kq = k // N_CHIPS

    def body(a_ref, b_ref, buf_ref, o_ref):
        @pl.when(pl.program_id(0) == 0)
        def _():
            o_ref[...] = jnp.zeros(o_ref.shape, F32)

        res = _dot_tn(a_ref[...], b_ref[...])
        for q in range(N_CHIPS):
            o_ref[q] += res[q * kq:(q + 1) * kq, :]

    return _pcall(
        body, name=name, grid=(s // ts,),
        in_specs=[_rows(ts, k), _rows(ts, n), pl.BlockSpec(memory_space=pl.ANY)],
        out_specs=pl.BlockSpec((N_CHIPS, kq, n), lambda i: (0, row_block, 0)),
        out_shape=jax.ShapeDtypeStruct(into.shape, F32), input_output_aliases={2: 0},
        compiler_params=_seq(1),
    )(a, b, into)


def _adam_math(w, g, m, v):
    m2 = ADAM_B1 * m + (1.0 - ADAM_B1) * g
    v2 = ADAM_B2 * v + (1.0 - ADAM_B2) * (g * g)
    m_hat = m2 / (1.0 - ADAM_B1 ** ADAM_STEP)
    v_hat = v2 / (1.0 - ADAM_B2 ** ADAM_STEP)
    delta = -ADAM_LR * (m_hat / (jnp.sqrt(v_hat) + ADAM_EPS) + ADAM_WD * w)
    return delta, m2, v2


def _adam(w, g, m, v, g_row0, tr, name):
    r, c = w.shape
    blk0 = g_row0 // tr

    def body(w_ref, g_ref, m_ref, v_ref, go_ref, d_ref, mo_ref, vo_ref):
        gv = g_ref[...]
        go_ref[...] = gv
        d_ref[...], mo_ref[...], vo_ref[...] = _adam_math(w_ref[...], gv, m_ref[...], v_ref[...])

    spec = _rows(tr, c)
    g_spec = pl.BlockSpec((tr, c), lambda i: (blk0 + i, 0))
    o = jax.ShapeDtypeStruct((r, c), F32)
    return _pcall(body, name=name, grid=(r // tr,), in_specs=[spec, g_spec, spec, spec], out_specs=[spec] * 4,
                  out_shape=[o, o, o, o], compiler_params=_seq(1))(w, g, m, v)


def _small_update(gathered, q_idx, small_w, small_m, small_v, conv_w, conv_m, conv_v):
    d = gathered.shape[2]
    ns = len(_SMALL)
    cw = conv_w[0].shape[1]
    conv_rows = ((T_CAW, CONV_A), (T_CBW, CONV_B))

    def body(q_ref, g_ref, *refs):
        ins, outs = refs[:3 * (ns + 2)], refs[3 * (ns + 2):]
        tot_ref, loss_ref = outs[0], outs[1]
        outs = outs[2:]
        tot = g_ref[0]
        for dev in range(1, N_DEV):
            tot = tot + g_ref[dev]
        tot_ref[...] = tot
        loss_ref[...] = (0.5 / d) * jnp.sum(tot_ref[T_LOSS:T_LOSS + 1, :], axis=1, keepdims=True)
        for p, (_, rows) in enumerate(_SMALL):
            w_ref, m_ref, v_ref = ins[p], ins[ns + 2 + p], ins[2 * (ns + 2) + p]
            go, do, mo, vo = outs[4 * p:4 * p + 4]
            for j, row in enumerate(rows):
                sl = slice(j * d, (j + 1) * d)
                gv = tot_ref[row:row + 1, :]
                go[:, sl] = gv
                do[:, sl], mo[:, sl], vo[:, sl] = _adam_math(w_ref[:, sl], gv, m_ref[:, sl], v_ref[:, sl])
        for p, (row, taps) in enumerate(conv_rows):
            w_ref, m_ref, v_ref = ins[ns + p], ins[ns + 2 + ns + p], ins[2 * (ns + 2) + ns + p]
            go, do, mo, vo = outs[4 * (ns + p):4 * (ns + p) + 4]
            gv = tot_ref[row:row + taps, 0:cw]
            for qq in range(1, N_CHIPS):
                gv = jnp.where(q_ref[0] == qq, tot_ref[row:row + taps, qq * cw:(qq + 1) * cw], gv)
            go[...] = gv
            do[...], mo[...], vo[...] = _adam_math(w_ref[...], gv, m_ref[...], v_ref[...])

    params = list(small_w) + list(conv_w) + list(small_m) + list(conv_m) + list(small_v) + list(conv_v)
    out_shape = [jax.ShapeDtypeStruct((T_ROWS, d), F32), jax.ShapeDtypeStruct((1, 1), F32)]
    for w in list(small_w) + list(conv_w):
        out_shape += [jax.ShapeDtypeStruct(w.shape, F32)] * 4
    vm = pl.BlockSpec(memory_space=pltpu.VMEM)
    return _pcall(
        body, name="small_update", out_shape=out_shape,
        in_specs=[pl.BlockSpec(memory_space=pltpu.SMEM), vm] + [vm] * len(params),
        out_specs=[vm] * len(out_shape), compiler_params=_cparams(),
    )(q_idx, gathered, *params)


def kernel(x, c, w_ada, b_ada, w_in, b_in, conv_a_w, conv_a_b, ln_a_g, ln_a_b, w_a_out, b_a_out, conv_b_w, w_b_out, w_o, b_o, ln1_g, ln1_b, w_up, b_up, w_down, b_down, ln2_g, ln2_b, loss_target, m_w_ada, m_b_ada, m_w_in, m_b_in, m_conv_a_w, m_conv_a_b, m_ln_a_g, m_ln_a_b, m_w_a_out, m_b_a_out, m_conv_b_w, m_w_b_out, m_w_o, m_b_o, m_ln1_g, m_ln1_b, m_w_up, m_b_up, m_w_down, m_b_down, m_ln2_g, m_ln2_b, v_w_ada, v_b_ada, v_w_in, v_b_in, v_conv_a_w, v_conv_a_b, v_ln_a_g, v_ln_a_b, v_w_a_out, v_b_a_out, v_conv_b_w, v_w_b_out, v_w_o, v_b_o, v_ln1_g, v_ln1_b, v_w_up, v_b_up, v_w_down, v_b_down, v_ln2_g, v_ln2_b):
    given = dict(locals())
    s, d = x.shape[1], x.shape[2]
    xi, yi, ci = _my_pos()
    q = 2 * xi + yi
    me = 4 * xi + 2 * yi + ci
    i32 = jnp.int32
    q_arr = jnp.reshape(q, (1,)).astype(i32)
    others = [2 * ox + oy for ox, oy in _other_chips(xi, yi)]
    halves_idx = jnp.stack([ci] + others).astype(i32)
    chips_idx = jnp.stack([q, ci]).astype(i32)
    kq = d // N_CHIPS
    tm = min(256, s)
    tmx = min(128, s)
    rc = min(32, tmx)

    def sq(a):
        return a.reshape(a.shape[1:])

    x2, tgt = sq(x), sq(loss_target)

    n_ada = w_ada.shape[2]
    pre = jnp.concatenate([
        jnp.broadcast_to(c, (SUBLANES, d)),
        jnp.pad(sq(conv_a_w), ((0, HALO_A - CONV_A), (0, d - kq))),
        jnp.pad(sq(conv_b_w), ((0, HALO_B - CONV_B), (0, d - kq)))], axis=0)
    pre_all = _all_gather_small(pre, "gather_c_conv")
    c_all = pre_all[:, 0, :]
    caw = jnp.concatenate([pre_all[2 * p, SUBLANES:SUBLANES + HALO_A, :kq] for p in range(N_CHIPS)], axis=1)
    cbw = jnp.concatenate([pre_all[2 * p, SUBLANES + HALO_A:, :kq] for p in range(N_CHIPS)], axis=1)
    b_ada_sh = lax.dynamic_slice(b_ada, (0, q * n_ada), (1, n_ada))
    mod_part = _ada_fwd(c_all, sq(w_ada), b_ada_sh)
    mod_all = _all_gather_small(mod_part, "gather_mod")
    mod_rows = lax.dynamic_slice(mod_all, (0, me, 0), (N_DEV, 1, n_ada))[0::2, 0, :]
    mod = mod_rows.reshape(6, d)
    vecs = jnp.concatenate([mod, conv_a_b, ln_a_g, ln_a_b, b_a_out, b_o, ln1_g, ln1_b, b_down, ln2_g, ln2_b], axis=0)

    wa = _place_shard([sq(w_in)], q_arr, "place_w_in")
    wb = _place_shard([sq(w_up), sq(w_down), sq(w_a_out), sq(w_b_out), sq(w_o)], q_arr, "place_w_rest")
    wa, wb = _all_gather_weights(wa, wb)

    h1, z, u1, ya, yb, out1, u3, vv, mg = _fwd_token_mixing(x2, vecs, wa, b_in, caw, cbw, wb, tmx, rc)

    h2, fb, df0, do2, do1, dxp, macc, dbup = _mlp_fwd_bwd(x2, out1, tgt, vecs, b_up, wb, tm)
    dz, dya, dyb, gx, xacc, dcaw, dcbw, dbin, iacc = _bwd_token_mixing(
        do1, z, u1, ya, yb, x2, dxp, vecs, caw, cbw, wa, wb, tmx, rc)

    ts = min(2048, s)
    rest_rows = wb.shape[1]
    ga = _dw(h1, dz, False, ts, "dw_in")
    gb = _dw(h2, df0, False, ts, "dw_up", rows_total=rest_rows)
    gb = _dw(fb, do2, True, ts, "dw_down", into=gb, rows_total=rest_rows, row_block=1)
    small0 = 2 * d // kq
    gb = _dw_rows(u3, dya, ts, "dw_a_out", gb, small0)
    gb = _dw_rows(vv, dyb, ts, "dw_b_out", gb, small0 + 1)
    gb = _dw_rows(mg, do1, ts, "dw_o", gb, small0 + 2)

    ra, rb = _rs_to_sibling(ga, gb)
    tra, trb = ra.shape[1] // 4, rb.shape[1] // 4
    pa = _add_halves(ga, ra, halves_idx, tra, "rs_add_halves_in")
    pb = _add_halves(gb, rb, halves_idx, trb, "rs_add_halves_rest")
    r3a, r3b = _rs_to_chips(pa, pb)
    fa = _add_chips(ga, ra, r3a, chips_idx, tra, "rs_add_chips_in")
    fb_ = _add_chips(gb, rb, r3b, chips_idx, trb, "rs_add_chips_rest")
    g_in, g_b = _rs_join_halves(fa, fb_)

    table = jnp.concatenate([iacc, macc, dbin, dcaw, xacc, dcbw, dbup], axis=0)
    gathered = _all_gather_small(table, "gather_small_grads")

    names = [n for n, _ in _SMALL]
    res = _small_update(
        gathered, q_arr,
        [given[n] for n in names], [given["m_" + n] for n in names], [given["v_" + n] for n in names],
        [sq(conv_a_w), sq(conv_b_w)], [sq(m_conv_a_w), sq(m_conv_b_w)], [sq(v_conv_a_w), sq(v_conv_b_w)])
    loss = res[1].reshape(())
    upd = {}
    for p, n in enumerate(names + ["conv_a_w", "conv_b_w"]):
        upd[n] = res[2 + 4 * p:6 + 4 * p]

    dmod_all = jnp.stack([gathered[:, r, :] for r in _SMALL[0][1]], axis=1).reshape(N_DEV, 6 * d)
    dmod_sh = lax.dynamic_slice(dmod_all, (0, q * n_ada), (N_DEV, n_ada))
    g_ada = _ada_bwd(c_all.T, dmod_sh)
    upd["w_ada"] = _adam(sq(w_ada), g_ada, sq(m_w_ada), sq(v_w_ada), 0, min(256, d), "adam_w_ada")

    upd["w_in"] = _adam(sq(w_in), g_in, sq(m_w_in), sq(v_w_in), 0, min(256, d), "adam_w_in")
    r0 = 0
    for n in ("w_up", "w_down", "w_a_out", "w_b_out", "w_o"):
        w = sq(given[n])
        upd[n] = _adam(w, g_b, sq(given["m_" + n]), sq(given["v_" + n]), r0, min(256, w.shape[0]), "adam_" + n)
        r0 += w.shape[0]

    order = ["w_ada", "b_ada", "w_in", "b_in", "conv_a_w", "conv_a_b", "ln_a_g", "ln_a_b", "w_a_out", "b_a_out", "conv_b_w",
             "w_b_out", "w_o", "b_o", "ln1_g", "ln1_b", "w_up", "b_up", "w_down", "b_down", "ln2_g", "ln2_b"]
    outs = [loss, gx.reshape(x.shape)]
    for k in range(4):
        outs += [upd[n][k].reshape(given[n].shape) for n in order]
    return tuple(outs)
```

```python
import jax
import jax.numpy as jnp
from jax import lax
from jax.experimental import pallas as pl
from jax.experimental.pallas import tpu as pltpu

F32 = jnp.float32
BF16 = jnp.bfloat16
MESH = pl.DeviceIdType.MESH

LN_EPS = 1e-5
DEPTH = 1
ALPHA = (2.0 * DEPTH) ** 0.25
CONV_A = 31
CONV_B = 3
SUBLANES = 8
HALO_A = 32
HALO_B = 8
N_CHIPS = 4
N_DEV = 8
ADAM_LR = 0.001
ADAM_B1 = 0.9
ADAM_B2 = 0.999
ADAM_EPS = 1e-08
ADAM_WD = 0.01
ADAM_STEP = 10
VMEM_LIMIT = 56 * 1024 * 1024

V_SHIFT1, V_SCALE1, V_GATE1, V_SHIFT2, V_SCALE2, V_GATE2 = 0, 1, 2, 3, 4, 5
V_CAB, V_LNAG, V_LNAB, V_BAO, V_BO, V_LN1G, V_LN1B, V_BDN, V_LN2G, V_LN2B = 6, 7, 8, 9, 10, 11, 12, 13, 14, 15

M_LN2G, M_LN2B, M_GATE2, M_BDN, M_SHIFT2, M_SCALE2, M_LN1G, M_LN1B, M_GATE1, M_BO, M_LOSS = range(11)
X_BAO, X_LNAG, X_LNAB, X_CAB = range(4)
I_SHIFT1, I_SCALE1 = 0, 1

T_I, T_M, T_BIN, T_CAW, T_X, T_CBW, T_BUP, T_ROWS = 0, 8, 24, 32, 64, 72, 80, 88
T_LOSS = T_M + M_LOSS
_SMALL = (
    ("b_ada", (T_I + I_SHIFT1, T_I + I_SCALE1, T_M + M_GATE1, T_M + M_SHIFT2, T_M + M_SCALE2, T_M + M_GATE2)),
    ("b_in", tuple(T_BIN + j for j in range(7))),
    ("conv_a_b", (T_X + X_CAB,)), ("ln_a_g", (T_X + X_LNAG,)), ("ln_a_b", (T_X + X_LNAB,)), ("b_a_out", (T_X + X_BAO,)),
    ("b_o", (T_M + M_BO,)), ("ln1_g", (T_M + M_LN1G,)), ("ln1_b", (T_M + M_LN1B,)),
    ("b_up", tuple(T_BUP + j for j in range(4))),
    ("b_down", (T_M + M_BDN,)), ("ln2_g", (T_M + M_LN2G,)), ("ln2_b", (T_M + M_LN2B,)),
)


def _pcall(body, **kw):
    return pl.pallas_call(body, **kw)


def _cparams(**kw):
    return pltpu.CompilerParams(vmem_limit_bytes=VMEM_LIMIT, **kw)


def _seq(n):
    return _cparams(dimension_semantics=("arbitrary",) * n)


def _full(shape, single=False):
    nd = len(shape)
    if single:
        return pl.BlockSpec(shape, lambda *_: (0,) * nd, pipeline_mode=pl.Buffered(1))
    return pl.BlockSpec(shape, lambda *_: (0,) * nd)


def _rows(tm, width):
    return pl.BlockSpec((tm, width), lambda i: (i, 0))


def _sig(x):
    return jax.nn.sigmoid(x)


def _ln(x):
    mu = jnp.mean(x, axis=-1, keepdims=True)
    xc = x - mu
    var = jnp.mean(xc * xc, axis=-1, keepdims=True)
    rstd = lax.rsqrt(var + LN_EPS)
    return xc * rstd, rstd


def _ln_bwd(dxh, xh, rstd):
    m1 = jnp.mean(dxh, axis=-1, keepdims=True)
    m2 = jnp.mean(dxh * xh, axis=-1, keepdims=True)
    return rstd * (dxh - m1 - xh * m2)


def _rsum(v):
    return jnp.sum(v, axis=0, keepdims=True)


def _dot(a, b):
    return jnp.dot(a, b, preferred_element_type=F32)


def _dot_nt(a, b):
    return lax.dot_general(a, b, (((1,), (1,)), ((), ())), preferred_element_type=F32)


def _dot_tn(a, b):
    return lax.dot_general(a, b, (((0,), (0,)), ((), ())), preferred_element_type=F32)


def _my_pos():
    return lax.axis_index("x"), lax.axis_index("y"), lax.axis_index("c")


def _other_chips(x, y):
    return [(1 - x, y), (x, 1 - y), (1 - x, 1 - y)]


def _all_gather_small(v, name):
    r, c = v.shape

    def body(v_ref, out_ref, send_sems, recv_sems, local_sem):
        x, y, cc = _my_pos()
        me = 4 * x + 2 * y + cc
        mine = pltpu.make_async_copy(v_ref, out_ref.at[me], local_sem)
        mine.start()
        sends = []
        for rel in range(1, N_DEV):
            rx, ry, rc = (rel >> 2) & 1, (rel >> 1) & 1, rel & 1
            peer = (1 - x if rx else x, 1 - y if ry else y, 1 - cc if rc else cc)
            cp = pltpu.make_async_remote_copy(
                src_ref=v_ref, dst_ref=out_ref.at[me], send_sem=send_sems.at[rel - 1], recv_sem=recv_sems.at[rel - 1],
                device_id=peer, device_id_type=MESH)
            cp.start()
            sends.append(cp)
        for rel in range(1, N_DEV):
            rx, ry, rc = (rel >> 2) & 1, (rel >> 1) & 1, rel & 1
            peer = (1 - x if rx else x, 1 - y if ry else y, 1 - cc if rc else cc)
            slot = 4 * peer[0] + 2 * peer[1] + peer[2]
            pltpu.make_async_remote_copy(
                src_ref=v_ref, dst_ref=out_ref.at[slot], send_sem=send_sems.at[rel - 1], recv_sem=recv_sems.at[rel - 1],
                device_id=peer, device_id_type=MESH).wait_recv()
        for cp in sends:
            cp.wait_send()
        mine.wait()

    return _pcall(
        body, name=name,
        out_shape=jax.ShapeDtypeStruct((N_DEV, r, c), v.dtype),
        in_specs=[pl.BlockSpec(memory_space=pltpu.VMEM)],
        out_specs=pl.BlockSpec(memory_space=pltpu.VMEM),
        scratch_shapes=[pltpu.SemaphoreType.DMA((N_DEV - 1,)), pltpu.SemaphoreType.DMA((N_DEV - 1,)),
                        pltpu.SemaphoreType.DMA],
        compiler_params=_cparams(),
    )(v)


def _place_shard(parts, q_idx, name):
    rows = sum(p.shape[0] for p in parts)
    w = parts[0].shape[1]

    def body(q_ref, *refs):
        o_ref = refs[-1]
        r0 = 0
        for p_ref in refs[:-1]:
            n = p_ref.shape[0]
            o_ref[r0:r0 + n, :] = p_ref[...].astype(BF16)
            r0 += n

    grid_spec = pltpu.PrefetchScalarGridSpec(
        num_scalar_prefetch=1, grid=(1,),
        in_specs=[pl.BlockSpec(p.shape, lambda i, q: (0, 0)) for p in parts],
        out_specs=pl.BlockSpec((None, rows, w), lambda i, q: (q[0], 0, 0)))
    return _pcall(body, name=name, grid_spec=grid_spec, out_shape=jax.ShapeDtypeStruct((N_CHIPS, rows, w), BF16),
                  compiler_params=_seq(1))(q_idx, *parts)


def _all_gather_weights(wa, wb):
    bufs = (wa, wb)
    nb = len(bufs)

    def body(a_ref, b_ref, ao_ref, bo_ref, send_sems, recv_sems):
        x, y, c = _my_pos()
        q = 2 * x + y
        srcs, outs = (a_ref, b_ref), (ao_ref, bo_ref)
        sibling = (x, y, 1 - c)
        chips = _other_chips(x, y)

        def half(ref, slot, h):
            rows = ref.shape[1] // 2
            return ref.at[slot, pl.ds(h * rows, rows)]

        first = []
        for j, chip in enumerate(chips):
            for b in range(nb):
                k = j * nb + b
                cp = pltpu.make_async_remote_copy(
                    src_ref=half(srcs[b], q, c), dst_ref=half(outs[b], q, c),
                    send_sem=send_sems.at[k], recv_sem=recv_sems.at[k], device_id=(*chip, c), device_id_type=MESH)
                cp.start()
                first.append(cp)
        passed = []
        for j, chip in enumerate(chips):
            qj = 2 * chip[0] + chip[1]
            for b in range(nb):
                k = j * nb + b
                landed = half(outs[b], qj, c)
                pltpu.make_async_remote_copy(
                    src_ref=landed, dst_ref=landed, send_sem=send_sems.at[k], recv_sem=recv_sems.at[k],
                    device_id=(*chip, c), device_id_type=MESH).wait_recv()
                k2 = 3 * nb + k
                cp = pltpu.make_async_remote_copy(
                    src_ref=landed, dst_ref=landed, send_sem=send_sems.at[k2], recv_sem=recv_sems.at[k2],
                    device_id=sibling, device_id_type=MESH)
                cp.start()
                passed.append(cp)
        for j, chip in enumerate(chips):
            qj = 2 * chip[0] + chip[1]
            for b in range(nb):
                k2 = 3 * nb + j * nb + b
                other = half(outs[b], qj, 1 - c)
                pltpu.make_async_remote_copy(
                    src_ref=other, dst_ref=other, send_sem=send_sems.at[k2], recv_sem=recv_sems.at[k2],
                    device_id=sibling, device_id_type=MESH).wait_recv()
        for cp in first + passed:
            cp.wait_send()

    any_spec = pl.BlockSpec(memory_space=pl.ANY)
    return _pcall(
        body, name="all_gather_weights",
        out_shape=[jax.ShapeDtypeStruct(b.shape, b.dtype) for b in bufs],
        in_specs=[any_spec] * nb, out_specs=[any_spec] * nb, input_output_aliases={0: 0, 1: 1},
        scratch_shapes=[pltpu.SemaphoreType.DMA((6 * nb,)), pltpu.SemaphoreType.DMA((6 * nb,))],
        compiler_params=_cparams(),
    )(*bufs)


class _SiblingHalf:
    n_sems = 1

    @staticmethod
    def out_shape(g):
        return jax.ShapeDtypeStruct((g.shape[0], g.shape[1] // 2, g.shape[2]), g.dtype)

    @staticmethod
    def copies(g_ref, r_ref, send_sems, recv_sems):
        x, y, c = _my_pos()
        rows = r_ref.shape[1]
        return [pltpu.make_async_remote_copy(
            src_ref=g_ref.at[:, pl.ds((1 - c) * rows, rows)], dst_ref=r_ref,
            send_sem=send_sems.at[0], recv_sem=recv_sems.at[0], device_id=(x, y, 1 - c), device_id_type=MESH)]


class _ChipBlocks:
    n_sems = 3

    @staticmethod
    def out_shape(p):
        return jax.ShapeDtypeStruct(p.shape, p.dtype)

    @staticmethod
    def copies(p_ref, r_ref, send_sems, recv_sems):
        x, y, c = _my_pos()
        return [pltpu.make_async_remote_copy(
            src_ref=p_ref.at[j], dst_ref=r_ref.at[j], send_sem=send_sems.at[j], recv_sem=recv_sems.at[j],
            device_id=(*chip, c), device_id_type=MESH) for j, chip in enumerate(_other_chips(x, y))]


def _exchange(plan, src, name):
    def body(s_ref, o_ref, send_sems, recv_sems):
        cps = plan.copies(s_ref, o_ref, send_sems, recv_sems)
        for cp in cps:
            cp.start()
        for cp in cps:
            cp.wait()

    any_spec = pl.BlockSpec(memory_space=pl.ANY)
    return _pcall(
        body, name=name, out_shape=plan.out_shape(src), in_specs=[any_spec], out_specs=any_spec,
        scratch_shapes=[pltpu.SemaphoreType.DMA((plan.n_sems,)), pltpu.SemaphoreType.DMA((plan.n_sems,))],
        compiler_params=_cparams(),
    )(src)


def _carried_start(plan, first, refs):
    @pl.when(first)
    def _():
        for cp in plan.copies(*refs):
            cp.start()


def _carried_wait(plan, last, refs):
    @pl.when(last)
    def _():
        for cp in plan.copies(*refs):
            cp.wait()


def _carried_specs(plan, src):
    any_spec = pl.BlockSpec(memory_space=pl.ANY)
    sems = [pltpu.SemaphoreType.DMA((plan.n_sems,)), pltpu.SemaphoreType.DMA((plan.n_sems,))]
    return any_spec, any_spec, plan.out_shape(src), sems


def _rs_join_halves(fa, fb):
    bufs = (fa, fb)
    nb = len(bufs)

    def body(a_ref, b_ref, ao_ref, bo_ref, send_sems, recv_sems):
        x, y, c = _my_pos()
        srcs, outs = (a_ref, b_ref), (ao_ref, bo_ref)
        cps = []
        for b in range(nb):
            rows = srcs[b].shape[0] // 2
            cp = pltpu.make_async_remote_copy(
                src_ref=srcs[b].at[pl.ds(c * rows, rows)], dst_ref=outs[b].at[pl.ds(c * rows, rows)],
                send_sem=send_sems.at[b], recv_sem=recv_sems.at[b], device_id=(x, y, 1 - c), device_id_type=MESH)
            cp.start()
            cps.append(cp)
        for b in range(nb):
            rows = srcs[b].shape[0] // 2
            theirs = outs[b].at[pl.ds((1 - c) * rows, rows)]
            pltpu.make_async_remote_copy(
                src_ref=theirs, dst_ref=theirs, send_sem=send_sems.at[b], recv_sem=recv_sems.at[b],
                device_id=(x, y, 1 - c), device_id_type=MESH).wait_recv()
        for cp in cps:
            cp.wait_send()

    any_spec = pl.BlockSpec(memory_space=pl.ANY)
    return _pcall(
        body, name="rs_join_halves",
        out_shape=[jax.ShapeDtypeStruct(b.shape, b.dtype) for b in bufs],
        in_specs=[any_spec] * nb, out_specs=[any_spec] * nb, input_output_aliases={0: 0, 1: 1},
        scratch_shapes=[pltpu.SemaphoreType.DMA((nb,)), pltpu.SemaphoreType.DMA((nb,))],
        compiler_params=_cparams(),
    )(*bufs)


def _add_halves(g, r, idx, tr, name):
    _, rows, w = r.shape
    nt = rows // tr

    def body(i_ref, g_ref, r_ref, o_ref):
        o_ref[...] = (g_ref[...] + r_ref[...]).astype(BF16)

    grid_spec = pltpu.PrefetchScalarGridSpec(
        num_scalar_prefetch=1, grid=(3, nt),
        in_specs=[pl.BlockSpec((None, tr, w), lambda j, i, ix: (ix[1 + j], ix[0] * nt + i, 0)),
                  pl.BlockSpec((None, tr, w), lambda j, i, ix: (ix[1 + j], i, 0))],
        out_specs=pl.BlockSpec((None, tr, w), lambda j, i, ix: (j, i, 0)))
    return _pcall(body, name=name, grid_spec=grid_spec,
                  out_shape=jax.ShapeDtypeStruct((3, rows, w), BF16), compiler_params=_seq(2))(idx, g, r)


def _add_chips(g, r, r3, idx, tr, name):
    _, rows, w = r.shape
    nt = rows // tr

    def body(i_ref, g_ref, r_ref, a_ref, b_ref, c_ref, o_ref):
        own = g_ref[...] + r_ref[...]
        o_ref[...] = ((own + a_ref[...].astype(F32)) + b_ref[...].astype(F32)) + c_ref[...].astype(F32)

    def other(j):
        return pl.BlockSpec((None, tr, w), lambda i, ix: (j, i, 0))

    grid_spec = pltpu.PrefetchScalarGridSpec(
        num_scalar_prefetch=1, grid=(nt,),
        in_specs=[pl.BlockSpec((None, tr, w), lambda i, ix: (ix[0], ix[1] * nt + i, 0)),
                  pl.BlockSpec((None, tr, w), lambda i, ix: (ix[0], i, 0)), other(0), other(1), other(2)],
        out_specs=pl.BlockSpec((tr, w), lambda i, ix: (ix[1] * nt + i, 0)))
    return _pcall(body, name=name, grid_spec=grid_spec,
                  out_shape=jax.ShapeDtypeStruct((2 * rows, w), F32), compiler_params=_seq(1))(idx, g, r, r3, r3, r3)


def _ada_fwd(c_all, w_sh, b_sh):
    def body(c_ref, w_ref, b_ref, o_ref):
        cv = c_ref[...]
        ca = cv * _sig(cv)
        o_ref[...] = jnp.dot(ca, w_ref[...], preferred_element_type=F32, precision=lax.Precision.HIGHEST) + b_ref[...]

    return _pcall(body, name="ada_fwd", out_shape=jax.ShapeDtypeStruct((c_all.shape[0], w_sh.shape[1]), F32),
                  compiler_params=_cparams())(c_all, w_sh, b_sh)


def _ada_bwd(c_all_t, dmod_sh):
    def body(c_ref, d_ref, o_ref):
        cv = c_ref[...]
        ca = cv * _sig(cv)
        o_ref[...] = jnp.dot(ca, d_ref[...], preferred_element_type=F32, precision=lax.Precision.HIGHEST)

    return _pcall(body, name="ada_bwd", out_shape=jax.ShapeDtypeStruct((c_all_t.shape[0], dmod_sh.shape[1]), F32),
                  compiler_params=_cparams())(c_all_t, dmod_sh)


def _conv31_causal(ext_ref, sh_ref, w_ref, bias, out_ref, tm, d, rc):
    off = HALO_A - (CONV_A - 1)
    n = tm + HALO_A - SUBLANES
    for s in range(SUBLANES):
        if s == 0:
            src = ext_ref
        else:
            sh_ref[0:n, :] = ext_ref[s:s + n, :]
            src = sh_ref
        taps = [k for k in range(CONV_A) if (off + k) % SUBLANES == s]
        for r0 in range(0, tm, rc):
            acc = jnp.broadcast_to(bias, (rc, d)) if s == 0 else out_ref[r0:r0 + rc, :]
            for k in taps:
                a = r0 + (off + k) // SUBLANES * SUBLANES
                acc = acc + w_ref[k:k + 1, :] * src[a:a + rc, :]
            out_ref[r0:r0 + rc, :] = acc


def _conv31_adjoint(dp_ref, sh_ref, ext_ref, w_ref, dx_ref, dw_ref, tm, d, rc):
    off = HALO_A - (CONV_A - 1)
    lead = SUBLANES + CONV_A - 1
    n = tm + HALO_A
    row = lax.broadcasted_iota(jnp.int32, (SUBLANES, d), 0)
    for s in range(SUBLANES):
        if s == 0:
            src = dp_ref
        else:
            sh_ref[0:n, :] = dp_ref[s:s + n, :]
            src = sh_ref
        taps = [k for k in range(CONV_A) if (lead - k) % SUBLANES == s]
        for r0 in range(0, tm, rc):
            acc = jnp.zeros((rc, d), F32) if s == 0 else dx_ref[r0:r0 + rc, :]
            for k in taps:
                a = r0 + (lead - k) // SUBLANES * SUBLANES
                acc = acc + w_ref[k:k + 1, :] * src[a:a + rc, :]
            dx_ref[r0:r0 + rc, :] = acc
        for k in range(CONV_A):
            if (-(off + k)) % SUBLANES != s:
                continue
            e = s + off + k - SUBLANES
            tot = _rsum(src[0:tm, :] * ext_ref[e:e + tm, :])
            tail = src[tm:tm + SUBLANES, :] * ext_ref[tm + e:tm + e + SUBLANES, :]
            tot = tot + _rsum(jnp.where(row < SUBLANES - s, tail, 0.0))
            dw_ref[k:k + 1, :] += tot


def _conv3_causal(ext_ref, w_ref, out_ref, tm, d, rc):
    off = HALO_B - (CONV_B - 1)
    for r0 in range(0, tm, rc):
        acc = jnp.zeros((rc, d), F32)
        for k in range(CONV_B):
            acc = acc + w_ref[k:k + 1, :] * ext_ref[r0 + off + k:r0 + off + k + rc, :]
        out_ref[r0:r0 + rc, :] = acc


def _conv3_anticausal(ext_ref, w_ref, out_ref, tm, d, rc):
    for r0 in range(0, tm, rc):
        acc = jnp.zeros((rc, d), F32)
        for k in range(CONV_B):
            o = CONV_B - 1 - k
            acc = acc + w_ref[k:k + 1, :] * ext_ref[r0 + o:r0 + o + rc, :]
        out_ref[r0:r0 + rc, :] = acc


def _fwd_in(x, vecs, wa, b_in, tm):
    s, d = x.shape
    nq, _, nw = wa.shape

    def body(x_ref, v_ref, w_ref, b_ref, h_ref, z_ref):
        xh, _ = _ln(x_ref[...])
        h = (xh * (1.0 + v_ref[V_SCALE1:V_SCALE1 + 1, :]) + v_ref[V_SHIFT1:V_SHIFT1 + 1, :]).astype(BF16)
        h_ref[...] = h
        for q in range(nq):
            z_ref[:, q * nw:(q + 1) * nw] = _dot(h, w_ref[q]) + b_ref[:, q * nw:(q + 1) * nw]

    return _pcall(
        body, name="fwd_in", grid=(s // tm,),
        in_specs=[_rows(tm, d), _full(vecs.shape), _full(wa.shape, single=True), _full(b_in.shape)],
        out_specs=[_rows(tm, d), _rows(tm, nq * nw)],
        out_shape=[jax.ShapeDtypeStruct((s, d), BF16), jax.ShapeDtypeStruct((s, nq * nw), F32)],
        compiler_params=_seq(1),
    )(x, vecs, wa, b_in)


def _fwd_mix(z, vecs, caw, cbw, wb, tm, rc):
    s = z.shape[0]
    d = vecs.shape[1]
    nq = wb.shape[0]
    kq = d // nq
    base = 2 * d // kq

    def body(z_ref, v_ref, caw_ref, cbw_ref, wao_ref, wbo_ref, wo_ref,
             u1_ref, ya_ref, yb_ref, o1_ref, u3_ref, vv_ref, mg_ref, ext_ref, sh_ref, pext_ref, q_ref):
        @pl.when(pl.program_id(0) == 0)
        def _():
            ext_ref[0:HALO_A, :] = jnp.zeros((HALO_A, d), F32)
            pext_ref[0:HALO_B, :] = jnp.zeros((HALO_B, d), F32)

        ext_ref[HALO_A:HALO_A + tm, :] = z_ref[:, 0:d] * _sig(z_ref[:, d:2 * d])
        _conv31_causal(ext_ref, sh_ref, caw_ref, v_ref[V_CAB:V_CAB + 1, :], u1_ref, tm, d, rc)
        ext_ref[0:HALO_A, :] = ext_ref[tm:tm + HALO_A, :]
        xa, _ = _ln(u1_ref[...])
        u2 = xa * v_ref[V_LNAG:V_LNAG + 1, :] + v_ref[V_LNAB:V_LNAB + 1, :]
        u3 = (u2 * _sig(u2)).astype(BF16)
        u3_ref[...] = u3
        ya = jnp.broadcast_to(v_ref[V_BAO:V_BAO + 1, :], (tm, d))
        for q in range(nq):
            ya = ya + _dot(u3[:, q * kq:(q + 1) * kq], wao_ref[q])
        ya_ref[...] = ya

        pext_ref[HALO_B:HALO_B + tm, :] = z_ref[:, 3 * d:4 * d] * z_ref[:, 4 * d:5 * d]
        _conv3_causal(pext_ref, cbw_ref, q_ref, tm, d, rc)
        pext_ref[0:HALO_B, :] = pext_ref[tm:tm + HALO_B, :]
        vv = (z_ref[:, 2 * d:3 * d] * q_ref[...]).astype(BF16)
        vv_ref[...] = vv
        yb = jnp.zeros((tm, d), F32)
        for q in range(nq):
            yb = yb + _dot(vv[:, q * kq:(q + 1) * kq], wbo_ref[q])
        yb_ref[...] = yb

        mg = (_sig(z_ref[:, 5 * d:6 * d]) * ya + _sig(z_ref[:, 6 * d:7 * d]) * yb).astype(BF16)
        mg_ref[...] = mg
        o1 = jnp.broadcast_to(v_ref[V_BO:V_BO + 1, :], (tm, d))
        for q in range(nq):
            o1 = o1 + _dot(mg[:, q * kq:(q + 1) * kq], wo_ref[q])
        o1_ref[...] = o1

    def wspec(j):
        return pl.BlockSpec((nq, kq, d), lambda i: (0, base + j, 0), pipeline_mode=pl.Buffered(1))

    f32o = jax.ShapeDtypeStruct((s, d), F32)
    b16o = jax.ShapeDtypeStruct((s, d), BF16)
    return _pcall(
        body, name="fwd_mix", grid=(s // tm,),
        in_specs=[_rows(tm, 7 * d), _full(vecs.shape), _full(caw.shape), _full(cbw.shape), wspec(0), wspec(1), wspec(2)],
        out_specs=[_rows(tm, d)] * 7,
        out_shape=[f32o, f32o, f32o, f32o, b16o, b16o, b16o],
        scratch_shapes=[pltpu.VMEM((HALO_A + tm, d), F32), pltpu.VMEM((HALO_A + tm, d), F32),
                        pltpu.VMEM((HALO_B + tm, d), F32), pltpu.VMEM((tm, d), F32)],
        compiler_params=_seq(1),
    )(z, vecs, caw, cbw, wb, wb, wb)


def _mlp_fwd_bwd(x, out1, tgt, vecs, b_up, wb, tm):
    s, d = x.shape
    nq = wb.shape[0]
    dff = nq * d

    def body(x_ref, o1_ref, t_ref, v_ref, bup_ref, wup_ref, wdn_ref,
             h2_ref, f_ref, df0_ref, do2_ref, do1_ref, dxp_ref, acc_ref, dbup_ref, f0_ref):
        @pl.when(pl.program_id(0) == 0)
        def _():
            acc_ref[...] = jnp.zeros(acc_ref.shape, F32)
            dbup_ref[...] = jnp.zeros(dbup_ref.shape, F32)

        def vec(r):
            return v_ref[r:r + 1, :]

        def accum(r, val):
            acc_ref[r:r + 1, :] += _rsum(val)

        out1v = o1_ref[...]
        r1 = ALPHA * x_ref[...] + (1.0 + vec(V_GATE1)) * out1v
        xh1, rstd1 = _ln(r1)
        x1 = xh1 * vec(V_LN1G) + vec(V_LN1B)
        xn1, rstdn = _ln(x1)
        h2 = (xn1 * (1.0 + vec(V_SCALE2)) + vec(V_SHIFT2)).astype(BF16)
        h2_ref[...] = h2
        out2 = jnp.broadcast_to(vec(V_BDN), (tm, d))
        for q in range(nq):
            f0 = _dot(h2, wup_ref[q]) + bup_ref[:, q * d:(q + 1) * d]
            rl = jnp.maximum(f0, 0.0)
            f0_ref[:, q * d:(q + 1) * d] = rl
            fb = (rl * rl).astype(BF16)
            f_ref[:, q * d:(q + 1) * d] = fb
            out2 = out2 + _dot(fb, wdn_ref[q])
        r2 = ALPHA * x1 + (1.0 + vec(V_GATE2)) * out2
        xh2, rstd2 = _ln(r2)
        yv = xh2 * vec(V_LN2G) + vec(V_LN2B)
        err = yv - t_ref[...]
        accum(M_LOSS, err * err)
        dy = err * (1.0 / d)
        accum(M_LN2G, dy * xh2)
        accum(M_LN2B, dy)
        dr2 = _ln_bwd(dy * vec(V_LN2G), xh2, rstd2)
        accum(M_GATE2, dr2 * out2)
        dout2 = (1.0 + vec(V_GATE2)) * dr2
        accum(M_BDN, dout2)
        do2b = dout2.astype(BF16)
        do2_ref[...] = do2b
        dh2 = jnp.zeros((tm, d), F32)
        for q in range(nq):
            df0 = _dot_nt(do2b, wdn_ref[q]) * (2.0 * f0_ref[:, q * d:(q + 1) * d])
            dbup_ref[q:q + 1, :] += _rsum(df0)
            df0b = df0.astype(BF16)
            df0_ref[:, q * d:(q + 1) * d] = df0b
            dh2 = dh2 + _dot_nt(df0b, wup_ref[q])
        accum(M_SHIFT2, dh2)
        accum(M_SCALE2, dh2 * xn1)
        dx1 = ALPHA * dr2 + _ln_bwd(dh2 * (1.0 + vec(V_SCALE2)), xn1, rstdn)
        accum(M_LN1G, dx1 * xh1)
        accum(M_LN1B, dx1)
        dr1 = _ln_bwd(dx1 * vec(V_LN1G), xh1, rstd1)
        accum(M_GATE1, dr1 * out1v)
        dout1 = (1.0 + vec(V_GATE1)) * dr1
        accum(M_BO, dout1)
        do1_ref[...] = dout1.astype(BF16)
        dxp_ref[...] = ALPHA * dr1

    def wspec(j):
        return pl.BlockSpec((nq, d, d), lambda i: (0, j, 0), pipeline_mode=pl.Buffered(1))

    b16 = lambda w: jax.ShapeDtypeStruct((s, w), BF16)
    return _pcall(
        body, name="mlp_fwd_bwd", grid=(s // tm,),
        in_specs=[_rows(tm, d), _rows(tm, d), _rows(tm, d), _full(vecs.shape), _full(b_up.shape), wspec(0), wspec(1)],
        out_specs=[_rows(tm, d), _rows(tm, dff), _rows(tm, dff), _rows(tm, d), _rows(tm, d), _rows(tm, d),
                   _full((16, d)), _full((SUBLANES, d))],
        out_shape=[b16(d), b16(dff), b16(dff), b16(d), b16(d), jax.ShapeDtypeStruct((s, d), F32),
                   jax.ShapeDtypeStruct((16, d), F32), jax.ShapeDtypeStruct((SUBLANES, d), F32)],
        scratch_shapes=[pltpu.VMEM((tm, dff), F32)],
        compiler_params=_seq(1),
    )(x, out1, tgt, vecs, b_up, wb, wb)


def _mix_bwd(dout1, z, u1, ya, yb, vecs, caw, cbw, wb, tm, rc):
    s = z.shape[0]
    d = vecs.shape[1]
    nq = wb.shape[0]
    kq = d // nq
    base = 2 * d // kq
    nt = s // tm
    hb = tm // HALO_A

    def body(do1_ref, z_ref, zh_ref, u1_ref, ya_ref, yb_ref, v_ref, caw_ref, cbw_ref, wao_ref, wbo_ref, wo_ref,
             dz_ref, dya_ref, dyb_ref, acc_ref, dcaw_ref, dcbw_ref, dbin_ref,
             ext_ref, du1p_ref, sh_ref, pext_ref, dqe_ref, tmp_ref):
        i = pl.program_id(0)

        @pl.when(i == 0)
        def _():
            acc_ref[...] = jnp.zeros(acc_ref.shape, F32)
            dcaw_ref[...] = jnp.zeros(dcaw_ref.shape, F32)
            dcbw_ref[...] = jnp.zeros(dcbw_ref.shape, F32)
            dbin_ref[...] = jnp.zeros(dbin_ref.shape, F32)
            du1p_ref[0:SUBLANES, :] = jnp.zeros((SUBLANES, d), F32)
            du1p_ref[SUBLANES + tm:SUBLANES + tm + HALO_A, :] = jnp.zeros((HALO_A, d), F32)
            dqe_ref[tm:tm + HALO_B, :] = jnp.zeros((HALO_B, d), F32)

        def vec(r):
            return v_ref[r:r + 1, :]

        def accum(r, val):
            acc_ref[r:r + 1, :] += _rsum(val)

        def put_dz(j, val):
            dbin_ref[j:j + 1, :] += _rsum(val)
            dz_ref[:, j * d:(j + 1) * d] = val.astype(BF16)

        has_history = i < nt - 1

        do1 = do1_ref[...]
        dmg = jnp.concatenate([_dot_nt(do1, wo_ref[q]) for q in range(nq)], axis=1)
        sga = _sig(z_ref[:, 5 * d:6 * d])
        sgb = _sig(z_ref[:, 6 * d:7 * d])
        dya = dmg * sga
        dyb = dmg * sgb
        accum(X_BAO, dya)
        put_dz(5, dya * ya_ref[...] * (1.0 - sga))
        put_dz(6, dyb * yb_ref[...] * (1.0 - sgb))
        dyab = dya.astype(BF16)
        dybb = dyb.astype(BF16)
        dya_ref[...] = dyab
        dyb_ref[...] = dybb

        du3 = jnp.concatenate([_dot_nt(dyab, wao_ref[q]) for q in range(nq)], axis=1)
        xa, rstda = _ln(u1_ref[...])
        u2 = xa * vec(V_LNAG) + vec(V_LNAB)
        s2 = _sig(u2)
        du2 = du3 * (s2 * (1.0 + u2 * (1.0 - s2)))
        accum(X_LNAG, du2 * xa)
        accum(X_LNAB, du2)
        du1 = _ln_bwd(du2 * vec(V_LNAG), xa, rstda)
        accum(X_CAB, du1)
        du1p_ref[SUBLANES:SUBLANES + tm, :] = du1
        sg = _sig(z_ref[:, d:2 * d])
        aval = z_ref[:, 0:d]
        ext_ref[HALO_A:HALO_A + tm, :] = aval * sg
        ext_ref[0:HALO_A, :] = jnp.where(has_history, zh_ref[:, 0:d] * _sig(zh_ref[:, d:2 * d]), 0.0)
        _conv31_adjoint(du1p_ref, sh_ref, ext_ref, caw_ref, tmp_ref, dcaw_ref, tm, d, rc)
        du1p_ref[SUBLANES + tm:SUBLANES + tm + HALO_A, :] = du1p_ref[SUBLANES:SUBLANES + HALO_A, :]
        du0 = tmp_ref[...]
        put_dz(0, du0 * sg)
        put_dz(1, du0 * aval * sg * (1.0 - sg))

        dv = jnp.concatenate([_dot_nt(dybb, wbo_ref[q]) for q in range(nq)], axis=1)
        bgc = z_ref[:, 3 * d:4 * d]
        bx = z_ref[:, 4 * d:5 * d]
        pext_ref[HALO_B:HALO_B + tm, :] = bgc * bx
        pext_ref[0:HALO_B, :] = jnp.where(
            has_history, zh_ref[HALO_A - HALO_B:HALO_A, 3 * d:4 * d] * zh_ref[HALO_A - HALO_B:HALO_A, 4 * d:5 * d], 0.0)
        _conv3_causal(pext_ref, cbw_ref, tmp_ref, tm, d, rc)
        put_dz(2, dv * tmp_ref[...])
        dq = dv * z_ref[:, 2 * d:3 * d]
        dqe_ref[0:tm, :] = dq
        offb = HALO_B - (CONV_B - 1)
        for k in range(CONV_B):
            dcbw_ref[k:k + 1, :] += _rsum(dq * pext_ref[offb + k:offb + k + tm, :])
        _conv3_anticausal(dqe_ref, cbw_ref, tmp_ref, tm, d, rc)
        dqe_ref[tm:tm + HALO_B, :] = dqe_ref[0:HALO_B, :]
        dp = tmp_ref[...]
        put_dz(3, dp * bx)
        put_dz(4, dp * bgc)

    def rev(width):
        return pl.BlockSpec((tm, width), lambda i: (nt - 1 - i, 0))

    def wspec(j):
        return pl.BlockSpec((nq, kq, d), lambda i: (0, base + j, 0), pipeline_mode=pl.Buffered(1))

    halo = pl.BlockSpec((HALO_A, 7 * d), lambda i: (jnp.maximum((nt - 1 - i) * hb - 1, 0), 0))
    b16 = jax.ShapeDtypeStruct((s, d), BF16)
    acc8 = jax.ShapeDtypeStruct((SUBLANES, d), F32)
    return _pcall(
        body, name="mix_bwd", grid=(nt,),
        in_specs=[rev(d), rev(7 * d), halo, rev(d), rev(d), rev(d), _full(vecs.shape), _full(caw.shape), _full(cbw.shape),
                  wspec(0), wspec(1), wspec(2)],
        out_specs=[rev(7 * d), rev(d), rev(d), _full((SUBLANES, d)), _full((HALO_A, d)), _full((HALO_B, d)),
                   _full((SUBLANES, d))],
        out_shape=[jax.ShapeDtypeStruct((s, 7 * d), BF16), b16, b16, acc8,
                   jax.ShapeDtypeStruct((HALO_A, d), F32), jax.ShapeDtypeStruct((HALO_B, d), F32), acc8],
        scratch_shapes=[pltpu.VMEM((HALO_A + tm, d), F32), pltpu.VMEM((SUBLANES + tm + HALO_A, d), F32),
                        pltpu.VMEM((tm + HALO_A, d), F32), pltpu.VMEM((HALO_B + tm, d), F32),
                        pltpu.VMEM((tm + HALO_B, d), F32), pltpu.VMEM((tm, d), F32)],
        compiler_params=_seq(1),
    )(dout1, z, z, u1, ya, yb, vecs, caw, cbw, wb, wb, wb)


def _in_bwd(dz, x, dxp, vecs, wa, tm, plan, plan_src):
    s, d = x.shape
    nq, _, nw = wa.shape
    nt = s // tm

    def body(dz_ref, x_ref, dxp_ref, v_ref, w_ref, src_ref, gx_ref, acc_ref, dst_ref, send_sems, recv_sems):
        i = pl.program_id(0)
        comm = (src_ref, dst_ref, send_sems, recv_sems)
        _carried_start(plan, i == 0, comm)

        @pl.when(i == 0)
        def _():
            acc_ref[...] = jnp.zeros(acc_ref.shape, F32)

        dh1 = jnp.zeros((tm, d), F32)
        for q in range(nq):
            dh1 = dh1 + _dot_nt(dz_ref[:, q * nw:(q + 1) * nw], w_ref[q])
        xh, rstd = _ln(x_ref[...])
        acc_ref[I_SHIFT1:I_SHIFT1 + 1, :] += _rsum(dh1)
        acc_ref[I_SCALE1:I_SCALE1 + 1, :] += _rsum(dh1 * xh)
        gx_ref[...] = dxp_ref[...] + _ln_bwd(dh1 * (1.0 + v_ref[V_SCALE1:V_SCALE1 + 1, :]), xh, rstd)
        _carried_wait(plan, i == nt - 1, comm)

    c_in, c_out, c_shape, c_sems = _carried_specs(plan, plan_src)
    return _pcall(
        body, name="in_bwd", grid=(nt,),
        in_specs=[_rows(tm, nq * nw), _rows(tm, d), _rows(tm, d), _full(vecs.shape), _full(wa.shape, single=True), c_in],
        out_specs=[_rows(tm, d), _full((SUBLANES, d)), c_out],
        out_shape=[jax.ShapeDtypeStruct((s, d), F32), jax.ShapeDtypeStruct((SUBLANES, d), F32), c_shape],
        scratch_shapes=c_sems,
        compiler_params=_seq(1),
    )(dz, x, dxp, vecs, wa, plan_src)


def _dw(a, b, split_a, ts, name, into=None, rows_total=None, row_block=0):
    s = a.shape[0]
    ka = a.shape[1] // N_CHIPS if split_a else a.shape[1]
    nb = b.shape[1] if split_a else b.shape[1] // N_CHIPS
    rows_total = ka if rows_total is None else rows_total

    def body(a_ref, b_ref, *rest):
        o_ref = rest[-1]

        @pl.when(pl.program_id(1) == 0)
        def _():
            o_ref[...] = jnp.zeros(o_ref.shape, F32)

        o_ref[...] += _dot_tn(a_ref[...], b_ref[...])

    a_spec = pl.BlockSpec((ts, ka), (lambda q, i: (i, q)) if split_a else (lambda q, i: (i, 0)))
    b_spec = pl.BlockSpec((ts, nb), (lambda q, i: (i, 0)) if split_a else (lambda q, i: (i, q)))
    extra = {} if into is None else dict(input_output_aliases={2: 0})
    return _pcall(
        body, name=name, grid=(N_CHIPS, s // ts),
        in_specs=[a_spec, b_spec] + ([] if into is None else [pl.BlockSpec(memory_space=pl.ANY)]),
        out_specs=pl.BlockSpec((None, ka, nb), lambda q, i: (q, row_block, 0)),
        out_shape=jax.ShapeDtypeStruct((N_CHIPS, rows_total, nb), F32),
        compiler_params=_seq(2), **extra,
    )(*((a, b) if into is None else (a, b, into)))


def _dw_carrying(a, b, ts, name, plan, plan_src):
    s, k = a.shape
    nb = b.shape[1] // N_CHIPS
    ns = s // ts

    def body(a_ref, b_ref, src_ref, o_ref, dst_ref, send_sems, recv_sems):
        q, i = pl.program_id(0), pl.program_id(1)
        comm = (src_ref, dst_ref, send_sems, recv_sems)
        _carried_start(plan, jnp.logical_and(q == 0, i == 0), comm)

        @pl.when(i == 0)
        def _():
            o_ref[...] = jnp.zeros(o_ref.shape, F32)

        o_ref[...] += _dot_tn(a_ref[...], b_ref[...])
        _carried_wait(plan, jnp.logical_and(q == N_CHIPS - 1, i == ns - 1), comm)

    c_in, c_out, c_shape, c_sems = _carried_specs(plan, plan_src)
    return _pcall(
        body, name=name, grid=(N_CHIPS, ns),
        in_specs=[pl.BlockSpec((ts, k), lambda q, i: (i, 0)), pl.BlockSpec((ts, nb), lambda q, i: (i, q)), c_in],
        out_specs=[pl.BlockSpec((None, k, nb), lambda q, i: (q, 0, 0)), c_out],
        out_shape=[jax.ShapeDtypeStruct((N_CHIPS, k, nb), F32), c_shape],
        scratch_shapes=c_sems,
        compiler_params=_seq(2),
    )(a, b, plan_src)


def _dw_rows(a, b, ts, name, into, row_block):
    s, k = a.shape
    n = b.shape[1]
    kq = k // N_CHIPS

    def body(a_ref, b_ref, buf_ref, o_ref):
        @pl.when(pl.program_id(0) == 0)
        def _():
            o_ref[...] = jnp.zeros(o_ref.shape, F32)

        res = _dot_tn(a_ref[...], b_ref[...])
        for q in range(N_CHIPS):
            o_ref[q] += res[q * kq:(q + 1) * kq, :]

    return _pcall(
        body, name=name, grid=(s // ts,),
        in_specs=[_rows(ts, k), _rows(ts, n), pl.BlockSpec(memory_space=pl.ANY)],
        out_specs=pl.BlockSpec((N_CHIPS, kq, n), lambda i: (0, row_block, 0)),
        out_shape=jax.ShapeDtypeStruct(into.shape, F32), input_output_aliases={2: 0},
        compiler_params=_seq(1),
    )(a, b, into)


def _adam_math(w, g, m, v):
    m2 = ADAM_B1 * m + (1.0 - ADAM_B1) * g
    v2 = ADAM_B2 * v + (1.0 - ADAM_B2) * (g * g)
    m_hat = m2 / (1.0 - ADAM_B1 ** ADAM_STEP)
    v_hat = v2 / (1.0 - ADAM_B2 ** ADAM_STEP)
    delta = -ADAM_LR * (m_hat / (jnp.sqrt(v_hat) + ADAM_EPS) + ADAM_WD * w)
    return delta, m2, v2


def _adam(w, g, m, v, g_row0, tr, name):
    r, c = w.shape
    blk0 = g_row0 // tr

    def body(w_ref, g_ref, m_ref, v_ref, go_ref, d_ref, mo_ref, vo_ref):
        gv = g_ref[...]
        go_ref[...] = gv
        d_ref[...], mo_ref[...], vo_ref[...] = _adam_math(w_ref[...], gv, m_ref[...], v_ref[...])

    spec = _rows(tr, c)
    g_spec = pl.BlockSpec((tr, c), lambda i: (blk0 + i, 0))
    o = jax.ShapeDtypeStruct((r, c), F32)
    return _pcall(body, name=name, grid=(r // tr,), in_specs=[spec, g_spec, spec, spec], out_specs=[spec] * 4,
                  out_shape=[o, o, o, o], compiler_params=_seq(1))(w, g, m, v)


def _small_update(gathered, q_idx, small_w, small_m, small_v, conv_w, conv_m, conv_v):
    d = gathered.shape[2]
    ns = len(_SMALL)
    cw = conv_w[0].shape[1]
    conv_rows = ((T_CAW, CONV_A), (T_CBW, CONV_B))

    def body(q_ref, g_ref, *refs):
        ins, outs = refs[:3 * (ns + 2)], refs[3 * (ns + 2):]
        tot_ref, loss_ref = outs[0], outs[1]
        outs = outs[2:]
        tot = g_ref[0]
        for dev in range(1, N_DEV):
            tot = tot + g_ref[dev]
        tot_ref[...] = tot
        loss_ref[...] = (0.5 / d) * jnp.sum(tot_ref[T_LOSS:T_LOSS + 1, :], axis=1, keepdims=True)
        for p, (_, rows) in enumerate(_SMALL):
            w_ref, m_ref, v_ref = ins[p], ins[ns + 2 + p], ins[2 * (ns + 2) + p]
            go, do, mo, vo = outs[4 * p:4 * p + 4]
            for j, row in enumerate(rows):
                sl = slice(j * d, (j + 1) * d)
                gv = tot_ref[row:row + 1, :]
                go[:, sl] = gv
                do[:, sl], mo[:, sl], vo[:, sl] = _adam_math(w_ref[:, sl], gv, m_ref[:, sl], v_ref[:, sl])
        for p, (row, taps) in enumerate(conv_rows):
            w_ref, m_ref, v_ref = ins[ns + p], ins[ns + 2 + ns + p], ins[2 * (ns + 2) + ns + p]
            go, do, mo, vo = outs[4 * (ns + p):4 * (ns + p) + 4]
            gv = tot_ref[row:row + taps, 0:cw]
            for qq in range(1, N_CHIPS):
                gv = jnp.where(q_ref[0] == qq, tot_ref[row:row + taps, qq * cw:(qq + 1) * cw], gv)
            go[...] = gv
            do[...], mo[...], vo[...] = _adam_math(w_ref[...], gv, m_ref[...], v_ref[...])

    params = list(small_w) + list(conv_w) + list(small_m) + list(conv_m) + list(small_v) + list(conv_v)
    out_shape = [jax.ShapeDtypeStruct((T_ROWS, d), F32), jax.ShapeDtypeStruct((1, 1), F32)]
    for w in list(small_w) + list(conv_w):
        out_shape += [jax.ShapeDtypeStruct(w.shape, F32)] * 4
    vm = pl.BlockSpec(memory_space=pltpu.VMEM)
    return _pcall(
        body, name="small_update", out_shape=out_shape,
        in_specs=[pl.BlockSpec(memory_space=pltpu.SMEM), vm] + [vm] * len(params),
        out_specs=[vm] * len(out_shape), compiler_params=_cparams(),
    )(q_idx, gathered, *params)


def kernel(x, c, w_ada, b_ada, w_in, b_in, conv_a_w, conv_a_b, ln_a_g, ln_a_b, w_a_out, b_a_out, conv_b_w, w_b_out, w_o, b_o, ln1_g, ln1_b, w_up, b_up, w_down, b_down, ln2_g, ln2_b, loss_target, m_w_ada, m_b_ada, m_w_in, m_b_in, m_conv_a_w, m_conv_a_b, m_ln_a_g, m_ln_a_b, m_w_a_out, m_b_a_out, m_conv_b_w, m_w_b_out, m_w_o, m_b_o, m_ln1_g, m_ln1_b, m_w_up, m_b_up, m_w_down, m_b_down, m_ln2_g, m_ln2_b, v_w_ada, v_b_ada, v_w_in, v_b_in, v_conv_a_w, v_conv_a_b, v_ln_a_g, v_ln_a_b, v_w_a_out, v_b_a_out, v_conv_b_w, v_w_b_out, v_w_o, v_b_o, v_ln1_g, v_ln1_b, v_w_up, v_b_up, v_w_down, v_b_down, v_ln2_g, v_ln2_b):
    given = dict(locals())
    s, d = x.shape[1], x.shape[2]
    xi, yi, ci = _my_pos()
    q = 2 * xi + yi
    me = 4 * xi + 2 * yi + ci
    i32 = jnp.int32
    q_arr = jnp.reshape(q, (1,)).astype(i32)
    others = [2 * ox + oy for ox, oy in _other_chips(xi, yi)]
    halves_idx = jnp.stack([ci] + others).astype(i32)
    chips_idx = jnp.stack([q, ci]).astype(i32)
    kq = d // N_CHIPS
    tm = min(256, s)
    rc = min(32, tm)

    def sq(a):
        return a.reshape(a.shape[1:])

    x2, tgt = sq(x), sq(loss_target)

    n_ada = w_ada.shape[2]
    pre = jnp.concatenate([
        jnp.broadcast_to(c, (SUBLANES, d)),
        jnp.pad(sq(conv_a_w), ((0, HALO_A - CONV_A), (0, d - kq))),
        jnp.pad(sq(conv_b_w), ((0, HALO_B - CONV_B), (0, d - kq)))], axis=0)
    pre_all = _all_gather_small(pre, "gather_c_conv")
    c_all = pre_all[:, 0, :]
    caw = jnp.concatenate([pre_all[2 * p, SUBLANES:SUBLANES + HALO_A, :kq] for p in range(N_CHIPS)], axis=1)
    cbw = jnp.concatenate([pre_all[2 * p, SUBLANES + HALO_A:, :kq] for p in range(N_CHIPS)], axis=1)
    b_ada_sh = lax.dynamic_slice(b_ada, (0, q * n_ada), (1, n_ada))
    mod_part = _ada_fwd(c_all, sq(w_ada), b_ada_sh)
    mod_all = _all_gather_small(mod_part, "gather_mod")
    mod_rows = lax.dynamic_slice(mod_all, (0, me, 0), (N_DEV, 1, n_ada))[0::2, 0, :]
    mod = mod_rows.reshape(6, d)
    vecs = jnp.concatenate([mod, conv_a_b, ln_a_g, ln_a_b, b_a_out, b_o, ln1_g, ln1_b, b_down, ln2_g, ln2_b], axis=0)

    wa = _place_shard([sq(w_in)], q_arr, "place_w_in")
    wb = _place_shard([sq(w_up), sq(w_down), sq(w_a_out), sq(w_b_out), sq(w_o)], q_arr, "place_w_rest")
    wa, wb = _all_gather_weights(wa, wb)

    h1, z = _fwd_in(x2, vecs, wa, b_in, tm)
    u1, ya, yb, out1, u3, vv, mg = _fwd_mix(z, vecs, caw, cbw, wb, tm, rc)

    ts = min(2048, s)
    rest_rows = wb.shape[1]
    small0 = 2 * d // kq
    h2, fb, df0, do2, do1, dxp, macc, dbup = _mlp_fwd_bwd(x2, out1, tgt, vecs, b_up, wb, tm)
    gb = _dw(h2, df0, False, ts, "dw_up", rows_total=rest_rows)
    gb = _dw(fb, do2, True, ts, "dw_down", into=gb, rows_total=rest_rows, row_block=1)
    dz, dya, dyb, xacc, dcaw, dcbw, dbin = _mix_bwd(do1, z, u1, ya, yb, vecs, caw, cbw, wb, tm, rc)
    gb = _dw_rows(u3, dya, ts, "dw_a_out", gb, small0)
    gb = _dw_rows(vv, dyb, ts, "dw_b_out", gb, small0 + 1)
    gb = _dw_rows(mg, do1, ts, "dw_o", gb, small0 + 2)

    trb, tra = rest_rows // 8, d // 8
    gx, iacc, rb = _in_bwd(dz, x2, dxp, vecs, wa, tm, _SiblingHalf, gb)
    pb = _add_halves(gb, rb, halves_idx, trb, "rs_add_halves_rest")
    ga, r3b = _dw_carrying(h1, dz, ts, "dw_in", _ChipBlocks, pb)
    ra = _exchange(_SiblingHalf, ga, "rs_to_sibling")
    pa = _add_halves(ga, ra, halves_idx, tra, "rs_add_halves_in")
    r3a = _exchange(_ChipBlocks, pa, "rs_to_chips")
    fa = _add_chips(ga, ra, r3a, chips_idx, tra, "rs_add_chips_in")
    fb_ = _add_chips(gb, rb, r3b, chips_idx, trb, "rs_add_chips_rest")
    g_in, g_b = _rs_join_halves(fa, fb_)

    table = jnp.concatenate([iacc, macc, dbin, dcaw, xacc, dcbw, dbup], axis=0)
    gathered = _all_gather_small(table, "gather_small_grads")

    names = [n for n, _ in _SMALL]
    res = _small_update(
        gathered, q_arr,
        [given[n] for n in names], [given["m_" + n] for n in names], [given["v_" + n] for n in names],
        [sq(conv_a_w), sq(conv_b_w)], [sq(m_conv_a_w), sq(m_conv_b_w)], [sq(v_conv_a_w), sq(v_conv_b_w)])
    loss = res[1].reshape(())
    upd = {}
    for p, n in enumerate(names + ["conv_a_w", "conv_b_w"]):
        upd[n] = res[2 + 4 * p:6 + 4 * p]

    dmod_all = jnp.stack([gathered[:, r, :] for r in _SMALL[0][1]], axis=1).reshape(N_DEV, 6 * d)
    dmod_sh = lax.dynamic_slice(dmod_all, (0, q * n_ada), (N_DEV, n_ada))
    g_ada = _ada_bwd(c_all.T, dmod_sh)
    upd["w_ada"] = _adam(sq(w_ada), g_ada, sq(m_w_ada), sq(v_w_ada), 0, min(256, d), "adam_w_ada")

    upd["w_in"] = _adam(sq(w_in), g_in, sq(m_w_in), sq(v_w_in), 0, min(256, d), "adam_w_in")
    r0 = 0
    for n in ("w_up", "w_down", "w_a_out", "w_b_out", "w_o"):
        w = sq(given[n])
        upd[n] = _adam(w, g_b, sq(given["m_" + n]), sq(given["v_" + n]), r0, min(256, w.shape[0]), "adam_" + n)
        r0 += w.shape[0]

    order = ["w_ada", "b_ada", "w_in", "b_in", "conv_a_w", "conv_a_b", "ln_a_g", "ln_a_b", "w_a_out", "b_a_out", "conv_b_w",
             "w_b_out", "w_o", "b_o", "ln1_g", "ln1_b", "w_up", "b_up", "w_down", "b_down", "ln2_g", "ln2_b"]
    outs = [loss, gx.reshape(x.shape)]
    for k in range(4):
        outs += [upd[n][k].reshape(given[n].shape) for n in order]
    return tuple(outs)
```

```python
import jax
import jax.numpy as jnp
from jax import lax
from jax.experimental import pallas as pl
from jax.experimental.pallas import tpu as pltpu

F32 = jnp.float32
BF16 = jnp.bfloat16
MESH = pl.DeviceIdType.MESH

LN_EPS = 1e-5
DEPTH = 1
ALPHA = (2.0 * DEPTH) ** 0.25
CONV_A = 31
CONV_B = 3
SUBLANES = 8
HALO_A = 32
HALO_B = 8
N_CHIPS = 4
N_DEV = 8
ADAM_LR = 0.001
ADAM_B1 = 0.9
ADAM_B2 = 0.999
ADAM_EPS = 1e-08
ADAM_WD = 0.01
ADAM_STEP = 10
VMEM_LIMIT = 56 * 1024 * 1024

V_SHIFT1, V_SCALE1, V_GATE1, V_SHIFT2, V_SCALE2, V_GATE2 = 0, 1, 2, 3, 4, 5
V_CAB, V_LNAG, V_LNAB, V_BAO, V_BO, V_LN1G, V_LN1B, V_BDN, V_LN2G, V_LN2B = 6, 7, 8, 9, 10, 11, 12, 13, 14, 15

M_LN2G, M_LN2B, M_GATE2, M_BDN, M_SHIFT2, M_SCALE2, M_LN1G, M_LN1B, M_GATE1, M_BO, M_LOSS = range(11)
X_BAO, X_LNAG, X_LNAB, X_CAB = range(4)
I_SHIFT1, I_SCALE1 = 0, 1

T_I, T_M, T_BIN, T_CAW, T_X, T_CBW, T_BUP, T_ROWS = 0, 8, 24, 32, 64, 72, 80, 88
T_LOSS = T_M + M_LOSS
_SMALL = (
    ("b_ada", (T_I + I_SHIFT1, T_I + I_SCALE1, T_M + M_GATE1, T_M + M_SHIFT2, T_M + M_SCALE2, T_M + M_GATE2)),
    ("b_in", tuple(T_BIN + j for j in range(7))),
    ("conv_a_b", (T_X + X_CAB,)), ("ln_a_g", (T_X + X_LNAG,)), ("ln_a_b", (T_X + X_LNAB,)), ("b_a_out", (T_X + X_BAO,)),
    ("b_o", (T_M + M_BO,)), ("ln1_g", (T_M + M_LN1G,)), ("ln1_b", (T_M + M_LN1B,)),
    ("b_up", tuple(T_BUP + j for j in range(4))),
    ("b_down", (T_M + M_BDN,)), ("ln2_g", (T_M + M_LN2G,)), ("ln2_b", (T_M + M_LN2B,)),
)


def _pcall(body, **kw):
    return pl.pallas_call(body, **kw)


def _cparams(**kw):
    return pltpu.CompilerParams(vmem_limit_bytes=VMEM_LIMIT, **kw)


def _seq(n):
    return _cparams(dimension_semantics=("arbitrary",) * n)


def _full(shape, single=False):
    nd = len(shape)
    if single:
        return pl.BlockSpec(shape, lambda *_: (0,) * nd, pipeline_mode=pl.Buffered(1))
    return pl.BlockSpec(shape, lambda *_: (0,) * nd)


def _rows(tm, width):
    return pl.BlockSpec((tm, width), lambda i: (i, 0))


def _sig(x):
    return jax.nn.sigmoid(x)


def _ln(x):
    mu = jnp.mean(x, axis=-1, keepdims=True)
    xc = x - mu
    var = jnp.mean(xc * xc, axis=-1, keepdims=True)
    rstd = lax.rsqrt(var + LN_EPS)
    return xc * rstd, rstd


def _ln_bwd(dxh, xh, rstd):
    m1 = jnp.mean(dxh, axis=-1, keepdims=True)
    m2 = jnp.mean(dxh * xh, axis=-1, keepdims=True)
    return rstd * (dxh - m1 - xh * m2)


def _rsum(v):
    return jnp.sum(v, axis=0, keepdims=True)


def _dot(a, b):
    return jnp.dot(a, b, preferred_element_type=F32)


def _dot_nt(a, b):
    return lax.dot_general(a, b, (((1,), (1,)), ((), ())), preferred_element_type=F32)


def _dot_tn(a, b):
    return lax.dot_general(a, b, (((0,), (0,)), ((), ())), preferred_element_type=F32)


def _my_pos():
    return lax.axis_index("x"), lax.axis_index("y"), lax.axis_index("c")


def _other_chips(x, y):
    return [(1 - x, y), (x, 1 - y), (1 - x, 1 - y)]


def _all_gather_small(v, name):
    r, c = v.shape

    def body(v_ref, out_ref, send_sems, recv_sems, local_sem):
        x, y, cc = _my_pos()
        me = 4 * x + 2 * y + cc
        mine = pltpu.make_async_copy(v_ref, out_ref.at[me], local_sem)
        mine.start()
        sends = []
        for rel in range(1, N_DEV):
            rx, ry, rc = (rel >> 2) & 1, (rel >> 1) & 1, rel & 1
            peer = (1 - x if rx else x, 1 - y if ry else y, 1 - cc if rc else cc)
            cp = pltpu.make_async_remote_copy(
                src_ref=v_ref, dst_ref=out_ref.at[me], send_sem=send_sems.at[rel - 1], recv_sem=recv_sems.at[rel - 1],
                device_id=peer, device_id_type=MESH)
            cp.start()
            sends.append(cp)
        for rel in range(1, N_DEV):
            rx, ry, rc = (rel >> 2) & 1, (rel >> 1) & 1, rel & 1
            peer = (1 - x if rx else x, 1 - y if ry else y, 1 - cc if rc else cc)
            slot = 4 * peer[0] + 2 * peer[1] + peer[2]
            pltpu.make_async_remote_copy(
                src_ref=v_ref, dst_ref=out_ref.at[slot], send_sem=send_sems.at[rel - 1], recv_sem=recv_sems.at[rel - 1],
                device_id=peer, device_id_type=MESH).wait_recv()
        for cp in sends:
            cp.wait_send()
        mine.wait()

    return _pcall(
        body, name=name,
        out_shape=jax.ShapeDtypeStruct((N_DEV, r, c), v.dtype),
        in_specs=[pl.BlockSpec(memory_space=pltpu.VMEM)],
        out_specs=pl.BlockSpec(memory_space=pltpu.VMEM),
        scratch_shapes=[pltpu.SemaphoreType.DMA((N_DEV - 1,)), pltpu.SemaphoreType.DMA((N_DEV - 1,)),
                        pltpu.SemaphoreType.DMA],
        compiler_params=_cparams(),
    )(v)


def _place_shard(parts, q_idx, name):
    rows = sum(p.shape[0] for p in parts)
    w = parts[0].shape[1]

    def body(q_ref, *refs):
        o_ref = refs[-1]
        r0 = 0
        for p_ref in refs[:-1]:
            n = p_ref.shape[0]
            o_ref[r0:r0 + n, :] = p_ref[...].astype(BF16)
            r0 += n

    grid_spec = pltpu.PrefetchScalarGridSpec(
        num_scalar_prefetch=1, grid=(1,),
        in_specs=[pl.BlockSpec(p.shape, lambda i, q: (0, 0)) for p in parts],
        out_specs=pl.BlockSpec((None, rows, w), lambda i, q: (q[0], 0, 0)))
    return _pcall(body, name=name, grid_spec=grid_spec, out_shape=jax.ShapeDtypeStruct((N_CHIPS, rows, w), BF16),
                  compiler_params=_seq(1))(q_idx, *parts)


class _GatherShards:
    n_sems = 6

    @staticmethod
    def _half(ref, slot, h):
        rows = ref.shape[1] // 2
        return ref.at[slot, pl.ds(h * rows, rows)]

    @classmethod
    def _copies(cls, in_ref, out_ref, send_sems, recv_sems):
        x, y, c = _my_pos()
        q = 2 * x + y
        sibling = (x, y, 1 - c)
        sends, landed, forwards, passed = [], [], [], []
        for j, chip in enumerate(_other_chips(x, y)):
            qj = 2 * chip[0] + chip[1]

            def copy(src, dst, k, to):
                return pltpu.make_async_remote_copy(src_ref=src, dst_ref=dst, send_sem=send_sems.at[k],
                                                    recv_sem=recv_sems.at[k], device_id=to, device_id_type=MESH)

            mine, theirs = cls._half(out_ref, qj, c), cls._half(out_ref, qj, 1 - c)
            sends.append(copy(cls._half(in_ref, q, c), cls._half(out_ref, q, c), j, (*chip, c)))
            landed.append(copy(mine, mine, j, (*chip, c)))
            forwards.append(copy(mine, mine, 3 + j, sibling))
            passed.append(copy(theirs, theirs, 3 + j, sibling))
        return sends, landed, forwards, passed

    @classmethod
    def start(cls, *refs):
        for cp in cls._copies(*refs)[0]:
            cp.start()

    @classmethod
    def relay(cls, *refs):
        _, landed, forwards, _ = cls._copies(*refs)
        for arrived, onward in zip(landed, forwards):
            arrived.wait_recv()
            onward.start()

    @classmethod
    def finish(cls, *refs):
        sends, _, forwards, passed = cls._copies(*refs)
        for cp in passed:
            cp.wait_recv()
        for cp in sends + forwards:
            cp.wait_send()


def _all_gather_weights(w):
    def body(i_ref, o_ref, send_sems, recv_sems):
        refs = (i_ref, o_ref, send_sems, recv_sems)
        _GatherShards.start(*refs)
        _GatherShards.relay(*refs)
        _GatherShards.finish(*refs)

    any_spec = pl.BlockSpec(memory_space=pl.ANY)
    n = _GatherShards.n_sems
    return _pcall(
        body, name="all_gather_weights", out_shape=jax.ShapeDtypeStruct(w.shape, w.dtype),
        in_specs=[any_spec], out_specs=any_spec, input_output_aliases={0: 0},
        scratch_shapes=[pltpu.SemaphoreType.DMA((n,)), pltpu.SemaphoreType.DMA((n,))],
        compiler_params=_cparams(),
    )(w)


class _SiblingHalf:
    n_sems = 1

    @staticmethod
    def out_shape(g):
        return jax.ShapeDtypeStruct((g.shape[0], g.shape[1] // 2, g.shape[2]), g.dtype)

    @staticmethod
    def copies(g_ref, r_ref, send_sems, recv_sems, base):
        x, y, c = _my_pos()
        rows = r_ref.shape[1]
        return [pltpu.make_async_remote_copy(
            src_ref=g_ref.at[:, pl.ds((1 - c) * rows, rows)], dst_ref=r_ref,
            send_sem=send_sems.at[base], recv_sem=recv_sems.at[base], device_id=(x, y, 1 - c), device_id_type=MESH)]


class _ChipBlocks:
    n_sems = 3

    @staticmethod
    def out_shape(p):
        return jax.ShapeDtypeStruct(p.shape, p.dtype)

    @staticmethod
    def copies(p_ref, r_ref, send_sems, recv_sems, base):
        x, y, c = _my_pos()
        return [pltpu.make_async_remote_copy(
            src_ref=p_ref.at[j], dst_ref=r_ref.at[j], send_sem=send_sems.at[base + j], recv_sem=recv_sems.at[base + j],
            device_id=(*chip, c), device_id_type=MESH) for j, chip in enumerate(_other_chips(x, y))]


def _plan_copies(plan, src_refs, dst_refs, send_sems, recv_sems):
    cps = []
    for b, (s_ref, d_ref) in enumerate(zip(src_refs, dst_refs)):
        cps += plan.copies(s_ref, d_ref, send_sems, recv_sems, b * plan.n_sems)
    return cps


def _plan_sems(plan, n):
    return [pltpu.SemaphoreType.DMA((n * plan.n_sems,)), pltpu.SemaphoreType.DMA((n * plan.n_sems,))]


def _exchange(plan, src, name):
    def body(s_ref, o_ref, send_sems, recv_sems):
        cps = _plan_copies(plan, [s_ref], [o_ref], send_sems, recv_sems)
        for cp in cps:
            cp.start()
        for cp in cps:
            cp.wait()

    any_spec = pl.BlockSpec(memory_space=pl.ANY)
    return _pcall(
        body, name=name, out_shape=plan.out_shape(src), in_specs=[any_spec], out_specs=any_spec,
        scratch_shapes=_plan_sems(plan, 1), compiler_params=_cparams(),
    )(src)


def _carried_start(plan, first, comm):
    @pl.when(first)
    def _():
        for cp in _plan_copies(plan, *comm):
            cp.start()


def _carried_wait(plan, last, comm):
    @pl.when(last)
    def _():
        for cp in _plan_copies(plan, *comm):
            cp.wait()


def _rs_join_halves(fa, fb):
    bufs = (fa, fb)
    nb = len(bufs)

    def body(a_ref, b_ref, ao_ref, bo_ref, send_sems, recv_sems):
        x, y, c = _my_pos()
        srcs, outs = (a_ref, b_ref), (ao_ref, bo_ref)
        cps = []
        for b in range(nb):
            rows = srcs[b].shape[0] // 2
            cp = pltpu.make_async_remote_copy(
                src_ref=srcs[b].at[pl.ds(c * rows, rows)], dst_ref=outs[b].at[pl.ds(c * rows, rows)],
                send_sem=send_sems.at[b], recv_sem=recv_sems.at[b], device_id=(x, y, 1 - c), device_id_type=MESH)
            cp.start()
            cps.append(cp)
        for b in range(nb):
            rows = srcs[b].shape[0] // 2
            theirs = outs[b].at[pl.ds((1 - c) * rows, rows)]
            pltpu.make_async_remote_copy(
                src_ref=theirs, dst_ref=theirs, send_sem=send_sems.at[b], recv_sem=recv_sems.at[b],
                device_id=(x, y, 1 - c), device_id_type=MESH).wait_recv()
        for cp in cps:
            cp.wait_send()

    any_spec = pl.BlockSpec(memory_space=pl.ANY)
    return _pcall(
        body, name="rs_join_halves",
        out_shape=[jax.ShapeDtypeStruct(b.shape, b.dtype) for b in bufs],
        in_specs=[any_spec] * nb, out_specs=[any_spec] * nb, input_output_aliases={0: 0, 1: 1},
        scratch_shapes=[pltpu.SemaphoreType.DMA((nb,)), pltpu.SemaphoreType.DMA((nb,))],
        compiler_params=_cparams(),
    )(*bufs)


def _add_halves(g, r, idx, tr, name):
    _, rows, w = r.shape
    nt = rows // tr

    def body(i_ref, g_ref, r_ref, o_ref):
        o_ref[...] = (g_ref[...] + r_ref[...]).astype(BF16)

    grid_spec = pltpu.PrefetchScalarGridSpec(
        num_scalar_prefetch=1, grid=(3, nt),
        in_specs=[pl.BlockSpec((None, tr, w), lambda j, i, ix: (ix[1 + j], ix[0] * nt + i, 0)),
                  pl.BlockSpec((None, tr, w), lambda j, i, ix: (ix[1 + j], i, 0))],
        out_specs=pl.BlockSpec((None, tr, w), lambda j, i, ix: (j, i, 0)))
    return _pcall(body, name=name, grid_spec=grid_spec,
                  out_shape=jax.ShapeDtypeStruct((3, rows, w), BF16), compiler_params=_seq(2))(idx, g, r)


def _add_chips(g, r, r3, idx, tr, name):
    _, rows, w = r.shape
    nt = rows // tr

    def body(i_ref, g_ref, r_ref, a_ref, b_ref, c_ref, o_ref):
        own = g_ref[...] + r_ref[...]
        o_ref[...] = ((own + a_ref[...].astype(F32)) + b_ref[...].astype(F32)) + c_ref[...].astype(F32)

    def other(j):
        return pl.BlockSpec((None, tr, w), lambda i, ix: (j, i, 0))

    grid_spec = pltpu.PrefetchScalarGridSpec(
        num_scalar_prefetch=1, grid=(nt,),
        in_specs=[pl.BlockSpec((None, tr, w), lambda i, ix: (ix[0], ix[1] * nt + i, 0)),
                  pl.BlockSpec((None, tr, w), lambda i, ix: (ix[0], i, 0)), other(0), other(1), other(2)],
        out_specs=pl.BlockSpec((tr, w), lambda i, ix: (ix[1] * nt + i, 0)))
    return _pcall(body, name=name, grid_spec=grid_spec,
                  out_shape=jax.ShapeDtypeStruct((2 * rows, w), F32), compiler_params=_seq(1))(idx, g, r, r3, r3, r3)


def _ada_fwd(c_all, w_sh, b_sh):
    def body(c_ref, w_ref, b_ref, o_ref):
        cv = c_ref[...]
        ca = cv * _sig(cv)
        o_ref[...] = jnp.dot(ca, w_ref[...], preferred_element_type=F32, precision=lax.Precision.HIGHEST) + b_ref[...]

    return _pcall(body, name="ada_fwd", out_shape=jax.ShapeDtypeStruct((c_all.shape[0], w_sh.shape[1]), F32),
                  compiler_params=_cparams())(c_all, w_sh, b_sh)


def _ada_bwd(c_all_t, dmod_sh):
    def body(c_ref, d_ref, o_ref):
        cv = c_ref[...]
        ca = cv * _sig(cv)
        o_ref[...] = jnp.dot(ca, d_ref[...], preferred_element_type=F32, precision=lax.Precision.HIGHEST)

    return _pcall(body, name="ada_bwd", out_shape=jax.ShapeDtypeStruct((c_all_t.shape[0], dmod_sh.shape[1]), F32),
                  compiler_params=_cparams())(c_all_t, dmod_sh)


def _conv31_causal(ext_ref, sh_ref, w_ref, bias, out_ref, tm, d, rc):
    off = HALO_A - (CONV_A - 1)
    n = tm + HALO_A - SUBLANES
    for s in range(SUBLANES):
        if s == 0:
            src = ext_ref
        else:
            sh_ref[0:n, :] = ext_ref[s:s + n, :]
            src = sh_ref
        taps = [k for k in range(CONV_A) if (off + k) % SUBLANES == s]
        for r0 in range(0, tm, rc):
            acc = jnp.broadcast_to(bias, (rc, d)) if s == 0 else out_ref[r0:r0 + rc, :]
            for k in taps:
                a = r0 + (off + k) // SUBLANES * SUBLANES
                acc = acc + w_ref[k:k + 1, :] * src[a:a + rc, :]
            out_ref[r0:r0 + rc, :] = acc


def _conv31_adjoint(dp_ref, sh_ref, ext_ref, w_ref, dx_ref, dw_ref, tm, d, rc):
    off = HALO_A - (CONV_A - 1)
    lead = SUBLANES + CONV_A - 1
    n = tm + HALO_A
    row = lax.broadcasted_iota(jnp.int32, (SUBLANES, d), 0)
    for s in range(SUBLANES):
        if s == 0:
            src = dp_ref
        else:
            sh_ref[0:n, :] = dp_ref[s:s + n, :]
            src = sh_ref
        taps = [k for k in range(CONV_A) if (lead - k) % SUBLANES == s]
        for r0 in range(0, tm, rc):
            acc = jnp.zeros((rc, d), F32) if s == 0 else dx_ref[r0:r0 + rc, :]
            for k in taps:
                a = r0 + (lead - k) // SUBLANES * SUBLANES
                acc = acc + w_ref[k:k + 1, :] * src[a:a + rc, :]
            dx_ref[r0:r0 + rc, :] = acc
        for k in range(CONV_A):
            if (-(off + k)) % SUBLANES != s:
                continue
            e = s + off + k - SUBLANES
            tot = _rsum(src[0:tm, :] * ext_ref[e:e + tm, :])
            tail = src[tm:tm + SUBLANES, :] * ext_ref[tm + e:tm + e + SUBLANES, :]
            tot = tot + _rsum(jnp.where(row < SUBLANES - s, tail, 0.0))
            dw_ref[k:k + 1, :] += tot


def _conv3_causal(ext_ref, w_ref, out_ref, tm, d, rc):
    off = HALO_B - (CONV_B - 1)
    for r0 in range(0, tm, rc):
        acc = jnp.zeros((rc, d), F32)
        for k in range(CONV_B):
            acc = acc + w_ref[k:k + 1, :] * ext_ref[r0 + off + k:r0 + off + k + rc, :]
        out_ref[r0:r0 + rc, :] = acc


def _conv3_anticausal(ext_ref, w_ref, out_ref, tm, d, rc):
    for r0 in range(0, tm, rc):
        acc = jnp.zeros((rc, d), F32)
        for k in range(CONV_B):
            o = CONV_B - 1 - k
            acc = acc + w_ref[k:k + 1, :] * ext_ref[r0 + o:r0 + o + rc, :]
        out_ref[r0:r0 + rc, :] = acc


def _fwd_in(x, vecs, wa, b_in, wb, tm):
    s, d = x.shape
    nq, _, nw = wa.shape
    nt = s // tm

    def body(x_ref, v_ref, w_ref, b_ref, wbi_ref, h_ref, z_ref, wbo_ref, send_sems, recv_sems):
        i = pl.program_id(0)
        gather = (wbi_ref, wbo_ref, send_sems, recv_sems)
        pl.when(i == 0)(lambda: _GatherShards.start(*gather))
        pl.when(i == nt // 2)(lambda: _GatherShards.relay(*gather))
        xh, _ = _ln(x_ref[...])
        h = (xh * (1.0 + v_ref[V_SCALE1:V_SCALE1 + 1, :]) + v_ref[V_SHIFT1:V_SHIFT1 + 1, :]).astype(BF16)
        h_ref[...] = h
        for q in range(nq):
            z_ref[:, q * nw:(q + 1) * nw] = _dot(h, w_ref[q]) + b_ref[:, q * nw:(q + 1) * nw]
        pl.when(i == nt - 1)(lambda: _GatherShards.finish(*gather))

    any_spec = pl.BlockSpec(memory_space=pl.ANY)
    n = _GatherShards.n_sems
    return _pcall(
        body, name="fwd_in", grid=(nt,),
        in_specs=[_rows(tm, d), _full(vecs.shape), _full(wa.shape, single=True), _full(b_in.shape), any_spec],
        out_specs=[_rows(tm, d), _rows(tm, nq * nw), any_spec],
        out_shape=[jax.ShapeDtypeStruct((s, d), BF16), jax.ShapeDtypeStruct((s, nq * nw), F32),
                   jax.ShapeDtypeStruct(wb.shape, wb.dtype)],
        input_output_aliases={4: 2},
        scratch_shapes=[pltpu.SemaphoreType.DMA((n,)), pltpu.SemaphoreType.DMA((n,))],
        compiler_params=_seq(1),
    )(x, vecs, wa, b_in, wb)


def _fwd_mix(z, vecs, caw, cbw, wb, tm, rc):
    s = z.shape[0]
    d = vecs.shape[1]
    nq = wb.shape[0]
    kq = d // nq
    base = 2 * d // kq

    def body(z_ref, v_ref, caw_ref, cbw_ref, wao_ref, wbo_ref, wo_ref,
             u1_ref, ya_ref, yb_ref, o1_ref, u3_ref, vv_ref, mg_ref, ext_ref, sh_ref, pext_ref, q_ref):
        @pl.when(pl.program_id(0) == 0)
        def _():
            ext_ref[0:HALO_A, :] = jnp.zeros((HALO_A, d), F32)
            pext_ref[0:HALO_B, :] = jnp.zeros((HALO_B, d), F32)

        ext_ref[HALO_A:HALO_A + tm, :] = z_ref[:, 0:d] * _sig(z_ref[:, d:2 * d])
        _conv31_causal(ext_ref, sh_ref, caw_ref, v_ref[V_CAB:V_CAB + 1, :], u1_ref, tm, d, rc)
        ext_ref[0:HALO_A, :] = ext_ref[tm:tm + HALO_A, :]
        xa, _ = _ln(u1_ref[...])
        u2 = xa * v_ref[V_LNAG:V_LNAG + 1, :] + v_ref[V_LNAB:V_LNAB + 1, :]
        u3 = (u2 * _sig(u2)).astype(BF16)
        u3_ref[...] = u3
        ya = jnp.broadcast_to(v_ref[V_BAO:V_BAO + 1, :], (tm, d))
        for q in range(nq):
            ya = ya + _dot(u3[:, q * kq:(q + 1) * kq], wao_ref[q])
        ya_ref[...] = ya

        pext_ref[HALO_B:HALO_B + tm, :] = z_ref[:, 3 * d:4 * d] * z_ref[:, 4 * d:5 * d]
        _conv3_causal(pext_ref, cbw_ref, q_ref, tm, d, rc)
        pext_ref[0:HALO_B, :] = pext_ref[tm:tm + HALO_B, :]
        vv = (z_ref[:, 2 * d:3 * d] * q_ref[...]).astype(BF16)
        vv_ref[...] = vv
        yb = jnp.zeros((tm, d), F32)
        for q in range(nq):
            yb = yb + _dot(vv[:, q * kq:(q + 1) * kq], wbo_ref[q])
        yb_ref[...] = yb

        mg = (_sig(z_ref[:, 5 * d:6 * d]) * ya + _sig(z_ref[:, 6 * d:7 * d]) * yb).astype(BF16)
        mg_ref[...] = mg
        o1 = jnp.broadcast_to(v_ref[V_BO:V_BO + 1, :], (tm, d))
        for q in range(nq):
            o1 = o1 + _dot(mg[:, q * kq:(q + 1) * kq], wo_ref[q])
        o1_ref[...] = o1

    def wspec(j):
        return pl.BlockSpec((nq, kq, d), lambda i: (0, base + j, 0), pipeline_mode=pl.Buffered(1))

    f32o = jax.ShapeDtypeStruct((s, d), F32)
    b16o = jax.ShapeDtypeStruct((s, d), BF16)
    return _pcall(
        body, name="fwd_mix", grid=(s // tm,),
        in_specs=[_rows(tm, 7 * d), _full(vecs.shape), _full(caw.shape), _full(cbw.shape), wspec(0), wspec(1), wspec(2)],
        out_specs=[_rows(tm, d)] * 7,
        out_shape=[f32o, f32o, f32o, f32o, b16o, b16o, b16o],
        scratch_shapes=[pltpu.VMEM((HALO_A + tm, d), F32), pltpu.VMEM((HALO_A + tm, d), F32),
                        pltpu.VMEM((HALO_B + tm, d), F32), pltpu.VMEM((tm, d), F32)],
        compiler_params=_seq(1),
    )(z, vecs, caw, cbw, wb, wb, wb)


def _mlp_fwd_bwd(x, out1, tgt, vecs, b_up, wb, tm):
    s, d = x.shape
    nq = wb.shape[0]
    dff = nq * d

    def body(x_ref, o1_ref, t_ref, v_ref, bup_ref, wup_ref, wdn_ref,
             h2_ref, f_ref, df0_ref, do2_ref, do1_ref, dxp_ref, acc_ref, dbup_ref, f0_ref):
        @pl.when(pl.program_id(0) == 0)
        def _():
            acc_ref[...] = jnp.zeros(acc_ref.shape, F32)
            dbup_ref[...] = jnp.zeros(dbup_ref.shape, F32)

        def vec(r):
            return v_ref[r:r + 1, :]

        def accum(r, val):
            acc_ref[r:r + 1, :] += _rsum(val)

        out1v = o1_ref[...]
        r1 = ALPHA * x_ref[...] + (1.0 + vec(V_GATE1)) * out1v
        xh1, rstd1 = _ln(r1)
        x1 = xh1 * vec(V_LN1G) + vec(V_LN1B)
        xn1, rstdn = _ln(x1)
        h2 = (xn1 * (1.0 + vec(V_SCALE2)) + vec(V_SHIFT2)).astype(BF16)
        h2_ref[...] = h2
        out2 = jnp.broadcast_to(vec(V_BDN), (tm, d))
        for q in range(nq):
            f0 = _dot(h2, wup_ref[q]) + bup_ref[:, q * d:(q + 1) * d]
            rl = jnp.maximum(f0, 0.0)
            f0_ref[:, q * d:(q + 1) * d] = rl
            fb = (rl * rl).astype(BF16)
            f_ref[:, q * d:(q + 1) * d] = fb
            out2 = out2 + _dot(fb, wdn_ref[q])
        r2 = ALPHA * x1 + (1.0 + vec(V_GATE2)) * out2
        xh2, rstd2 = _ln(r2)
        yv = xh2 * vec(V_LN2G) + vec(V_LN2B)
        err = yv - t_ref[...]
        accum(M_LOSS, err * err)
        dy = err * (1.0 / d)
        accum(M_LN2G, dy * xh2)
        accum(M_LN2B, dy)
        dr2 = _ln_bwd(dy * vec(V_LN2G), xh2, rstd2)
        accum(M_GATE2, dr2 * out2)
        dout2 = (1.0 + vec(V_GATE2)) * dr2
        accum(M_BDN, dout2)
        do2b = dout2.astype(BF16)
        do2_ref[...] = do2b
        dh2 = jnp.zeros((tm, d), F32)
        for q in range(nq):
            df0 = _dot_nt(do2b, wdn_ref[q]) * (2.0 * f0_ref[:, q * d:(q + 1) * d])
            dbup_ref[q:q + 1, :] += _rsum(df0)
            df0b = df0.astype(BF16)
            df0_ref[:, q * d:(q + 1) * d] = df0b
            dh2 = dh2 + _dot_nt(df0b, wup_ref[q])
        accum(M_SHIFT2, dh2)
        accum(M_SCALE2, dh2 * xn1)
        dx1 = ALPHA * dr2 + _ln_bwd(dh2 * (1.0 + vec(V_SCALE2)), xn1, rstdn)
        accum(M_LN1G, dx1 * xh1)
        accum(M_LN1B, dx1)
        dr1 = _ln_bwd(dx1 * vec(V_LN1G), xh1, rstd1)
        accum(M_GATE1, dr1 * out1v)
        dout1 = (1.0 + vec(V_GATE1)) * dr1
        accum(M_BO, dout1)
        do1_ref[...] = dout1.astype(BF16)
        dxp_ref[...] = ALPHA * dr1

    def wspec(j):
        return pl.BlockSpec((nq, d, d), lambda i: (0, j, 0), pipeline_mode=pl.Buffered(1))

    b16 = lambda w: jax.ShapeDtypeStruct((s, w), BF16)
    return _pcall(
        body, name="mlp_fwd_bwd", grid=(s // tm,),
        in_specs=[_rows(tm, d), _rows(tm, d), _rows(tm, d), _full(vecs.shape), _full(b_up.shape), wspec(0), wspec(1)],
        out_specs=[_rows(tm, d), _rows(tm, dff), _rows(tm, dff), _rows(tm, d), _rows(tm, d), _rows(tm, d),
                   _full((16, d)), _full((SUBLANES, d))],
        out_shape=[b16(d), b16(dff), b16(dff), b16(d), b16(d), jax.ShapeDtypeStruct((s, d), F32),
                   jax.ShapeDtypeStruct((16, d), F32), jax.ShapeDtypeStruct((SUBLANES, d), F32)],
        scratch_shapes=[pltpu.VMEM((tm, dff), F32)],
        compiler_params=_seq(1),
    )(x, out1, tgt, vecs, b_up, wb, wb)


def _mix_bwd(dout1, z, u1, ya, yb, vecs, caw, cbw, wb, tm, rc):
    s = z.shape[0]
    d = vecs.shape[1]
    nq = wb.shape[0]
    kq = d // nq
    base = 2 * d // kq
    nt = s // tm
    hb = tm // HALO_A

    def body(do1_ref, z_ref, zh_ref, u1_ref, ya_ref, yb_ref, v_ref, caw_ref, cbw_ref, wao_ref, wbo_ref, wo_ref,
             dz_ref, dya_ref, dyb_ref, acc_ref, dcaw_ref, dcbw_ref, dbin_ref,
             ext_ref, du1p_ref, sh_ref, pext_ref, dqe_ref, tmp_ref):
        i = pl.program_id(0)

        @pl.when(i == 0)
        def _():
            acc_ref[...] = jnp.zeros(acc_ref.shape, F32)
            dcaw_ref[...] = jnp.zeros(dcaw_ref.shape, F32)
            dcbw_ref[...] = jnp.zeros(dcbw_ref.shape, F32)
            dbin_ref[...] = jnp.zeros(dbin_ref.shape, F32)
            du1p_ref[0:SUBLANES, :] = jnp.zeros((SUBLANES, d), F32)
            du1p_ref[SUBLANES + tm:SUBLANES + tm + HALO_A, :] = jnp.zeros((HALO_A, d), F32)
            dqe_ref[tm:tm + HALO_B, :] = jnp.zeros((HALO_B, d), F32)

        def vec(r):
            return v_ref[r:r + 1, :]

        def accum(r, val):
            acc_ref[r:r + 1, :] += _rsum(val)

        def put_dz(j, val):
            dbin_ref[j:j + 1, :] += _rsum(val)
            dz_ref[:, j * d:(j + 1) * d] = val.astype(BF16)

        has_history = i < nt - 1

        do1 = do1_ref[...]
        dmg = jnp.concatenate([_dot_nt(do1, wo_ref[q]) for q in range(nq)], axis=1)
        sga = _sig(z_ref[:, 5 * d:6 * d])
        sgb = _sig(z_ref[:, 6 * d:7 * d])
        dya = dmg * sga
        dyb = dmg * sgb
        accum(X_BAO, dya)
        put_dz(5, dya * ya_ref[...] * (1.0 - sga))
        put_dz(6, dyb * yb_ref[...] * (1.0 - sgb))
        dyab = dya.astype(BF16)
        dybb = dyb.astype(BF16)
        dya_ref[...] = dyab
        dyb_ref[...] = dybb

        du3 = jnp.concatenate([_dot_nt(dyab, wao_ref[q]) for q in range(nq)], axis=1)
        xa, rstda = _ln(u1_ref[...])
        u2 = xa * vec(V_LNAG) + vec(V_LNAB)
        s2 = _sig(u2)
        du2 = du3 * (s2 * (1.0 + u2 * (1.0 - s2)))
        accum(X_LNAG, du2 * xa)
        accum(X_LNAB, du2)
        du1 = _ln_bwd(du2 * vec(V_LNAG), xa, rstda)
        accum(X_CAB, du1)
        du1p_ref[SUBLANES:SUBLANES + tm, :] = du1
        sg = _sig(z_ref[:, d:2 * d])
        aval = z_ref[:, 0:d]
        ext_ref[HALO_A:HALO_A + tm, :] = aval * sg
        ext_ref[0:HALO_A, :] = jnp.where(has_history, zh_ref[:, 0:d] * _sig(zh_ref[:, d:2 * d]), 0.0)
        _conv31_adjoint(du1p_ref, sh_ref, ext_ref, caw_ref, tmp_ref, dcaw_ref, tm, d, rc)
        du1p_ref[SUBLANES + tm:SUBLANES + tm + HALO_A, :] = du1p_ref[SUBLANES:SUBLANES + HALO_A, :]
        du0 = tmp_ref[...]
        put_dz(0, du0 * sg)
        put_dz(1, du0 * aval * sg * (1.0 - sg))

        dv = jnp.concatenate([_dot_nt(dybb, wbo_ref[q]) for q in range(nq)], axis=1)
        bgc = z_ref[:, 3 * d:4 * d]
        bx = z_ref[:, 4 * d:5 * d]
        pext_ref[HALO_B:HALO_B + tm, :] = bgc * bx
        pext_ref[0:HALO_B, :] = jnp.where(
            has_history, zh_ref[HALO_A - HALO_B:HALO_A, 3 * d:4 * d] * zh_ref[HALO_A - HALO_B:HALO_A, 4 * d:5 * d], 0.0)
        _conv3_causal(pext_ref, cbw_ref, tmp_ref, tm, d, rc)
        put_dz(2, dv * tmp_ref[...])
        dq = dv * z_ref[:, 2 * d:3 * d]
        dqe_ref[0:tm, :] = dq
        offb = HALO_B - (CONV_B - 1)
        for k in range(CONV_B):
            dcbw_ref[k:k + 1, :] += _rsum(dq * pext_ref[offb + k:offb + k + tm, :])
        _conv3_anticausal(dqe_ref, cbw_ref, tmp_ref, tm, d, rc)
        dqe_ref[tm:tm + HALO_B, :] = dqe_ref[0:HALO_B, :]
        dp = tmp_ref[...]
        put_dz(3, dp * bx)
        put_dz(4, dp * bgc)

    def rev(width):
        return pl.BlockSpec((tm, width), lambda i: (nt - 1 - i, 0))

    def wspec(j):
        return pl.BlockSpec((nq, kq, d), lambda i: (0, base + j, 0), pipeline_mode=pl.Buffered(1))

    halo = pl.BlockSpec((HALO_A, 7 * d), lambda i: (jnp.maximum((nt - 1 - i) * hb - 1, 0), 0))
    b16 = jax.ShapeDtypeStruct((s, d), BF16)
    acc8 = jax.ShapeDtypeStruct((SUBLANES, d), F32)
    return _pcall(
        body, name="mix_bwd", grid=(nt,),
        in_specs=[rev(d), rev(7 * d), halo, rev(d), rev(d), rev(d), _full(vecs.shape), _full(caw.shape), _full(cbw.shape),
                  wspec(0), wspec(1), wspec(2)],
        out_specs=[rev(7 * d), rev(d), rev(d), _full((SUBLANES, d)), _full((HALO_A, d)), _full((HALO_B, d)),
                   _full((SUBLANES, d))],
        out_shape=[jax.ShapeDtypeStruct((s, 7 * d), BF16), b16, b16, acc8,
                   jax.ShapeDtypeStruct((HALO_A, d), F32), jax.ShapeDtypeStruct((HALO_B, d), F32), acc8],
        scratch_shapes=[pltpu.VMEM((HALO_A + tm, d), F32), pltpu.VMEM((SUBLANES + tm + HALO_A, d), F32),
                        pltpu.VMEM((tm + HALO_A, d), F32), pltpu.VMEM((HALO_B + tm, d), F32),
                        pltpu.VMEM((tm + HALO_B, d), F32), pltpu.VMEM((tm, d), F32)],
        compiler_params=_seq(1),
    )(dout1, z, z, u1, ya, yb, vecs, caw, cbw, wb, wb, wb)


def _in_bwd(dz, x, dxp, vecs, wa, tm, plan, plan_srcs):
    s, d = x.shape
    nq, _, nw = wa.shape
    nt = s // tm
    nc = len(plan_srcs)

    def body(dz_ref, x_ref, dxp_ref, v_ref, w_ref, *rest):
        src_refs, (gx_ref, acc_ref), dst_refs = rest[:nc], rest[nc:nc + 2], rest[nc + 2:2 * nc + 2]
        send_sems, recv_sems = rest[2 * nc + 2:]
        i = pl.program_id(0)
        comm = (src_refs, dst_refs, send_sems, recv_sems)
        _carried_start(plan, i == 0, comm)

        @pl.when(i == 0)
        def _():
            acc_ref[...] = jnp.zeros(acc_ref.shape, F32)

        dh1 = jnp.zeros((tm, d), F32)
        for q in range(nq):
            dh1 = dh1 + _dot_nt(dz_ref[:, q * nw:(q + 1) * nw], w_ref[q])
        xh, rstd = _ln(x_ref[...])
        acc_ref[I_SHIFT1:I_SHIFT1 + 1, :] += _rsum(dh1)
        acc_ref[I_SCALE1:I_SCALE1 + 1, :] += _rsum(dh1 * xh)
        gx_ref[...] = dxp_ref[...] + _ln_bwd(dh1 * (1.0 + v_ref[V_SCALE1:V_SCALE1 + 1, :]), xh, rstd)
        _carried_wait(plan, i == nt - 1, comm)

    any_spec = pl.BlockSpec(memory_space=pl.ANY)
    return _pcall(
        body, name="in_bwd", grid=(nt,),
        in_specs=[_rows(tm, nq * nw), _rows(tm, d), _rows(tm, d), _full(vecs.shape), _full(wa.shape, single=True)]
        + [any_spec] * nc,
        out_specs=[_rows(tm, d), _full((SUBLANES, d))] + [any_spec] * nc,
        out_shape=[jax.ShapeDtypeStruct((s, d), F32), jax.ShapeDtypeStruct((SUBLANES, d), F32)]
        + [plan.out_shape(p) for p in plan_srcs],
        scratch_shapes=_plan_sems(plan, nc),
        compiler_params=_seq(1),
    )(dz, x, dxp, vecs, wa, *plan_srcs)


def _dw(a, b, split_a, ts, name, into=None, rows_total=None, row_block=0):
    s = a.shape[0]
    ka = a.shape[1] // N_CHIPS if split_a else a.shape[1]
    nb = b.shape[1] if split_a else b.shape[1] // N_CHIPS
    rows_total = ka if rows_total is None else rows_total

    def body(a_ref, b_ref, *rest):
        o_ref = rest[-1]

        @pl.when(pl.program_id(1) == 0)
        def _():
            o_ref[...] = jnp.zeros(o_ref.shape, F32)

        o_ref[...] += _dot_tn(a_ref[...], b_ref[...])

    a_spec = pl.BlockSpec((ts, ka), (lambda q, i: (i, q)) if split_a else (lambda q, i: (i, 0)))
    b_spec = pl.BlockSpec((ts, nb), (lambda q, i: (i, 0)) if split_a else (lambda q, i: (i, q)))
    extra = {} if into is None else dict(input_output_aliases={2: 0})
    return _pcall(
        body, name=name, grid=(N_CHIPS, s // ts),
        in_specs=[a_spec, b_spec] + ([] if into is None else [pl.BlockSpec(memory_space=pl.ANY)]),
        out_specs=pl.BlockSpec((None, ka, nb), lambda q, i: (q, row_block, 0)),
        out_shape=jax.ShapeDtypeStruct((N_CHIPS, rows_total, nb), F32),
        compiler_params=_seq(2), **extra,
    )(*((a, b) if into is None else (a, b, into)))


def _dw_carrying(a, b, ts, name, plan, plan_src):
    s, k = a.shape
    nb = b.shape[1] // N_CHIPS
    ns = s // ts

    def body(a_ref, b_ref, src_ref, o_ref, dst_ref, send_sems, recv_sems):
        q, i = pl.program_id(0), pl.program_id(1)
        comm = ([src_ref], [dst_ref], send_sems, recv_sems)
        _carried_start(plan, jnp.logical_and(q == 0, i == 0), comm)

        @pl.when(i == 0)
        def _():
            o_ref[...] = jnp.zeros(o_ref.shape, F32)

        o_ref[...] += _dot_tn(a_ref[...], b_ref[...])
        _carried_wait(plan, jnp.logical_and(q == N_CHIPS - 1, i == ns - 1), comm)

    any_spec = pl.BlockSpec(memory_space=pl.ANY)
    return _pcall(
        body, name=name, grid=(N_CHIPS, ns),
        in_specs=[pl.BlockSpec((ts, k), lambda q, i: (i, 0)), pl.BlockSpec((ts, nb), lambda q, i: (i, q)), any_spec],
        out_specs=[pl.BlockSpec((None, k, nb), lambda q, i: (q, 0, 0)), any_spec],
        out_shape=[jax.ShapeDtypeStruct((N_CHIPS, k, nb), F32), plan.out_shape(plan_src)],
        scratch_shapes=_plan_sems(plan, 1),
        compiler_params=_seq(2),
    )(a, b, plan_src)


def _dw_rows(a, b, ts, name, into, row_block):
    s, k = a.shape
    n = b.shape[1]
    kq = k // N_CHIPS

    def body(a_ref, b_ref, buf_ref, o_ref):
        @pl.when(pl.program_id(0) == 0)
        def _():
            o_ref[...] = jnp.zeros(o_ref.shape, F32)

        res = _dot_tn(a_ref[...], b_ref[...])
        for q in range(N_CHIPS):
            o_ref[q] += res[q * kq:(q + 1) * kq, :]

    return _pcall(
        body, name=name, grid=(s // ts,),
        in_specs=[_rows(ts, k), _rows(ts, n), pl.BlockSpec(memory_space=pl.ANY)],
        out_specs=pl.BlockSpec((N_CHIPS, kq, n), lambda i: (0, row_block, 0)),
        out_shape=jax.ShapeDtypeStruct(into.shape, F32), input_output_aliases={2: 0},
        compiler_params=_seq(1),
    )(a, b, into)


def _adam_math(w, g, m, v):
    m2 = ADAM_B1 * m + (1.0 - ADAM_B1) * g
    v2 = ADAM_B2 * v + (1.0 - ADAM_B2) * (g * g)
    m_hat = m2 / (1.0 - ADAM_B1 ** ADAM_STEP)
    v_hat = v2 / (1.0 - ADAM_B2 ** ADAM_STEP)
    delta = -ADAM_LR * (m_hat / (jnp.sqrt(v_hat) + ADAM_EPS) + ADAM_WD * w)
    return delta, m2, v2


def _adam(w, g, m, v, g_row0, tr, name):
    r, c = w.shape
    blk0 = g_row0 // tr

    def body(w_ref, g_ref, m_ref, v_ref, go_ref, d_ref, mo_ref, vo_ref):
        gv = g_ref[...]
        go_ref[...] = gv
        d_ref[...], mo_ref[...], vo_ref[...] = _adam_math(w_ref[...], gv, m_ref[...], v_ref[...])

    spec = _rows(tr, c)
    g_spec = pl.BlockSpec((tr, c), lambda i: (blk0 + i, 0))
    o = jax.ShapeDtypeStruct((r, c), F32)
    return _pcall(body, name=name, grid=(r // tr,), in_specs=[spec, g_spec, spec, spec], out_specs=[spec] * 4,
                  out_shape=[o, o, o, o], compiler_params=_seq(1))(w, g, m, v)


def _small_update(gathered, q_idx, small_w, small_m, small_v, conv_w, conv_m, conv_v):
    d = gathered.shape[2]
    ns = len(_SMALL)
    cw = conv_w[0].shape[1]
    conv_rows = ((T_CAW, CONV_A), (T_CBW, CONV_B))

    def body(q_ref, g_ref, *refs):
        ins, outs = refs[:3 * (ns + 2)], refs[3 * (ns + 2):]
        tot_ref, loss_ref = outs[0], outs[1]
        outs = outs[2:]
        tot = g_ref[0]
        for dev in range(1, N_DEV):
            tot = tot + g_ref[dev]
        tot_ref[...] = tot
        loss_ref[...] = (0.5 / d) * jnp.sum(tot_ref[T_LOSS:T_LOSS + 1, :], axis=1, keepdims=True)
        for p, (_, rows) in enumerate(_SMALL):
            w_ref, m_ref, v_ref = ins[p], ins[ns + 2 + p], ins[2 * (ns + 2) + p]
            go, do, mo, vo = outs[4 * p:4 * p + 4]
            for j, row in enumerate(rows):
                sl = slice(j * d, (j + 1) * d)
                gv = tot_ref[row:row + 1, :]
                go[:, sl] = gv
                do[:, sl], mo[:, sl], vo[:, sl] = _adam_math(w_ref[:, sl], gv, m_ref[:, sl], v_ref[:, sl])
        for p, (row, taps) in enumerate(conv_rows):
            w_ref, m_ref, v_ref = ins[ns + p], ins[ns + 2 + ns + p], ins[2 * (ns + 2) + ns + p]
            go, do, mo, vo = outs[4 * (ns + p):4 * (ns + p) + 4]
            gv = tot_ref[row:row + taps, 0:cw]
            for qq in range(1, N_CHIPS):
                gv = jnp.where(q_ref[0] == qq, tot_ref[row:row + taps, qq * cw:(qq + 1) * cw], gv)
            go[...] = gv
            do[...], mo[...], vo[...] = _adam_math(w_ref[...], gv, m_ref[...], v_ref[...])

    params = list(small_w) + list(conv_w) + list(small_m) + list(conv_m) + list(small_v) + list(conv_v)
    out_shape = [jax.ShapeDtypeStruct((T_ROWS, d), F32), jax.ShapeDtypeStruct((1, 1), F32)]
    for w in list(small_w) + list(conv_w):
        out_shape += [jax.ShapeDtypeStruct(w.shape, F32)] * 4
    vm = pl.BlockSpec(memory_space=pltpu.VMEM)
    return _pcall(
        body, name="small_update", out_shape=out_shape,
        in_specs=[pl.BlockSpec(memory_space=pltpu.SMEM), vm] + [vm] * len(params),
        out_specs=[vm] * len(out_shape), compiler_params=_cparams(),
    )(q_idx, gathered, *params)


def kernel(x, c, w_ada, b_ada, w_in, b_in, conv_a_w, conv_a_b, ln_a_g, ln_a_b, w_a_out, b_a_out, conv_b_w, w_b_out, w_o, b_o, ln1_g, ln1_b, w_up, b_up, w_down, b_down, ln2_g, ln2_b, loss_target, m_w_ada, m_b_ada, m_w_in, m_b_in, m_conv_a_w, m_conv_a_b, m_ln_a_g, m_ln_a_b, m_w_a_out, m_b_a_out, m_conv_b_w, m_w_b_out, m_w_o, m_b_o, m_ln1_g, m_ln1_b, m_w_up, m_b_up, m_w_down, m_b_down, m_ln2_g, m_ln2_b, v_w_ada, v_b_ada, v_w_in, v_b_in, v_conv_a_w, v_conv_a_b, v_ln_a_g, v_ln_a_b, v_w_a_out, v_b_a_out, v_conv_b_w, v_w_b_out, v_w_o, v_b_o, v_ln1_g, v_ln1_b, v_w_up, v_b_up, v_w_down, v_b_down, v_ln2_g, v_ln2_b):
    given = dict(locals())
    s, d = x.shape[1], x.shape[2]
    xi, yi, ci = _my_pos()
    q = 2 * xi + yi
    me = 4 * xi + 2 * yi + ci
    i32 = jnp.int32
    q_arr = jnp.reshape(q, (1,)).astype(i32)
    others = [2 * ox + oy for ox, oy in _other_chips(xi, yi)]
    halves_idx = jnp.stack([ci] + others).astype(i32)
    chips_idx = jnp.stack([q, ci]).astype(i32)
    kq = d // N_CHIPS
    tm = min(256, s)
    rc = min(32, tm)

    def sq(a):
        return a.reshape(a.shape[1:])

    x2, tgt = sq(x), sq(loss_target)

    n_ada = w_ada.shape[2]
    pre = jnp.concatenate([
        jnp.broadcast_to(c, (SUBLANES, d)),
        jnp.pad(sq(conv_a_w), ((0, HALO_A - CONV_A), (0, d - kq))),
        jnp.pad(sq(conv_b_w), ((0, HALO_B - CONV_B), (0, d - kq)))], axis=0)
    pre_all = _all_gather_small(pre, "gather_c_conv")
    c_all = pre_all[:, 0, :]
    caw = jnp.concatenate([pre_all[2 * p, SUBLANES:SUBLANES + HALO_A, :kq] for p in range(N_CHIPS)], axis=1)
    cbw = jnp.concatenate([pre_all[2 * p, SUBLANES + HALO_A:, :kq] for p in range(N_CHIPS)], axis=1)
    b_ada_sh = lax.dynamic_slice(b_ada, (0, q * n_ada), (1, n_ada))
    mod_part = _ada_fwd(c_all, sq(w_ada), b_ada_sh)
    mod_all = _all_gather_small(mod_part, "gather_mod")
    mod_rows = lax.dynamic_slice(mod_all, (0, me, 0), (N_DEV, 1, n_ada))[0::2, 0, :]
    mod = mod_rows.reshape(6, d)
    vecs = jnp.concatenate([mod, conv_a_b, ln_a_g, ln_a_b, b_a_out, b_o, ln1_g, ln1_b, b_down, ln2_g, ln2_b], axis=0)

    wa = _place_shard([sq(w_in)], q_arr, "place_w_in")
    wb = _place_shard([sq(w_up), sq(w_down), sq(w_a_out), sq(w_b_out), sq(w_o)], q_arr, "place_w_rest")
    wa = _all_gather_weights(wa)

    h1, z, wb = _fwd_in(x2, vecs, wa, b_in, wb, tm)
    u1, ya, yb, out1, u3, vv, mg = _fwd_mix(z, vecs, caw, cbw, wb, tm, rc)

    ts = min(2048, s)
    rest_rows = wb.shape[1]
    small0 = 2 * d // kq
    h2, fb, df0, do2, do1, dxp, macc, dbup = _mlp_fwd_bwd(x2, out1, tgt, vecs, b_up, wb, tm)
    gb = _dw(h2, df0, False, ts, "dw_up", rows_total=rest_rows)
    gb = _dw(fb, do2, True, ts, "dw_down", into=gb, rows_total=rest_rows, row_block=1)
    dz, dya, dyb, xacc, dcaw, dcbw, dbin = _mix_bwd(do1, z, u1, ya, yb, vecs, caw, cbw, wb, tm, rc)
    gb = _dw_rows(u3, dya, ts, "dw_a_out", gb, small0)
    gb = _dw_rows(vv, dyb, ts, "dw_b_out", gb, small0 + 1)
    gb = _dw_rows(mg, do1, ts, "dw_o", gb, small0 + 2)

    trb, tra = rest_rows // 8, d // 8
    ga, rb = _dw_carrying(h1, dz, ts, "dw_in", _SiblingHalf, gb)
    pb = _add_halves(gb, rb, halves_idx, trb, "rs_add_halves_rest")
    ra = _exchange(_SiblingHalf, ga, "rs_to_sibling")
    pa = _add_halves(ga, ra, halves_idx, tra, "rs_add_halves_in")
    gx, iacc, r3a, r3b = _in_bwd(dz, x2, dxp, vecs, wa, tm, _ChipBlocks, [pa, pb])
    fa = _add_chips(ga, ra, r3a, chips_idx, tra, "rs_add_chips_in")
    fb_ = _add_chips(gb, rb, r3b, chips_idx, trb, "rs_add_chips_rest")
    g_in, g_b = _rs_join_halves(fa, fb_)

    table = jnp.concatenate([iacc, macc, dbin, dcaw, xacc, dcbw, dbup], axis=0)
    gathered = _all_gather_small(table, "gather_small_grads")

    names = [n for n, _ in _SMALL]
    res = _small_update(
        gathered, q_arr,
        [given[n] for n in names], [given["m_" + n] for n in names], [given["v_" + n] for n in names],
        [sq(conv_a_w), sq(conv_b_w)], [sq(m_conv_a_w), sq(m_conv_b_w)], [sq(v_conv_a_w), sq(v_conv_b_w)])
    loss = res[1].reshape(())
    upd = {}
    for p, n in enumerate(names + ["conv_a_w", "conv_b_w"]):
        upd[n] = res[2 + 4 * p:6 + 4 * p]

    dmod_all = jnp.stack([gathered[:, r, :] for r in _SMALL[0][1]], axis=1).reshape(N_DEV, 6 * d)
    dmod_sh = lax.dynamic_slice(dmod_all, (0, q * n_ada), (N_DEV, n_ada))
    g_ada = _ada_bwd(c_all.T, dmod_sh)
    upd["w_ada"] = _adam(sq(w_ada), g_ada, sq(m_w_ada), sq(v_w_ada), 0, min(256, d), "adam_w_ada")

    upd["w_in"] = _adam(sq(w_in), g_in, sq(m_w_in), sq(v_w_in), 0, min(256, d), "adam_w_in")
    r0 = 0
    for n in ("w_up", "w_down", "w_a_out", "w_b_out", "w_o"):
        w = sq(given[n])
        upd[n] = _adam(w, g_b, sq(given["m_" + n]), sq(given["v_" + n]), r0, min(256, w.shape[0]), "adam_" + n)
        r0 += w.shape[0]

    order = ["w_ada", "b_ada", "w_in", "b_in", "conv_a_w", "conv_a_b", "ln_a_g", "ln_a_b", "w_a_out", "b_a_out", "conv_b_w",
             "w_b_out", "w_o", "b_o", "ln1_g", "ln1_b", "w_up", "b_up", "w_down", "b_down", "ln2_g", "ln2_b"]
    outs = [loss, gx.reshape(x.shape)]
    for k in range(4):
        outs += [upd[n][k].reshape(given[n].shape) for n in order]
    return tuple(outs)
```

```python
import jax
import jax.numpy as jnp
from jax import lax
from jax.experimental import pallas as pl
from jax.experimental.pallas import tpu as pltpu

F32 = jnp.float32
BF16 = jnp.bfloat16
MESH = pl.DeviceIdType.MESH

LN_EPS = 1e-5
DEPTH = 1
ALPHA = (2.0 * DEPTH) ** 0.25
CONV_A = 31
CONV_B = 3
SUBLANES = 8
HALO_A = 32
HALO_B = 8
N_CHIPS = 4
N_DEV = 8
ADAM_LR = 0.001
ADAM_B1 = 0.9
ADAM_B2 = 0.999
ADAM_EPS = 1e-08
ADAM_WD = 0.01
ADAM_STEP = 10
VMEM_LIMIT = 56 * 1024 * 1024

V_SHIFT1, V_SCALE1, V_GATE1, V_SHIFT2, V_SCALE2, V_GATE2 = 0, 1, 2, 3, 4, 5
V_CAB, V_LNAG, V_LNAB, V_BAO, V_BO, V_LN1G, V_LN1B, V_BDN, V_LN2G, V_LN2B = 6, 7, 8, 9, 10, 11, 12, 13, 14, 15

M_LN2G, M_LN2B, M_GATE2, M_BDN, M_SHIFT2, M_SCALE2, M_LN1G, M_LN1B, M_GATE1, M_BO, M_LOSS = range(11)
X_BAO, X_LNAG, X_LNAB, X_CAB = range(4)
I_SHIFT1, I_SCALE1 = 0, 1

T_I, T_M, T_BIN, T_CAW, T_X, T_CBW, T_BUP, T_ROWS = 0, 8, 24, 32, 64, 72, 80, 88
T_LOSS = T_M + M_LOSS
_SMALL = (
    ("b_ada", (T_I + I_SHIFT1, T_I + I_SCALE1, T_M + M_GATE1, T_M + M_SHIFT2, T_M + M_SCALE2, T_M + M_GATE2)),
    ("b_in", tuple(T_BIN + j for j in range(7))),
    ("conv_a_b", (T_X + X_CAB,)), ("ln_a_g", (T_X + X_LNAG,)), ("ln_a_b", (T_X + X_LNAB,)), ("b_a_out", (T_X + X_BAO,)),
    ("b_o", (T_M + M_BO,)), ("ln1_g", (T_M + M_LN1G,)), ("ln1_b", (T_M + M_LN1B,)),
    ("b_up", tuple(T_BUP + j for j in range(4))),
    ("b_down", (T_M + M_BDN,)), ("ln2_g", (T_M + M_LN2G,)), ("ln2_b", (T_M + M_LN2B,)),
)


def _pcall(body, **kw):
    return pl.pallas_call(body, **kw)


def _cparams(**kw):
    return pltpu.CompilerParams(vmem_limit_bytes=VMEM_LIMIT, **kw)


def _seq(n):
    return _cparams(dimension_semantics=("arbitrary",) * n)


def _full(shape, single=False):
    nd = len(shape)
    if single:
        return pl.BlockSpec(shape, lambda *_: (0,) * nd, pipeline_mode=pl.Buffered(1))
    return pl.BlockSpec(shape, lambda *_: (0,) * nd)


def _rows(tm, width):
    return pl.BlockSpec((tm, width), lambda i: (i, 0))


def _sig(x):
    return jax.nn.sigmoid(x)


def _ln(x):
    mu = jnp.mean(x, axis=-1, keepdims=True)
    xc = x - mu
    var = jnp.mean(xc * xc, axis=-1, keepdims=True)
    rstd = lax.rsqrt(var + LN_EPS)
    return xc * rstd, rstd


def _ln_bwd(dxh, xh, rstd):
    m1 = jnp.mean(dxh, axis=-1, keepdims=True)
    m2 = jnp.mean(dxh * xh, axis=-1, keepdims=True)
    return rstd * (dxh - m1 - xh * m2)


def _rsum(v):
    return jnp.sum(v, axis=0, keepdims=True)


def _dot(a, b):
    return jnp.dot(a, b, preferred_element_type=F32)


def _dot_nt(a, b):
    return lax.dot_general(a, b, (((1,), (1,)), ((), ())), preferred_element_type=F32)


def _dot_tn(a, b):
    return lax.dot_general(a, b, (((0,), (0,)), ((), ())), preferred_element_type=F32)


def _my_pos():
    return lax.axis_index("x"), lax.axis_index("y"), lax.axis_index("c")


def _other_chips(x, y):
    return [(1 - x, y), (x, 1 - y), (1 - x, 1 - y)]


def _all_gather_small(v, name):
    r, c = v.shape

    def body(v_ref, out_ref, send_sems, recv_sems, local_sem):
        x, y, cc = _my_pos()
        me = 4 * x + 2 * y + cc
        mine = pltpu.make_async_copy(v_ref, out_ref.at[me], local_sem)
        mine.start()
        sends = []
        for rel in range(1, N_DEV):
            rx, ry, rc = (rel >> 2) & 1, (rel >> 1) & 1, rel & 1
            peer = (1 - x if rx else x, 1 - y if ry else y, 1 - cc if rc else cc)
            cp = pltpu.make_async_remote_copy(
                src_ref=v_ref, dst_ref=out_ref.at[me], send_sem=send_sems.at[rel - 1], recv_sem=recv_sems.at[rel - 1],
                device_id=peer, device_id_type=MESH)
            cp.start()
            sends.append(cp)
        for rel in range(1, N_DEV):
            rx, ry, rc = (rel >> 2) & 1, (rel >> 1) & 1, rel & 1
            peer = (1 - x if rx else x, 1 - y if ry else y, 1 - cc if rc else cc)
            slot = 4 * peer[0] + 2 * peer[1] + peer[2]
            pltpu.make_async_remote_copy(
                src_ref=v_ref, dst_ref=out_ref.at[slot], send_sem=send_sems.at[rel - 1], recv_sem=recv_sems.at[rel - 1],
                device_id=peer, device_id_type=MESH).wait_recv()
        for cp in sends:
            cp.wait_send()
        mine.wait()

    return _pcall(
        body, name=name,
        out_shape=jax.ShapeDtypeStruct((N_DEV, r, c), v.dtype),
        in_specs=[pl.BlockSpec(memory_space=pltpu.VMEM)],
        out_specs=pl.BlockSpec(memory_space=pltpu.VMEM),
        scratch_shapes=[pltpu.SemaphoreType.DMA((N_DEV - 1,)), pltpu.SemaphoreType.DMA((N_DEV - 1,)),
                        pltpu.SemaphoreType.DMA],
        compiler_params=_cparams(),
    )(v)


def _place_shard(parts, q_idx, name):
    rows = sum(p.shape[0] for p in parts)
    w = parts[0].shape[1]

    def body(q_ref, *refs):
        o_ref = refs[-1]
        r0 = 0
        for p_ref in refs[:-1]:
            n = p_ref.shape[0]
            o_ref[r0:r0 + n, :] = p_ref[...].astype(BF16)
            r0 += n

    grid_spec = pltpu.PrefetchScalarGridSpec(
        num_scalar_prefetch=1, grid=(1,),
        in_specs=[pl.BlockSpec(p.shape, lambda i, q: (0, 0)) for p in parts],
        out_specs=pl.BlockSpec((None, rows, w), lambda i, q: (q[0], 0, 0)))
    return _pcall(body, name=name, grid_spec=grid_spec, out_shape=jax.ShapeDtypeStruct((N_CHIPS, rows, w), BF16),
                  compiler_params=_seq(1))(q_idx, *parts)


class _GatherShards:
    n_sems = 6

    @staticmethod
    def _half(ref, slot, h):
        rows = ref.shape[1] // 2
        return ref.at[slot, pl.ds(h * rows, rows)]

    @classmethod
    def _copies(cls, in_ref, out_ref, send_sems, recv_sems):
        x, y, c = _my_pos()
        q = 2 * x + y
        sibling = (x, y, 1 - c)
        sends, landed, forwards, passed = [], [], [], []
        for j, chip in enumerate(_other_chips(x, y)):
            qj = 2 * chip[0] + chip[1]

            def copy(src, dst, k, to):
                return pltpu.make_async_remote_copy(src_ref=src, dst_ref=dst, send_sem=send_sems.at[k],
                                                    recv_sem=recv_sems.at[k], device_id=to, device_id_type=MESH)

            mine, theirs = cls._half(out_ref, qj, c), cls._half(out_ref, qj, 1 - c)
            sends.append(copy(cls._half(in_ref, q, c), cls._half(out_ref, q, c), j, (*chip, c)))
            landed.append(copy(mine, mine, j, (*chip, c)))
            forwards.append(copy(mine, mine, 3 + j, sibling))
            passed.append(copy(theirs, theirs, 3 + j, sibling))
        return sends, landed, forwards, passed

    @classmethod
    def start(cls, *refs):
        for cp in cls._copies(*refs)[0]:
            cp.start()

    @classmethod
    def relay(cls, *refs):
        _, landed, forwards, _ = cls._copies(*refs)
        for arrived, onward in zip(landed, forwards):
            arrived.wait_recv()
            onward.start()

    @classmethod
    def finish(cls, *refs):
        sends, _, forwards, passed = cls._copies(*refs)
        for cp in passed:
            cp.wait_recv()
        for cp in sends + forwards:
            cp.wait_send()


def _all_gather_weights(w):
    def body(i_ref, o_ref, send_sems, recv_sems):
        refs = (i_ref, o_ref, send_sems, recv_sems)
        _GatherShards.start(*refs)
        _GatherShards.relay(*refs)
        _GatherShards.finish(*refs)

    any_spec = pl.BlockSpec(memory_space=pl.ANY)
    n = _GatherShards.n_sems
    return _pcall(
        body, name="all_gather_weights", out_shape=jax.ShapeDtypeStruct(w.shape, w.dtype),
        in_specs=[any_spec], out_specs=any_spec, input_output_aliases={0: 0},
        scratch_shapes=[pltpu.SemaphoreType.DMA((n,)), pltpu.SemaphoreType.DMA((n,))],
        compiler_params=_cparams(),
    )(w)


class _SiblingHalf:
    n_sems = 1

    @staticmethod
    def out_shape(g):
        return jax.ShapeDtypeStruct((g.shape[0], g.shape[1] // 2, g.shape[2]), g.dtype)

    @staticmethod
    def copies(g_ref, r_ref, send_sems, recv_sems, base):
        x, y, c = _my_pos()
        rows = r_ref.shape[1]
        return [pltpu.make_async_remote_copy(
            src_ref=g_ref.at[:, pl.ds((1 - c) * rows, rows)], dst_ref=r_ref,
            send_sem=send_sems.at[base], recv_sem=recv_sems.at[base], device_id=(x, y, 1 - c), device_id_type=MESH)]


class _ChipBlocks:
    n_sems = 3

    @staticmethod
    def out_shape(p):
        return jax.ShapeDtypeStruct(p.shape, p.dtype)

    @staticmethod
    def copies(p_ref, r_ref, send_sems, recv_sems, base):
        x, y, c = _my_pos()
        return [pltpu.make_async_remote_copy(
            src_ref=p_ref.at[j], dst_ref=r_ref.at[j], send_sem=send_sems.at[base + j], recv_sem=recv_sems.at[base + j],
            device_id=(*chip, c), device_id_type=MESH) for j, chip in enumerate(_other_chips(x, y))]


def _plan_copies(plan, src_refs, dst_refs, send_sems, recv_sems):
    cps = []
    for b, (s_ref, d_ref) in enumerate(zip(src_refs, dst_refs)):
        cps += plan.copies(s_ref, d_ref, send_sems, recv_sems, b * plan.n_sems)
    return cps


def _plan_sems(plan, n):
    return [pltpu.SemaphoreType.DMA((n * plan.n_sems,)), pltpu.SemaphoreType.DMA((n * plan.n_sems,))]


def _exchange(plan, src, name):
    def body(s_ref, o_ref, send_sems, recv_sems):
        cps = _plan_copies(plan, [s_ref], [o_ref], send_sems, recv_sems)
        for cp in cps:
            cp.start()
        for cp in cps:
            cp.wait()

    any_spec = pl.BlockSpec(memory_space=pl.ANY)
    return _pcall(
        body, name=name, out_shape=plan.out_shape(src), in_specs=[any_spec], out_specs=any_spec,
        scratch_shapes=_plan_sems(plan, 1), compiler_params=_cparams(),
    )(src)


def _carried_start(plan, first, comm):
    @pl.when(first)
    def _():
        for cp in _plan_copies(plan, *comm):
            cp.start()


def _carried_wait(plan, last, comm):
    @pl.when(last)
    def _():
        for cp in _plan_copies(plan, *comm):
            cp.wait()


def _rs_join_halves(fa, fb):
    bufs = (fa, fb)
    nb = len(bufs)

    def body(a_ref, b_ref, ao_ref, bo_ref, send_sems, recv_sems):
        x, y, c = _my_pos()
        srcs, outs = (a_ref, b_ref), (ao_ref, bo_ref)
        cps = []
        for b in range(nb):
            rows = srcs[b].shape[0] // 2
            cp = pltpu.make_async_remote_copy(
                src_ref=srcs[b].at[pl.ds(c * rows, rows)], dst_ref=outs[b].at[pl.ds(c * rows, rows)],
                send_sem=send_sems.at[b], recv_sem=recv_sems.at[b], device_id=(x, y, 1 - c), device_id_type=MESH)
            cp.start()
            cps.append(cp)
        for b in range(nb):
            rows = srcs[b].shape[0] // 2
            theirs = outs[b].at[pl.ds((1 - c) * rows, rows)]
            pltpu.make_async_remote_copy(
                src_ref=theirs, dst_ref=theirs, send_sem=send_sems.at[b], recv_sem=recv_sems.at[b],
                device_id=(x, y, 1 - c), device_id_type=MESH).wait_recv()
        for cp in cps:
            cp.wait_send()

    any_spec = pl.BlockSpec(memory_space=pl.ANY)
    return _pcall(
        body, name="rs_join_halves",
        out_shape=[jax.ShapeDtypeStruct(b.shape, b.dtype) for b in bufs],
        in_specs=[any_spec] * nb, out_specs=[any_spec] * nb, input_output_aliases={0: 0, 1: 1},
        scratch_shapes=[pltpu.SemaphoreType.DMA((nb,)), pltpu.SemaphoreType.DMA((nb,))],
        compiler_params=_cparams(),
    )(*bufs)


def _add_halves(g, r, idx, tr, name):
    _, rows, w = r.shape
    nt = rows // tr

    def body(i_ref, g_ref, r_ref, o_ref):
        o_ref[...] = (g_ref[...] + r_ref[...]).astype(BF16)

    grid_spec = pltpu.PrefetchScalarGridSpec(
        num_scalar_prefetch=1, grid=(3, nt),
        in_specs=[pl.BlockSpec((None, tr, w), lambda j, i, ix: (ix[1 + j], ix[0] * nt + i, 0)),
                  pl.BlockSpec((None, tr, w), lambda j, i, ix: (ix[1 + j], i, 0))],
        out_specs=pl.BlockSpec((None, tr, w), lambda j, i, ix: (j, i, 0)))
    return _pcall(body, name=name, grid_spec=grid_spec,
                  out_shape=jax.ShapeDtypeStruct((3, rows, w), BF16), compiler_params=_seq(2))(idx, g, r)


def _add_chips(g, r, r3, idx, tr, name):
    _, rows, w = r.shape
    nt = rows // tr

    def body(i_ref, g_ref, r_ref, a_ref, b_ref, c_ref, o_ref):
        own = g_ref[...] + r_ref[...]
        o_ref[...] = ((own + a_ref[...].astype(F32)) + b_ref[...].astype(F32)) + c_ref[...].astype(F32)

    def other(j):
        return pl.BlockSpec((None, tr, w), lambda i, ix: (j, i, 0))

    grid_spec = pltpu.PrefetchScalarGridSpec(
        num_scalar_prefetch=1, grid=(nt,),
        in_specs=[pl.BlockSpec((None, tr, w), lambda i, ix: (ix[0], ix[1] * nt + i, 0)),
                  pl.BlockSpec((None, tr, w), lambda i, ix: (ix[0], i, 0)), other(0), other(1), other(2)],
        out_specs=pl.BlockSpec((tr, w), lambda i, ix: (ix[1] * nt + i, 0)))
    return _pcall(body, name=name, grid_spec=grid_spec,
                  out_shape=jax.ShapeDtypeStruct((2 * rows, w), F32), compiler_params=_seq(1))(idx, g, r, r3, r3, r3)


def _ada_fwd(c_all, w_sh, b_sh):
    def body(c_ref, w_ref, b_ref, o_ref):
        cv = c_ref[...]
        ca = cv * _sig(cv)
        o_ref[...] = jnp.dot(ca, w_ref[...], preferred_element_type=F32, precision=lax.Precision.HIGHEST) + b_ref[...]

    return _pcall(body, name="ada_fwd", out_shape=jax.ShapeDtypeStruct((c_all.shape[0], w_sh.shape[1]), F32),
                  compiler_params=_cparams())(c_all, w_sh, b_sh)


def _ada_bwd(c_all_t, dmod_sh):
    def body(c_ref, d_ref, o_ref):
        cv = c_ref[...]
        ca = cv * _sig(cv)
        o_ref[...] = jnp.dot(ca, d_ref[...], preferred_element_type=F32, precision=lax.Precision.HIGHEST)

    return _pcall(body, name="ada_bwd", out_shape=jax.ShapeDtypeStruct((c_all_t.shape[0], dmod_sh.shape[1]), F32),
                  compiler_params=_cparams())(c_all_t, dmod_sh)


def _conv_causal(ext_ref, sh_ref, w_ref, ntaps, halo, bias, out_ref, tm, d, rc):
    off = halo - (ntaps - 1)
    n = tm + halo - SUBLANES
    started = False
    for s in range(SUBLANES):
        taps = [(k, (off + k) // SUBLANES * SUBLANES) for k in range(ntaps) if (off + k) % SUBLANES == s]
        if not taps:
            continue
        if s == 0:
            src = ext_ref
        else:
            sh_ref[0:n, :] = ext_ref[s:s + n, :]
            src = sh_ref

        def chunk(i, carry, src=src, taps=taps, fresh=not started):
            r0 = pl.multiple_of(i * rc, rc)
            if fresh:
                acc = jnp.zeros((rc, d), F32) if bias is None else jnp.broadcast_to(bias, (rc, d))
            else:
                acc = out_ref[pl.ds(r0, rc), :]
            for k, a in taps:
                acc = acc + w_ref[k:k + 1, :] * src[pl.ds(r0 + a, rc), :]
            out_ref[pl.ds(r0, rc), :] = acc
            return carry

        lax.fori_loop(0, tm // rc, chunk, 0)
        started = True


def _conv_adjoint(dp_ref, sh_ref, ext_ref, w_ref, dx_ref, dw_ref, ntaps, halo, tm, d, rc):
    off = halo - (ntaps - 1)
    lead = SUBLANES + ntaps - 1
    n = tm + halo
    row = lax.broadcasted_iota(jnp.int32, (SUBLANES, d), 0)
    started = False
    for s in range(SUBLANES):
        taps = [(k, (lead - k) // SUBLANES * SUBLANES) for k in range(ntaps) if (lead - k) % SUBLANES == s]
        wtaps = [(k, s + off + k - SUBLANES) for k in range(ntaps) if (-(off + k)) % SUBLANES == s]
        if not taps and not wtaps:
            continue
        if s == 0:
            src = dp_ref
        else:
            sh_ref[0:n, :] = dp_ref[s:s + n, :]
            src = sh_ref

        if taps:
            def chunk(i, carry, src=src, taps=taps, fresh=not started):
                r0 = pl.multiple_of(i * rc, rc)
                acc = jnp.zeros((rc, d), F32) if fresh else dx_ref[pl.ds(r0, rc), :]
                for k, a in taps:
                    acc = acc + w_ref[k:k + 1, :] * src[pl.ds(r0 + a, rc), :]
                dx_ref[pl.ds(r0, rc), :] = acc
                return carry

            lax.fori_loop(0, tm // rc, chunk, 0)
            started = True

        if wtaps:
            def wchunk(i, accs, src=src, wtaps=wtaps):
                r0 = pl.multiple_of(i * SUBLANES, SUBLANES)
                g = src[pl.ds(r0, SUBLANES), :]
                return tuple(acc + g * ext_ref[pl.ds(r0 + e, SUBLANES), :] for acc, (_, e) in zip(accs, wtaps))

            accs = lax.fori_loop(0, tm // SUBLANES, wchunk,
                                 tuple(jnp.zeros((SUBLANES, d), F32) for _ in wtaps), unroll=2)
            for acc, (k, e) in zip(accs, wtaps):
                tail = src[tm:tm + SUBLANES, :] * ext_ref[tm + e:tm + e + SUBLANES, :]
                dw_ref[k:k + 1, :] += _rsum(acc + jnp.where(row < SUBLANES - s, tail, 0.0))


def _fwd_in(x, vecs, wa, b_in, wb, tm):
    s, d = x.shape
    nq, _, nw = wa.shape
    nt = s // tm

    def body(x_ref, v_ref, w_ref, b_ref, wbi_ref, h_ref, z_ref, wbo_ref, send_sems, recv_sems):
        i = pl.program_id(0)
        gather = (wbi_ref, wbo_ref, send_sems, recv_sems)
        pl.when(i == 0)(lambda: _GatherShards.start(*gather))
        pl.when(i == nt // 2)(lambda: _GatherShards.relay(*gather))
        xh, _ = _ln(x_ref[...])
        h = (xh * (1.0 + v_ref[V_SCALE1:V_SCALE1 + 1, :]) + v_ref[V_SHIFT1:V_SHIFT1 + 1, :]).astype(BF16)
        h_ref[...] = h
        for q in range(nq):
            z_ref[:, q * nw:(q + 1) * nw] = _dot(h, w_ref[q]) + b_ref[:, q * nw:(q + 1) * nw]
        pl.when(i == nt - 1)(lambda: _GatherShards.finish(*gather))

    any_spec = pl.BlockSpec(memory_space=pl.ANY)
    n = _GatherShards.n_sems
    return _pcall(
        body, name="fwd_in", grid=(nt,),
        in_specs=[_rows(tm, d), _full(vecs.shape), _full(wa.shape, single=True), _full(b_in.shape), any_spec],
        out_specs=[_rows(tm, d), _rows(tm, nq * nw), any_spec],
        out_shape=[jax.ShapeDtypeStruct((s, d), BF16), jax.ShapeDtypeStruct((s, nq * nw), F32),
                   jax.ShapeDtypeStruct(wb.shape, wb.dtype)],
        input_output_aliases={4: 2},
        scratch_shapes=[pltpu.SemaphoreType.DMA((n,)), pltpu.SemaphoreType.DMA((n,))],
        compiler_params=_seq(1),
    )(x, vecs, wa, b_in, wb)


def _fwd_mix(z, vecs, caw, cbw, wb, tm, rc):
    s = z.shape[0]
    d = vecs.shape[1]
    nq = wb.shape[0]
    kq = d // nq
    base = 2 * d // kq

    def body(z_ref, v_ref, caw_ref, cbw_ref, wao_ref, wbo_ref, wo_ref,
             u1_ref, ya_ref, yb_ref, o1_ref, u3_ref, vv_ref, mg_ref, ext_ref, sh_ref, pext_ref, q_ref):
        @pl.when(pl.program_id(0) == 0)
        def _():
            ext_ref[0:HALO_A, :] = jnp.zeros((HALO_A, d), F32)
            pext_ref[0:HALO_B, :] = jnp.zeros((HALO_B, d), F32)

        ext_ref[HALO_A:HALO_A + tm, :] = z_ref[:, 0:d] * _sig(z_ref[:, d:2 * d])
        _conv_causal(ext_ref, sh_ref, caw_ref, CONV_A, HALO_A, v_ref[V_CAB:V_CAB + 1, :], u1_ref, tm, d, rc)
        ext_ref[0:HALO_A, :] = ext_ref[tm:tm + HALO_A, :]
        xa, _ = _ln(u1_ref[...])
        u2 = xa * v_ref[V_LNAG:V_LNAG + 1, :] + v_ref[V_LNAB:V_LNAB + 1, :]
        u3 = (u2 * _sig(u2)).astype(BF16)
        u3_ref[...] = u3
        ya = jnp.broadcast_to(v_ref[V_BAO:V_BAO + 1, :], (tm, d))
        for q in range(nq):
            ya = ya + _dot(u3[:, q * kq:(q + 1) * kq], wao_ref[q])
        ya_ref[...] = ya

        pext_ref[HALO_B:HALO_B + tm, :] = z_ref[:, 3 * d:4 * d] * z_ref[:, 4 * d:5 * d]
        _conv_causal(pext_ref, sh_ref, cbw_ref, CONV_B, HALO_B, None, q_ref, tm, d, rc)
        pext_ref[0:HALO_B, :] = pext_ref[tm:tm + HALO_B, :]
        vv = (z_ref[:, 2 * d:3 * d] * q_ref[...]).astype(BF16)
        vv_ref[...] = vv
        yb = jnp.zeros((tm, d), F32)
        for q in range(nq):
            yb = yb + _dot(vv[:, q * kq:(q + 1) * kq], wbo_ref[q])
        yb_ref[...] = yb

        mg = (_sig(z_ref[:, 5 * d:6 * d]) * ya + _sig(z_ref[:, 6 * d:7 * d]) * yb).astype(BF16)
        mg_ref[...] = mg
        o1 = jnp.broadcast_to(v_ref[V_BO:V_BO + 1, :], (tm, d))
        for q in range(nq):
            o1 = o1 + _dot(mg[:, q * kq:(q + 1) * kq], wo_ref[q])
        o1_ref[...] = o1

    def wspec(j):
        return pl.BlockSpec((nq, kq, d), lambda i: (0, base + j, 0), pipeline_mode=pl.Buffered(1))

    f32o = jax.ShapeDtypeStruct((s, d), F32)
    b16o = jax.ShapeDtypeStruct((s, d), BF16)
    return _pcall(
        body, name="fwd_mix", grid=(s // tm,),
        in_specs=[_rows(tm, 7 * d), _full(vecs.shape), _full(caw.shape), _full(cbw.shape), wspec(0), wspec(1), wspec(2)],
        out_specs=[_rows(tm, d)] * 7,
        out_shape=[f32o, f32o, f32o, f32o, b16o, b16o, b16o],
        scratch_shapes=[pltpu.VMEM((HALO_A + tm, d), F32), pltpu.VMEM((HALO_A + tm, d), F32),
                        pltpu.VMEM((HALO_B + tm, d), F32), pltpu.VMEM((tm, d), F32)],
        compiler_params=_seq(1),
    )(z, vecs, caw, cbw, wb, wb, wb)


def _mlp_fwd_bwd(x, out1, tgt, vecs, b_up, wb, tm):
    s, d = x.shape
    nq = wb.shape[0]
    dff = nq * d

    def body(x_ref, o1_ref, t_ref, v_ref, bup_ref, wup_ref, wdn_ref,
             h2_ref, f_ref, df0_ref, do2_ref, do1_ref, dxp_ref, acc_ref, dbup_ref, f0_ref):
        @pl.when(pl.program_id(0) == 0)
        def _():
            acc_ref[...] = jnp.zeros(acc_ref.shape, F32)
            dbup_ref[...] = jnp.zeros(dbup_ref.shape, F32)

        def vec(r):
            return v_ref[r:r + 1, :]

        def accum(r, val):
            acc_ref[r:r + 1, :] += _rsum(val)

        out1v = o1_ref[...]
        r1 = ALPHA * x_ref[...] + (1.0 + vec(V_GATE1)) * out1v
        xh1, rstd1 = _ln(r1)
        x1 = xh1 * vec(V_LN1G) + vec(V_LN1B)
        xn1, rstdn = _ln(x1)
        h2 = (xn1 * (1.0 + vec(V_SCALE2)) + vec(V_SHIFT2)).astype(BF16)
        h2_ref[...] = h2
        out2 = jnp.broadcast_to(vec(V_BDN), (tm, d))
        for q in range(nq):
            f0 = _dot(h2, wup_ref[q]) + bup_ref[:, q * d:(q + 1) * d]
            rl = jnp.maximum(f0, 0.0)
            f0_ref[:, q * d:(q + 1) * d] = rl
            fb = (rl * rl).astype(BF16)
            f_ref[:, q * d:(q + 1) * d] = fb
            out2 = out2 + _dot(fb, wdn_ref[q])
        r2 = ALPHA * x1 + (1.0 + vec(V_GATE2)) * out2
        xh2, rstd2 = _ln(r2)
        yv = xh2 * vec(V_LN2G) + vec(V_LN2B)
        err = yv - t_ref[...]
        accum(M_LOSS, err * err)
        dy = err * (1.0 / d)
        accum(M_LN2G, dy * xh2)
        accum(M_LN2B, dy)
        dr2 = _ln_bwd(dy * vec(V_LN2G), xh2, rstd2)
        accum(M_GATE2, dr2 * out2)
        dout2 = (1.0 + vec(V_GATE2)) * dr2
        accum(M_BDN, dout2)
        do2b = dout2.astype(BF16)
        do2_ref[...] = do2b
        dh2 = jnp.zeros((tm, d), F32)
        for q in range(nq):
            df0 = _dot_nt(do2b, wdn_ref[q]) * (2.0 * f0_ref[:, q * d:(q + 1) * d])
            dbup_ref[q:q + 1, :] += _rsum(df0)
            df0b = df0.astype(BF16)
            df0_ref[:, q * d:(q + 1) * d] = df0b
            dh2 = dh2 + _dot_nt(df0b, wup_ref[q])
        accum(M_SHIFT2, dh2)
        accum(M_SCALE2, dh2 * xn1)
        dx1 = ALPHA * dr2 + _ln_bwd(dh2 * (1.0 + vec(V_SCALE2)), xn1, rstdn)
        accum(M_LN1G, dx1 * xh1)
        accum(M_LN1B, dx1)
        dr1 = _ln_bwd(dx1 * vec(V_LN1G), xh1, rstd1)
        accum(M_GATE1, dr1 * out1v)
        dout1 = (1.0 + vec(V_GATE1)) * dr1
        accum(M_BO, dout1)
        do1_ref[...] = dout1.astype(BF16)
        dxp_ref[...] = ALPHA * dr1

    def wspec(j):
        return pl.BlockSpec((nq, d, d), lambda i: (0, j, 0), pipeline_mode=pl.Buffered(1))

    b16 = lambda w: jax.ShapeDtypeStruct((s, w), BF16)
    return _pcall(
        body, name="mlp_fwd_bwd", grid=(s // tm,),
        in_specs=[_rows(tm, d), _rows(tm, d), _rows(tm, d), _full(vecs.shape), _full(b_up.shape), wspec(0), wspec(1)],
        out_specs=[_rows(tm, d), _rows(tm, dff), _rows(tm, dff), _rows(tm, d), _rows(tm, d), _rows(tm, d),
                   _full((16, d)), _full((SUBLANES, d))],
        out_shape=[b16(d), b16(dff), b16(dff), b16(d), b16(d), jax.ShapeDtypeStruct((s, d), F32),
                   jax.ShapeDtypeStruct((16, d), F32), jax.ShapeDtypeStruct((SUBLANES, d), F32)],
        scratch_shapes=[pltpu.VMEM((tm, dff), F32)],
        compiler_params=_seq(1),
    )(x, out1, tgt, vecs, b_up, wb, wb)


def _mix_bwd(dout1, z, u1, ya, yb, vecs, caw, cbw, wb, tm, rc):
    s = z.shape[0]
    d = vecs.shape[1]
    nq = wb.shape[0]
    kq = d // nq
    base = 2 * d // kq
    nt = s // tm
    hb = tm // HALO_A

    def body(do1_ref, z_ref, zh_ref, u1_ref, ya_ref, yb_ref, v_ref, caw_ref, cbw_ref, wao_ref, wbo_ref, wo_ref,
             dz_ref, dya_ref, dyb_ref, acc_ref, dcaw_ref, dcbw_ref, dbin_ref,
             ext_ref, du1p_ref, sh_ref, pext_ref, dqp_ref, tmp_ref):
        i = pl.program_id(0)

        @pl.when(i == 0)
        def _():
            acc_ref[...] = jnp.zeros(acc_ref.shape, F32)
            dcaw_ref[...] = jnp.zeros(dcaw_ref.shape, F32)
            dcbw_ref[...] = jnp.zeros(dcbw_ref.shape, F32)
            dbin_ref[...] = jnp.zeros(dbin_ref.shape, F32)
            du1p_ref[0:SUBLANES, :] = jnp.zeros((SUBLANES, d), F32)
            du1p_ref[SUBLANES + tm:SUBLANES + tm + HALO_A, :] = jnp.zeros((HALO_A, d), F32)
            dqp_ref[0:SUBLANES, :] = jnp.zeros((SUBLANES, d), F32)
            dqp_ref[SUBLANES + tm:SUBLANES + tm + HALO_B, :] = jnp.zeros((HALO_B, d), F32)

        def vec(r):
            return v_ref[r:r + 1, :]

        def accum(r, val):
            acc_ref[r:r + 1, :] += _rsum(val)

        def put_dz(j, val):
            dbin_ref[j:j + 1, :] += _rsum(val)
            dz_ref[:, j * d:(j + 1) * d] = val.astype(BF16)

        has_history = i < nt - 1

        do1 = do1_ref[...]
        dmg = jnp.concatenate([_dot_nt(do1, wo_ref[q]) for q in range(nq)], axis=1)
        sga = _sig(z_ref[:, 5 * d:6 * d])
        sgb = _sig(z_ref[:, 6 * d:7 * d])
        dya = dmg * sga
        dyb = dmg * sgb
        accum(X_BAO, dya)
        put_dz(5, dya * ya_ref[...] * (1.0 - sga))
        put_dz(6, dyb * yb_ref[...] * (1.0 - sgb))
        dyab = dya.astype(BF16)
        dybb = dyb.astype(BF16)
        dya_ref[...] = dyab
        dyb_ref[...] = dybb

        du3 = jnp.concatenate([_dot_nt(dyab, wao_ref[q]) for q in range(nq)], axis=1)
        xa, rstda = _ln(u1_ref[...])
        u2 = xa * vec(V_LNAG) + vec(V_LNAB)
        s2 = _sig(u2)
        du2 = du3 * (s2 * (1.0 + u2 * (1.0 - s2)))
        accum(X_LNAG, du2 * xa)
        accum(X_LNAB, du2)
        du1 = _ln_bwd(du2 * vec(V_LNAG), xa, rstda)
        accum(X_CAB, du1)
        du1p_ref[SUBLANES:SUBLANES + tm, :] = du1
        sg = _sig(z_ref[:, d:2 * d])
        aval = z_ref[:, 0:d]
        ext_ref[HALO_A:HALO_A + tm, :] = aval * sg
        ext_ref[0:HALO_A, :] = jnp.where(has_history, zh_ref[:, 0:d] * _sig(zh_ref[:, d:2 * d]), 0.0)
        _conv_adjoint(du1p_ref, sh_ref, ext_ref, caw_ref, tmp_ref, dcaw_ref, CONV_A, HALO_A, tm, d, rc)
        du1p_ref[SUBLANES + tm:SUBLANES + tm + HALO_A, :] = du1p_ref[SUBLANES:SUBLANES + HALO_A, :]
        du0 = tmp_ref[...]
        put_dz(0, du0 * sg)
        put_dz(1, du0 * aval * sg * (1.0 - sg))

        dv = jnp.concatenate([_dot_nt(dybb, wbo_ref[q]) for q in range(nq)], axis=1)
        bgc = z_ref[:, 3 * d:4 * d]
        bx = z_ref[:, 4 * d:5 * d]
        pext_ref[HALO_B:HALO_B + tm, :] = bgc * bx
        pext_ref[0:HALO_B, :] = jnp.where(
            has_history, zh_ref[HALO_A - HALO_B:HALO_A, 3 * d:4 * d] * zh_ref[HALO_A - HALO_B:HALO_A, 4 * d:5 * d], 0.0)
        _conv_causal(pext_ref, sh_ref, cbw_ref, CONV_B, HALO_B, None, tmp_ref, tm, d, rc)
        put_dz(2, dv * tmp_ref[...])
        dqp_ref[SUBLANES:SUBLANES + tm, :] = dv * z_ref[:, 2 * d:3 * d]
        _conv_adjoint(dqp_ref, sh_ref, pext_ref, cbw_ref, tmp_ref, dcbw_ref, CONV_B, HALO_B, tm, d, rc)
        dqp_ref[SUBLANES + tm:SUBLANES + tm + HALO_B, :] = dqp_ref[SUBLANES:SUBLANES + HALO_B, :]
        dp = tmp_ref[...]
        put_dz(3, dp * bx)
        put_dz(4, dp * bgc)

    def rev(width):
        return pl.BlockSpec((tm, width), lambda i: (nt - 1 - i, 0))

    def wspec(j):
        return pl.BlockSpec((nq, kq, d), lambda i: (0, base + j, 0), pipeline_mode=pl.Buffered(1))

    halo = pl.BlockSpec((HALO_A, 7 * d), lambda i: (jnp.maximum((nt - 1 - i) * hb - 1, 0), 0))
    b16 = jax.ShapeDtypeStruct((s, d), BF16)
    acc8 = jax.ShapeDtypeStruct((SUBLANES, d), F32)
    return _pcall(
        body, name="mix_bwd", grid=(nt,),
        in_specs=[rev(d), rev(7 * d), halo, rev(d), rev(d), rev(d), _full(vecs.shape), _full(caw.shape), _full(cbw.shape),
                  wspec(0), wspec(1), wspec(2)],
        out_specs=[rev(7 * d), rev(d), rev(d), _full((SUBLANES, d)), _full((HALO_A, d)), _full((HALO_B, d)),
                   _full((SUBLANES, d))],
        out_shape=[jax.ShapeDtypeStruct((s, 7 * d), BF16), b16, b16, acc8,
                   jax.ShapeDtypeStruct((HALO_A, d), F32), jax.ShapeDtypeStruct((HALO_B, d), F32), acc8],
        scratch_shapes=[pltpu.VMEM((HALO_A + tm, d), F32), pltpu.VMEM((SUBLANES + tm + HALO_A, d), F32),
                        pltpu.VMEM((tm + HALO_A, d), F32), pltpu.VMEM((HALO_B + tm, d), F32),
                        pltpu.VMEM((SUBLANES + tm + HALO_B, d), F32), pltpu.VMEM((tm, d), F32)],
        compiler_params=_seq(1),
    )(dout1, z, z, u1, ya, yb, vecs, caw, cbw, wb, wb, wb)


def _in_bwd(dz, x, dxp, vecs, wa, tm, plan, plan_srcs):
    s, d = x.shape
    nq, _, nw = wa.shape
    nt = s // tm
    nc = len(plan_srcs)

    def body(dz_ref, x_ref, dxp_ref, v_ref, w_ref, *rest):
        src_refs, (gx_ref, acc_ref), dst_refs = rest[:nc], rest[nc:nc + 2], rest[nc + 2:2 * nc + 2]
        send_sems, recv_sems = rest[2 * nc + 2:]
        i = pl.program_id(0)
        comm = (src_refs, dst_refs, send_sems, recv_sems)
        _carried_start(plan, i == 0, comm)

        @pl.when(i == 0)
        def _():
            acc_ref[...] = jnp.zeros(acc_ref.shape, F32)

        dh1 = jnp.zeros((tm, d), F32)
        for q in range(nq):
            dh1 = dh1 + _dot_nt(dz_ref[:, q * nw:(q + 1) * nw], w_ref[q])
        xh, rstd = _ln(x_ref[...])
        acc_ref[I_SHIFT1:I_SHIFT1 + 1, :] += _rsum(dh1)
        acc_ref[I_SCALE1:I_SCALE1 + 1, :] += _rsum(dh1 * xh)
        gx_ref[...] = dxp_ref[...] + _ln_bwd(dh1 * (1.0 + v_ref[V_SCALE1:V_SCALE1 + 1, :]), xh, rstd)
        _carried_wait(plan, i == nt - 1, comm)

    any_spec = pl.BlockSpec(memory_space=pl.ANY)
    return _pcall(
        body, name="in_bwd", grid=(nt,),
        in_specs=[_rows(tm, nq * nw), _rows(tm, d), _rows(tm, d), _full(vecs.shape), _full(wa.shape, single=True)]
        + [any_spec] * nc,
        out_specs=[_rows(tm, d), _full((SUBLANES, d))] + [any_spec] * nc,
        out_shape=[jax.ShapeDtypeStruct((s, d), F32), jax.ShapeDtypeStruct((SUBLANES, d), F32)]
        + [plan.out_shape(p) for p in plan_srcs],
        scratch_shapes=_plan_sems(plan, nc),
        compiler_params=_seq(1),
    )(dz, x, dxp, vecs, wa, *plan_srcs)


def _dw(a, b, split_a, ts, name, into=None, rows_total=None, row_block=0):
    s = a.shape[0]
    ka = a.shape[1] // N_CHIPS if split_a else a.shape[1]
    nb = b.shape[1] if split_a else b.shape[1] // N_CHIPS
    rows_total = ka if rows_total is None else rows_total

    def body(a_ref, b_ref, *rest):
        o_ref = rest[-1]

        @pl.when(pl.program_id(1) == 0)
        def _():
            o_ref[...] = jnp.zeros(o_ref.shape, F32)

        o_ref[...] += _dot_tn(a_ref[...], b_ref[...])

    a_spec = pl.BlockSpec((ts, ka), (lambda q, i: (i, q)) if split_a else (lambda q, i: (i, 0)))
    b_spec = pl.BlockSpec((ts, nb), (lambda q, i: (i, 0)) if split_a else (lambda q, i: (i, q)))
    extra = {} if into is None else dict(input_output_aliases={2: 0})
    return _pcall(
        body, name=name, grid=(N_CHIPS, s // ts),
        in_specs=[a_spec, b_spec] + ([] if into is None else [pl.BlockSpec(memory_space=pl.ANY)]),
        out_specs=pl.BlockSpec((None, ka, nb), lambda q, i: (q, row_block, 0)),
        out_shape=jax.ShapeDtypeStruct((N_CHIPS, rows_total, nb), F32),
        compiler_params=_seq(2), **extra,
    )(*((a, b) if into is None else (a, b, into)))


def _dw_carrying(a, b, ts, name, plan, plan_src):
    s, k = a.shape
    nb = b.shape[1] // N_CHIPS
    ns = s // ts

    def body(a_ref, b_ref, src_ref, o_ref, dst_ref, send_sems, recv_sems):
        q, i = pl.program_id(0), pl.program_id(1)
        comm = ([src_ref], [dst_ref], send_sems, recv_sems)
        _carried_start(plan, jnp.logical_and(q == 0, i == 0), comm)

        @pl.when(i == 0)
        def _():
            o_ref[...] = jnp.zeros(o_ref.shape, F32)

        o_ref[...] += _dot_tn(a_ref[...], b_ref[...])
        _carried_wait(plan, jnp.logical_and(q == N_CHIPS - 1, i == ns - 1), comm)

    any_spec = pl.BlockSpec(memory_space=pl.ANY)
    return _pcall(
        body, name=name, grid=(N_CHIPS, ns),
        in_specs=[pl.BlockSpec((ts, k), lambda q, i: (i, 0)), pl.BlockSpec((ts, nb), lambda q, i: (i, q)), any_spec],
        out_specs=[pl.BlockSpec((None, k, nb), lambda q, i: (q, 0, 0)), any_spec],
        out_shape=[jax.ShapeDtypeStruct((N_CHIPS, k, nb), F32), plan.out_shape(plan_src)],
        scratch_shapes=_plan_sems(plan, 1),
        compiler_params=_seq(2),
    )(a, b, plan_src)


def _dw_rows(a, b, ts, name, into, row_block):
    s, k = a.shape
    n = b.shape[1]
    kq = k // N_CHIPS

    def body(a_ref, b_ref, buf_ref, o_ref):
        @pl.when(pl.program_id(0) == 0)
        def _():
            o_ref[...] = jnp.zeros(o_ref.shape, F32)

        res = _dot_tn(a_ref[...], b_ref[...])
        for q in range(N_CHIPS):
            o_ref[q] += res[q * kq:(q + 1) * kq, :]

    return _pcall(
        body, name=name, grid=(s // ts,),
        in_specs=[_rows(ts, k), _rows(ts, n), pl.BlockSpec(memory_space=pl.ANY)],
        out_specs=pl.BlockSpec((N_CHIPS, kq, n), lambda i: (0, row_block, 0)),
        out_shape=jax.ShapeDtypeStruct(into.shape, F32), input_output_aliases={2: 0},
        compiler_params=_seq(1),
    )(a, b, into)


def _adam_math(w, g, m, v):
    m2 = ADAM_B1 * m + (1.0 - ADAM_B1) * g
    v2 = ADAM_B2 * v + (1.0 - ADAM_B2) * (g * g)
    m_hat = m2 / (1.0 - ADAM_B1 ** ADAM_STEP)
    v_hat = v2 / (1.0 - ADAM_B2 ** ADAM_STEP)
    delta = -ADAM_LR * (m_hat / (jnp.sqrt(v_hat) + ADAM_EPS) + ADAM_WD * w)
    return delta, m2, v2


def _adam(w, g, m, v, g_row0, tr, name):
    r, c = w.shape
    blk0 = g_row0 // tr

    def body(w_ref, g_ref, m_ref, v_ref, go_ref, d_ref, mo_ref, vo_ref):
        gv = g_ref[...]
        go_ref[...] = gv
        d_ref[...], mo_ref[...], vo_ref[...] = _adam_math(w_ref[...], gv, m_ref[...], v_ref[...])

    spec = _rows(tr, c)
    g_spec = pl.BlockSpec((tr, c), lambda i: (blk0 + i, 0))
    o = jax.ShapeDtypeStruct((r, c), F32)
    return _pcall(body, name=name, grid=(r // tr,), in_specs=[spec, g_spec, spec, spec], out_specs=[spec] * 4,
                  out_shape=[o, o, o, o], compiler_params=_seq(1))(w, g, m, v)


def _small_update(gathered, q_idx, small_w, small_m, small_v, conv_w, conv_m, conv_v):
    d = gathered.shape[2]
    ns = len(_SMALL)
    cw = conv_w[0].shape[1]
    conv_rows = ((T_CAW, CONV_A), (T_CBW, CONV_B))

    def body(q_ref, g_ref, *refs):
        ins, outs = refs[:3 * (ns + 2)], refs[3 * (ns + 2):]
        tot_ref, loss_ref = outs[0], outs[1]
        outs = outs[2:]
        tot = g_ref[0]
        for dev in range(1, N_DEV):
            tot = tot + g_ref[dev]
        tot_ref[...] = tot
        loss_ref[...] = (0.5 / d) * jnp.sum(tot_ref[T_LOSS:T_LOSS + 1, :], axis=1, keepdims=True)
        for p, (_, rows) in enumerate(_SMALL):
            w_ref, m_ref, v_ref = ins[p], ins[ns + 2 + p], ins[2 * (ns + 2) + p]
            go, do, mo, vo = outs[4 * p:4 * p + 4]
            for j, row in enumerate(rows):
                sl = slice(j * d, (j + 1) * d)
                gv = tot_ref[row:row + 1, :]
                go[:, sl] = gv
                do[:, sl], mo[:, sl], vo[:, sl] = _adam_math(w_ref[:, sl], gv, m_ref[:, sl], v_ref[:, sl])
        for p, (row, taps) in enumerate(conv_rows):
            w_ref, m_ref, v_ref = ins[ns + p], ins[ns + 2 + ns + p], ins[2 * (ns + 2) + ns + p]
            go, do, mo, vo = outs[4 * (ns + p):4 * (ns + p) + 4]
            gv = tot_ref[row:row + taps, 0:cw]
            for qq in range(1, N_CHIPS):
                gv = jnp.where(q_ref[0] == qq, tot_ref[row:row + taps, qq * cw:(qq + 1) * cw], gv)
            go[...] = gv
            do[...], mo[...], vo[...] = _adam_math(w_ref[...], gv, m_ref[...], v_ref[...])

    params = list(small_w) + list(conv_w) + list(small_m) + list(conv_m) + list(small_v) + list(conv_v)
    out_shape = [jax.ShapeDtypeStruct((T_ROWS, d), F32), jax.ShapeDtypeStruct((1, 1), F32)]
    for w in list(small_w) + list(conv_w):
        out_shape += [jax.ShapeDtypeStruct(w.shape, F32)] * 4
    vm = pl.BlockSpec(memory_space=pltpu.VMEM)
    return _pcall(
        body, name="small_update", out_shape=out_shape,
        in_specs=[pl.BlockSpec(memory_space=pltpu.SMEM), vm] + [vm] * len(params),
        out_specs=[vm] * len(out_shape), compiler_params=_cparams(),
    )(q_idx, gathered, *params)


def kernel(x, c, w_ada, b_ada, w_in, b_in, conv_a_w, conv_a_b, ln_a_g, ln_a_b, w_a_out, b_a_out, conv_b_w, w_b_out, w_o, b_o, ln1_g, ln1_b, w_up, b_up, w_down, b_down, ln2_g, ln2_b, loss_target, m_w_ada, m_b_ada, m_w_in, m_b_in, m_conv_a_w, m_conv_a_b, m_ln_a_g, m_ln_a_b, m_w_a_out, m_b_a_out, m_conv_b_w, m_w_b_out, m_w_o, m_b_o, m_ln1_g, m_ln1_b, m_w_up, m_b_up, m_w_down, m_b_down, m_ln2_g, m_ln2_b, v_w_ada, v_b_ada, v_w_in, v_b_in, v_conv_a_w, v_conv_a_b, v_ln_a_g, v_ln_a_b, v_w_a_out, v_b_a_out, v_conv_b_w, v_w_b_out, v_w_o, v_b_o, v_ln1_g, v_ln1_b, v_w_up, v_b_up, v_w_down, v_b_down, v_ln2_g, v_ln2_b):
    given = dict(locals())
    s, d = x.shape[1], x.shape[2]
    xi, yi, ci = _my_pos()
    q = 2 * xi + yi
    me = 4 * xi + 2 * yi + ci
    i32 = jnp.int32
    q_arr = jnp.reshape(q, (1,)).astype(i32)
    others = [2 * ox + oy for ox, oy in _other_chips(xi, yi)]
    halves_idx = jnp.stack([ci] + others).astype(i32)
    chips_idx = jnp.stack([q, ci]).astype(i32)
    kq = d // N_CHIPS
    tm = min(256, s)
    rc = min(32, tm)

    def sq(a):
        return a.reshape(a.shape[1:])

    x2, tgt = sq(x), sq(loss_target)

    n_ada = w_ada.shape[2]
    pre = jnp.concatenate([
        jnp.broadcast_to(c, (SUBLANES, d)),
        jnp.pad(sq(conv_a_w), ((0, HALO_A - CONV_A), (0, d - kq))),
        jnp.pad(sq(conv_b_w), ((0, HALO_B - CONV_B), (0, d - kq)))], axis=0)
    pre_all = _all_gather_small(pre, "gather_c_conv")
    c_all = pre_all[:, 0, :]
    caw = jnp.concatenate([pre_all[2 * p, SUBLANES:SUBLANES + HALO_A, :kq] for p in range(N_CHIPS)], axis=1)
    cbw = jnp.concatenate([pre_all[2 * p, SUBLANES + HALO_A:, :kq] for p in range(N_CHIPS)], axis=1)
    b_ada_sh = lax.dynamic_slice(b_ada, (0, q * n_ada), (1, n_ada))
    mod_part = _ada_fwd(c_all, sq(w_ada), b_ada_sh)
    mod_all = _all_gather_small(mod_part, "gather_mod")
    mod_rows = lax.dynamic_slice(mod_all, (0, me, 0), (N_DEV, 1, n_ada))[0::2, 0, :]
    mod = mod_rows.reshape(6, d)
    vecs = jnp.concatenate([mod, conv_a_b, ln_a_g, ln_a_b, b_a_out, b_o, ln1_g, ln1_b, b_down, ln2_g, ln2_b], axis=0)

    wa = _place_shard([sq(w_in)], q_arr, "place_w_in")
    wb = _place_shard([sq(w_up), sq(w_down), sq(w_a_out), sq(w_b_out), sq(w_o)], q_arr, "place_w_rest")
    wa = _all_gather_weights(wa)

    h1, z, wb = _fwd_in(x2, vecs, wa, b_in, wb, tm)
    u1, ya, yb, out1, u3, vv, mg = _fwd_mix(z, vecs, caw, cbw, wb, tm, rc)

    ts = min(2048, s)
    rest_rows = wb.shape[1]
    small0 = 2 * d // kq
    h2, fb, df0, do2, do1, dxp, macc, dbup = _mlp_fwd_bwd(x2, out1, tgt, vecs, b_up, wb, tm)
    gb = _dw(h2, df0, False, ts, "dw_up", rows_total=rest_rows)
    gb = _dw(fb, do2, True, ts, "dw_down", into=gb, rows_total=rest_rows, row_block=1)
    dz, dya, dyb, xacc, dcaw, dcbw, dbin = _mix_bwd(do1, z, u1, ya, yb, vecs, caw, cbw, wb, tm, rc)
    gb = _dw_rows(u3, dya, ts, "dw_a_out", gb, small0)
    gb = _dw_rows(vv, dyb, ts, "dw_b_out", gb, small0 + 1)
    gb = _dw_rows(mg, do1, ts, "dw_o", gb, small0 + 2)

    trb, tra = rest_rows // 8, d // 8
    ga, rb = _dw_carrying(h1, dz, ts, "dw_in", _SiblingHalf, gb)
    pb = _add_halves(gb, rb, halves_idx, trb, "rs_add_halves_rest")
    ra = _exchange(_SiblingHalf, ga, "rs_to_sibling")
    pa = _add_halves(ga, ra, halves_idx, tra, "rs_add_halves_in")
    gx, iacc, r3a, r3b = _in_bwd(dz, x2, dxp, vecs, wa, tm, _ChipBlocks, [pa, pb])
    fa = _add_chips(ga, ra, r3a, chips_idx, tra, "rs_add_chips_in")
    fb_ = _add_chips(gb, rb, r3b, chips_idx, trb, "rs_add_chips_rest")
    g_in, g_b = _rs_join_halves(fa, fb_)

    table = jnp.concatenate([iacc, macc, dbin, dcaw, xacc, dcbw, dbup], axis=0)
    gathered = _all_gather_small(table, "gather_small_grads")

    names = [n for n, _ in _SMALL]
    res = _small_update(
        gathered, q_arr,
        [given[n] for n in names], [given["m_" + n] for n in names], [given["v_" + n] for n in names],
        [sq(conv_a_w), sq(conv_b_w)], [sq(m_conv_a_w), sq(m_conv_b_w)], [sq(v_conv_a_w), sq(v_conv_b_w)])
    loss = res[1].reshape(())
    upd = {}
    for p, n in enumerate(names + ["conv_a_w", "conv_b_w"]):
        upd[n] = res[2 + 4 * p:6 + 4 * p]

    dmod_all = jnp.stack([gathered[:, r, :] for r in _SMALL[0][1]], axis=1).reshape(N_DEV, 6 * d)
    dmod_sh = lax.dynamic_slice(dmod_all, (0, q * n_ada), (N_DEV, n_ada))
    g_ada = _ada_bwd(c_all.T, dmod_sh)
    upd["w_ada"] = _adam(sq(w_ada), g_ada, sq(m_w_ada), sq(v_w_ada), 0, min(256, d), "adam_w_ada")

    upd["w_in"] = _adam(sq(w_in), g_in, sq(m_w_in), sq(v_w_in), 0, min(256, d), "adam_w_in")
    r0 = 0
    for n in ("w_up", "w_down", "w_a_out", "w_b_out", "w_o"):
        w = sq(given[n])
        upd[n] = _adam(w, g_b, sq(given["m_" + n]), sq(given["v_" + n]), r0, min(256, w.shape[0]), "adam_" + n)
        r0 += w.shape[0]

    order = ["w_ada", "b_ada", "w_in", "b_in", "conv_a_w", "conv_a_b", "ln_a_g", "ln_a_b", "w_a_out", "b_a_out", "conv_b_w",
             "w_b_out", "w_o", "b_o", "ln1_g", "ln1_b", "w_up", "b_up", "w_down", "b_down", "ln2_g", "ln2_b"]
    outs = [loss, gx.reshape(x.shape)]
    for k in range(4):
        outs += [upd[n][k].reshape(given[n].shape) for n in order]
    return tuple(outs)
```

```python
import jax
import jax.numpy as jnp
from jax import lax
from jax.experimental import pallas as pl
from jax.experimental.pallas import tpu as pltpu

F32 = jnp.float32
BF16 = jnp.bfloat16
MESH = pl.DeviceIdType.MESH

LN_EPS = 1e-5
DEPTH = 1
ALPHA = (2.0 * DEPTH) ** 0.25
CONV_A = 31
CONV_B = 3
SUBLANES = 8
HALO_A = 32
HALO_B = 8
N_CHIPS = 4
N_DEV = 8
ADAM_LR = 0.001
ADAM_B1 = 0.9
ADAM_B2 = 0.999
ADAM_EPS = 1e-08
ADAM_WD = 0.01
ADAM_STEP = 10
VMEM_LIMIT = 56 * 1024 * 1024

V_SHIFT1, V_SCALE1, V_GATE1, V_SHIFT2, V_SCALE2, V_GATE2 = 0, 1, 2, 3, 4, 5
V_CAB, V_LNAG, V_LNAB, V_BAO, V_BO, V_LN1G, V_LN1B, V_BDN, V_LN2G, V_LN2B = 6, 7, 8, 9, 10, 11, 12, 13, 14, 15

M_LN2G, M_LN2B, M_GATE2, M_BDN, M_SHIFT2, M_SCALE2, M_LN1G, M_LN1B, M_GATE1, M_BO, M_LOSS = range(11)
X_BAO, X_LNAG, X_LNAB, X_CAB = range(4)
I_SHIFT1, I_SCALE1 = 0, 1

T_I, T_M, T_BIN, T_CAW, T_X, T_CBW, T_BUP, T_ROWS = 0, 8, 24, 32, 64, 72, 80, 88
T_LOSS = T_M + M_LOSS
_SMALL = (
    ("b_ada", (T_I + I_SHIFT1, T_I + I_SCALE1, T_M + M_GATE1, T_M + M_SHIFT2, T_M + M_SCALE2, T_M + M_GATE2)),
    ("b_in", tuple(T_BIN + j for j in range(7))),
    ("conv_a_b", (T_X + X_CAB,)), ("ln_a_g", (T_X + X_LNAG,)), ("ln_a_b", (T_X + X_LNAB,)), ("b_a_out", (T_X + X_BAO,)),
    ("b_o", (T_M + M_BO,)), ("ln1_g", (T_M + M_LN1G,)), ("ln1_b", (T_M + M_LN1B,)),
    ("b_up", tuple(T_BUP + j for j in range(4))),
    ("b_down", (T_M + M_BDN,)), ("ln2_g", (T_M + M_LN2G,)), ("ln2_b", (T_M + M_LN2B,)),
)


def _pcall(body, **kw):
    return pl.pallas_call(body, **kw)


def _cparams(**kw):
    return pltpu.CompilerParams(vmem_limit_bytes=VMEM_LIMIT, **kw)


def _seq(n):
    return _cparams(dimension_semantics=("arbitrary",) * n)


def _full(shape, single=False):
    nd = len(shape)
    if single:
        return pl.BlockSpec(shape, lambda *_: (0,) * nd, pipeline_mode=pl.Buffered(1))
    return pl.BlockSpec(shape, lambda *_: (0,) * nd)


def _rows(tm, width):
    return pl.BlockSpec((tm, width), lambda i: (i, 0))


def _sig(x):
    return jax.nn.sigmoid(x)


def _ln(x):
    mu = jnp.mean(x, axis=-1, keepdims=True)
    xc = x - mu
    var = jnp.mean(xc * xc, axis=-1, keepdims=True)
    rstd = lax.rsqrt(var + LN_EPS)
    return xc * rstd, rstd


def _ln_bwd(dxh, xh, rstd):
    m1 = jnp.mean(dxh, axis=-1, keepdims=True)
    m2 = jnp.mean(dxh * xh, axis=-1, keepdims=True)
    return rstd * (dxh - m1 - xh * m2)


def _rsum(v):
    return jnp.sum(v, axis=0, keepdims=True)


def _dot(a, b):
    return jnp.dot(a, b, preferred_element_type=F32)


def _dot_nt(a, b):
    return lax.dot_general(a, b, (((1,), (1,)), ((), ())), preferred_element_type=F32)


def _dot_tn(a, b):
    return lax.dot_general(a, b, (((0,), (0,)), ((), ())), preferred_element_type=F32)


def _my_pos():
    return lax.axis_index("x"), lax.axis_index("y"), lax.axis_index("c")


def _other_chips(x, y):
    return [(1 - x, y), (x, 1 - y), (1 - x, 1 - y)]


def _small_gather(v_ref, out_ref, send_sems, recv_sems, local_sem):
    x, y, cc = _my_pos()
    me = 4 * x + 2 * y + cc
    mine = pltpu.make_async_copy(v_ref, out_ref.at[me], local_sem)
    mine.start()
    sends = []
    for rel in range(1, N_DEV):
        rx, ry, rc = (rel >> 2) & 1, (rel >> 1) & 1, rel & 1
        peer = (1 - x if rx else x, 1 - y if ry else y, 1 - cc if rc else cc)
        cp = pltpu.make_async_remote_copy(
            src_ref=v_ref, dst_ref=out_ref.at[me], send_sem=send_sems.at[rel - 1], recv_sem=recv_sems.at[rel - 1],
            device_id=peer, device_id_type=MESH)
        cp.start()
        sends.append(cp)
    for rel in range(1, N_DEV):
        rx, ry, rc = (rel >> 2) & 1, (rel >> 1) & 1, rel & 1
        peer = (1 - x if rx else x, 1 - y if ry else y, 1 - cc if rc else cc)
        slot = 4 * peer[0] + 2 * peer[1] + peer[2]
        pltpu.make_async_remote_copy(
            src_ref=v_ref, dst_ref=out_ref.at[slot], send_sem=send_sems.at[rel - 1], recv_sem=recv_sems.at[rel - 1],
            device_id=peer, device_id_type=MESH).wait_recv()
    for cp in sends:
        cp.wait_send()
    mine.wait()


_SMALL_GATHER_SEMS = [pltpu.SemaphoreType.DMA((N_DEV - 1,)), pltpu.SemaphoreType.DMA((N_DEV - 1,)),
                      pltpu.SemaphoreType.DMA]


def _all_gather_small(v, name):
    r, c = v.shape

    def body(*refs):
        _small_gather(*refs)

    return _pcall(
        body, name=name,
        out_shape=jax.ShapeDtypeStruct((N_DEV, r, c), v.dtype),
        in_specs=[pl.BlockSpec(memory_space=pltpu.VMEM)],
        out_specs=pl.BlockSpec(memory_space=pltpu.VMEM),
        scratch_shapes=list(_SMALL_GATHER_SEMS),
        compiler_params=_cparams(),
    )(v)


def _prologue(pre, w_sh, b_sh, wa):
    r, d = pre.shape
    n_ada = w_sh.shape[1]
    ng = _GatherShards.n_sems

    def body(pre_ref, w_ref, b_ref, wai_ref, pre_all_ref, mod_all_ref, wao_ref, mod_ref,
             s1, r1, l1, s2, r2, l2, sg, rg):
        gather = (wai_ref, wao_ref, sg, rg)
        _GatherShards.start(*gather)
        _small_gather(pre_ref, pre_all_ref, s1, r1, l1)
        cv = jnp.concatenate([pre_all_ref[dev, 0:1, :] for dev in range(N_DEV)], axis=0)
        ca = cv * _sig(cv)
        mod_ref[...] = jnp.dot(ca, w_ref[...], preferred_element_type=F32, precision=lax.Precision.HIGHEST) + b_ref[...]
        _small_gather(mod_ref, mod_all_ref, s2, r2, l2)
        _GatherShards.relay(*gather)
        _GatherShards.finish(*gather)

    vm = pl.BlockSpec(memory_space=pltpu.VMEM)
    any_spec = pl.BlockSpec(memory_space=pl.ANY)
    return _pcall(
        body, name="prologue",
        out_shape=[jax.ShapeDtypeStruct((N_DEV, r, d), F32), jax.ShapeDtypeStruct((N_DEV, N_DEV, n_ada), F32),
                   jax.ShapeDtypeStruct(wa.shape, wa.dtype)],
        in_specs=[vm, vm, vm, any_spec], out_specs=[vm, vm, any_spec], input_output_aliases={3: 2},
        scratch_shapes=[pltpu.VMEM((N_DEV, n_ada), F32)] + list(_SMALL_GATHER_SEMS) + list(_SMALL_GATHER_SEMS)
        + [pltpu.SemaphoreType.DMA((ng,)), pltpu.SemaphoreType.DMA((ng,))],
        compiler_params=_cparams(),
    )(pre, w_sh, b_sh, wa)


def _place_shard(parts, q_idx, name):
    rows = sum(p.shape[0] for p in parts)
    w = parts[0].shape[1]

    def body(q_ref, *refs):
        o_ref = refs[-1]
        r0 = 0
        for p_ref in refs[:-1]:
            n = p_ref.shape[0]
            o_ref[r0:r0 + n, :] = p_ref[...].astype(BF16)
            r0 += n

    grid_spec = pltpu.PrefetchScalarGridSpec(
        num_scalar_prefetch=1, grid=(1,),
        in_specs=[pl.BlockSpec(p.shape, lambda i, q: (0, 0)) for p in parts],
        out_specs=pl.BlockSpec((None, rows, w), lambda i, q: (q[0], 0, 0)))
    return _pcall(body, name=name, grid_spec=grid_spec, out_shape=jax.ShapeDtypeStruct((N_CHIPS, rows, w), BF16),
                  compiler_params=_seq(1))(q_idx, *parts)


class _GatherShards:
    n_sems = 6

    @staticmethod
    def _half(ref, slot, h):
        rows = ref.shape[1] // 2
        return ref.at[slot, pl.ds(h * rows, rows)]

    @classmethod
    def _copies(cls, in_ref, out_ref, send_sems, recv_sems):
        x, y, c = _my_pos()
        q = 2 * x + y
        sibling = (x, y, 1 - c)
        sends, landed, forwards, passed = [], [], [], []
        for j, chip in enumerate(_other_chips(x, y)):
            qj = 2 * chip[0] + chip[1]

            def copy(src, dst, k, to):
                return pltpu.make_async_remote_copy(src_ref=src, dst_ref=dst, send_sem=send_sems.at[k],
                                                    recv_sem=recv_sems.at[k], device_id=to, device_id_type=MESH)

            mine, theirs = cls._half(out_ref, qj, c), cls._half(out_ref, qj, 1 - c)
            sends.append(copy(cls._half(in_ref, q, c), cls._half(out_ref, q, c), j, (*chip, c)))
            landed.append(copy(mine, mine, j, (*chip, c)))
            forwards.append(copy(mine, mine, 3 + j, sibling))
            passed.append(copy(theirs, theirs, 3 + j, sibling))
        return sends, landed, forwards, passed

    @classmethod
    def start(cls, *refs):
        for cp in cls._copies(*refs)[0]:
            cp.start()

    @classmethod
    def relay(cls, *refs):
        _, landed, forwards, _ = cls._copies(*refs)
        for arrived, onward in zip(landed, forwards):
            arrived.wait_recv()
            onward.start()

    @classmethod
    def finish(cls, *refs):
        sends, _, forwards, passed = cls._copies(*refs)
        for cp in passed:
            cp.wait_recv()
        for cp in sends + forwards:
            cp.wait_send()


class _SiblingHalf:
    n_sems = 1

    @staticmethod
    def out_shape(g):
        return jax.ShapeDtypeStruct((g.shape[0], g.shape[1] // 2, g.shape[2]), g.dtype)

    @staticmethod
    def copies(g_ref, r_ref, send_sems, recv_sems, base):
        x, y, c = _my_pos()
        rows = r_ref.shape[1]
        return [pltpu.make_async_remote_copy(
            src_ref=g_ref.at[:, pl.ds((1 - c) * rows, rows)], dst_ref=r_ref,
            send_sem=send_sems.at[base], recv_sem=recv_sems.at[base], device_id=(x, y, 1 - c), device_id_type=MESH)]


class _ChipBlocks:
    n_sems = 3

    @staticmethod
    def out_shape(p):
        return jax.ShapeDtypeStruct(p.shape, p.dtype)

    @staticmethod
    def copies(p_ref, r_ref, send_sems, recv_sems, base):
        x, y, c = _my_pos()
        return [pltpu.make_async_remote_copy(
            src_ref=p_ref.at[j], dst_ref=r_ref.at[j], send_sem=send_sems.at[base + j], recv_sem=recv_sems.at[base + j],
            device_id=(*chip, c), device_id_type=MESH) for j, chip in enumerate(_other_chips(x, y))]


def _plan_copies(plan, src_refs, dst_refs, send_sems, recv_sems):
    cps = []
    for b, (s_ref, d_ref) in enumerate(zip(src_refs, dst_refs)):
        cps += plan.copies(s_ref, d_ref, send_sems, recv_sems, b * plan.n_sems)
    return cps


def _plan_sems(plan, n):
    return [pltpu.SemaphoreType.DMA((n * plan.n_sems,)), pltpu.SemaphoreType.DMA((n * plan.n_sems,))]


def _exchange(plan, src, name):
    def body(s_ref, o_ref, send_sems, recv_sems):
        cps = _plan_copies(plan, [s_ref], [o_ref], send_sems, recv_sems)
        for cp in cps:
            cp.start()
        for cp in cps:
            cp.wait()

    any_spec = pl.BlockSpec(memory_space=pl.ANY)
    return _pcall(
        body, name=name, out_shape=plan.out_shape(src), in_specs=[any_spec], out_specs=any_spec,
        scratch_shapes=_plan_sems(plan, 1), compiler_params=_cparams(),
    )(src)


def _carried_start(plan, first, comm):
    @pl.when(first)
    def _():
        for cp in _plan_copies(plan, *comm):
            cp.start()


def _carried_wait(plan, last, comm):
    @pl.when(last)
    def _():
        for cp in _plan_copies(plan, *comm):
            cp.wait()


def _rs_join_halves(fa, fb):
    bufs = (fa, fb)
    nb = len(bufs)

    def body(a_ref, b_ref, ao_ref, bo_ref, send_sems, recv_sems):
        x, y, c = _my_pos()
        srcs, outs = (a_ref, b_ref), (ao_ref, bo_ref)
        cps = []
        for b in range(nb):
            rows = srcs[b].shape[0] // 2
            cp = pltpu.make_async_remote_copy(
                src_ref=srcs[b].at[pl.ds(c * rows, rows)], dst_ref=outs[b].at[pl.ds(c * rows, rows)],
                send_sem=send_sems.at[b], recv_sem=recv_sems.at[b], device_id=(x, y, 1 - c), device_id_type=MESH)
            cp.start()
            cps.append(cp)
        for b in range(nb):
            rows = srcs[b].shape[0] // 2
            theirs = outs[b].at[pl.ds((1 - c) * rows, rows)]
            pltpu.make_async_remote_copy(
                src_ref=theirs, dst_ref=theirs, send_sem=send_sems.at[b], recv_sem=recv_sems.at[b],
                device_id=(x, y, 1 - c), device_id_type=MESH).wait_recv()
        for cp in cps:
            cp.wait_send()

    any_spec = pl.BlockSpec(memory_space=pl.ANY)
    return _pcall(
        body, name="rs_join_halves",
        out_shape=[jax.ShapeDtypeStruct(b.shape, b.dtype) for b in bufs],
        in_specs=[any_spec] * nb, out_specs=[any_spec] * nb, input_output_aliases={0: 0, 1: 1},
        scratch_shapes=[pltpu.SemaphoreType.DMA((nb,)), pltpu.SemaphoreType.DMA((nb,))],
        compiler_params=_cparams(),
    )(*bufs)


def _add_halves(g, r, idx, tr, name):
    _, rows, w = r.shape
    nt = rows // tr

    def body(i_ref, g_ref, r_ref, o_ref):
        o_ref[...] = (g_ref[...] + r_ref[...]).astype(BF16)

    grid_spec = pltpu.PrefetchScalarGridSpec(
        num_scalar_prefetch=1, grid=(3, nt),
        in_specs=[pl.BlockSpec((None, tr, w), lambda j, i, ix: (ix[1 + j], ix[0] * nt + i, 0)),
                  pl.BlockSpec((None, tr, w), lambda j, i, ix: (ix[1 + j], i, 0))],
        out_specs=pl.BlockSpec((None, tr, w), lambda j, i, ix: (j, i, 0)))
    return _pcall(body, name=name, grid_spec=grid_spec,
                  out_shape=jax.ShapeDtypeStruct((3, rows, w), BF16), compiler_params=_seq(2))(idx, g, r)


def _add_chips(g, r, r3, idx, tr, name):
    _, rows, w = r.shape
    nt = rows // tr

    def body(i_ref, g_ref, r_ref, a_ref, b_ref, c_ref, o_ref):
        own = g_ref[...] + r_ref[...]
        o_ref[...] = ((own + a_ref[...].astype(F32)) + b_ref[...].astype(F32)) + c_ref[...].astype(F32)

    def other(j):
        return pl.BlockSpec((None, tr, w), lambda i, ix: (j, i, 0))

    grid_spec = pltpu.PrefetchScalarGridSpec(
        num_scalar_prefetch=1, grid=(nt,),
        in_specs=[pl.BlockSpec((None, tr, w), lambda i, ix: (ix[0], ix[1] * nt + i, 0)),
                  pl.BlockSpec((None, tr, w), lambda i, ix: (ix[0], i, 0)), other(0), other(1), other(2)],
        out_specs=pl.BlockSpec((tr, w), lambda i, ix: (ix[1] * nt + i, 0)))
    return _pcall(body, name=name, grid_spec=grid_spec,
                  out_shape=jax.ShapeDtypeStruct((2 * rows, w), F32), compiler_params=_seq(1))(idx, g, r, r3, r3, r3)


def _ada_bwd(c_all_t, dmod_sh):
    def body(c_ref, d_ref, o_ref):
        cv = c_ref[...]
        ca = cv * _sig(cv)
        o_ref[...] = jnp.dot(ca, d_ref[...], preferred_element_type=F32, precision=lax.Precision.HIGHEST)

    return _pcall(body, name="ada_bwd", out_shape=jax.ShapeDtypeStruct((c_all_t.shape[0], dmod_sh.shape[1]), F32),
                  compiler_params=_cparams())(c_all_t, dmod_sh)


def _conv_causal(ext_ref, sh_ref, w_ref, ntaps, halo, bias, out_ref, tm, d, rc):
    off = halo - (ntaps - 1)
    n = tm + halo - SUBLANES
    started = False
    for s in range(SUBLANES):
        taps = [(k, (off + k) // SUBLANES * SUBLANES) for k in range(ntaps) if (off + k) % SUBLANES == s]
        if not taps:
            continue
        if s == 0:
            src = ext_ref
        else:
            sh_ref[0:n, :] = ext_ref[s:s + n, :]
            src = sh_ref
        for r0 in range(0, tm, rc):
            if started:
                acc = out_ref[r0:r0 + rc, :]
            else:
                acc = jnp.zeros((rc, d), F32) if bias is None else jnp.broadcast_to(bias, (rc, d))
            for k, a in taps:
                acc = acc + w_ref[k:k + 1, :] * src[r0 + a:r0 + a + rc, :]
            out_ref[r0:r0 + rc, :] = acc
        started = True


def _conv_adjoint(dp_ref, sh_ref, ext_ref, w_ref, dx_ref, dw_ref, ntaps, halo, tm, d, rc):
    off = halo - (ntaps - 1)
    lead = SUBLANES + ntaps - 1
    n = tm + halo
    row = lax.broadcasted_iota(jnp.int32, (SUBLANES, d), 0)
    started = False
    for s in range(SUBLANES):
        taps = [(k, (lead - k) // SUBLANES * SUBLANES) for k in range(ntaps) if (lead - k) % SUBLANES == s]
        wtaps = [(k, s + off + k - SUBLANES) for k in range(ntaps) if (-(off + k)) % SUBLANES == s]
        if not taps and not wtaps:
            continue
        if s == 0:
            src = dp_ref
        else:
            sh_ref[0:n, :] = dp_ref[s:s + n, :]
            src = sh_ref

        if taps:
            for r0 in range(0, tm, rc):
                acc = dx_ref[r0:r0 + rc, :] if started else jnp.zeros((rc, d), F32)
                for k, a in taps:
                    acc = acc + w_ref[k:k + 1, :] * src[r0 + a:r0 + a + rc, :]
                dx_ref[r0:r0 + rc, :] = acc
            started = True
        for k, e in wtaps:
            tot = _rsum(src[0:tm, :] * ext_ref[e:e + tm, :])
            tail = src[tm:tm + SUBLANES, :] * ext_ref[tm + e:tm + e + SUBLANES, :]
            dw_ref[k:k + 1, :] += tot + _rsum(jnp.where(row < SUBLANES - s, tail, 0.0))


def _fwd_in(x, vecs, wa, b_in, wb, tm):
    s, d = x.shape
    nq, _, nw = wa.shape
    nt = s // tm

    def body(x_ref, v_ref, w_ref, b_ref, wbi_ref, h_ref, z_ref, wbo_ref, send_sems, recv_sems):
        i = pl.program_id(0)
        gather = (wbi_ref, wbo_ref, send_sems, recv_sems)
        pl.when(i == 0)(lambda: _GatherShards.start(*gather))
        pl.when(i == nt // 2)(lambda: _GatherShards.relay(*gather))
        xh, _ = _ln(x_ref[...])
        h = (xh * (1.0 + v_ref[V_SCALE1:V_SCALE1 + 1, :]) + v_ref[V_SHIFT1:V_SHIFT1 + 1, :]).astype(BF16)
        h_ref[...] = h
        for q in range(nq):
            z_ref[:, q * nw:(q + 1) * nw] = _dot(h, w_ref[q]) + b_ref[:, q * nw:(q + 1) * nw]
        pl.when(i == nt - 1)(lambda: _GatherShards.finish(*gather))

    any_spec = pl.BlockSpec(memory_space=pl.ANY)
    n = _GatherShards.n_sems
    return _pcall(
        body, name="fwd_in", grid=(nt,),
        in_specs=[_rows(tm, d), _full(vecs.shape), _full(wa.shape, single=True), _full(b_in.shape), any_spec],
        out_specs=[_rows(tm, d), _rows(tm, nq * nw), any_spec],
        out_shape=[jax.ShapeDtypeStruct((s, d), BF16), jax.ShapeDtypeStruct((s, nq * nw), F32),
                   jax.ShapeDtypeStruct(wb.shape, wb.dtype)],
        input_output_aliases={4: 2},
        scratch_shapes=[pltpu.SemaphoreType.DMA((n,)), pltpu.SemaphoreType.DMA((n,))],
        compiler_params=_seq(1),
    )(x, vecs, wa, b_in, wb)


def _fwd_mix(z, vecs, caw, cbw, wb, tm, rc):
    s = z.shape[0]
    d = vecs.shape[1]
    nq = wb.shape[0]
    kq = d // nq
    base = 2 * d // kq

    def body(z_ref, v_ref, caw_ref, cbw_ref, wao_ref, wbo_ref, wo_ref,
             u1_ref, ya_ref, yb_ref, o1_ref, u3_ref, vv_ref, mg_ref, ext_ref, sh_ref, pext_ref, q_ref):
        @pl.when(pl.program_id(0) == 0)
        def _():
            ext_ref[0:HALO_A, :] = jnp.zeros((HALO_A, d), F32)
            pext_ref[0:HALO_B, :] = jnp.zeros((HALO_B, d), F32)

        ext_ref[HALO_A:HALO_A + tm, :] = z_ref[:, 0:d] * _sig(z_ref[:, d:2 * d])
        _conv_causal(ext_ref, sh_ref, caw_ref, CONV_A, HALO_A, v_ref[V_CAB:V_CAB + 1, :], u1_ref, tm, d, rc)
        ext_ref[0:HALO_A, :] = ext_ref[tm:tm + HALO_A, :]
        xa, _ = _ln(u1_ref[...])
        u2 = xa * v_ref[V_LNAG:V_LNAG + 1, :] + v_ref[V_LNAB:V_LNAB + 1, :]
        u3 = (u2 * _sig(u2)).astype(BF16)
        u3_ref[...] = u3
        ya = jnp.broadcast_to(v_ref[V_BAO:V_BAO + 1, :], (tm, d))
        for q in range(nq):
            ya = ya + _dot(u3[:, q * kq:(q + 1) * kq], wao_ref[q])
        ya_ref[...] = ya

        pext_ref[HALO_B:HALO_B + tm, :] = z_ref[:, 3 * d:4 * d] * z_ref[:, 4 * d:5 * d]
        _conv_causal(pext_ref, sh_ref, cbw_ref, CONV_B, HALO_B, None, q_ref, tm, d, rc)
        pext_ref[0:HALO_B, :] = pext_ref[tm:tm + HALO_B, :]
        vv = (z_ref[:, 2 * d:3 * d] * q_ref[...]).astype(BF16)
        vv_ref[...] = vv
        yb = jnp.zeros((tm, d), F32)
        for q in range(nq):
            yb = yb + _dot(vv[:, q * kq:(q + 1) * kq], wbo_ref[q])
        yb_ref[...] = yb

        mg = (_sig(z_ref[:, 5 * d:6 * d]) * ya + _sig(z_ref[:, 6 * d:7 * d]) * yb).astype(BF16)
        mg_ref[...] = mg
        o1 = jnp.broadcast_to(v_ref[V_BO:V_BO + 1, :], (tm, d))
        for q in range(nq):
            o1 = o1 + _dot(mg[:, q * kq:(q + 1) * kq], wo_ref[q])
        o1_ref[...] = o1

    def wspec(j):
        return pl.BlockSpec((nq, kq, d), lambda i: (0, base + j, 0), pipeline_mode=pl.Buffered(1))

    f32o = jax.ShapeDtypeStruct((s, d), F32)
    b16o = jax.ShapeDtypeStruct((s, d), BF16)
    return _pcall(
        body, name="fwd_mix", grid=(s // tm,),
        in_specs=[_rows(tm, 7 * d), _full(vecs.shape), _full(caw.shape), _full(cbw.shape), wspec(0), wspec(1), wspec(2)],
        out_specs=[_rows(tm, d)] * 7,
        out_shape=[f32o, f32o, f32o, f32o, b16o, b16o, b16o],
        scratch_shapes=[pltpu.VMEM((HALO_A + tm, d), F32), pltpu.VMEM((HALO_A + tm, d), F32),
                        pltpu.VMEM((HALO_B + tm, d), F32), pltpu.VMEM((tm, d), F32)],
        compiler_params=_seq(1),
    )(z, vecs, caw, cbw, wb, wb, wb)


def _mlp_fwd_bwd(x, out1, tgt, vecs, b_up, wb, tm, groups):
    s, d = x.shape
    nq = wb.shape[0]
    dff = nq * d
    hm = tm // groups

    def body(x_ref, o1_ref, t_ref, v_ref, bup_ref, wup_ref, wdn_ref,
             h2_ref, f_ref, df0_ref, do2_ref, do1_ref, dxp_ref, acc_ref, dbup_ref, f0_ref):
        @pl.when(pl.program_id(0) == 0)
        def _():
            acc_ref[...] = jnp.zeros(acc_ref.shape, F32)
            dbup_ref[...] = jnp.zeros(dbup_ref.shape, F32)

        def vec(r):
            return v_ref[r:r + 1, :]

        def accum(r, val):
            acc_ref[r:r + 1, :] += _rsum(val)

        for g in range(groups):
            rs = slice(g * hm, (g + 1) * hm)
            out1v = o1_ref[rs, :]
            r1 = ALPHA * x_ref[rs, :] + (1.0 + vec(V_GATE1)) * out1v
            xh1, rstd1 = _ln(r1)
            x1 = xh1 * vec(V_LN1G) + vec(V_LN1B)
            xn1, rstdn = _ln(x1)
            h2 = (xn1 * (1.0 + vec(V_SCALE2)) + vec(V_SHIFT2)).astype(BF16)
            h2_ref[rs, :] = h2
            out2 = jnp.broadcast_to(vec(V_BDN), (hm, d))
            for q in range(nq):
                f0 = _dot(h2, wup_ref[q]) + bup_ref[:, q * d:(q + 1) * d]
                rl = jnp.maximum(f0, 0.0)
                f0_ref[rs, q * d:(q + 1) * d] = rl
                fb = (rl * rl).astype(BF16)
                f_ref[rs, q * d:(q + 1) * d] = fb
                out2 = out2 + _dot(fb, wdn_ref[q])
            r2 = ALPHA * x1 + (1.0 + vec(V_GATE2)) * out2
            xh2, rstd2 = _ln(r2)
            yv = xh2 * vec(V_LN2G) + vec(V_LN2B)
            err = yv - t_ref[rs, :]
            accum(M_LOSS, err * err)
            dy = err * (1.0 / d)
            accum(M_LN2G, dy * xh2)
            accum(M_LN2B, dy)
            dr2 = _ln_bwd(dy * vec(V_LN2G), xh2, rstd2)
            accum(M_GATE2, dr2 * out2)
            dout2 = (1.0 + vec(V_GATE2)) * dr2
            accum(M_BDN, dout2)
            do2b = dout2.astype(BF16)
            do2_ref[rs, :] = do2b
            dh2 = jnp.zeros((hm, d), F32)
            for q in range(nq):
                df0 = _dot_nt(do2b, wdn_ref[q]) * (2.0 * f0_ref[rs, q * d:(q + 1) * d])
                dbup_ref[q:q + 1, :] += _rsum(df0)
                df0b = df0.astype(BF16)
                df0_ref[rs, q * d:(q + 1) * d] = df0b
                dh2 = dh2 + _dot_nt(df0b, wup_ref[q])
            accum(M_SHIFT2, dh2)
            accum(M_SCALE2, dh2 * xn1)
            dx1 = ALPHA * dr2 + _ln_bwd(dh2 * (1.0 + vec(V_SCALE2)), xn1, rstdn)
            accum(M_LN1G, dx1 * xh1)
            accum(M_LN1B, dx1)
            dr1 = _ln_bwd(dx1 * vec(V_LN1G), xh1, rstd1)
            accum(M_GATE1, dr1 * out1v)
            dout1 = (1.0 + vec(V_GATE1)) * dr1
            accum(M_BO, dout1)
            do1_ref[rs, :] = dout1.astype(BF16)
            dxp_ref[rs, :] = ALPHA * dr1

    def wspec(j):
        return pl.BlockSpec((nq, d, d), lambda i: (0, j, 0), pipeline_mode=pl.Buffered(1))

    b16 = lambda w: jax.ShapeDtypeStruct((s, w), BF16)
    return _pcall(
        body, name="mlp_fwd_bwd", grid=(s // tm,),
        in_specs=[_rows(tm, d), _rows(tm, d), _rows(tm, d), _full(vecs.shape), _full(b_up.shape), wspec(0), wspec(1)],
        out_specs=[_rows(tm, d), _rows(tm, dff), _rows(tm, dff), _rows(tm, d), _rows(tm, d), _rows(tm, d),
                   _full((16, d)), _full((SUBLANES, d))],
        out_shape=[b16(d), b16(dff), b16(dff), b16(d), b16(d), jax.ShapeDtypeStruct((s, d), F32),
                   jax.ShapeDtypeStruct((16, d), F32), jax.ShapeDtypeStruct((SUBLANES, d), F32)],
        scratch_shapes=[pltpu.VMEM((tm, dff), F32)],
        compiler_params=_seq(1),
    )(x, out1, tgt, vecs, b_up, wb, wb)


def _mix_bwd(dout1, z, u1, ya, yb, vecs, caw, cbw, wb, tm, rc):
    s = z.shape[0]
    d = vecs.shape[1]
    nq = wb.shape[0]
    kq = d // nq
    base = 2 * d // kq
    nt = s // tm
    hb = tm // HALO_A

    def body(do1_ref, z_ref, zh_ref, u1_ref, ya_ref, yb_ref, v_ref, caw_ref, cbw_ref, wao_ref, wbo_ref, wo_ref,
             dz_ref, dya_ref, dyb_ref, acc_ref, dcaw_ref, dcbw_ref, dbin_ref,
             ext_ref, du1p_ref, sh_ref, pext_ref, dqp_ref, tmp_ref):
        i = pl.program_id(0)

        @pl.when(i == 0)
        def _():
            acc_ref[...] = jnp.zeros(acc_ref.shape, F32)
            dcaw_ref[...] = jnp.zeros(dcaw_ref.shape, F32)
            dcbw_ref[...] = jnp.zeros(dcbw_ref.shape, F32)
            dbin_ref[...] = jnp.zeros(dbin_ref.shape, F32)
            du1p_ref[0:SUBLANES, :] = jnp.zeros((SUBLANES, d), F32)
            du1p_ref[SUBLANES + tm:SUBLANES + tm + HALO_A, :] = jnp.zeros((HALO_A, d), F32)
            dqp_ref[0:SUBLANES, :] = jnp.zeros((SUBLANES, d), F32)
            dqp_ref[SUBLANES + tm:SUBLANES + tm + HALO_B, :] = jnp.zeros((HALO_B, d), F32)

        def vec(r):
            return v_ref[r:r + 1, :]

        def accum(r, val):
            acc_ref[r:r + 1, :] += _rsum(val)

        def put_dz(j, val):
            dbin_ref[j:j + 1, :] += _rsum(val)
            dz_ref[:, j * d:(j + 1) * d] = val.astype(BF16)

        has_history = i < nt - 1

        do1 = do1_ref[...]
        dmg = jnp.concatenate([_dot_nt(do1, wo_ref[q]) for q in range(nq)], axis=1)
        sga = _sig(z_ref[:, 5 * d:6 * d])
        sgb = _sig(z_ref[:, 6 * d:7 * d])
        dya = dmg * sga
        dyb = dmg * sgb
        accum(X_BAO, dya)
        put_dz(5, dya * ya_ref[...] * (1.0 - sga))
        put_dz(6, dyb * yb_ref[...] * (1.0 - sgb))
        dyab = dya.astype(BF16)
        dybb = dyb.astype(BF16)
        dya_ref[...] = dyab
        dyb_ref[...] = dybb

        du3 = jnp.concatenate([_dot_nt(dyab, wao_ref[q]) for q in range(nq)], axis=1)
        xa, rstda = _ln(u1_ref[...])
        u2 = xa * vec(V_LNAG) + vec(V_LNAB)
        s2 = _sig(u2)
        du2 = du3 * (s2 * (1.0 + u2 * (1.0 - s2)))
        accum(X_LNAG, du2 * xa)
        accum(X_LNAB, du2)
        du1 = _ln_bwd(du2 * vec(V_LNAG), xa, rstda)
        accum(X_CAB, du1)
        du1p_ref[SUBLANES:SUBLANES + tm, :] = du1
        sg = _sig(z_ref[:, d:2 * d])
        aval = z_ref[:, 0:d]
        ext_ref[HALO_A:HALO_A + tm, :] = aval * sg
        ext_ref[0:HALO_A, :] = jnp.where(has_history, zh_ref[:, 0:d] * _sig(zh_ref[:, d:2 * d]), 0.0)
        _conv_adjoint(du1p_ref, sh_ref, ext_ref, caw_ref, tmp_ref, dcaw_ref, CONV_A, HALO_A, tm, d, rc)
        du1p_ref[SUBLANES + tm:SUBLANES + tm + HALO_A, :] = du1p_ref[SUBLANES:SUBLANES + HALO_A, :]
        du0 = tmp_ref[...]
        put_dz(0, du0 * sg)
        put_dz(1, du0 * aval * sg * (1.0 - sg))

        dv = jnp.concatenate([_dot_nt(dybb, wbo_ref[q]) for q in range(nq)], axis=1)
        bgc = z_ref[:, 3 * d:4 * d]
        bx = z_ref[:, 4 * d:5 * d]
        pext_ref[HALO_B:HALO_B + tm, :] = bgc * bx
        pext_ref[0:HALO_B, :] = jnp.where(
            has_history, zh_ref[HALO_A - HALO_B:HALO_A, 3 * d:4 * d] * zh_ref[HALO_A - HALO_B:HALO_A, 4 * d:5 * d], 0.0)
        _conv_causal(pext_ref, sh_ref, cbw_ref, CONV_B, HALO_B, None, tmp_ref, tm, d, rc)
        put_dz(2, dv * tmp_ref[...])
        dqp_ref[SUBLANES:SUBLANES + tm, :] = dv * z_ref[:, 2 * d:3 * d]
        _conv_adjoint(dqp_ref, sh_ref, pext_ref, cbw_ref, tmp_ref, dcbw_ref, CONV_B, HALO_B, tm, d, rc)
        dqp_ref[SUBLANES + tm:SUBLANES + tm + HALO_B, :] = dqp_ref[SUBLANES:SUBLANES + HALO_B, :]
        dp = tmp_ref[...]
        put_dz(3, dp * bx)
        put_dz(4, dp * bgc)

    def rev(width):
        return pl.BlockSpec((tm, width), lambda i: (nt - 1 - i, 0))

    def wspec(j):
        return pl.BlockSpec((nq, kq, d), lambda i: (0, base + j, 0), pipeline_mode=pl.Buffered(1))

    halo = pl.BlockSpec((HALO_A, 7 * d), lambda i: (jnp.maximum((nt - 1 - i) * hb - 1, 0), 0))
    b16 = jax.ShapeDtypeStruct((s, d), BF16)
    acc8 = jax.ShapeDtypeStruct((SUBLANES, d), F32)
    return _pcall(
        body, name="mix_bwd", grid=(nt,),
        in_specs=[rev(d), rev(7 * d), halo, rev(d), rev(d), rev(d), _full(vecs.shape), _full(caw.shape), _full(cbw.shape),
                  wspec(0), wspec(1), wspec(2)],
        out_specs=[rev(7 * d), rev(d), rev(d), _full((SUBLANES, d)), _full((HALO_A, d)), _full((HALO_B, d)),
                   _full((SUBLANES, d))],
        out_shape=[jax.ShapeDtypeStruct((s, 7 * d), BF16), b16, b16, acc8,
                   jax.ShapeDtypeStruct((HALO_A, d), F32), jax.ShapeDtypeStruct((HALO_B, d), F32), acc8],
        scratch_shapes=[pltpu.VMEM((HALO_A + tm, d), F32), pltpu.VMEM((SUBLANES + tm + HALO_A, d), F32),
                        pltpu.VMEM((tm + HALO_A, d), F32), pltpu.VMEM((HALO_B + tm, d), F32),
                        pltpu.VMEM((SUBLANES + tm + HALO_B, d), F32), pltpu.VMEM((tm, d), F32)],
        compiler_params=_seq(1),
    )(dout1, z, z, u1, ya, yb, vecs, caw, cbw, wb, wb, wb)


def _in_bwd(dz, x, dxp, vecs, wa, tm, plan, plan_srcs):
    s, d = x.shape
    nq, _, nw = wa.shape
    nt = s // tm
    nc = len(plan_srcs)

    def body(dz_ref, x_ref, dxp_ref, v_ref, w_ref, *rest):
        src_refs, (gx_ref, acc_ref), dst_refs = rest[:nc], rest[nc:nc + 2], rest[nc + 2:2 * nc + 2]
        send_sems, recv_sems = rest[2 * nc + 2:]
        i = pl.program_id(0)
        comm = (src_refs, dst_refs, send_sems, recv_sems)
        _carried_start(plan, i == 0, comm)

        @pl.when(i == 0)
        def _():
            acc_ref[...] = jnp.zeros(acc_ref.shape, F32)

        dh1 = jnp.zeros((tm, d), F32)
        for q in range(nq):
            dh1 = dh1 + _dot_nt(dz_ref[:, q * nw:(q + 1) * nw], w_ref[q])
        xh, rstd = _ln(x_ref[...])
        acc_ref[I_SHIFT1:I_SHIFT1 + 1, :] += _rsum(dh1)
        acc_ref[I_SCALE1:I_SCALE1 + 1, :] += _rsum(dh1 * xh)
        gx_ref[...] = dxp_ref[...] + _ln_bwd(dh1 * (1.0 + v_ref[V_SCALE1:V_SCALE1 + 1, :]), xh, rstd)
        _carried_wait(plan, i == nt - 1, comm)

    any_spec = pl.BlockSpec(memory_space=pl.ANY)
    return _pcall(
        body, name="in_bwd", grid=(nt,),
        in_specs=[_rows(tm, nq * nw), _rows(tm, d), _rows(tm, d), _full(vecs.shape), _full(wa.shape, single=True)]
        + [any_spec] * nc,
        out_specs=[_rows(tm, d), _full((SUBLANES, d))] + [any_spec] * nc,
        out_shape=[jax.ShapeDtypeStruct((s, d), F32), jax.ShapeDtypeStruct((SUBLANES, d), F32)]
        + [plan.out_shape(p) for p in plan_srcs],
        scratch_shapes=_plan_sems(plan, nc),
        compiler_params=_seq(1),
    )(dz, x, dxp, vecs, wa, *plan_srcs)


def _dw(a, b, split_a, ts, name, into=None, rows_total=None, row_block=0):
    s = a.shape[0]
    ka = a.shape[1] // N_CHIPS if split_a else a.shape[1]
    nb = b.shape[1] if split_a else b.shape[1] // N_CHIPS
    rows_total = ka if rows_total is None else rows_total

    def body(a_ref, b_ref, *rest):
        o_ref = rest[-1]

        @pl.when(pl.program_id(1) == 0)
        def _():
            o_ref[...] = jnp.zeros(o_ref.shape, F32)

        o_ref[...] += _dot_tn(a_ref[...], b_ref[...])

    a_spec = pl.BlockSpec((ts, ka), (lambda q, i: (i, q)) if split_a else (lambda q, i: (i, 0)))
    b_spec = pl.BlockSpec((ts, nb), (lambda q, i: (i, 0)) if split_a else (lambda q, i: (i, q)))
    extra = {} if into is None else dict(input_output_aliases={2: 0})
    return _pcall(
        body, name=name, grid=(N_CHIPS, s // ts),
        in_specs=[a_spec, b_spec] + ([] if into is None else [pl.BlockSpec(memory_space=pl.ANY)]),
        out_specs=pl.BlockSpec((None, ka, nb), lambda q, i: (q, row_block, 0)),
        out_shape=jax.ShapeDtypeStruct((N_CHIPS, rows_total, nb), F32),
        compiler_params=_seq(2), **extra,
    )(*((a, b) if into is None else (a, b, into)))


def _dw_carrying(a, b, ts, name, plan, plan_src):
    s, k = a.shape
    nb = b.shape[1] // N_CHIPS
    ns = s // ts

    def body(a_ref, b_ref, src_ref, o_ref, dst_ref, send_sems, recv_sems):
        q, i = pl.program_id(0), pl.program_id(1)
        comm = ([src_ref], [dst_ref], send_sems, recv_sems)
        _carried_start(plan, jnp.logical_and(q == 0, i == 0), comm)

        @pl.when(i == 0)
        def _():
            o_ref[...] = jnp.zeros(o_ref.shape, F32)

        o_ref[...] += _dot_tn(a_ref[...], b_ref[...])
        _carried_wait(plan, jnp.logical_and(q == N_CHIPS - 1, i == ns - 1), comm)

    any_spec = pl.BlockSpec(memory_space=pl.ANY)
    return _pcall(
        body, name=name, grid=(N_CHIPS, ns),
        in_specs=[pl.BlockSpec((ts, k), lambda q, i: (i, 0)), pl.BlockSpec((ts, nb), lambda q, i: (i, q)), any_spec],
        out_specs=[pl.BlockSpec((None, k, nb), lambda q, i: (q, 0, 0)), any_spec],
        out_shape=[jax.ShapeDtypeStruct((N_CHIPS, k, nb), F32), plan.out_shape(plan_src)],
        scratch_shapes=_plan_sems(plan, 1),
        compiler_params=_seq(2),
    )(a, b, plan_src)


def _dw_rows(a, b, ts, name, into, row_block):
    s, k = a.shape
    n = b.shape[1]
    kq = k // N_CHIPS

    def body(a_ref, b_ref, buf_ref, o_ref):
        @pl.when(pl.program_id(0) == 0)
        def _():
            o_ref[...] = jnp.zeros(o_ref.shape, F32)

        res = _dot_tn(a_ref[...], b_ref[...])
        for q in range(N_CHIPS):
            o_ref[q] += res[q * kq:(q + 1) * kq, :]

    return _pcall(
        body, name=name, grid=(s // ts,),
        in_specs=[_rows(ts, k), _rows(ts, n), pl.BlockSpec(memory_space=pl.ANY)],
        out_specs=pl.BlockSpec((N_CHIPS, kq, n), lambda i: (0, row_block, 0)),
        out_shape=jax.ShapeDtypeStruct(into.shape, F32), input_output_aliases={2: 0},
        compiler_params=_seq(1),
    )(a, b, into)


def _adam_math(w, g, m, v):
    m2 = ADAM_B1 * m + (1.0 - ADAM_B1) * g
    v2 = ADAM_B2 * v + (1.0 - ADAM_B2) * (g * g)
    m_hat = m2 / (1.0 - ADAM_B1 ** ADAM_STEP)
    v_hat = v2 / (1.0 - ADAM_B2 ** ADAM_STEP)
    delta = -ADAM_LR * (m_hat / (jnp.sqrt(v_hat) + ADAM_EPS) + ADAM_WD * w)
    return delta, m2, v2


def _adam(w, g, m, v, g_row0, tr, name):
    r, c = w.shape
    blk0 = g_row0 // tr

    def body(w_ref, g_ref, m_ref, v_ref, go_ref, d_ref, mo_ref, vo_ref):
        gv = g_ref[...]
        go_ref[...] = gv
        d_ref[...], mo_ref[...], vo_ref[...] = _adam_math(w_ref[...], gv, m_ref[...], v_ref[...])

    spec = _rows(tr, c)
    g_spec = pl.BlockSpec((tr, c), lambda i: (blk0 + i, 0))
    o = jax.ShapeDtypeStruct((r, c), F32)
    return _pcall(body, name=name, grid=(r // tr,), in_specs=[spec, g_spec, spec, spec], out_specs=[spec] * 4,
                  out_shape=[o, o, o, o], compiler_params=_seq(1))(w, g, m, v)


def _small_update(gathered, q_idx, small_w, small_m, small_v, conv_w, conv_m, conv_v):
    d = gathered.shape[2]
    ns = len(_SMALL)
    cw = conv_w[0].shape[1]
    conv_rows = ((T_CAW, CONV_A), (T_CBW, CONV_B))

    def body(q_ref, g_ref, *refs):
        ins, outs = refs[:3 * (ns + 2)], refs[3 * (ns + 2):]
        tot_ref, loss_ref = outs[0], outs[1]
        outs = outs[2:]
        tot = g_ref[0]
        for dev in range(1, N_DEV):
            tot = tot + g_ref[dev]
        tot_ref[...] = tot
        loss_ref[...] = (0.5 / d) * jnp.sum(tot_ref[T_LOSS:T_LOSS + 1, :], axis=1, keepdims=True)
        for p, (_, rows) in enumerate(_SMALL):
            w_ref, m_ref, v_ref = ins[p], ins[ns + 2 + p], ins[2 * (ns + 2) + p]
            go, do, mo, vo = outs[4 * p:4 * p + 4]
            for j, row in enumerate(rows):
                sl = slice(j * d, (j + 1) * d)
                gv = tot_ref[row:row + 1, :]
                go[:, sl] = gv
                do[:, sl], mo[:, sl], vo[:, sl] = _adam_math(w_ref[:, sl], gv, m_ref[:, sl], v_ref[:, sl])
        for p, (row, taps) in enumerate(conv_rows):
            w_ref, m_ref, v_ref = ins[ns + p], ins[ns + 2 + ns + p], ins[2 * (ns + 2) + ns + p]
            go, do, mo, vo = outs[4 * (ns + p):4 * (ns + p) + 4]
            gv = tot_ref[row:row + taps, 0:cw]
            for qq in range(1, N_CHIPS):
                gv = jnp.where(q_ref[0] == qq, tot_ref[row:row + taps, qq * cw:(qq + 1) * cw], gv)
            go[...] = gv
            do[...], mo[...], vo[...] = _adam_math(w_ref[...], gv, m_ref[...], v_ref[...])

    params = list(small_w) + list(conv_w) + list(small_m) + list(conv_m) + list(small_v) + list(conv_v)
    out_shape = [jax.ShapeDtypeStruct((T_ROWS, d), F32), jax.ShapeDtypeStruct((1, 1), F32)]
    for w in list(small_w) + list(conv_w):
        out_shape += [jax.ShapeDtypeStruct(w.shape, F32)] * 4
    vm = pl.BlockSpec(memory_space=pltpu.VMEM)
    return _pcall(
        body, name="small_update", out_shape=out_shape,
        in_specs=[pl.BlockSpec(memory_space=pltpu.SMEM), vm] + [vm] * len(params),
        out_specs=[vm] * len(out_shape), compiler_params=_cparams(),
    )(q_idx, gathered, *params)


def kernel(x, c, w_ada, b_ada, w_in, b_in, conv_a_w, conv_a_b, ln_a_g, ln_a_b, w_a_out, b_a_out, conv_b_w, w_b_out, w_o, b_o, ln1_g, ln1_b, w_up, b_up, w_down, b_down, ln2_g, ln2_b, loss_target, m_w_ada, m_b_ada, m_w_in, m_b_in, m_conv_a_w, m_conv_a_b, m_ln_a_g, m_ln_a_b, m_w_a_out, m_b_a_out, m_conv_b_w, m_w_b_out, m_w_o, m_b_o, m_ln1_g, m_ln1_b, m_w_up, m_b_up, m_w_down, m_b_down, m_ln2_g, m_ln2_b, v_w_ada, v_b_ada, v_w_in, v_b_in, v_conv_a_w, v_conv_a_b, v_ln_a_g, v_ln_a_b, v_w_a_out, v_b_a_out, v_conv_b_w, v_w_b_out, v_w_o, v_b_o, v_ln1_g, v_ln1_b, v_w_up, v_b_up, v_w_down, v_b_down, v_ln2_g, v_ln2_b):
    given = dict(locals())
    s, d = x.shape[1], x.shape[2]
    xi, yi, ci = _my_pos()
    q = 2 * xi + yi
    me = 4 * xi + 2 * yi + ci
    i32 = jnp.int32
    q_arr = jnp.reshape(q, (1,)).astype(i32)
    others = [2 * ox + oy for ox, oy in _other_chips(xi, yi)]
    halves_idx = jnp.stack([ci] + others).astype(i32)
    chips_idx = jnp.stack([q, ci]).astype(i32)
    kq = d // N_CHIPS
    tm = min(256, s)
    rc = min(32, tm)

    def sq(a):
        return a.reshape(a.shape[1:])

    x2, tgt = sq(x), sq(loss_target)

    wa = _place_shard([sq(w_in)], q_arr, "place_w_in")
    wb = _place_shard([sq(w_up), sq(w_down), sq(w_a_out), sq(w_b_out), sq(w_o)], q_arr, "place_w_rest")

    n_ada = w_ada.shape[2]
    pre = jnp.concatenate([
        jnp.broadcast_to(c, (SUBLANES, d)),
        jnp.pad(sq(conv_a_w), ((0, HALO_A - CONV_A), (0, d - kq))),
        jnp.pad(sq(conv_b_w), ((0, HALO_B - CONV_B), (0, d - kq)))], axis=0)
    b_ada_sh = lax.dynamic_slice(b_ada, (0, q * n_ada), (1, n_ada))
    pre_all, mod_all, wa = _prologue(pre, sq(w_ada), b_ada_sh, wa)
    c_all = pre_all[:, 0, :]
    caw = jnp.concatenate([pre_all[2 * p, SUBLANES:SUBLANES + HALO_A, :kq] for p in range(N_CHIPS)], axis=1)
    cbw = jnp.concatenate([pre_all[2 * p, SUBLANES + HALO_A:, :kq] for p in range(N_CHIPS)], axis=1)
    mod_rows = lax.dynamic_slice(mod_all, (0, me, 0), (N_DEV, 1, n_ada))[0::2, 0, :]
    mod = mod_rows.reshape(6, d)
    vecs = jnp.concatenate([mod, conv_a_b, ln_a_g, ln_a_b, b_a_out, b_o, ln1_g, ln1_b, b_down, ln2_g, ln2_b], axis=0)

    h1, z, wb = _fwd_in(x2, vecs, wa, b_in, wb, tm)
    u1, ya, yb, out1, u3, vv, mg = _fwd_mix(z, vecs, caw, cbw, wb, tm, rc)

    ts = min(2048, s)
    rest_rows = wb.shape[1]
    small0 = 2 * d // kq
    h2, fb, df0, do2, do1, dxp, macc, dbup = _mlp_fwd_bwd(x2, out1, tgt, vecs, b_up, wb, tm, 1)
    gb = _dw(h2, df0, False, ts, "dw_up", rows_total=rest_rows)
    gb = _dw(fb, do2, True, ts, "dw_down", into=gb, rows_total=rest_rows, row_block=1)
    dz, dya, dyb, xacc, dcaw, dcbw, dbin = _mix_bwd(do1, z, u1, ya, yb, vecs, caw, cbw, wb, tm, rc)
    gb = _dw_rows(u3, dya, ts, "dw_a_out", gb, small0)
    gb = _dw_rows(vv, dyb, ts, "dw_b_out", gb, small0 + 1)
    gb = _dw_rows(mg, do1, ts, "dw_o", gb, small0 + 2)

    trb, tra = rest_rows // 8, d // 8
    ga, rb = _dw_carrying(h1, dz, ts, "dw_in", _SiblingHalf, gb)
    pb = _add_halves(gb, rb, halves_idx, trb, "rs_add_halves_rest")
    ra = _exchange(_SiblingHalf, ga, "rs_to_sibling")
    pa = _add_halves(ga, ra, halves_idx, tra, "rs_add_halves_in")
    gx, iacc, r3a, r3b = _in_bwd(dz, x2, dxp, vecs, wa, tm, _ChipBlocks, [pa, pb])
    fa = _add_chips(ga, ra, r3a, chips_idx, tra, "rs_add_chips_in")
    fb_ = _add_chips(gb, rb, r3b, chips_idx, trb, "rs_add_chips_rest")
    g_in, g_b = _rs_join_halves(fa, fb_)

    table = jnp.concatenate([iacc, macc, dbin, dcaw, xacc, dcbw, dbup], axis=0)
    gathered = _all_gather_small(table, "gather_small_grads")

    names = [n for n, _ in _SMALL]
    res = _small_update(
        gathered, q_arr,
        [given[n] for n in names], [given["m_" + n] for n in names], [given["v_" + n] for n in names],
        [sq(conv_a_w), sq(conv_b_w)], [sq(m_conv_a_w), sq(m_conv_b_w)], [sq(v_conv_a_w), sq(v_conv_b_w)])
    loss = res[1].reshape(())
    upd = {}
    for p, n in enumerate(names + ["conv_a_w", "conv_b_w"]):
        upd[n] = res[2 + 4 * p:6 + 4 * p]

    dmod_all = jnp.stack([gathered[:, r, :] for r in _SMALL[0][1]], axis=1).reshape(N_DEV, 6 * d)
    dmod_sh = lax.dynamic_slice(dmod_all, (0, q * n_ada), (N_DEV, n_ada))
    g_ada = _ada_bwd(c_all.T, dmod_sh)
    upd["w_ada"] = _adam(sq(w_ada), g_ada, sq(m_w_ada), sq(v_w_ada), 0, min(256, d), "adam_w_ada")

    upd["w_in"] = _adam(sq(w_in), g_in, sq(m_w_in), sq(v_w_in), 0, min(256, d), "adam_w_in")
    r0 = 0
    for n in ("w_up", "w_down", "w_a_out", "w_b_out", "w_o"):
        w = sq(given[n])
        upd[n] = _adam(w, g_b, sq(given["m_" + n]), sq(given["v_" + n]), r0, min(256, w.shape[0]), "adam_" + n)
        r0 += w.shape[0]

    order = ["w_ada", "b_ada", "w_in", "b_in", "conv_a_w", "conv_a_b", "ln_a_g", "ln_a_b", "w_a_out", "b_a_out", "conv_b_w",
             "w_b_out", "w_o", "b_o", "ln1_g", "ln1_b", "w_up", "b_up", "w_down", "b_down", "ln2_g", "ln2_b"]
    outs = [loss, gx.reshape(x.shape)]
    for k in range(4):
        outs += [upd[n][k].reshape(given[n].shape) for n in order]
    return tuple(outs)
```

```python
import jax
import jax.numpy as jnp
from jax import lax
from jax.experimental import pallas as pl
from jax.experimental.pallas import tpu as pltpu

F32 = jnp.float32
BF16 = jnp.bfloat16
MESH = pl.DeviceIdType.MESH

LN_EPS = 1e-5
DEPTH = 1
ALPHA = (2.0 * DEPTH) ** 0.25
CONV_A = 31
CONV_B = 3
SUBLANES = 8
HALO_A = 32
HALO_B = 8
SHIFTED_COPIES = 4
N_CHIPS = 4
N_DEV = 8
ADAM_LR = 0.001
ADAM_B1 = 0.9
ADAM_B2 = 0.999
ADAM_EPS = 1e-08
ADAM_WD = 0.01
ADAM_STEP = 10
VMEM_LIMIT = 56 * 1024 * 1024

V_SHIFT1, V_SCALE1, V_GATE1, V_SHIFT2, V_SCALE2, V_GATE2 = 0, 1, 2, 3, 4, 5
V_CAB, V_LNAG, V_LNAB, V_BAO, V_BO, V_LN1G, V_LN1B, V_BDN, V_LN2G, V_LN2B = 6, 7, 8, 9, 10, 11, 12, 13, 14, 15

M_LN2G, M_LN2B, M_GATE2, M_BDN, M_SHIFT2, M_SCALE2, M_LN1G, M_LN1B, M_GATE1, M_BO, M_LOSS = range(11)
X_BAO, X_LNAG, X_LNAB, X_CAB = range(4)
I_SHIFT1, I_SCALE1 = 0, 1

T_I, T_M, T_BIN, T_CAW, T_X, T_CBW, T_BUP, T_ROWS = 0, 8, 24, 32, 64, 72, 80, 88
T_LOSS = T_M + M_LOSS
_SMALL = (
    ("b_ada", (T_I + I_SHIFT1, T_I + I_SCALE1, T_M + M_GATE1, T_M + M_SHIFT2, T_M + M_SCALE2, T_M + M_GATE2)),
    ("b_in", tuple(T_BIN + j for j in range(7))),
    ("conv_a_b", (T_X + X_CAB,)), ("ln_a_g", (T_X + X_LNAG,)), ("ln_a_b", (T_X + X_LNAB,)), ("b_a_out", (T_X + X_BAO,)),
    ("b_o", (T_M + M_BO,)), ("ln1_g", (T_M + M_LN1G,)), ("ln1_b", (T_M + M_LN1B,)),
    ("b_up", tuple(T_BUP + j for j in range(4))),
    ("b_down", (T_M + M_BDN,)), ("ln2_g", (T_M + M_LN2G,)), ("ln2_b", (T_M + M_LN2B,)),
)


def _pcall(body, **kw):
    return pl.pallas_call(body, **kw)


def _cparams(**kw):
    return pltpu.CompilerParams(vmem_limit_bytes=VMEM_LIMIT, **kw)


def _seq(n):
    return _cparams(dimension_semantics=("arbitrary",) * n)


def _full(shape, single=False):
    nd = len(shape)
    if single:
        return pl.BlockSpec(shape, lambda *_: (0,) * nd, pipeline_mode=pl.Buffered(1))
    return pl.BlockSpec(shape, lambda *_: (0,) * nd)


def _rows(tm, width):
    return pl.BlockSpec((tm, width), lambda i: (i, 0))


def _sig(x):
    return jax.nn.sigmoid(x)


def _ln(x):
    mu = jnp.mean(x, axis=-1, keepdims=True)
    xc = x - mu
    var = jnp.mean(xc * xc, axis=-1, keepdims=True)
    rstd = lax.rsqrt(var + LN_EPS)
    return xc * rstd, rstd


def _ln_bwd(dxh, xh, rstd):
    m1 = jnp.mean(dxh, axis=-1, keepdims=True)
    m2 = jnp.mean(dxh * xh, axis=-1, keepdims=True)
    return rstd * (dxh - m1 - xh * m2)


def _rsum(v):
    return jnp.sum(v, axis=0, keepdims=True)


def _dot(a, b):
    return jnp.dot(a, b, preferred_element_type=F32)


def _dot_nt(a, b):
    return lax.dot_general(a, b, (((1,), (1,)), ((), ())), preferred_element_type=F32)


def _dot_tn(a, b):
    return lax.dot_general(a, b, (((0,), (0,)), ((), ())), preferred_element_type=F32)


def _my_pos():
    return lax.axis_index("x"), lax.axis_index("y"), lax.axis_index("c")


def _other_chips(x, y):
    return [(1 - x, y), (x, 1 - y), (1 - x, 1 - y)]


def _small_gather(v_ref, out_ref, send_sems, recv_sems, local_sem):
    x, y, cc = _my_pos()
    me = 4 * x + 2 * y + cc
    mine = pltpu.make_async_copy(v_ref, out_ref.at[me], local_sem)
    mine.start()
    sends = []
    for rel in range(1, N_DEV):
        rx, ry, rc = (rel >> 2) & 1, (rel >> 1) & 1, rel & 1
        peer = (1 - x if rx else x, 1 - y if ry else y, 1 - cc if rc else cc)
        cp = pltpu.make_async_remote_copy(
            src_ref=v_ref, dst_ref=out_ref.at[me], send_sem=send_sems.at[rel - 1], recv_sem=recv_sems.at[rel - 1],
            device_id=peer, device_id_type=MESH)
        cp.start()
        sends.append(cp)
    for rel in range(1, N_DEV):
        rx, ry, rc = (rel >> 2) & 1, (rel >> 1) & 1, rel & 1
        peer = (1 - x if rx else x, 1 - y if ry else y, 1 - cc if rc else cc)
        slot = 4 * peer[0] + 2 * peer[1] + peer[2]
        pltpu.make_async_remote_copy(
            src_ref=v_ref, dst_ref=out_ref.at[slot], send_sem=send_sems.at[rel - 1], recv_sem=recv_sems.at[rel - 1],
            device_id=peer, device_id_type=MESH).wait_recv()
    for cp in sends:
        cp.wait_send()
    mine.wait()


_SMALL_GATHER_SEMS = [pltpu.SemaphoreType.DMA((N_DEV - 1,)), pltpu.SemaphoreType.DMA((N_DEV - 1,)),
                      pltpu.SemaphoreType.DMA]


def _all_gather_small(v, name):
    r, c = v.shape

    def body(*refs):
        _small_gather(*refs)

    return _pcall(
        body, name=name,
        out_shape=jax.ShapeDtypeStruct((N_DEV, r, c), v.dtype),
        in_specs=[pl.BlockSpec(memory_space=pltpu.VMEM)],
        out_specs=pl.BlockSpec(memory_space=pltpu.VMEM),
        scratch_shapes=list(_SMALL_GATHER_SEMS),
        compiler_params=_cparams(),
    )(v)


def _prologue(pre, w_sh, b_sh, wa):
    r, d = pre.shape
    n_ada = w_sh.shape[1]
    ng = _GatherShards.n_sems

    def body(pre_ref, w_ref, b_ref, wai_ref, pre_all_ref, mod_all_ref, wao_ref, mod_ref,
             s1, r1, l1, s2, r2, l2, sg, rg):
        gather = (wai_ref, wao_ref, sg, rg)
        _GatherShards.start(*gather)
        _small_gather(pre_ref, pre_all_ref, s1, r1, l1)
        cv = jnp.concatenate([pre_all_ref[dev, 0:1, :] for dev in range(N_DEV)], axis=0)
        ca = cv * _sig(cv)
        mod_ref[...] = jnp.dot(ca, w_ref[...], preferred_element_type=F32, precision=lax.Precision.HIGHEST) + b_ref[...]
        _small_gather(mod_ref, mod_all_ref, s2, r2, l2)
        _GatherShards.relay(*gather)
        _GatherShards.finish(*gather)

    vm = pl.BlockSpec(memory_space=pltpu.VMEM)
    any_spec = pl.BlockSpec(memory_space=pl.ANY)
    return _pcall(
        body, name="prologue",
        out_shape=[jax.ShapeDtypeStruct((N_DEV, r, d), F32), jax.ShapeDtypeStruct((N_DEV, N_DEV, n_ada), F32),
                   jax.ShapeDtypeStruct(wa.shape, wa.dtype)],
        in_specs=[vm, vm, vm, any_spec], out_specs=[vm, vm, any_spec], input_output_aliases={3: 2},
        scratch_shapes=[pltpu.VMEM((N_DEV, n_ada), F32)] + list(_SMALL_GATHER_SEMS) + list(_SMALL_GATHER_SEMS)
        + [pltpu.SemaphoreType.DMA((ng,)), pltpu.SemaphoreType.DMA((ng,))],
        compiler_params=_cparams(),
    )(pre, w_sh, b_sh, wa)


def _place_shard(parts, q_idx, name):
    rows = sum(p.shape[0] for p in parts)
    w = parts[0].shape[1]

    def body(q_ref, *refs):
        o_ref = refs[-1]
        r0 = 0
        for p_ref in refs[:-1]:
            n = p_ref.shape[0]
            o_ref[r0:r0 + n, :] = p_ref[...].astype(BF16)
            r0 += n

    grid_spec = pltpu.PrefetchScalarGridSpec(
        num_scalar_prefetch=1, grid=(1,),
        in_specs=[pl.BlockSpec(p.shape, lambda i, q: (0, 0)) for p in parts],
        out_specs=pl.BlockSpec((None, rows, w), lambda i, q: (q[0], 0, 0)))
    return _pcall(body, name=name, grid_spec=grid_spec, out_shape=jax.ShapeDtypeStruct((N_CHIPS, rows, w), BF16),
                  compiler_params=_seq(1))(q_idx, *parts)


class _GatherShards:
    n_sems = 6

    @staticmethod
    def _half(ref, slot, h):
        rows = ref.shape[1] // 2
        return ref.at[slot, pl.ds(h * rows, rows)]

    @classmethod
    def _copies(cls, in_ref, out_ref, send_sems, recv_sems):
        x, y, c = _my_pos()
        q = 2 * x + y
        sibling = (x, y, 1 - c)
        sends, landed, forwards, passed = [], [], [], []
        for j, chip in enumerate(_other_chips(x, y)):
            qj = 2 * chip[0] + chip[1]

            def copy(src, dst, k, to):
                return pltpu.make_async_remote_copy(src_ref=src, dst_ref=dst, send_sem=send_sems.at[k],
                                                    recv_sem=recv_sems.at[k], device_id=to, device_id_type=MESH)

            mine, theirs = cls._half(out_ref, qj, c), cls._half(out_ref, qj, 1 - c)
            sends.append(copy(cls._half(in_ref, q, c), cls._half(out_ref, q, c), j, (*chip, c)))
            landed.append(copy(mine, mine, j, (*chip, c)))
            forwards.append(copy(mine, mine, 3 + j, sibling))
            passed.append(copy(theirs, theirs, 3 + j, sibling))
        return sends, landed, forwards, passed

    @classmethod
    def start(cls, *refs):
        for cp in cls._copies(*refs)[0]:
            cp.start()

    @classmethod
    def relay(cls, *refs):
        _, landed, forwards, _ = cls._copies(*refs)
        for arrived, onward in zip(landed, forwards):
            arrived.wait_recv()
            onward.start()

    @classmethod
    def finish(cls, *refs):
        sends, _, forwards, passed = cls._copies(*refs)
        for cp in passed:
            cp.wait_recv()
        for cp in sends + forwards:
            cp.wait_send()


class _SiblingHalf:
    n_sems = 1

    @staticmethod
    def out_shape(g):
        return jax.ShapeDtypeStruct((g.shape[0], g.shape[1] // 2, g.shape[2]), g.dtype)

    @staticmethod
    def copies(g_ref, r_ref, send_sems, recv_sems, base):
        x, y, c = _my_pos()
        rows = r_ref.shape[1]
        return [pltpu.make_async_remote_copy(
            src_ref=g_ref.at[:, pl.ds((1 - c) * rows, rows)], dst_ref=r_ref,
            send_sem=send_sems.at[base], recv_sem=recv_sems.at[base], device_id=(x, y, 1 - c), device_id_type=MESH)]


class _ChipBlocks:
    n_sems = 3

    @staticmethod
    def out_shape(p):
        return jax.ShapeDtypeStruct(p.shape, p.dtype)

    @staticmethod
    def copies(p_ref, r_ref, send_sems, recv_sems, base):
        x, y, c = _my_pos()
        return [pltpu.make_async_remote_copy(
            src_ref=p_ref.at[j], dst_ref=r_ref.at[j], send_sem=send_sems.at[base + j], recv_sem=recv_sems.at[base + j],
            device_id=(*chip, c), device_id_type=MESH) for j, chip in enumerate(_other_chips(x, y))]


class _TableToAll:
    n_sems = N_DEV

    @staticmethod
    def out_shape(t):
        return jax.ShapeDtypeStruct((N_DEV,) + t.shape, t.dtype)

    @staticmethod
    def copies(t_ref, all_ref, send_sems, recv_sems, base):
        x, y, c = _my_pos()
        me = 4 * x + 2 * y + c
        cps = [pltpu.make_async_copy(t_ref, all_ref.at[me], send_sems.at[base + N_DEV - 1])]
        for rel in range(1, N_DEV):
            rx, ry, rc = (rel >> 2) & 1, (rel >> 1) & 1, rel & 1
            peer = (1 - x if rx else x, 1 - y if ry else y, 1 - c if rc else c)
            cps.append(_SlotCopy(t_ref, all_ref, me, 4 * peer[0] + 2 * peer[1] + peer[2], peer,
                                 send_sems.at[base + rel - 1], recv_sems.at[base + rel - 1]))
        return cps


class _SlotCopy:
    def __init__(self, src_ref, all_ref, my_slot, peer_slot, peer, send_sem, recv_sem):
        self._send = pltpu.make_async_remote_copy(src_ref=src_ref, dst_ref=all_ref.at[my_slot], send_sem=send_sem,
                                                  recv_sem=recv_sem, device_id=peer, device_id_type=MESH)
        self._recv = pltpu.make_async_remote_copy(src_ref=src_ref, dst_ref=all_ref.at[peer_slot], send_sem=send_sem,
                                                  recv_sem=recv_sem, device_id=peer, device_id_type=MESH)

    def start(self):
        self._send.start()

    def wait(self):
        self._send.wait_send()
        self._recv.wait_recv()


def _plan_copies(plans, src_refs, dst_refs, send_sems, recv_sems):
    cps, base = [], 0
    for plan, s_ref, d_ref in zip(plans, src_refs, dst_refs):
        cps += plan.copies(s_ref, d_ref, send_sems, recv_sems, base)
        base += plan.n_sems
    return cps


def _plan_sems(plans):
    n = sum(p.n_sems for p in plans)
    return [pltpu.SemaphoreType.DMA((n,)), pltpu.SemaphoreType.DMA((n,))]


def _exchange(plan, src, name):
    def body(s_ref, o_ref, send_sems, recv_sems):
        cps = _plan_copies([plan], [s_ref], [o_ref], send_sems, recv_sems)
        for cp in cps:
            cp.start()
        for cp in cps:
            cp.wait()

    any_spec = pl.BlockSpec(memory_space=pl.ANY)
    return _pcall(
        body, name=name, out_shape=plan.out_shape(src), in_specs=[any_spec], out_specs=any_spec,
        scratch_shapes=_plan_sems([plan]), compiler_params=_cparams(),
    )(src)


def _carried_start(plans, first, comm):
    @pl.when(first)
    def _():
        for cp in _plan_copies(plans, *comm):
            cp.start()


def _carried_wait(plans, last, comm):
    @pl.when(last)
    def _():
        for cp in _plan_copies(plans, *comm):
            cp.wait()


def _rs_join_halves(fa, fb):
    bufs = (fa, fb)
    nb = len(bufs)

    def body(a_ref, b_ref, ao_ref, bo_ref, send_sems, recv_sems):
        x, y, c = _my_pos()
        srcs, outs = (a_ref, b_ref), (ao_ref, bo_ref)
        cps = []
        for b in range(nb):
            rows = srcs[b].shape[0] // 2
            cp = pltpu.make_async_remote_copy(
                src_ref=srcs[b].at[pl.ds(c * rows, rows)], dst_ref=outs[b].at[pl.ds(c * rows, rows)],
                send_sem=send_sems.at[b], recv_sem=recv_sems.at[b], device_id=(x, y, 1 - c), device_id_type=MESH)
            cp.start()
            cps.append(cp)
        for b in range(nb):
            rows = srcs[b].shape[0] // 2
            theirs = outs[b].at[pl.ds((1 - c) * rows, rows)]
            pltpu.make_async_remote_copy(
                src_ref=theirs, dst_ref=theirs, send_sem=send_sems.at[b], recv_sem=recv_sems.at[b],
                device_id=(x, y, 1 - c), device_id_type=MESH).wait_recv()
        for cp in cps:
            cp.wait_send()

    any_spec = pl.BlockSpec(memory_space=pl.ANY)
    return _pcall(
        body, name="rs_join_halves",
        out_shape=[jax.ShapeDtypeStruct(b.shape, b.dtype) for b in bufs],
        in_specs=[any_spec] * nb, out_specs=[any_spec] * nb, input_output_aliases={0: 0, 1: 1},
        scratch_shapes=[pltpu.SemaphoreType.DMA((nb,)), pltpu.SemaphoreType.DMA((nb,))],
        compiler_params=_cparams(),
    )(*bufs)


def _add_halves(g, r, idx, tr, name):
    _, rows, w = r.shape
    nt = rows // tr

    def body(i_ref, g_ref, r_ref, o_ref):
        o_ref[...] = (g_ref[...] + r_ref[...]).astype(BF16)

    grid_spec = pltpu.PrefetchScalarGridSpec(
        num_scalar_prefetch=1, grid=(3, nt),
        in_specs=[pl.BlockSpec((None, tr, w), lambda j, i, ix: (ix[1 + j], ix[0] * nt + i, 0)),
                  pl.BlockSpec((None, tr, w), lambda j, i, ix: (ix[1 + j], i, 0))],
        out_specs=pl.BlockSpec((None, tr, w), lambda j, i, ix: (j, i, 0)))
    return _pcall(body, name=name, grid_spec=grid_spec,
                  out_shape=jax.ShapeDtypeStruct((3, rows, w), BF16), compiler_params=_seq(2))(idx, g, r)


def _add_chips(g, r, r3, idx, tr, name):
    _, rows, w = r.shape
    nt = rows // tr

    def body(i_ref, g_ref, r_ref, a_ref, b_ref, c_ref, o_ref):
        own = g_ref[...] + r_ref[...]
        o_ref[...] = ((own + a_ref[...].astype(F32)) + b_ref[...].astype(F32)) + c_ref[...].astype(F32)

    def other(j):
        return pl.BlockSpec((None, tr, w), lambda i, ix: (j, i, 0))

    grid_spec = pltpu.PrefetchScalarGridSpec(
        num_scalar_prefetch=1, grid=(nt,),
        in_specs=[pl.BlockSpec((None, tr, w), lambda i, ix: (ix[0], ix[1] * nt + i, 0)),
                  pl.BlockSpec((None, tr, w), lambda i, ix: (ix[0], i, 0)), other(0), other(1), other(2)],
        out_specs=pl.BlockSpec((tr, w), lambda i, ix: (ix[1] * nt + i, 0)))
    return _pcall(body, name=name, grid_spec=grid_spec,
                  out_shape=jax.ShapeDtypeStruct((2 * rows, w), F32), compiler_params=_seq(1))(idx, g, r, r3, r3, r3)


def _ada_bwd(c_all_t, dmod_sh):
    def body(c_ref, d_ref, o_ref):
        cv = c_ref[...]
        ca = cv * _sig(cv)
        o_ref[...] = jnp.dot(ca, d_ref[...], preferred_element_type=F32, precision=lax.Precision.HIGHEST)

    return _pcall(body, name="ada_bwd", out_shape=jax.ShapeDtypeStruct((c_all_t.shape[0], dmod_sh.shape[1]), F32),
                  compiler_params=_cparams())(c_all_t, dmod_sh)


def _conv_causal(ext_ref, sh_ref, w_ref, ntaps, halo, bias, out_ref, tm, d, rc):
    off = halo - (ntaps - 1)
    n = tm + halo - SUBLANES
    taps = {s: [(k, (off + k) // SUBLANES * SUBLANES) for k in range(ntaps) if (off + k) % SUBLANES == s]
            for s in range(SUBLANES)}
    residues = [s for s in range(SUBLANES) if taps[s]]
    per_pass = sh_ref.shape[0]
    started = False
    for g0 in range(0, len(residues), per_pass):
        srcs = {}
        for j, s in enumerate(residues[g0:g0 + per_pass]):
            if s == 0:
                srcs[s] = ext_ref
            else:
                sh_ref[j, 0:n, :] = ext_ref[s:s + n, :]
                srcs[s] = sh_ref.at[j]
        for r0 in range(0, tm, rc):
            if started:
                acc = out_ref[r0:r0 + rc, :]
            else:
                acc = jnp.zeros((rc, d), F32) if bias is None else jnp.broadcast_to(bias, (rc, d))
            for s, src in srcs.items():
                for k, a in taps[s]:
                    acc = acc + w_ref[k:k + 1, :] * src[r0 + a:r0 + a + rc, :]
            out_ref[r0:r0 + rc, :] = acc
        started = True


def _conv_adjoint(dp_ref, sh_ref, ext_ref, w_ref, dx_ref, dw_ref, ntaps, halo, tm, d, rc):
    off = halo - (ntaps - 1)
    lead = SUBLANES + ntaps - 1
    n = tm + halo
    row = lax.broadcasted_iota(jnp.int32, (SUBLANES, d), 0)
    taps = {s: [(k, (lead - k) // SUBLANES * SUBLANES) for k in range(ntaps) if (lead - k) % SUBLANES == s]
            for s in range(SUBLANES)}
    wtaps = {s: [(k, s + off + k - SUBLANES) for k in range(ntaps) if (-(off + k)) % SUBLANES == s]
             for s in range(SUBLANES)}
    residues = [s for s in range(SUBLANES) if taps[s] or wtaps[s]]
    per_pass = sh_ref.shape[0]
    rw = 2 * SUBLANES
    started = False
    for g0 in range(0, len(residues), per_pass):
        srcs = {}
        for j, s in enumerate(residues[g0:g0 + per_pass]):
            if s == 0:
                srcs[s] = dp_ref
            else:
                sh_ref[j, 0:n, :] = dp_ref[s:s + n, :]
                srcs[s] = sh_ref.at[j]
        if any(taps[s] for s in srcs):
            for r0 in range(0, tm, rc):
                acc = dx_ref[r0:r0 + rc, :] if started else jnp.zeros((rc, d), F32)
                for s, src in srcs.items():
                    for k, a in taps[s]:
                        acc = acc + w_ref[k:k + 1, :] * src[r0 + a:r0 + a + rc, :]
                dx_ref[r0:r0 + rc, :] = acc
            started = True
        for s, src in srcs.items():
            if not wtaps[s]:
                continue
            sums = [jnp.zeros((SUBLANES, d), F32) for _ in wtaps[s]]
            for r0 in range(0, tm, rw):
                g = src[r0:r0 + rw, :]
                for j, (_, e) in enumerate(wtaps[s]):
                    p = g * ext_ref[r0 + e:r0 + e + rw, :]
                    for r8 in range(0, rw, SUBLANES):
                        sums[j] = sums[j] + p[r8:r8 + SUBLANES, :]
            for j, (k, e) in enumerate(wtaps[s]):
                tail = src[tm:tm + SUBLANES, :] * ext_ref[tm + e:tm + e + SUBLANES, :]
                dw_ref[k:k + 1, :] += _rsum(sums[j] + jnp.where(row < SUBLANES - s, tail, 0.0))


def _fwd_in(x, vecs, wa, b_in, wb, tm):
    s, d = x.shape
    nq, _, nw = wa.shape
    nt = s // tm

    def body(x_ref, v_ref, w_ref, b_ref, wbi_ref, h_ref, z_ref, wbo_ref, send_sems, recv_sems):
        i = pl.program_id(0)
        gather = (wbi_ref, wbo_ref, send_sems, recv_sems)
        pl.when(i == 0)(lambda: _GatherShards.start(*gather))
        pl.when(i == nt // 2)(lambda: _GatherShards.relay(*gather))
        xh, _ = _ln(x_ref[...])
        h = (xh * (1.0 + v_ref[V_SCALE1:V_SCALE1 + 1, :]) + v_ref[V_SHIFT1:V_SHIFT1 + 1, :]).astype(BF16)
        h_ref[...] = h
        for q in range(nq):
            z_ref[:, q * nw:(q + 1) * nw] = _dot(h, w_ref[q]) + b_ref[:, q * nw:(q + 1) * nw]
        pl.when(i == nt - 1)(lambda: _GatherShards.finish(*gather))

    any_spec = pl.BlockSpec(memory_space=pl.ANY)
    n = _GatherShards.n_sems
    return _pcall(
        body, name="fwd_in", grid=(nt,),
        in_specs=[_rows(tm, d), _full(vecs.shape), _full(wa.shape, single=True), _full(b_in.shape), any_spec],
        out_specs=[_rows(tm, d), _rows(tm, nq * nw), any_spec],
        out_shape=[jax.ShapeDtypeStruct((s, d), BF16), jax.ShapeDtypeStruct((s, nq * nw), F32),
                   jax.ShapeDtypeStruct(wb.shape, wb.dtype)],
        input_output_aliases={4: 2},
        scratch_shapes=[pltpu.SemaphoreType.DMA((n,)), pltpu.SemaphoreType.DMA((n,))],
        compiler_params=_seq(1),
    )(x, vecs, wa, b_in, wb)


def _fwd_mix(z, vecs, caw, cbw, wb, tm, rc):
    s = z.shape[0]
    d = vecs.shape[1]
    nq = wb.shape[0]
    kq = d // nq
    base = 2 * d // kq

    def body(z_ref, v_ref, caw_ref, cbw_ref, wao_ref, wbo_ref, wo_ref,
             u1_ref, ya_ref, yb_ref, o1_ref, u3_ref, vv_ref, mg_ref, ext_ref, sh_ref, pext_ref, q_ref):
        @pl.when(pl.program_id(0) == 0)
        def _():
            ext_ref[0:HALO_A, :] = jnp.zeros((HALO_A, d), F32)
            pext_ref[0:HALO_B, :] = jnp.zeros((HALO_B, d), F32)

        ext_ref[HALO_A:HALO_A + tm, :] = z_ref[:, 0:d] * _sig(z_ref[:, d:2 * d])
        _conv_causal(ext_ref, sh_ref, caw_ref, CONV_A, HALO_A, v_ref[V_CAB:V_CAB + 1, :], u1_ref, tm, d, rc)
        ext_ref[0:HALO_A, :] = ext_ref[tm:tm + HALO_A, :]
        xa, _ = _ln(u1_ref[...])
        u2 = xa * v_ref[V_LNAG:V_LNAG + 1, :] + v_ref[V_LNAB:V_LNAB + 1, :]
        u3 = (u2 * _sig(u2)).astype(BF16)
        u3_ref[...] = u3
        ya = jnp.broadcast_to(v_ref[V_BAO:V_BAO + 1, :], (tm, d))
        for q in range(nq):
            ya = ya + _dot(u3[:, q * kq:(q + 1) * kq], wao_ref[q])
        ya_ref[...] = ya

        pext_ref[HALO_B:HALO_B + tm, :] = z_ref[:, 3 * d:4 * d] * z_ref[:, 4 * d:5 * d]
        _conv_causal(pext_ref, sh_ref, cbw_ref, CONV_B, HALO_B, None, q_ref, tm, d, rc)
        pext_ref[0:HALO_B, :] = pext_ref[tm:tm + HALO_B, :]
        vv = (z_ref[:, 2 * d:3 * d] * q_ref[...]).astype(BF16)
        vv_ref[...] = vv
        yb = jnp.zeros((tm, d), F32)
        for q in range(nq):
            yb = yb + _dot(vv[:, q * kq:(q + 1) * kq], wbo_ref[q])
        yb_ref[...] = yb

        mg = (_sig(z_ref[:, 5 * d:6 * d]) * ya + _sig(z_ref[:, 6 * d:7 * d]) * yb).astype(BF16)
        mg_ref[...] = mg
        o1 = jnp.broadcast_to(v_ref[V_BO:V_BO + 1, :], (tm, d))
        for q in range(nq):
            o1 = o1 + _dot(mg[:, q * kq:(q + 1) * kq], wo_ref[q])
        o1_ref[...] = o1

    def wspec(j):
        return pl.BlockSpec((nq, kq, d), lambda i: (0, base + j, 0), pipeline_mode=pl.Buffered(1))

    f32o = jax.ShapeDtypeStruct((s, d), F32)
    b16o = jax.ShapeDtypeStruct((s, d), BF16)
    return _pcall(
        body, name="fwd_mix", grid=(s // tm,),
        in_specs=[_rows(tm, 7 * d), _full(vecs.shape), _full(caw.shape), _full(cbw.shape), wspec(0), wspec(1), wspec(2)],
        out_specs=[_rows(tm, d)] * 7,
        out_shape=[f32o, f32o, f32o, f32o, b16o, b16o, b16o],
        scratch_shapes=[pltpu.VMEM((HALO_A + tm, d), F32), pltpu.VMEM((SHIFTED_COPIES, HALO_A + tm, d), F32),
                        pltpu.VMEM((HALO_B + tm, d), F32), pltpu.VMEM((tm, d), F32)],
        compiler_params=_seq(1),
    )(z, vecs, caw, cbw, wb, wb, wb)


def _mlp_fwd_bwd(x, out1, tgt, vecs, b_up, wb, tm, groups):
    s, d = x.shape
    nq = wb.shape[0]
    dff = nq * d
    hm = tm // groups

    def body(x_ref, o1_ref, t_ref, v_ref, bup_ref, wup_ref, wdn_ref,
             h2_ref, f_ref, df0_ref, do2_ref, do1_ref, dxp_ref, acc_ref, dbup_ref, f0_ref):
        @pl.when(pl.program_id(0) == 0)
        def _():
            acc_ref[...] = jnp.zeros(acc_ref.shape, F32)
            dbup_ref[...] = jnp.zeros(dbup_ref.shape, F32)

        def vec(r):
            return v_ref[r:r + 1, :]

        def accum(r, val):
            acc_ref[r:r + 1, :] += _rsum(val)

        for g in range(groups):
            rs = slice(g * hm, (g + 1) * hm)
            out1v = o1_ref[rs, :]
            r1 = ALPHA * x_ref[rs, :] + (1.0 + vec(V_GATE1)) * out1v
            xh1, rstd1 = _ln(r1)
            x1 = xh1 * vec(V_LN1G) + vec(V_LN1B)
            xn1, rstdn = _ln(x1)
            h2 = (xn1 * (1.0 + vec(V_SCALE2)) + vec(V_SHIFT2)).astype(BF16)
            h2_ref[rs, :] = h2
            out2 = jnp.broadcast_to(vec(V_BDN), (hm, d))
            for q in range(nq):
                f0 = _dot(h2, wup_ref[q]) + bup_ref[:, q * d:(q + 1) * d]
                rl = jnp.maximum(f0, 0.0)
                f0_ref[rs, q * d:(q + 1) * d] = rl
                fb = (rl * rl).astype(BF16)
                f_ref[rs, q * d:(q + 1) * d] = fb
                out2 = out2 + _dot(fb, wdn_ref[q])
            r2 = ALPHA * x1 + (1.0 + vec(V_GATE2)) * out2
            xh2, rstd2 = _ln(r2)
            yv = xh2 * vec(V_LN2G) + vec(V_LN2B)
            err = yv - t_ref[rs, :]
            accum(M_LOSS, err * err)
            dy = err * (1.0 / d)
            accum(M_LN2G, dy * xh2)
            accum(M_LN2B, dy)
            dr2 = _ln_bwd(dy * vec(V_LN2G), xh2, rstd2)
            accum(M_GATE2, dr2 * out2)
            dout2 = (1.0 + vec(V_GATE2)) * dr2
            accum(M_BDN, dout2)
            do2b = dout2.astype(BF16)
            do2_ref[rs, :] = do2b
            dh2 = jnp.zeros((hm, d), F32)
            for q in range(nq):
                df0 = _dot_nt(do2b, wdn_ref[q]) * (2.0 * f0_ref[rs, q * d:(q + 1) * d])
                dbup_ref[q:q + 1, :] += _rsum(df0)
                df0b = df0.astype(BF16)
                df0_ref[rs, q * d:(q + 1) * d] = df0b
                dh2 = dh2 + _dot_nt(df0b, wup_ref[q])
            accum(M_SHIFT2, dh2)
            accum(M_SCALE2, dh2 * xn1)
            dx1 = ALPHA * dr2 + _ln_bwd(dh2 * (1.0 + vec(V_SCALE2)), xn1, rstdn)
            accum(M_LN1G, dx1 * xh1)
            accum(M_LN1B, dx1)
            dr1 = _ln_bwd(dx1 * vec(V_LN1G), xh1, rstd1)
            accum(M_GATE1, dr1 * out1v)
            dout1 = (1.0 + vec(V_GATE1)) * dr1
            accum(M_BO, dout1)
            do1_ref[rs, :] = dout1.astype(BF16)
            dxp_ref[rs, :] = ALPHA * dr1

    def wspec(j):
        return pl.BlockSpec((nq, d, d), lambda i: (0, j, 0), pipeline_mode=pl.Buffered(1))

    b16 = lambda w: jax.ShapeDtypeStruct((s, w), BF16)
    return _pcall(
        body, name="mlp_fwd_bwd", grid=(s // tm,),
        in_specs=[_rows(tm, d), _rows(tm, d), _rows(tm, d), _full(vecs.shape), _full(b_up.shape), wspec(0), wspec(1)],
        out_specs=[_rows(tm, d), _rows(tm, dff), _rows(tm, dff), _rows(tm, d), _rows(tm, d), _rows(tm, d),
                   _full((16, d)), _full((SUBLANES, d))],
        out_shape=[b16(d), b16(dff), b16(dff), b16(d), b16(d), jax.ShapeDtypeStruct((s, d), F32),
                   jax.ShapeDtypeStruct((16, d), F32), jax.ShapeDtypeStruct((SUBLANES, d), F32)],
        scratch_shapes=[pltpu.VMEM((tm, dff), F32)],
        compiler_params=_seq(1),
    )(x, out1, tgt, vecs, b_up, wb, wb)


def _mix_bwd(dout1, z, u1, ya, yb, vecs, caw, cbw, wb, tm, rc):
    s = z.shape[0]
    d = vecs.shape[1]
    nq = wb.shape[0]
    kq = d // nq
    base = 2 * d // kq
    nt = s // tm
    hb = tm // HALO_A

    def body(do1_ref, z_ref, zh_ref, u1_ref, ya_ref, yb_ref, v_ref, caw_ref, cbw_ref, wao_ref, wbo_ref, wo_ref,
             dz_ref, dya_ref, dyb_ref, acc_ref, dcaw_ref, dcbw_ref, dbin_ref,
             ext_ref, du1p_ref, sh_ref, pext_ref, dqp_ref, tmp_ref):
        i = pl.program_id(0)

        @pl.when(i == 0)
        def _():
            acc_ref[...] = jnp.zeros(acc_ref.shape, F32)
            dcaw_ref[...] = jnp.zeros(dcaw_ref.shape, F32)
            dcbw_ref[...] = jnp.zeros(dcbw_ref.shape, F32)
            dbin_ref[...] = jnp.zeros(dbin_ref.shape, F32)
            du1p_ref[0:SUBLANES, :] = jnp.zeros((SUBLANES, d), F32)
            du1p_ref[SUBLANES + tm:SUBLANES + tm + HALO_A, :] = jnp.zeros((HALO_A, d), F32)
            dqp_ref[0:SUBLANES, :] = jnp.zeros((SUBLANES, d), F32)
            dqp_ref[SUBLANES + tm:SUBLANES + tm + HALO_B, :] = jnp.zeros((HALO_B, d), F32)

        def vec(r):
            return v_ref[r:r + 1, :]

        def accum(r, val):
            acc_ref[r:r + 1, :] += _rsum(val)

        def put_dz(j, val):
            dbin_ref[j:j + 1, :] += _rsum(val)
            dz_ref[:, j * d:(j + 1) * d] = val.astype(BF16)

        has_history = i < nt - 1

        do1 = do1_ref[...]
        dmg = jnp.concatenate([_dot_nt(do1, wo_ref[q]) for q in range(nq)], axis=1)
        sga = _sig(z_ref[:, 5 * d:6 * d])
        sgb = _sig(z_ref[:, 6 * d:7 * d])
        dya = dmg * sga
        dyb = dmg * sgb
        accum(X_BAO, dya)
        put_dz(5, dya * ya_ref[...] * (1.0 - sga))
        put_dz(6, dyb * yb_ref[...] * (1.0 - sgb))
        dyab = dya.astype(BF16)
        dybb = dyb.astype(BF16)
        dya_ref[...] = dyab
        dyb_ref[...] = dybb

        du3 = jnp.concatenate([_dot_nt(dyab, wao_ref[q]) for q in range(nq)], axis=1)
        xa, rstda = _ln(u1_ref[...])
        u2 = xa * vec(V_LNAG) + vec(V_LNAB)
        s2 = _sig(u2)
        du2 = du3 * (s2 * (1.0 + u2 * (1.0 - s2)))
        accum(X_LNAG, du2 * xa)
        accum(X_LNAB, du2)
        du1 = _ln_bwd(du2 * vec(V_LNAG), xa, rstda)
        accum(X_CAB, du1)
        du1p_ref[SUBLANES:SUBLANES + tm, :] = du1
        sg = _sig(z_ref[:, d:2 * d])
        aval = z_ref[:, 0:d]
        ext_ref[HALO_A:HALO_A + tm, :] = aval * sg
        ext_ref[0:HALO_A, :] = jnp.where(has_history, zh_ref[:, 0:d] * _sig(zh_ref[:, d:2 * d]), 0.0)
        _conv_adjoint(du1p_ref, sh_ref, ext_ref, caw_ref, tmp_ref, dcaw_ref, CONV_A, HALO_A, tm, d, rc)
        du1p_ref[SUBLANES + tm:SUBLANES + tm + HALO_A, :] = du1p_ref[SUBLANES:SUBLANES + HALO_A, :]
        du0 = tmp_ref[...]
        put_dz(0, du0 * sg)
        put_dz(1, du0 * aval * sg * (1.0 - sg))

        dv = jnp.concatenate([_dot_nt(dybb, wbo_ref[q]) for q in range(nq)], axis=1)
        bgc = z_ref[:, 3 * d:4 * d]
        bx = z_ref[:, 4 * d:5 * d]
        pext_ref[HALO_B:HALO_B + tm, :] = bgc * bx
        pext_ref[0:HALO_B, :] = jnp.where(
            has_history, zh_ref[HALO_A - HALO_B:HALO_A, 3 * d:4 * d] * zh_ref[HALO_A - HALO_B:HALO_A, 4 * d:5 * d], 0.0)
        _conv_causal(pext_ref, sh_ref, cbw_ref, CONV_B, HALO_B, None, tmp_ref, tm, d, rc)
        put_dz(2, dv * tmp_ref[...])
        dqp_ref[SUBLANES:SUBLANES + tm, :] = dv * z_ref[:, 2 * d:3 * d]
        _conv_adjoint(dqp_ref, sh_ref, pext_ref, cbw_ref, tmp_ref, dcbw_ref, CONV_B, HALO_B, tm, d, rc)
        dqp_ref[SUBLANES + tm:SUBLANES + tm + HALO_B, :] = dqp_ref[SUBLANES:SUBLANES + HALO_B, :]
        dp = tmp_ref[...]
        put_dz(3, dp * bx)
        put_dz(4, dp * bgc)

    def rev(width):
        return pl.BlockSpec((tm, width), lambda i: (nt - 1 - i, 0))

    def wspec(j):
        return pl.BlockSpec((nq, kq, d), lambda i: (0, base + j, 0), pipeline_mode=pl.Buffered(1))

    halo = pl.BlockSpec((HALO_A, 7 * d), lambda i: (jnp.maximum((nt - 1 - i) * hb - 1, 0), 0))
    b16 = jax.ShapeDtypeStruct((s, d), BF16)
    acc8 = jax.ShapeDtypeStruct((SUBLANES, d), F32)
    return _pcall(
        body, name="mix_bwd", grid=(nt,),
        in_specs=[rev(d), rev(7 * d), halo, rev(d), rev(d), rev(d), _full(vecs.shape), _full(caw.shape), _full(cbw.shape),
                  wspec(0), wspec(1), wspec(2)],
        out_specs=[rev(7 * d), rev(d), rev(d), _full((SUBLANES, d)), _full((HALO_A, d)), _full((HALO_B, d)),
                   _full((SUBLANES, d))],
        out_shape=[jax.ShapeDtypeStruct((s, 7 * d), BF16), b16, b16, acc8,
                   jax.ShapeDtypeStruct((HALO_A, d), F32), jax.ShapeDtypeStruct((HALO_B, d), F32), acc8],
        scratch_shapes=[pltpu.VMEM((HALO_A + tm, d), F32), pltpu.VMEM((SUBLANES + tm + HALO_A, d), F32),
                        pltpu.VMEM((SHIFTED_COPIES, tm + HALO_A, d), F32), pltpu.VMEM((HALO_B + tm, d), F32),
                        pltpu.VMEM((SUBLANES + tm + HALO_B, d), F32), pltpu.VMEM((tm, d), F32)],
        compiler_params=_seq(1),
    )(dout1, z, z, u1, ya, yb, vecs, caw, cbw, wb, wb, wb)


def _in_bwd(dz, x, dxp, vecs, wa, tm, plans, plan_srcs):
    s, d = x.shape
    nq, _, nw = wa.shape
    nt = s // tm
    nc = len(plan_srcs)

    def body(dz_ref, x_ref, dxp_ref, v_ref, w_ref, *rest):
        src_refs, (gx_ref, acc_ref), dst_refs = rest[:nc], rest[nc:nc + 2], rest[nc + 2:2 * nc + 2]
        send_sems, recv_sems = rest[2 * nc + 2:]
        i = pl.program_id(0)
        comm = (src_refs, dst_refs, send_sems, recv_sems)
        _carried_start(plans, i == 0, comm)

        @pl.when(i == 0)
        def _():
            acc_ref[...] = jnp.zeros(acc_ref.shape, F32)

        dh1 = jnp.zeros((tm, d), F32)
        for q in range(nq):
            dh1 = dh1 + _dot_nt(dz_ref[:, q * nw:(q + 1) * nw], w_ref[q])
        xh, rstd = _ln(x_ref[...])
        acc_ref[I_SHIFT1:I_SHIFT1 + 1, :] += _rsum(dh1)
        acc_ref[I_SCALE1:I_SCALE1 + 1, :] += _rsum(dh1 * xh)
        gx_ref[...] = dxp_ref[...] + _ln_bwd(dh1 * (1.0 + v_ref[V_SCALE1:V_SCALE1 + 1, :]), xh, rstd)
        _carried_wait(plans, i == nt - 1, comm)

    any_spec = pl.BlockSpec(memory_space=pl.ANY)
    return _pcall(
        body, name="in_bwd", grid=(nt,),
        in_specs=[_rows(tm, nq * nw), _rows(tm, d), _rows(tm, d), _full(vecs.shape), _full(wa.shape, single=True)]
        + [any_spec] * nc,
        out_specs=[_rows(tm, d), _full((SUBLANES, d))] + [any_spec] * nc,
        out_shape=[jax.ShapeDtypeStruct((s, d), F32), jax.ShapeDtypeStruct((SUBLANES, d), F32)]
        + [plan.out_shape(p) for plan, p in zip(plans, plan_srcs)],
        scratch_shapes=_plan_sems(plans),
        compiler_params=_seq(1),
    )(dz, x, dxp, vecs, wa, *plan_srcs)


def _dw(a, b, split_a, ts, name, into=None, rows_total=None, row_block=0):
    s = a.shape[0]
    ka = a.shape[1] // N_CHIPS if split_a else a.shape[1]
    nb = b.shape[1] if split_a else b.shape[1] // N_CHIPS
    rows_total = ka if rows_total is None else rows_total

    def body(a_ref, b_ref, *rest):
        o_ref = rest[-1]

        @pl.when(pl.program_id(1) == 0)
        def _():
            o_ref[...] = jnp.zeros(o_ref.shape, F32)

        o_ref[...] += _dot_tn(a_ref[...], b_ref[...])

    a_spec = pl.BlockSpec((ts, ka), (lambda q, i: (i, q)) if split_a else (lambda q, i: (i, 0)))
    b_spec = pl.BlockSpec((ts, nb), (lambda q, i: (i, 0)) if split_a else (lambda q, i: (i, q)))
    extra = {} if into is None else dict(input_output_aliases={2: 0})
    return _pcall(
        body, name=name, grid=(N_CHIPS, s // ts),
        in_specs=[a_spec, b_spec] + ([] if into is None else [pl.BlockSpec(memory_space=pl.ANY)]),
        out_specs=pl.BlockSpec((None, ka, nb), lambda q, i: (q, row_block, 0)),
        out_shape=jax.ShapeDtypeStruct((N_CHIPS, rows_total, nb), F32),
        compiler_params=_seq(2), **extra,
    )(*((a, b) if into is None else (a, b, into)))


def _dw_carrying(a, b, ts, name, plan, plan_src):
    s, k = a.shape
    nb = b.shape[1] // N_CHIPS
    ns = s // ts

    def body(a_ref, b_ref, src_ref, o_ref, dst_ref, send_sems, recv_sems):
        q, i = pl.program_id(0), pl.program_id(1)
        comm = ([src_ref], [dst_ref], send_sems, recv_sems)
        _carried_start([plan], jnp.logical_and(q == 0, i == 0), comm)

        @pl.when(i == 0)
        def _():
            o_ref[...] = jnp.zeros(o_ref.shape, F32)

        o_ref[...] += _dot_tn(a_ref[...], b_ref[...])
        _carried_wait([plan], jnp.logical_and(q == N_CHIPS - 1, i == ns - 1), comm)

    any_spec = pl.BlockSpec(memory_space=pl.ANY)
    return _pcall(
        body, name=name, grid=(N_CHIPS, ns),
        in_specs=[pl.BlockSpec((ts, k), lambda q, i: (i, 0)), pl.BlockSpec((ts, nb), lambda q, i: (i, q)), any_spec],
        out_specs=[pl.BlockSpec((None, k, nb), lambda q, i: (q, 0, 0)), any_spec],
        out_shape=[jax.ShapeDtypeStruct((N_CHIPS, k, nb), F32), plan.out_shape(plan_src)],
        scratch_shapes=_plan_sems([plan]),
        compiler_params=_seq(2),
    )(a, b, plan_src)


def _dw_rows(a, b, ts, name, into, row_block):
    s, k = a.shape
    n = b.shape[1]
    kq = k // N_CHIPS

    def body(a_ref, b_ref, buf_ref, o_ref):
        @pl.when(pl.program_id(0) == 0)
        def _():
            o_ref[...] = jnp.zeros(o_ref.shape, F32)

        res = _dot_tn(a_ref[...], b_ref[...])
        for q in range(N_CHIPS):
            o_ref[q] += res[q * kq:(q + 1) * kq, :]

    return _pcall(
        body, name=name, grid=(s // ts,),
        in_specs=[_rows(ts, k), _rows(ts, n), pl.BlockSpec(memory_space=pl.ANY)],
        out_specs=pl.BlockSpec((N_CHIPS, kq, n), lambda i: (0, row_block, 0)),
        out_shape=jax.ShapeDtypeStruct(into.shape, F32), input_output_aliases={2: 0},
        compiler_params=_seq(1),
    )(a, b, into)


def _adam_math(w, g, m, v):
    m2 = ADAM_B1 * m + (1.0 - ADAM_B1) * g
    v2 = ADAM_B2 * v + (1.0 - ADAM_B2) * (g * g)
    m_hat = m2 / (1.0 - ADAM_B1 ** ADAM_STEP)
    v_hat = v2 / (1.0 - ADAM_B2 ** ADAM_STEP)
    delta = -ADAM_LR * (m_hat / (jnp.sqrt(v_hat) + ADAM_EPS) + ADAM_WD * w)
    return delta, m2, v2


def _adam(w, g, m, v, g_row0, tr, name):
    r, c = w.shape
    blk0 = g_row0 // tr

    def body(w_ref, g_ref, m_ref, v_ref, go_ref, d_ref, mo_ref, vo_ref):
        gv = g_ref[...]
        go_ref[...] = gv
        d_ref[...], mo_ref[...], vo_ref[...] = _adam_math(w_ref[...], gv, m_ref[...], v_ref[...])

    spec = _rows(tr, c)
    g_spec = pl.BlockSpec((tr, c), lambda i: (blk0 + i, 0))
    o = jax.ShapeDtypeStruct((r, c), F32)
    return _pcall(body, name=name, grid=(r // tr,), in_specs=[spec, g_spec, spec, spec], out_specs=[spec] * 4,
                  out_shape=[o, o, o, o], compiler_params=_seq(1))(w, g, m, v)


def _small_update(gathered_head, gathered, q_idx, small_w, small_m, small_v, conv_w, conv_m, conv_v):
    d = gathered.shape[2]
    ns = len(_SMALL)
    cw = conv_w[0].shape[1]
    conv_rows = ((T_CAW, CONV_A), (T_CBW, CONV_B))

    def body(q_ref, h_ref, g_ref, *refs):
        ins, outs = refs[:3 * (ns + 2)], refs[3 * (ns + 2):]
        tot_ref, loss_ref = outs[0], outs[1]
        outs = outs[2:]
        head, tot = h_ref[0], g_ref[0]
        for dev in range(1, N_DEV):
            head = head + h_ref[dev]
            tot = tot + g_ref[dev]
        tot_ref[0:T_M, :] = head
        tot_ref[T_M:T_ROWS, :] = tot
        loss_ref[...] = (0.5 / d) * jnp.sum(tot_ref[T_LOSS:T_LOSS + 1, :], axis=1, keepdims=True)
        for p, (_, rows) in enumerate(_SMALL):
            w_ref, m_ref, v_ref = ins[p], ins[ns + 2 + p], ins[2 * (ns + 2) + p]
            go, do, mo, vo = outs[4 * p:4 * p + 4]
            for j, row in enumerate(rows):
                sl = slice(j * d, (j + 1) * d)
                gv = tot_ref[row:row + 1, :]
                go[:, sl] = gv
                do[:, sl], mo[:, sl], vo[:, sl] = _adam_math(w_ref[:, sl], gv, m_ref[:, sl], v_ref[:, sl])
        for p, (row, taps) in enumerate(conv_rows):
            w_ref, m_ref, v_ref = ins[ns + p], ins[ns + 2 + ns + p], ins[2 * (ns + 2) + ns + p]
            go, do, mo, vo = outs[4 * (ns + p):4 * (ns + p) + 4]
            gv = tot_ref[row:row + taps, 0:cw]
            for qq in range(1, N_CHIPS):
                gv = jnp.where(q_ref[0] == qq, tot_ref[row:row + taps, qq * cw:(qq + 1) * cw], gv)
            go[...] = gv
            do[...], mo[...], vo[...] = _adam_math(w_ref[...], gv, m_ref[...], v_ref[...])

    params = list(small_w) + list(conv_w) + list(small_m) + list(conv_m) + list(small_v) + list(conv_v)
    out_shape = [jax.ShapeDtypeStruct((T_ROWS, d), F32), jax.ShapeDtypeStruct((1, 1), F32)]
    for w in list(small_w) + list(conv_w):
        out_shape += [jax.ShapeDtypeStruct(w.shape, F32)] * 4
    vm = pl.BlockSpec(memory_space=pltpu.VMEM)
    return _pcall(
        body, name="small_update", out_shape=out_shape,
        in_specs=[pl.BlockSpec(memory_space=pltpu.SMEM), vm, vm] + [vm] * len(params),
        out_specs=[vm] * len(out_shape), compiler_params=_cparams(),
    )(q_idx, gathered_head, gathered, *params)


def kernel(x, c, w_ada, b_ada, w_in, b_in, conv_a_w, conv_a_b, ln_a_g, ln_a_b, w_a_out, b_a_out, conv_b_w, w_b_out, w_o, b_o, ln1_g, ln1_b, w_up, b_up, w_down, b_down, ln2_g, ln2_b, loss_target, m_w_ada, m_b_ada, m_w_in, m_b_in, m_conv_a_w, m_conv_a_b, m_ln_a_g, m_ln_a_b, m_w_a_out, m_b_a_out, m_conv_b_w, m_w_b_out, m_w_o, m_b_o, m_ln1_g, m_ln1_b, m_w_up, m_b_up, m_w_down, m_b_down, m_ln2_g, m_ln2_b, v_w_ada, v_b_ada, v_w_in, v_b_in, v_conv_a_w, v_conv_a_b, v_ln_a_g, v_ln_a_b, v_w_a_out, v_b_a_out, v_conv_b_w, v_w_b_out, v_w_o, v_b_o, v_ln1_g, v_ln1_b, v_w_up, v_b_up, v_w_down, v_b_down, v_ln2_g, v_ln2_b):
    given = dict(locals())
    s, d = x.shape[1], x.shape[2]
    xi, yi, ci = _my_pos()
    q = 2 * xi + yi
    me = 4 * xi + 2 * yi + ci
    i32 = jnp.int32
    q_arr = jnp.reshape(q, (1,)).astype(i32)
    others = [2 * ox + oy for ox, oy in _other_chips(xi, yi)]
    halves_idx = jnp.stack([ci] + others).astype(i32)
    chips_idx = jnp.stack([q, ci]).astype(i32)
    kq = d // N_CHIPS
    tm = min(256, s)
    rc = min(32, tm)

    def sq(a):
        return a.reshape(a.shape[1:])

    x2, tgt = sq(x), sq(loss_target)

    wa = _place_shard([sq(w_in)], q_arr, "place_w_in")
    wb = _place_shard([sq(w_up), sq(w_down), sq(w_a_out), sq(w_b_out), sq(w_o)], q_arr, "place_w_rest")

    n_ada = w_ada.shape[2]
    pre = jnp.concatenate([
        jnp.broadcast_to(c, (SUBLANES, d)),
        jnp.pad(sq(conv_a_w), ((0, HALO_A - CONV_A), (0, d - kq))),
        jnp.pad(sq(conv_b_w), ((0, HALO_B - CONV_B), (0, d - kq)))], axis=0)
    b_ada_sh = lax.dynamic_slice(b_ada, (0, q * n_ada), (1, n_ada))
    pre_all, mod_all, wa = _prologue(pre, sq(w_ada), b_ada_sh, wa)
    c_all = pre_all[:, 0, :]
    caw = jnp.concatenate([pre_all[2 * p, SUBLANES:SUBLANES + HALO_A, :kq] for p in range(N_CHIPS)], axis=1)
    cbw = jnp.concatenate([pre_all[2 * p, SUBLANES + HALO_A:, :kq] for p in range(N_CHIPS)], axis=1)
    mod_rows = lax.dynamic_slice(mod_all, (0, me, 0), (N_DEV, 1, n_ada))[0::2, 0, :]
    mod = mod_rows.reshape(6, d)
    vecs = jnp.concatenate([mod, conv_a_b, ln_a_g, ln_a_b, b_a_out, b_o, ln1_g, ln1_b, b_down, ln2_g, ln2_b], axis=0)

    h1, z, wb = _fwd_in(x2, vecs, wa, b_in, wb, tm)
    u1, ya, yb, out1, u3, vv, mg = _fwd_mix(z, vecs, caw, cbw, wb, tm, rc)

    ts = min(2048, s)
    rest_rows = wb.shape[1]
    small0 = 2 * d // kq
    h2, fb, df0, do2, do1, dxp, macc, dbup = _mlp_fwd_bwd(x2, out1, tgt, vecs, b_up, wb, tm, 1)
    gb = _dw(h2, df0, False, ts, "dw_up", rows_total=rest_rows)
    gb = _dw(fb, do2, True, ts, "dw_down", into=gb, rows_total=rest_rows, row_block=1)
    dz, dya, dyb, xacc, dcaw, dcbw, dbin = _mix_bwd(do1, z, u1, ya, yb, vecs, caw, cbw, wb, tm, rc)
    gb = _dw_rows(u3, dya, ts, "dw_a_out", gb, small0)
    gb = _dw_rows(vv, dyb, ts, "dw_b_out", gb, small0 + 1)
    gb = _dw_rows(mg, do1, ts, "dw_o", gb, small0 + 2)

    trb, tra = rest_rows // 8, d // 8
    ga, rb = _dw_carrying(h1, dz, ts, "dw_in", _SiblingHalf, gb)
    pb = _add_halves(gb, rb, halves_idx, trb, "rs_add_halves_rest")
    ra = _exchange(_SiblingHalf, ga, "rs_to_sibling")
    pa = _add_halves(ga, ra, halves_idx, tra, "rs_add_halves_in")
    table = jnp.concatenate([macc, dbin, dcaw, xacc, dcbw, dbup], axis=0)
    gx, iacc, r3a, r3b, gathered = _in_bwd(dz, x2, dxp, vecs, wa, tm, [_ChipBlocks, _ChipBlocks, _TableToAll],
                                           [pa, pb, table])
    fa = _add_chips(ga, ra, r3a, chips_idx, tra, "rs_add_chips_in")
    fb_ = _add_chips(gb, rb, r3b, chips_idx, trb, "rs_add_chips_rest")
    g_in, g_b = _rs_join_halves(fa, fb_)

    gathered_head = _all_gather_small(iacc, "gather_ln0_sums")
    names = [n for n, _ in _SMALL]
    res = _small_update(
        gathered_head, gathered, q_arr,
        [given[n] for n in names], [given["m_" + n] for n in names], [given["v_" + n] for n in names],
        [sq(conv_a_w), sq(conv_b_w)], [sq(m_conv_a_w), sq(m_conv_b_w)], [sq(v_conv_a_w), sq(v_conv_b_w)])
    loss = res[1].reshape(())
    upd = {}
    for p, n in enumerate(names + ["conv_a_w", "conv_b_w"]):
        upd[n] = res[2 + 4 * p:6 + 4 * p]

    dmod_all = jnp.stack([gathered_head[:, r, :] if r < T_M else gathered[:, r - T_M, :] for r in _SMALL[0][1]],
                         axis=1).reshape(N_DEV, 6 * d)
    dmod_sh = lax.dynamic_slice(dmod_all, (0, q * n_ada), (N_DEV, n_ada))
    g_ada = _ada_bwd(c_all.T, dmod_sh)
    upd["w_ada"] = _adam(sq(w_ada), g_ada, sq(m_w_ada), sq(v_w_ada), 0, min(256, d), "adam_w_ada")

    upd["w_in"] = _adam(sq(w_in), g_in, sq(m_w_in), sq(v_w_in), 0, min(256, d), "adam_w_in")
    r0 = 0
    for n in ("w_up", "w_down", "w_a_out", "w_b_out", "w_o"):
        w = sq(given[n])
        upd[n] = _adam(w, g_b, sq(given["m_" + n]), sq(given["v_" + n]), r0, min(256, w.shape[0]), "adam_" + n)
        r0 += w.shape[0]

    order = ["w_ada", "b_ada", "w_in", "b_in", "conv_a_w", "conv_a_b", "ln_a_g", "ln_a_b", "w_a_out", "b_a_out", "conv_b_w",
             "w_b_out", "w_o", "b_o", "ln1_g", "ln1_b", "w_up", "b_up", "w_down", "b_down", "ln2_g", "ln2_b"]
    outs = [loss, gx.reshape(x.shape)]
    for k in range(4):
        outs += [upd[n][k].reshape(given[n].shape) for n in order]
    return tuple(outs)
```

```python
import jax
import jax.numpy as jnp
from jax import lax
from jax.experimental import pallas as pl
from jax.experimental.pallas import tpu as pltpu

F32 = jnp.float32
BF16 = jnp.bfloat16
MESH = pl.DeviceIdType.MESH

LN_EPS = 1e-5
DEPTH = 1
ALPHA = (2.0 * DEPTH) ** 0.25
CONV_A = 31
CONV_B = 3
SUBLANES = 8
HALO_A = 32
HALO_B = 8
SHIFTED_COPIES = 4
N_CHIPS = 4
N_DEV = 8
ADAM_LR = 0.001
ADAM_B1 = 0.9
ADAM_B2 = 0.999
ADAM_EPS = 1e-08
ADAM_WD = 0.01
ADAM_STEP = 10
VMEM_LIMIT = 56 * 1024 * 1024

V_SHIFT1, V_SCALE1, V_GATE1, V_SHIFT2, V_SCALE2, V_GATE2 = 0, 1, 2, 3, 4, 5
V_CAB, V_LNAG, V_LNAB, V_BAO, V_BO, V_LN1G, V_LN1B, V_BDN, V_LN2G, V_LN2B = 6, 7, 8, 9, 10, 11, 12, 13, 14, 15

M_LN2G, M_LN2B, M_GATE2, M_BDN, M_SHIFT2, M_SCALE2, M_LN1G, M_LN1B, M_GATE1, M_BO, M_LOSS = range(11)
X_BAO, X_LNAG, X_LNAB, X_CAB = range(4)
I_SHIFT1, I_SCALE1 = 0, 1

T_I, T_M, T_BIN, T_CAW, T_X, T_CBW, T_BUP, T_ROWS = 0, 8, 24, 32, 64, 72, 80, 88
T_LOSS = T_M + M_LOSS
_SMALL = (
    ("b_ada", (T_I + I_SHIFT1, T_I + I_SCALE1, T_M + M_GATE1, T_M + M_SHIFT2, T_M + M_SCALE2, T_M + M_GATE2)),
    ("b_in", tuple(T_BIN + j for j in range(7))),
    ("conv_a_b", (T_X + X_CAB,)), ("ln_a_g", (T_X + X_LNAG,)), ("ln_a_b", (T_X + X_LNAB,)), ("b_a_out", (T_X + X_BAO,)),
    ("b_o", (T_M + M_BO,)), ("ln1_g", (T_M + M_LN1G,)), ("ln1_b", (T_M + M_LN1B,)),
    ("b_up", tuple(T_BUP + j for j in range(4))),
    ("b_down", (T_M + M_BDN,)), ("ln2_g", (T_M + M_LN2G,)), ("ln2_b", (T_M + M_LN2B,)),
)


def _pcall(body, **kw):
    return pl.pallas_call(body, **kw)


def _cparams(**kw):
    return pltpu.CompilerParams(vmem_limit_bytes=VMEM_LIMIT, **kw)


def _seq(n):
    return _cparams(dimension_semantics=("arbitrary",) * n)


def _full(shape, single=False):
    nd = len(shape)
    if single:
        return pl.BlockSpec(shape, lambda *_: (0,) * nd, pipeline_mode=pl.Buffered(1))
    return pl.BlockSpec(shape, lambda *_: (0,) * nd)


def _rows(tm, width):
    return pl.BlockSpec((tm, width), lambda i: (i, 0))


def _sig(x):
    return jax.nn.sigmoid(x)


def _ln(x):
    mu = jnp.mean(x, axis=-1, keepdims=True)
    xc = x - mu
    var = jnp.mean(xc * xc, axis=-1, keepdims=True)
    rstd = lax.rsqrt(var + LN_EPS)
    return xc * rstd, rstd


def _ln_bwd(dxh, xh, rstd):
    m1 = jnp.mean(dxh, axis=-1, keepdims=True)
    m2 = jnp.mean(dxh * xh, axis=-1, keepdims=True)
    return rstd * (dxh - m1 - xh * m2)


def _rsum(v):
    return jnp.sum(v, axis=0, keepdims=True)


def _dot(a, b):
    return jnp.dot(a, b, preferred_element_type=F32)


def _dot_nt(a, b):
    return lax.dot_general(a, b, (((1,), (1,)), ((), ())), preferred_element_type=F32)


def _dot_tn(a, b):
    return lax.dot_general(a, b, (((0,), (0,)), ((), ())), preferred_element_type=F32)


def _my_pos():
    return lax.axis_index("x"), lax.axis_index("y"), lax.axis_index("c")


def _other_chips(x, y):
    return [(1 - x, y), (x, 1 - y), (1 - x, 1 - y)]


def _small_gather(v_ref, out_ref, send_sems, recv_sems, local_sem):
    x, y, cc = _my_pos()
    me = 4 * x + 2 * y + cc
    mine = pltpu.make_async_copy(v_ref, out_ref.at[me], local_sem)
    mine.start()
    sends = []
    for rel in range(1, N_DEV):
        rx, ry, rc = (rel >> 2) & 1, (rel >> 1) & 1, rel & 1
        peer = (1 - x if rx else x, 1 - y if ry else y, 1 - cc if rc else cc)
        cp = pltpu.make_async_remote_copy(
            src_ref=v_ref, dst_ref=out_ref.at[me], send_sem=send_sems.at[rel - 1], recv_sem=recv_sems.at[rel - 1],
            device_id=peer, device_id_type=MESH)
        cp.start()
        sends.append(cp)
    for rel in range(1, N_DEV):
        rx, ry, rc = (rel >> 2) & 1, (rel >> 1) & 1, rel & 1
        peer = (1 - x if rx else x, 1 - y if ry else y, 1 - cc if rc else cc)
        slot = 4 * peer[0] + 2 * peer[1] + peer[2]
        pltpu.make_async_remote_copy(
            src_ref=v_ref, dst_ref=out_ref.at[slot], send_sem=send_sems.at[rel - 1], recv_sem=recv_sems.at[rel - 1],
            device_id=peer, device_id_type=MESH).wait_recv()
    for cp in sends:
        cp.wait_send()
    mine.wait()


_SMALL_GATHER_SEMS = [pltpu.SemaphoreType.DMA((N_DEV - 1,)), pltpu.SemaphoreType.DMA((N_DEV - 1,)),
                      pltpu.SemaphoreType.DMA]


def _all_gather_small(v, name):
    r, c = v.shape

    def body(*refs):
        _small_gather(*refs)

    return _pcall(
        body, name=name,
        out_shape=jax.ShapeDtypeStruct((N_DEV, r, c), v.dtype),
        in_specs=[pl.BlockSpec(memory_space=pltpu.VMEM)],
        out_specs=pl.BlockSpec(memory_space=pltpu.VMEM),
        scratch_shapes=list(_SMALL_GATHER_SEMS),
        compiler_params=_cparams(),
    )(v)


def _prologue(pre, w_sh, b_sh, wa):
    r, d = pre.shape
    n_ada = w_sh.shape[1]
    ng = _GatherShards.n_sems

    def body(pre_ref, w_ref, b_ref, wai_ref, pre_all_ref, c_all_ref, mod_all_ref, wao_ref, mod_ref,
             s1, r1, l1, s2, r2, l2, sg, rg):
        gather = (wai_ref, wao_ref, sg, rg)
        _GatherShards.start(*gather)
        _small_gather(pre_ref, pre_all_ref, s1, r1, l1)
        cv = jnp.concatenate([pre_all_ref[dev, 0:1, :] for dev in range(N_DEV)], axis=0)
        c_all_ref[...] = cv
        ca = cv * _sig(cv)
        mod_ref[...] = jnp.dot(ca, w_ref[...], preferred_element_type=F32, precision=lax.Precision.HIGHEST) + b_ref[...]
        _small_gather(mod_ref, mod_all_ref, s2, r2, l2)
        _GatherShards.relay(*gather)
        _GatherShards.finish(*gather)

    vm = pl.BlockSpec(memory_space=pltpu.VMEM)
    any_spec = pl.BlockSpec(memory_space=pl.ANY)
    return _pcall(
        body, name="prologue",
        out_shape=[jax.ShapeDtypeStruct((N_DEV, r, d), F32), jax.ShapeDtypeStruct((N_DEV, d), F32),
                   jax.ShapeDtypeStruct((N_DEV, N_DEV, n_ada), F32), jax.ShapeDtypeStruct(wa.shape, wa.dtype)],
        in_specs=[vm, vm, vm, any_spec], out_specs=[vm, vm, vm, any_spec], input_output_aliases={3: 3},
        scratch_shapes=[pltpu.VMEM((N_DEV, n_ada), F32)] + list(_SMALL_GATHER_SEMS) + list(_SMALL_GATHER_SEMS)
        + [pltpu.SemaphoreType.DMA((ng,)), pltpu.SemaphoreType.DMA((ng,))],
        compiler_params=_cparams(),
    )(pre, w_sh, b_sh, wa)


def _place_shard(parts, q_idx, name):
    rows = sum(p.shape[0] for p in parts)
    w = parts[0].shape[1]

    def body(q_ref, *refs):
        o_ref = refs[-1]
        r0 = 0
        for p_ref in refs[:-1]:
            n = p_ref.shape[0]
            o_ref[r0:r0 + n, :] = p_ref[...].astype(BF16)
            r0 += n

    grid_spec = pltpu.PrefetchScalarGridSpec(
        num_scalar_prefetch=1, grid=(1,),
        in_specs=[pl.BlockSpec(p.shape, lambda i, q: (0, 0)) for p in parts],
        out_specs=pl.BlockSpec((None, rows, w), lambda i, q: (q[0], 0, 0)))
    return _pcall(body, name=name, grid_spec=grid_spec, out_shape=jax.ShapeDtypeStruct((N_CHIPS, rows, w), BF16),
                  compiler_params=_seq(1))(q_idx, *parts)


class _GatherShards:
    n_sems = 6

    @staticmethod
    def _half(ref, slot, h):
        rows = ref.shape[1] // 2
        return ref.at[slot, pl.ds(h * rows, rows)]

    @classmethod
    def _copies(cls, in_ref, out_ref, send_sems, recv_sems):
        x, y, c = _my_pos()
        q = 2 * x + y
        sibling = (x, y, 1 - c)
        sends, landed, forwards, passed = [], [], [], []
        for j, chip in enumerate(_other_chips(x, y)):
            qj = 2 * chip[0] + chip[1]

            def copy(src, dst, k, to):
                return pltpu.make_async_remote_copy(src_ref=src, dst_ref=dst, send_sem=send_sems.at[k],
                                                    recv_sem=recv_sems.at[k], device_id=to, device_id_type=MESH)

            mine, theirs = cls._half(out_ref, qj, c), cls._half(out_ref, qj, 1 - c)
            sends.append(copy(cls._half(in_ref, q, c), cls._half(out_ref, q, c), j, (*chip, c)))
            landed.append(copy(mine, mine, j, (*chip, c)))
            forwards.append(copy(mine, mine, 3 + j, sibling))
            passed.append(copy(theirs, theirs, 3 + j, sibling))
        return sends, landed, forwards, passed

    @classmethod
    def start(cls, *refs):
        for cp in cls._copies(*refs)[0]:
            cp.start()

    @classmethod
    def relay(cls, *refs):
        _, landed, forwards, _ = cls._copies(*refs)
        for arrived, onward in zip(landed, forwards):
            arrived.wait_recv()
            onward.start()

    @classmethod
    def finish(cls, *refs):
        sends, _, forwards, passed = cls._copies(*refs)
        for cp in passed:
            cp.wait_recv()
        for cp in sends + forwards:
            cp.wait_send()


class _SiblingHalf:
    n_sems = 1

    @staticmethod
    def out_shape(g):
        return jax.ShapeDtypeStruct((g.shape[0], g.shape[1] // 2, g.shape[2]), g.dtype)

    @staticmethod
    def copies(g_ref, r_ref, send_sems, recv_sems, base):
        x, y, c = _my_pos()
        rows = r_ref.shape[1]
        return [pltpu.make_async_remote_copy(
            src_ref=g_ref.at[:, pl.ds((1 - c) * rows, rows)], dst_ref=r_ref,
            send_sem=send_sems.at[base], recv_sem=recv_sems.at[base], device_id=(x, y, 1 - c), device_id_type=MESH)]


class _ChipBlocks:
    n_sems = 3

    @staticmethod
    def out_shape(p):
        return jax.ShapeDtypeStruct(p.shape, p.dtype)

    @staticmethod
    def copies(p_ref, r_ref, send_sems, recv_sems, base):
        x, y, c = _my_pos()
        return [pltpu.make_async_remote_copy(
            src_ref=p_ref.at[j], dst_ref=r_ref.at[j], send_sem=send_sems.at[base + j], recv_sem=recv_sems.at[base + j],
            device_id=(*chip, c), device_id_type=MESH) for j, chip in enumerate(_other_chips(x, y))]


class _TableToAll:
    n_sems = N_DEV

    @staticmethod
    def out_shape(t):
        return jax.ShapeDtypeStruct((N_DEV,) + t.shape, t.dtype)

    @staticmethod
    def copies(t_ref, all_ref, send_sems, recv_sems, base):
        x, y, c = _my_pos()
        me = 4 * x + 2 * y + c
        cps = [pltpu.make_async_copy(t_ref, all_ref.at[me], send_sems.at[base + N_DEV - 1])]
        for rel in range(1, N_DEV):
            rx, ry, rc = (rel >> 2) & 1, (rel >> 1) & 1, rel & 1
            peer = (1 - x if rx else x, 1 - y if ry else y, 1 - c if rc else c)
            cps.append(_SlotCopy(t_ref, all_ref, me, 4 * peer[0] + 2 * peer[1] + peer[2], peer,
                                 send_sems.at[base + rel - 1], recv_sems.at[base + rel - 1]))
        return cps


class _SlotCopy:
    def __init__(self, src_ref, all_ref, my_slot, peer_slot, peer, send_sem, recv_sem):
        self._send = pltpu.make_async_remote_copy(src_ref=src_ref, dst_ref=all_ref.at[my_slot], send_sem=send_sem,
                                                  recv_sem=recv_sem, device_id=peer, device_id_type=MESH)
        self._recv = pltpu.make_async_remote_copy(src_ref=src_ref, dst_ref=all_ref.at[peer_slot], send_sem=send_sem,
                                                  recv_sem=recv_sem, device_id=peer, device_id_type=MESH)

    def start(self):
        self._send.start()

    def wait(self):
        self._send.wait_send()
        self._recv.wait_recv()


def _plan_copies(plans, src_refs, dst_refs, send_sems, recv_sems):
    cps, base = [], 0
    for plan, s_ref, d_ref in zip(plans, src_refs, dst_refs):
        cps += plan.copies(s_ref, d_ref, send_sems, recv_sems, base)
        base += plan.n_sems
    return cps


def _plan_sems(plans):
    n = sum(p.n_sems for p in plans)
    return [pltpu.SemaphoreType.DMA((n,)), pltpu.SemaphoreType.DMA((n,))]


def _exchange(plan, src, name):
    def body(s_ref, o_ref, send_sems, recv_sems):
        cps = _plan_copies([plan], [s_ref], [o_ref], send_sems, recv_sems)
        for cp in cps:
            cp.start()
        for cp in cps:
            cp.wait()

    any_spec = pl.BlockSpec(memory_space=pl.ANY)
    return _pcall(
        body, name=name, out_shape=plan.out_shape(src), in_specs=[any_spec], out_specs=any_spec,
        scratch_shapes=_plan_sems([plan]), compiler_params=_cparams(),
    )(src)


def _carried_start(plans, first, comm):
    @pl.when(first)
    def _():
        for cp in _plan_copies(plans, *comm):
            cp.start()


def _carried_wait(plans, last, comm):
    @pl.when(last)
    def _():
        for cp in _plan_copies(plans, *comm):
            cp.wait()


def _rs_join_halves(fa, fb):
    bufs = (fa, fb)
    nb = len(bufs)

    def body(a_ref, b_ref, ao_ref, bo_ref, send_sems, recv_sems):
        x, y, c = _my_pos()
        srcs, outs = (a_ref, b_ref), (ao_ref, bo_ref)
        cps = []
        for b in range(nb):
            rows = srcs[b].shape[0] // 2
            cp = pltpu.make_async_remote_copy(
                src_ref=srcs[b].at[pl.ds(c * rows, rows)], dst_ref=outs[b].at[pl.ds(c * rows, rows)],
                send_sem=send_sems.at[b], recv_sem=recv_sems.at[b], device_id=(x, y, 1 - c), device_id_type=MESH)
            cp.start()
            cps.append(cp)
        for b in range(nb):
            rows = srcs[b].shape[0] // 2
            theirs = outs[b].at[pl.ds((1 - c) * rows, rows)]
            pltpu.make_async_remote_copy(
                src_ref=theirs, dst_ref=theirs, send_sem=send_sems.at[b], recv_sem=recv_sems.at[b],
                device_id=(x, y, 1 - c), device_id_type=MESH).wait_recv()
        for cp in cps:
            cp.wait_send()

    any_spec = pl.BlockSpec(memory_space=pl.ANY)
    return _pcall(
        body, name="rs_join_halves",
        out_shape=[jax.ShapeDtypeStruct(b.shape, b.dtype) for b in bufs],
        in_specs=[any_spec] * nb, out_specs=[any_spec] * nb, input_output_aliases={0: 0, 1: 1},
        scratch_shapes=[pltpu.SemaphoreType.DMA((nb,)), pltpu.SemaphoreType.DMA((nb,))],
        compiler_params=_cparams(),
    )(*bufs)


def _add_halves(g, r, idx, tr, name):
    _, rows, w = r.shape
    nt = rows // tr

    def body(i_ref, g_ref, r_ref, o_ref):
        o_ref[...] = (g_ref[...] + r_ref[...]).astype(BF16)

    grid_spec = pltpu.PrefetchScalarGridSpec(
        num_scalar_prefetch=1, grid=(3, nt),
        in_specs=[pl.BlockSpec((None, tr, w), lambda j, i, ix: (ix[1 + j], ix[0] * nt + i, 0)),
                  pl.BlockSpec((None, tr, w), lambda j, i, ix: (ix[1 + j], i, 0))],
        out_specs=pl.BlockSpec((None, tr, w), lambda j, i, ix: (j, i, 0)))
    return _pcall(body, name=name, grid_spec=grid_spec,
                  out_shape=jax.ShapeDtypeStruct((3, rows, w), BF16), compiler_params=_seq(2))(idx, g, r)


def _add_chips(g, r, r3, idx, tr, name):
    _, rows, w = r.shape
    nt = rows // tr

    def body(i_ref, g_ref, r_ref, a_ref, b_ref, c_ref, o_ref):
        own = g_ref[...] + r_ref[...]
        o_ref[...] = ((own + a_ref[...].astype(F32)) + b_ref[...].astype(F32)) + c_ref[...].astype(F32)

    def other(j):
        return pl.BlockSpec((None, tr, w), lambda i, ix: (j, i, 0))

    grid_spec = pltpu.PrefetchScalarGridSpec(
        num_scalar_prefetch=1, grid=(nt,),
        in_specs=[pl.BlockSpec((None, tr, w), lambda i, ix: (ix[0], ix[1] * nt + i, 0)),
                  pl.BlockSpec((None, tr, w), lambda i, ix: (ix[0], i, 0)), other(0), other(1), other(2)],
        out_specs=pl.BlockSpec((tr, w), lambda i, ix: (ix[1] * nt + i, 0)))
    return _pcall(body, name=name, grid_spec=grid_spec,
                  out_shape=jax.ShapeDtypeStruct((2 * rows, w), F32), compiler_params=_seq(1))(idx, g, r, r3, r3, r3)


def _ada_bwd(c_all_t, dmod_sh):
    def body(c_ref, d_ref, o_ref):
        cv = c_ref[...]
        ca = cv * _sig(cv)
        o_ref[...] = jnp.dot(ca, d_ref[...], preferred_element_type=F32, precision=lax.Precision.HIGHEST)

    return _pcall(body, name="ada_bwd", out_shape=jax.ShapeDtypeStruct((c_all_t.shape[0], dmod_sh.shape[1]), F32),
                  compiler_params=_cparams())(c_all_t, dmod_sh)


def _residue_passes(residues, copies):
    passes, current, used = [], [], 0
    for s in residues:
        if s != 0 and used == copies:
            passes.append(current)
            current, used = [], 0
        current.append(s)
        used += s != 0
    return passes + [current] if current else passes


def _conv_causal(ext_ref, sh_ref, w_ref, ntaps, halo, bias, out_ref, tm, d, rc):
    off = halo - (ntaps - 1)
    n = tm + halo - SUBLANES
    taps = {s: [(k, (off + k) // SUBLANES * SUBLANES) for k in range(ntaps) if (off + k) % SUBLANES == s]
            for s in range(SUBLANES)}
    started = False
    for group in _residue_passes([s for s in range(SUBLANES) if taps[s]], sh_ref.shape[0]):
        srcs = {}
        for s in group:
            if s == 0:
                srcs[s] = ext_ref
            else:
                j = len(srcs) - (0 in srcs)
                sh_ref[j, 0:n, :] = ext_ref[s:s + n, :]
                srcs[s] = sh_ref.at[j]
        for r0 in range(0, tm, rc):
            if started:
                acc = out_ref[r0:r0 + rc, :]
            else:
                acc = jnp.zeros((rc, d), F32) if bias is None else jnp.broadcast_to(bias, (rc, d))
            for s, src in srcs.items():
                for k, a in taps[s]:
                    acc = acc + w_ref[k:k + 1, :] * src[r0 + a:r0 + a + rc, :]
            out_ref[r0:r0 + rc, :] = acc
        started = True


def _conv_adjoint(dp_ref, sh_ref, ext_ref, w_ref, dx_ref, dw_ref, ntaps, halo, tm, d, rc):
    off = halo - (ntaps - 1)
    lead = SUBLANES + ntaps - 1
    n = tm + halo
    row = lax.broadcasted_iota(jnp.int32, (SUBLANES, d), 0)
    taps = {s: [(k, (lead - k) // SUBLANES * SUBLANES) for k in range(ntaps) if (lead - k) % SUBLANES == s]
            for s in range(SUBLANES)}
    wtaps = {s: [(k, s + off + k - SUBLANES) for k in range(ntaps) if (-(off + k)) % SUBLANES == s]
             for s in range(SUBLANES)}
    rw = 2 * SUBLANES
    started = False
    for group in _residue_passes([s for s in range(SUBLANES) if taps[s] or wtaps[s]], sh_ref.shape[0]):
        srcs = {}
        for s in group:
            if s == 0:
                srcs[s] = dp_ref
            else:
                j = len(srcs) - (0 in srcs)
                sh_ref[j, 0:n, :] = dp_ref[s:s + n, :]
                srcs[s] = sh_ref.at[j]
        if any(taps[s] for s in srcs):
            for r0 in range(0, tm, rc):
                acc = dx_ref[r0:r0 + rc, :] if started else jnp.zeros((rc, d), F32)
                for s, src in srcs.items():
                    for k, a in taps[s]:
                        acc = acc + w_ref[k:k + 1, :] * src[r0 + a:r0 + a + rc, :]
                dx_ref[r0:r0 + rc, :] = acc
            started = True
        for s, src in srcs.items():
            if not wtaps[s]:
                continue
            sums = [jnp.zeros((SUBLANES, d), F32) for _ in wtaps[s]]
            for r0 in range(0, tm, rw):
                g = src[r0:r0 + rw, :]
                for j, (_, e) in enumerate(wtaps[s]):
                    p = g * ext_ref[r0 + e:r0 + e + rw, :]
                    for r8 in range(0, rw, SUBLANES):
                        sums[j] = sums[j] + p[r8:r8 + SUBLANES, :]
            for j, (k, e) in enumerate(wtaps[s]):
                tail = src[tm:tm + SUBLANES, :] * ext_ref[tm + e:tm + e + SUBLANES, :]
                dw_ref[k:k + 1, :] += _rsum(sums[j] + jnp.where(row < SUBLANES - s, tail, 0.0))


def _fwd_in(x, vecs, wa, b_in, wb, tm):
    s, d = x.shape
    nq, _, nw = wa.shape
    nt = s // tm

    def body(x_ref, v_ref, w_ref, b_ref, wbi_ref, h_ref, z_ref, wbo_ref, send_sems, recv_sems):
        i = pl.program_id(0)
        gather = (wbi_ref, wbo_ref, send_sems, recv_sems)
        pl.when(i == 0)(lambda: _GatherShards.start(*gather))
        pl.when(i == nt // 2)(lambda: _GatherShards.relay(*gather))
        xh, _ = _ln(x_ref[...])
        h = (xh * (1.0 + v_ref[V_SCALE1:V_SCALE1 + 1, :]) + v_ref[V_SHIFT1:V_SHIFT1 + 1, :]).astype(BF16)
        h_ref[...] = h
        for q in range(nq):
            z_ref[:, q * nw:(q + 1) * nw] = _dot(h, w_ref[q]) + b_ref[:, q * nw:(q + 1) * nw]
        pl.when(i == nt - 1)(lambda: _GatherShards.finish(*gather))

    any_spec = pl.BlockSpec(memory_space=pl.ANY)
    n = _GatherShards.n_sems
    return _pcall(
        body, name="fwd_in", grid=(nt,),
        in_specs=[_rows(tm, d), _full(vecs.shape), _full(wa.shape, single=True), _full(b_in.shape), any_spec],
        out_specs=[_rows(tm, d), _rows(tm, nq * nw), any_spec],
        out_shape=[jax.ShapeDtypeStruct((s, d), BF16), jax.ShapeDtypeStruct((s, nq * nw), F32),
                   jax.ShapeDtypeStruct(wb.shape, wb.dtype)],
        input_output_aliases={4: 2},
        scratch_shapes=[pltpu.SemaphoreType.DMA((n,)), pltpu.SemaphoreType.DMA((n,))],
        compiler_params=_seq(1),
    )(x, vecs, wa, b_in, wb)


def _fwd_mix(z, vecs, caw, cbw, wb, tm, rc):
    s = z.shape[0]
    d = vecs.shape[1]
    nq = wb.shape[0]
    kq = d // nq
    base = 2 * d // kq

    def body(z_ref, v_ref, caw_ref, cbw_ref, wao_ref, wbo_ref, wo_ref,
             u1_ref, ya_ref, yb_ref, o1_ref, u3_ref, vv_ref, mg_ref, ext_ref, sh_ref, pext_ref, q_ref):
        @pl.when(pl.program_id(0) == 0)
        def _():
            ext_ref[0:HALO_A, :] = jnp.zeros((HALO_A, d), F32)
            pext_ref[0:HALO_B, :] = jnp.zeros((HALO_B, d), F32)

        ext_ref[HALO_A:HALO_A + tm, :] = z_ref[:, 0:d] * _sig(z_ref[:, d:2 * d])
        _conv_causal(ext_ref, sh_ref, caw_ref, CONV_A, HALO_A, v_ref[V_CAB:V_CAB + 1, :], u1_ref, tm, d, rc)
        ext_ref[0:HALO_A, :] = ext_ref[tm:tm + HALO_A, :]
        xa, _ = _ln(u1_ref[...])
        u2 = xa * v_ref[V_LNAG:V_LNAG + 1, :] + v_ref[V_LNAB:V_LNAB + 1, :]
        u3 = (u2 * _sig(u2)).astype(BF16)
        u3_ref[...] = u3
        ya = jnp.broadcast_to(v_ref[V_BAO:V_BAO + 1, :], (tm, d))
        for q in range(nq):
            ya = ya + _dot(u3[:, q * kq:(q + 1) * kq], wao_ref[q])
        ya_ref[...] = ya

        pext_ref[HALO_B:HALO_B + tm, :] = z_ref[:, 3 * d:4 * d] * z_ref[:, 4 * d:5 * d]
        _conv_causal(pext_ref, sh_ref, cbw_ref, CONV_B, HALO_B, None, q_ref, tm, d, rc)
        pext_ref[0:HALO_B, :] = pext_ref[tm:tm + HALO_B, :]
        vv = (z_ref[:, 2 * d:3 * d] * q_ref[...]).astype(BF16)
        vv_ref[...] = vv
        yb = jnp.zeros((tm, d), F32)
        for q in range(nq):
            yb = yb + _dot(vv[:, q * kq:(q + 1) * kq], wbo_ref[q])
        yb_ref[...] = yb

        mg = (_sig(z_ref[:, 5 * d:6 * d]) * ya + _sig(z_ref[:, 6 * d:7 * d]) * yb).astype(BF16)
        mg_ref[...] = mg
        o1 = jnp.broadcast_to(v_ref[V_BO:V_BO + 1, :], (tm, d))
        for q in range(nq):
            o1 = o1 + _dot(mg[:, q * kq:(q + 1) * kq], wo_ref[q])
        o1_ref[...] = o1

    def wspec(j):
        return pl.BlockSpec((nq, kq, d), lambda i: (0, base + j, 0), pipeline_mode=pl.Buffered(1))

    f32o = jax.ShapeDtypeStruct((s, d), F32)
    b16o = jax.ShapeDtypeStruct((s, d), BF16)
    return _pcall(
        body, name="fwd_mix", grid=(s // tm,),
        in_specs=[_rows(tm, 7 * d), _full(vecs.shape), _full(caw.shape), _full(cbw.shape), wspec(0), wspec(1), wspec(2)],
        out_specs=[_rows(tm, d)] * 7,
        out_shape=[f32o, f32o, f32o, f32o, b16o, b16o, b16o],
        scratch_shapes=[pltpu.VMEM((HALO_A + tm, d), F32), pltpu.VMEM((SHIFTED_COPIES, HALO_A + tm, d), F32),
                        pltpu.VMEM((HALO_B + tm, d), F32), pltpu.VMEM((tm, d), F32)],
        compiler_params=_seq(1),
    )(z, vecs, caw, cbw, wb, wb, wb)


def _mlp_fwd_bwd(x, out1, tgt, vecs, b_up, wb, tm, groups):
    s, d = x.shape
    nq = wb.shape[0]
    dff = nq * d
    hm = tm // groups

    def body(x_ref, o1_ref, t_ref, v_ref, bup_ref, wup_ref, wdn_ref,
             h2_ref, f_ref, df0_ref, do2_ref, do1_ref, dxp_ref, acc_ref, dbup_ref, f0_ref):
        @pl.when(pl.program_id(0) == 0)
        def _():
            acc_ref[...] = jnp.zeros(acc_ref.shape, F32)
            dbup_ref[...] = jnp.zeros(dbup_ref.shape, F32)

        def vec(r):
            return v_ref[r:r + 1, :]

        def accum(r, val):
            acc_ref[r:r + 1, :] += _rsum(val)

        for g in range(groups):
            rs = slice(g * hm, (g + 1) * hm)
            out1v = o1_ref[rs, :]
            r1 = ALPHA * x_ref[rs, :] + (1.0 + vec(V_GATE1)) * out1v
            xh1, rstd1 = _ln(r1)
            x1 = xh1 * vec(V_LN1G) + vec(V_LN1B)
            xn1, rstdn = _ln(x1)
            h2 = (xn1 * (1.0 + vec(V_SCALE2)) + vec(V_SHIFT2)).astype(BF16)
            h2_ref[rs, :] = h2
            out2 = jnp.broadcast_to(vec(V_BDN), (hm, d))
            for q in range(nq):
                f0 = _dot(h2, wup_ref[q]) + bup_ref[:, q * d:(q + 1) * d]
                rl = jnp.maximum(f0, 0.0)
                f0_ref[rs, q * d:(q + 1) * d] = rl
                fb = (rl * rl).astype(BF16)
                f_ref[rs, q * d:(q + 1) * d] = fb
                out2 = out2 + _dot(fb, wdn_ref[q])
            r2 = ALPHA * x1 + (1.0 + vec(V_GATE2)) * out2
            xh2, rstd2 = _ln(r2)
            yv = xh2 * vec(V_LN2G) + vec(V_LN2B)
            err = yv - t_ref[rs, :]
            accum(M_LOSS, err * err)
            dy = err * (1.0 / d)
            accum(M_LN2G, dy * xh2)
            accum(M_LN2B, dy)
            dr2 = _ln_bwd(dy * vec(V_LN2G), xh2, rstd2)
            accum(M_GATE2, dr2 * out2)
            dout2 = (1.0 + vec(V_GATE2)) * dr2
            accum(M_BDN, dout2)
            do2b = dout2.astype(BF16)
            do2_ref[rs, :] = do2b
            dh2 = jnp.zeros((hm, d), F32)
            for q in range(nq):
                df0 = _dot_nt(do2b, wdn_ref[q]) * (2.0 * f0_ref[rs, q * d:(q + 1) * d])
                dbup_ref[q:q + 1, :] += _rsum(df0)
                df0b = df0.astype(BF16)
                df0_ref[rs, q * d:(q + 1) * d] = df0b
                dh2 = dh2 + _dot_nt(df0b, wup_ref[q])
            accum(M_SHIFT2, dh2)
            accum(M_SCALE2, dh2 * xn1)
            dx1 = ALPHA * dr2 + _ln_bwd(dh2 * (1.0 + vec(V_SCALE2)), xn1, rstdn)
            accum(M_LN1G, dx1 * xh1)
            accum(M_LN1B, dx1)
            dr1 = _ln_bwd(dx1 * vec(V_LN1G), xh1, rstd1)
            accum(M_GATE1, dr1 * out1v)
            dout1 = (1.0 + vec(V_GATE1)) * dr1
            accum(M_BO, dout1)
            do1_ref[rs, :] = dout1.astype(BF16)
            dxp_ref[rs, :] = ALPHA * dr1

    def wspec(j):
        return pl.BlockSpec((nq, d, d), lambda i: (0, j, 0), pipeline_mode=pl.Buffered(1))

    b16 = lambda w: jax.ShapeDtypeStruct((s, w), BF16)
    return _pcall(
        body, name="mlp_fwd_bwd", grid=(s // tm,),
        in_specs=[_rows(tm, d), _rows(tm, d), _rows(tm, d), _full(vecs.shape), _full(b_up.shape), wspec(0), wspec(1)],
        out_specs=[_rows(tm, d), _rows(tm, dff), _rows(tm, dff), _rows(tm, d), _rows(tm, d), _rows(tm, d),
                   _full((16, d)), _full((SUBLANES, d))],
        out_shape=[b16(d), b16(dff), b16(dff), b16(d), b16(d), jax.ShapeDtypeStruct((s, d), F32),
                   jax.ShapeDtypeStruct((16, d), F32), jax.ShapeDtypeStruct((SUBLANES, d), F32)],
        scratch_shapes=[pltpu.VMEM((tm, dff), F32)],
        compiler_params=_seq(1),
    )(x, out1, tgt, vecs, b_up, wb, wb)


def _mix_bwd(dout1, z, u1, ya, yb, vecs, caw, cbw, wb, tm, rc):
    s = z.shape[0]
    d = vecs.shape[1]
    nq = wb.shape[0]
    kq = d // nq
    base = 2 * d // kq
    nt = s // tm
    hb = tm // HALO_A

    def body(do1_ref, z_ref, zh_ref, u1_ref, ya_ref, yb_ref, v_ref, caw_ref, cbw_ref, wao_ref, wbo_ref, wo_ref,
             dz_ref, dya_ref, dyb_ref, acc_ref, dcaw_ref, dcbw_ref, dbin_ref,
             ext_ref, du1p_ref, sh_ref, pext_ref, dqp_ref, tmp_ref):
        i = pl.program_id(0)

        @pl.when(i == 0)
        def _():
            acc_ref[...] = jnp.zeros(acc_ref.shape, F32)
            dcaw_ref[...] = jnp.zeros(dcaw_ref.shape, F32)
            dcbw_ref[...] = jnp.zeros(dcbw_ref.shape, F32)
            dbin_ref[...] = jnp.zeros(dbin_ref.shape, F32)
            du1p_ref[0:SUBLANES, :] = jnp.zeros((SUBLANES, d), F32)
            du1p_ref[SUBLANES + tm:SUBLANES + tm + HALO_A, :] = jnp.zeros((HALO_A, d), F32)
            dqp_ref[0:SUBLANES, :] = jnp.zeros((SUBLANES, d), F32)
            dqp_ref[SUBLANES + tm:SUBLANES + tm + HALO_B, :] = jnp.zeros((HALO_B, d), F32)

        def vec(r):
            return v_ref[r:r + 1, :]

        def accum(r, val):
            acc_ref[r:r + 1, :] += _rsum(val)

        def put_dz(j, val):
            dbin_ref[j:j + 1, :] += _rsum(val)
            dz_ref[:, j * d:(j + 1) * d] = val.astype(BF16)

        has_history = i < nt - 1

        do1 = do1_ref[...]
        dmg = jnp.concatenate([_dot_nt(do1, wo_ref[q]) for q in range(nq)], axis=1)
        sga = _sig(z_ref[:, 5 * d:6 * d])
        sgb = _sig(z_ref[:, 6 * d:7 * d])
        dya = dmg * sga
        dyb = dmg * sgb
        accum(X_BAO, dya)
        put_dz(5, dya * ya_ref[...] * (1.0 - sga))
        put_dz(6, dyb * yb_ref[...] * (1.0 - sgb))
        dyab = dya.astype(BF16)
        dybb = dyb.astype(BF16)
        dya_ref[...] = dyab
        dyb_ref[...] = dybb

        du3 = jnp.concatenate([_dot_nt(dyab, wao_ref[q]) for q in range(nq)], axis=1)
        xa, rstda = _ln(u1_ref[...])
        u2 = xa * vec(V_LNAG) + vec(V_LNAB)
        s2 = _sig(u2)
        du2 = du3 * (s2 * (1.0 + u2 * (1.0 - s2)))
        accum(X_LNAG, du2 * xa)
        accum(X_LNAB, du2)
        du1 = _ln_bwd(du2 * vec(V_LNAG), xa, rstda)
        accum(X_CAB, du1)
        du1p_ref[SUBLANES:SUBLANES + tm, :] = du1
        sg = _sig(z_ref[:, d:2 * d])
        aval = z_ref[:, 0:d]
        ext_ref[HALO_A:HALO_A + tm, :] = aval * sg
        ext_ref[0:HALO_A, :] = jnp.where(has_history, zh_ref[:, 0:d] * _sig(zh_ref[:, d:2 * d]), 0.0)
        _conv_adjoint(du1p_ref, sh_ref, ext_ref, caw_ref, tmp_ref, dcaw_ref, CONV_A, HALO_A, tm, d, rc)
        du1p_ref[SUBLANES + tm:SUBLANES + tm + HALO_A, :] = du1p_ref[SUBLANES:SUBLANES + HALO_A, :]
        du0 = tmp_ref[...]
        put_dz(0, du0 * sg)
        put_dz(1, du0 * aval * sg * (1.0 - sg))

        dv = jnp.concatenate([_dot_nt(dybb, wbo_ref[q]) for q in range(nq)], axis=1)
        bgc = z_ref[:, 3 * d:4 * d]
        bx = z_ref[:, 4 * d:5 * d]
        pext_ref[HALO_B:HALO_B + tm, :] = bgc * bx
        pext_ref[0:HALO_B, :] = jnp.where(
            has_history, zh_ref[HALO_A - HALO_B:HALO_A, 3 * d:4 * d] * zh_ref[HALO_A - HALO_B:HALO_A, 4 * d:5 * d], 0.0)
        _conv_causal(pext_ref, sh_ref, cbw_ref, CONV_B, HALO_B, None, tmp_ref, tm, d, rc)
        put_dz(2, dv * tmp_ref[...])
        dqp_ref[SUBLANES:SUBLANES + tm, :] = dv * z_ref[:, 2 * d:3 * d]
        _conv_adjoint(dqp_ref, sh_ref, pext_ref, cbw_ref, tmp_ref, dcbw_ref, CONV_B, HALO_B, tm, d, rc)
        dqp_ref[SUBLANES + tm:SUBLANES + tm + HALO_B, :] = dqp_ref[SUBLANES:SUBLANES + HALO_B, :]
        dp = tmp_ref[...]
        put_dz(3, dp * bx)
        put_dz(4, dp * bgc)

    def rev(width):
        return pl.BlockSpec((tm, width), lambda i: (nt - 1 - i, 0))

    def wspec(j):
        return pl.BlockSpec((nq, kq, d), lambda i: (0, base + j, 0), pipeline_mode=pl.Buffered(1))

    halo = pl.BlockSpec((HALO_A, 7 * d), lambda i: (jnp.maximum((nt - 1 - i) * hb - 1, 0), 0))
    b16 = jax.ShapeDtypeStruct((s, d), BF16)
    acc8 = jax.ShapeDtypeStruct((SUBLANES, d), F32)
    return _pcall(
        body, name="mix_bwd", grid=(nt,),
        in_specs=[rev(d), rev(7 * d), halo, rev(d), rev(d), rev(d), _full(vecs.shape), _full(caw.shape), _full(cbw.shape),
                  wspec(0), wspec(1), wspec(2)],
        out_specs=[rev(7 * d), rev(d), rev(d), _full((SUBLANES, d)), _full((HALO_A, d)), _full((HALO_B, d)),
                   _full((SUBLANES, d))],
        out_shape=[jax.ShapeDtypeStruct((s, 7 * d), BF16), b16, b16, acc8,
                   jax.ShapeDtypeStruct((HALO_A, d), F32), jax.ShapeDtypeStruct((HALO_B, d), F32), acc8],
        scratch_shapes=[pltpu.VMEM((HALO_A + tm, d), F32), pltpu.VMEM((SUBLANES + tm + HALO_A, d), F32),
                        pltpu.VMEM((SHIFTED_COPIES, tm + HALO_A, d), F32), pltpu.VMEM((HALO_B + tm, d), F32),
                        pltpu.VMEM((SUBLANES + tm + HALO_B, d), F32), pltpu.VMEM((tm, d), F32)],
        compiler_params=_seq(1),
    )(dout1, z, z, u1, ya, yb, vecs, caw, cbw, wb, wb, wb)


def _in_bwd(dz, x, dxp, vecs, wa, tm, plans, plan_srcs):
    s, d = x.shape
    nq, _, nw = wa.shape
    nt = s // tm
    nc = len(plan_srcs)

    def body(dz_ref, x_ref, dxp_ref, v_ref, w_ref, *rest):
        src_refs, (gx_ref, acc_ref), dst_refs = rest[:nc], rest[nc:nc + 2], rest[nc + 2:2 * nc + 2]
        send_sems, recv_sems = rest[2 * nc + 2:]
        i = pl.program_id(0)
        comm = (src_refs, dst_refs, send_sems, recv_sems)
        _carried_start(plans, i == 0, comm)

        @pl.when(i == 0)
        def _():
            acc_ref[...] = jnp.zeros(acc_ref.shape, F32)

        dh1 = jnp.zeros((tm, d), F32)
        for q in range(nq):
            dh1 = dh1 + _dot_nt(dz_ref[:, q * nw:(q + 1) * nw], w_ref[q])
        xh, rstd = _ln(x_ref[...])
        acc_ref[I_SHIFT1:I_SHIFT1 + 1, :] += _rsum(dh1)
        acc_ref[I_SCALE1:I_SCALE1 + 1, :] += _rsum(dh1 * xh)
        gx_ref[...] = dxp_ref[...] + _ln_bwd(dh1 * (1.0 + v_ref[V_SCALE1:V_SCALE1 + 1, :]), xh, rstd)
        _carried_wait(plans, i == nt - 1, comm)

    any_spec = pl.BlockSpec(memory_space=pl.ANY)
    return _pcall(
        body, name="in_bwd", grid=(nt,),
        in_specs=[_rows(tm, nq * nw), _rows(tm, d), _rows(tm, d), _full(vecs.shape), _full(wa.shape, single=True)]
        + [any_spec] * nc,
        out_specs=[_rows(tm, d), _full((SUBLANES, d))] + [any_spec] * nc,
        out_shape=[jax.ShapeDtypeStruct((s, d), F32), jax.ShapeDtypeStruct((SUBLANES, d), F32)]
        + [plan.out_shape(p) for plan, p in zip(plans, plan_srcs)],
        scratch_shapes=_plan_sems(plans),
        compiler_params=_seq(1),
    )(dz, x, dxp, vecs, wa, *plan_srcs)


def _dw(a, b, split_a, ts, name, into=None, rows_total=None, row_block=0):
    s = a.shape[0]
    ka = a.shape[1] // N_CHIPS if split_a else a.shape[1]
    nb = b.shape[1] if split_a else b.shape[1] // N_CHIPS
    rows_total = ka if rows_total is None else rows_total

    def body(a_ref, b_ref, *rest):
        o_ref = rest[-1]

        @pl.when(pl.program_id(1) == 0)
        def _():
            o_ref[...] = jnp.zeros(o_ref.shape, F32)

        o_ref[...] += _dot_tn(a_ref[...], b_ref[...])

    a_spec = pl.BlockSpec((ts, ka), (lambda q, i: (i, q)) if split_a else (lambda q, i: (i, 0)))
    b_spec = pl.BlockSpec((ts, nb), (lambda q, i: (i, 0)) if split_a else (lambda q, i: (i, q)))
    extra = {} if into is None else dict(input_output_aliases={2: 0})
    return _pcall(
        body, name=name, grid=(N_CHIPS, s // ts),
        in_specs=[a_spec, b_spec] + ([] if into is None else [pl.BlockSpec(memory_space=pl.ANY)]),
        out_specs=pl.BlockSpec((None, ka, nb), lambda q, i: (q, row_block, 0)),
        out_shape=jax.ShapeDtypeStruct((N_CHIPS, rows_total, nb), F32),
        compiler_params=_seq(2), **extra,
    )(*((a, b) if into is None else (a, b, into)))


def _dw_carrying(a, b, ts, name, plan, plan_src):
    s, k = a.shape
    nb = b.shape[1] // N_CHIPS
    ns = s // ts

    def body(a_ref, b_ref, src_ref, o_ref, dst_ref, send_sems, recv_sems):
        q, i = pl.program_id(0), pl.program_id(1)
        comm = ([src_ref], [dst_ref], send_sems, recv_sems)
        _carried_start([plan], jnp.logical_and(q == 0, i == 0), comm)

        @pl.when(i == 0)
        def _():
            o_ref[...] = jnp.zeros(o_ref.shape, F32)

        o_ref[...] += _dot_tn(a_ref[...], b_ref[...])
        _carried_wait([plan], jnp.logical_and(q == N_CHIPS - 1, i == ns - 1), comm)

    any_spec = pl.BlockSpec(memory_space=pl.ANY)
    return _pcall(
        body, name=name, grid=(N_CHIPS, ns),
        in_specs=[pl.BlockSpec((ts, k), lambda q, i: (i, 0)), pl.BlockSpec((ts, nb), lambda q, i: (i, q)), any_spec],
        out_specs=[pl.BlockSpec((None, k, nb), lambda q, i: (q, 0, 0)), any_spec],
        out_shape=[jax.ShapeDtypeStruct((N_CHIPS, k, nb), F32), plan.out_shape(plan_src)],
        scratch_shapes=_plan_sems([plan]),
        compiler_params=_seq(2),
    )(a, b, plan_src)


def _dw_rows(a, b, ts, name, into, row_block):
    s, k = a.shape
    n = b.shape[1]
    kq = k // N_CHIPS

    def body(a_ref, b_ref, buf_ref, o_ref):
        @pl.when(pl.program_id(0) == 0)
        def _():
            o_ref[...] = jnp.zeros(o_ref.shape, F32)

        res = _dot_tn(a_ref[...], b_ref[...])
        for q in range(N_CHIPS):
            o_ref[q] += res[q * kq:(q + 1) * kq, :]

    return _pcall(
        body, name=name, grid=(s // ts,),
        in_specs=[_rows(ts, k), _rows(ts, n), pl.BlockSpec(memory_space=pl.ANY)],
        out_specs=pl.BlockSpec((N_CHIPS, kq, n), lambda i: (0, row_block, 0)),
        out_shape=jax.ShapeDtypeStruct(into.shape, F32), input_output_aliases={2: 0},
        compiler_params=_seq(1),
    )(a, b, into)


def _adam_math(w, g, m, v):
    m2 = ADAM_B1 * m + (1.0 - ADAM_B1) * g
    v2 = ADAM_B2 * v + (1.0 - ADAM_B2) * (g * g)
    m_hat = m2 / (1.0 - ADAM_B1 ** ADAM_STEP)
    v_hat = v2 / (1.0 - ADAM_B2 ** ADAM_STEP)
    delta = -ADAM_LR * (m_hat / (jnp.sqrt(v_hat) + ADAM_EPS) + ADAM_WD * w)
    return delta, m2, v2


def _adam(w, g, m, v, g_row0, tr, name):
    r, c = w.shape
    blk0 = g_row0 // tr

    def body(w_ref, g_ref, m_ref, v_ref, go_ref, d_ref, mo_ref, vo_ref):
        gv = g_ref[...]
        go_ref[...] = gv
        d_ref[...], mo_ref[...], vo_ref[...] = _adam_math(w_ref[...], gv, m_ref[...], v_ref[...])

    spec = _rows(tr, c)
    g_spec = pl.BlockSpec((tr, c), lambda i: (blk0 + i, 0))
    o = jax.ShapeDtypeStruct((r, c), F32)
    return _pcall(body, name=name, grid=(r // tr,), in_specs=[spec, g_spec, spec, spec], out_specs=[spec] * 4,
                  out_shape=[o, o, o, o], compiler_params=_seq(1))(w, g, m, v)


def _small_update(gathered_head, gathered, q_idx, small_w, small_m, small_v, conv_w, conv_m, conv_v):
    d = gathered.shape[2]
    ns = len(_SMALL)
    cw = conv_w[0].shape[1]
    conv_rows = ((T_CAW, CONV_A), (T_CBW, CONV_B))

    def body(q_ref, h_ref, g_ref, *refs):
        ins, outs = refs[:3 * (ns + 2)], refs[3 * (ns + 2):]
        tot_ref, loss_ref = outs[0], outs[1]
        outs = outs[2:]
        head, tot = h_ref[0], g_ref[0]
        for dev in range(1, N_DEV):
            head = head + h_ref[dev]
            tot = tot + g_ref[dev]
        tot_ref[0:T_M, :] = head
        tot_ref[T_M:T_ROWS, :] = tot
        loss_ref[...] = (0.5 / d) * jnp.sum(tot_ref[T_LOSS:T_LOSS + 1, :], axis=1, keepdims=True)
        for p, (_, rows) in enumerate(_SMALL):
            w_ref, m_ref, v_ref = ins[p], ins[ns + 2 + p], ins[2 * (ns + 2) + p]
            go, do, mo, vo = outs[4 * p:4 * p + 4]
            for j, row in enumerate(rows):
                sl = slice(j * d, (j + 1) * d)
                gv = tot_ref[row:row + 1, :]
                go[:, sl] = gv
                do[:, sl], mo[:, sl], vo[:, sl] = _adam_math(w_ref[:, sl], gv, m_ref[:, sl], v_ref[:, sl])
        for p, (row, taps) in enumerate(conv_rows):
            w_ref, m_ref, v_ref = ins[ns + p], ins[ns + 2 + ns + p], ins[2 * (ns + 2) + ns + p]
            go, do, mo, vo = outs[4 * (ns + p):4 * (ns + p) + 4]
            gv = tot_ref[row:row + taps, 0:cw]
            for qq in range(1, N_CHIPS):
                gv = jnp.where(q_ref[0] == qq, tot_ref[row:row + taps, qq * cw:(qq + 1) * cw], gv)
            go[...] = gv
            do[...], mo[...], vo[...] = _adam_math(w_ref[...], gv, m_ref[...], v_ref[...])

    params = list(small_w) + list(conv_w) + list(small_m) + list(conv_m) + list(small_v) + list(conv_v)
    out_shape = [jax.ShapeDtypeStruct((T_ROWS, d), F32), jax.ShapeDtypeStruct((1, 1), F32)]
    for w in list(small_w) + list(conv_w):
        out_shape += [jax.ShapeDtypeStruct(w.shape, F32)] * 4
    vm = pl.BlockSpec(memory_space=pltpu.VMEM)
    return _pcall(
        body, name="small_update", out_shape=out_shape,
        in_specs=[pl.BlockSpec(memory_space=pltpu.SMEM), vm, vm] + [vm] * len(params),
        out_specs=[vm] * len(out_shape), compiler_params=_cparams(),
    )(q_idx, gathered_head, gathered, *params)


def kernel(x, c, w_ada, b_ada, w_in, b_in, conv_a_w, conv_a_b, ln_a_g, ln_a_b, w_a_out, b_a_out, conv_b_w, w_b_out, w_o, b_o, ln1_g, ln1_b, w_up, b_up, w_down, b_down, ln2_g, ln2_b, loss_target, m_w_ada, m_b_ada, m_w_in, m_b_in, m_conv_a_w, m_conv_a_b, m_ln_a_g, m_ln_a_b, m_w_a_out, m_b_a_out, m_conv_b_w, m_w_b_out, m_w_o, m_b_o, m_ln1_g, m_ln1_b, m_w_up, m_b_up, m_w_down, m_b_down, m_ln2_g, m_ln2_b, v_w_ada, v_b_ada, v_w_in, v_b_in, v_conv_a_w, v_conv_a_b, v_ln_a_g, v_ln_a_b, v_w_a_out, v_b_a_out, v_conv_b_w, v_w_b_out, v_w_o, v_b_o, v_ln1_g, v_ln1_b, v_w_up, v_b_up, v_w_down, v_b_down, v_ln2_g, v_ln2_b):
    given = dict(locals())
    s, d = x.shape[1], x.shape[2]
    xi, yi, ci = _my_pos()
    q = 2 * xi + yi
    me = 4 * xi + 2 * yi + ci
    i32 = jnp.int32
    q_arr = jnp.reshape(q, (1,)).astype(i32)
    others = [2 * ox + oy for ox, oy in _other_chips(xi, yi)]
    halves_idx = jnp.stack([ci] + others).astype(i32)
    chips_idx = jnp.stack([q, ci]).astype(i32)
    kq = d // N_CHIPS
    tm = min(256, s)
    rc = tm

    def sq(a):
        return a.reshape(a.shape[1:])

    x2, tgt = sq(x), sq(loss_target)

    wa = _place_shard([sq(w_in)], q_arr, "place_w_in")
    wb = _place_shard([sq(w_up), sq(w_down), sq(w_a_out), sq(w_b_out), sq(w_o)], q_arr, "place_w_rest")

    n_ada = w_ada.shape[2]
    pre = jnp.concatenate([
        jnp.broadcast_to(c, (SUBLANES, d)),
        jnp.pad(sq(conv_a_w), ((0, HALO_A - CONV_A), (0, d - kq))),
        jnp.pad(sq(conv_b_w), ((0, HALO_B - CONV_B), (0, d - kq)))], axis=0)
    b_ada_sh = lax.dynamic_slice(b_ada, (0, q * n_ada), (1, n_ada))
    pre_all, c_all, mod_all, wa = _prologue(pre, sq(w_ada), b_ada_sh, wa)
    caw = jnp.concatenate([pre_all[2 * p, SUBLANES:SUBLANES + HALO_A, :kq] for p in range(N_CHIPS)], axis=1)
    cbw = jnp.concatenate([pre_all[2 * p, SUBLANES + HALO_A:, :kq] for p in range(N_CHIPS)], axis=1)
    mod_rows = lax.dynamic_slice(mod_all, (0, me, 0), (N_DEV, 1, n_ada))[0::2, 0, :]
    mod = mod_rows.reshape(6, d)
    vecs = jnp.concatenate([mod, conv_a_b, ln_a_g, ln_a_b, b_a_out, b_o, ln1_g, ln1_b, b_down, ln2_g, ln2_b], axis=0)

    h1, z, wb = _fwd_in(x2, vecs, wa, b_in, wb, tm)
    u1, ya, yb, out1, u3, vv, mg = _fwd_mix(z, vecs, caw, cbw, wb, tm, rc)

    ts = min(2048, s)
    rest_rows = wb.shape[1]
    small0 = 2 * d // kq
    h2, fb, df0, do2, do1, dxp, macc, dbup = _mlp_fwd_bwd(x2, out1, tgt, vecs, b_up, wb, tm, 1)
    gb = _dw(h2, df0, False, ts, "dw_up", rows_total=rest_rows)
    gb = _dw(fb, do2, True, ts, "dw_down", into=gb, rows_total=rest_rows, row_block=1)
    dz, dya, dyb, xacc, dcaw, dcbw, dbin = _mix_bwd(do1, z, u1, ya, yb, vecs, caw, cbw, wb, tm, rc)
    gb = _dw_rows(u3, dya, ts, "dw_a_out", gb, small0)
    gb = _dw_rows(vv, dyb, ts, "dw_b_out", gb, small0 + 1)
    gb = _dw_rows(mg, do1, ts, "dw_o", gb, small0 + 2)

    trb, tra = rest_rows // 8, d // 8
    ga, rb = _dw_carrying(h1, dz, ts, "dw_in", _SiblingHalf, gb)
    pb = _add_halves(gb, rb, halves_idx, trb, "rs_add_halves_rest")
    ra = _exchange(_SiblingHalf, ga, "rs_to_sibling")
    pa = _add_halves(ga, ra, halves_idx, tra, "rs_add_halves_in")
    table = jnp.concatenate([macc, dbin, dcaw, xacc, dcbw, dbup], axis=0)
    gx, iacc, r3a, r3b, gathered = _in_bwd(dz, x2, dxp, vecs, wa, tm, [_ChipBlocks, _ChipBlocks, _TableToAll],
                                           [pa, pb, table])
    fa = _add_chips(ga, ra, r3a, chips_idx, tra, "rs_add_chips_in")
    fb_ = _add_chips(gb, rb, r3b, chips_idx, trb, "rs_add_chips_rest")
    g_in, g_b = _rs_join_halves(fa, fb_)

    gathered_head = _all_gather_small(iacc, "gather_ln0_sums")
    names = [n for n, _ in _SMALL]
    res = _small_update(
        gathered_head, gathered, q_arr,
        [given[n] for n in names], [given["m_" + n] for n in names], [given["v_" + n] for n in names],
        [sq(conv_a_w), sq(conv_b_w)], [sq(m_conv_a_w), sq(m_conv_b_w)], [sq(v_conv_a_w), sq(v_conv_b_w)])
    loss = res[1].reshape(())
    upd = {}
    for p, n in enumerate(names + ["conv_a_w", "conv_b_w"]):
        upd[n] = res[2 + 4 * p:6 + 4 * p]

    dmod_all = jnp.stack([gathered_head[:, r, :] if r < T_M else gathered[:, r - T_M, :] for r in _SMALL[0][1]],
                         axis=1).reshape(N_DEV, 6 * d)
    dmod_sh = lax.dynamic_slice(dmod_all, (0, q * n_ada), (N_DEV, n_ada))
    g_ada = _ada_bwd(c_all.T, dmod_sh)
    upd["w_ada"] = _adam(sq(w_ada), g_ada, sq(m_w_ada), sq(v_w_ada), 0, min(256, d), "adam_w_ada")

    upd["w_in"] = _adam(sq(w_in), g_in, sq(m_w_in), sq(v_w_in), 0, min(256, d), "adam_w_in")
    r0 = 0
    for n in ("w_up", "w_down", "w_a_out", "w_b_out", "w_o"):
        w = sq(given[n])
        upd[n] = _adam(w, g_b, sq(given["m_" + n]), sq(given["v_" + n]), r0, min(256, w.shape[0]), "adam_" + n)
        r0 += w.shape[0]

    order = ["w_ada", "b_ada", "w_in", "b_in", "conv_a_w", "conv_a_b", "ln_a_g", "ln_a_b", "w_a_out", "b_a_out", "conv_b_w",
             "w_b_out", "w_o", "b_o", "ln1_g", "ln1_b", "w_up", "b_up", "w_down", "b_down", "ln2_g", "ln2_b"]
    outs = [loss, gx.reshape(x.shape)]
    for k in range(4):
        outs += [upd[n][k].reshape(given[n].shape) for n in order]
    return tuple(outs)
```

```python
import jax
import jax.numpy as jnp
from jax import lax
from jax.experimental import pallas as pl
from jax.experimental.pallas import tpu as pltpu

F32 = jnp.float32
BF16 = jnp.bfloat16
MESH = pl.DeviceIdType.MESH

LN_EPS = 1e-5
DEPTH = 1
ALPHA = (2.0 * DEPTH) ** 0.25
CONV_A = 31
CONV_B = 3
SUBLANES = 8
HALO_A = 32
HALO_B = 8
SHIFTED_COPIES = 4
N_CHIPS = 4
N_DEV = 8
ADAM_LR = 0.001
ADAM_B1 = 0.9
ADAM_B2 = 0.999
ADAM_EPS = 1e-08
ADAM_WD = 0.01
ADAM_STEP = 10
VMEM_LIMIT = 56 * 1024 * 1024

V_SHIFT1, V_SCALE1, V_GATE1, V_SHIFT2, V_SCALE2, V_GATE2 = 0, 1, 2, 3, 4, 5
V_CAB, V_LNAG, V_LNAB, V_BAO, V_BO, V_LN1G, V_LN1B, V_BDN, V_LN2G, V_LN2B = 6, 7, 8, 9, 10, 11, 12, 13, 14, 15

M_LN2G, M_LN2B, M_GATE2, M_BDN, M_SHIFT2, M_SCALE2, M_LN1G, M_LN1B, M_GATE1, M_BO, M_LOSS = range(11)
X_BAO, X_LNAG, X_LNAB, X_CAB = range(4)
I_SHIFT1, I_SCALE1 = 0, 1

T_I, T_M, T_BIN, T_CAW, T_X, T_CBW, T_BUP, T_ROWS = 0, 8, 24, 32, 64, 72, 80, 88
T_LOSS = T_M + M_LOSS
_SMALL = (
    ("b_ada", (T_I + I_SHIFT1, T_I + I_SCALE1, T_M + M_GATE1, T_M + M_SHIFT2, T_M + M_SCALE2, T_M + M_GATE2)),
    ("b_in", tuple(T_BIN + j for j in range(7))),
    ("conv_a_b", (T_X + X_CAB,)), ("ln_a_g", (T_X + X_LNAG,)), ("ln_a_b", (T_X + X_LNAB,)), ("b_a_out", (T_X + X_BAO,)),
    ("b_o", (T_M + M_BO,)), ("ln1_g", (T_M + M_LN1G,)), ("ln1_b", (T_M + M_LN1B,)),
    ("b_up", tuple(T_BUP + j for j in range(4))),
    ("b_down", (T_M + M_BDN,)), ("ln2_g", (T_M + M_LN2G,)), ("ln2_b", (T_M + M_LN2B,)),
)


def _pcall(body, **kw):
    return pl.pallas_call(body, **kw)


def _cparams(**kw):
    return pltpu.CompilerParams(vmem_limit_bytes=VMEM_LIMIT, **kw)


def _seq(n):
    return _cparams(dimension_semantics=("arbitrary",) * n)


def _full(shape, single=False):
    nd = len(shape)
    if single:
        return pl.BlockSpec(shape, lambda *_: (0,) * nd, pipeline_mode=pl.Buffered(1))
    return pl.BlockSpec(shape, lambda *_: (0,) * nd)


def _rows(tm, width):
    return pl.BlockSpec((tm, width), lambda i: (i, 0))


def _sig(x):
    return jax.nn.sigmoid(x)


def _ln(x):
    mu = jnp.mean(x, axis=-1, keepdims=True)
    xc = x - mu
    var = jnp.mean(xc * xc, axis=-1, keepdims=True)
    rstd = lax.rsqrt(var + LN_EPS)
    return xc * rstd, rstd


def _ln_bwd(dxh, xh, rstd):
    m1 = jnp.mean(dxh, axis=-1, keepdims=True)
    m2 = jnp.mean(dxh * xh, axis=-1, keepdims=True)
    return rstd * (dxh - m1 - xh * m2)


def _rsum(v):
    return jnp.sum(v, axis=0, keepdims=True)


def _rsum_mxu(v):
    vb = v if v.dtype == BF16 else v.astype(BF16)
    return _dot(jnp.ones((2 * SUBLANES, v.shape[0]), BF16), vb)[0:1, :]


def _dot(a, b):
    return jnp.dot(a, b, preferred_element_type=F32)


def _dot_nt(a, b):
    return lax.dot_general(a, b, (((1,), (1,)), ((), ())), preferred_element_type=F32)


def _dot_tn(a, b):
    return lax.dot_general(a, b, (((0,), (0,)), ((), ())), preferred_element_type=F32)


def _my_pos():
    return lax.axis_index("x"), lax.axis_index("y"), lax.axis_index("c")


def _other_chips(x, y):
    return [(1 - x, y), (x, 1 - y), (1 - x, 1 - y)]


def _small_gather(v_ref, out_ref, send_sems, recv_sems, local_sem):
    x, y, cc = _my_pos()
    me = 4 * x + 2 * y + cc
    mine = pltpu.make_async_copy(v_ref, out_ref.at[me], local_sem)
    mine.start()
    sends = []
    for rel in range(1, N_DEV):
        rx, ry, rc = (rel >> 2) & 1, (rel >> 1) & 1, rel & 1
        peer = (1 - x if rx else x, 1 - y if ry else y, 1 - cc if rc else cc)
        cp = pltpu.make_async_remote_copy(
            src_ref=v_ref, dst_ref=out_ref.at[me], send_sem=send_sems.at[rel - 1], recv_sem=recv_sems.at[rel - 1],
            device_id=peer, device_id_type=MESH)
        cp.start()
        sends.append(cp)
    for rel in range(1, N_DEV):
        rx, ry, rc = (rel >> 2) & 1, (rel >> 1) & 1, rel & 1
        peer = (1 - x if rx else x, 1 - y if ry else y, 1 - cc if rc else cc)
        slot = 4 * peer[0] + 2 * peer[1] + peer[2]
        pltpu.make_async_remote_copy(
            src_ref=v_ref, dst_ref=out_ref.at[slot], send_sem=send_sems.at[rel - 1], recv_sem=recv_sems.at[rel - 1],
            device_id=peer, device_id_type=MESH).wait_recv()
    for cp in sends:
        cp.wait_send()
    mine.wait()


_SMALL_GATHER_SEMS = [pltpu.SemaphoreType.DMA((N_DEV - 1,)), pltpu.SemaphoreType.DMA((N_DEV - 1,)),
                      pltpu.SemaphoreType.DMA]


def _all_gather_small(v, name):
    r, c = v.shape

    def body(*refs):
        _small_gather(*refs)

    return _pcall(
        body, name=name,
        out_shape=jax.ShapeDtypeStruct((N_DEV, r, c), v.dtype),
        in_specs=[pl.BlockSpec(memory_space=pltpu.VMEM)],
        out_specs=pl.BlockSpec(memory_space=pltpu.VMEM),
        scratch_shapes=list(_SMALL_GATHER_SEMS),
        compiler_params=_cparams(),
    )(v)


def _prologue(pre, w_sh, b_sh, wa):
    r, d = pre.shape
    n_ada = w_sh.shape[1]
    ng = _GatherShards.n_sems

    def body(pre_ref, w_ref, b_ref, wai_ref, pre_all_ref, c_all_ref, mod_all_ref, wao_ref, mod_ref,
             s1, r1, l1, s2, r2, l2, sg, rg):
        gather = (wai_ref, wao_ref, sg, rg)
        _GatherShards.start(*gather)
        _small_gather(pre_ref, pre_all_ref, s1, r1, l1)
        cv = jnp.concatenate([pre_all_ref[dev, 0:1, :] for dev in range(N_DEV)], axis=0)
        c_all_ref[...] = cv
        ca = cv * _sig(cv)
        mod_ref[...] = jnp.dot(ca, w_ref[...], preferred_element_type=F32, precision=lax.Precision.HIGHEST) + b_ref[...]
        _small_gather(mod_ref, mod_all_ref, s2, r2, l2)
        _GatherShards.relay(*gather)
        _GatherShards.finish(*gather)

    vm = pl.BlockSpec(memory_space=pltpu.VMEM)
    any_spec = pl.BlockSpec(memory_space=pl.ANY)
    return _pcall(
        body, name="prologue",
        out_shape=[jax.ShapeDtypeStruct((N_DEV, r, d), F32), jax.ShapeDtypeStruct((N_DEV, d), F32),
                   jax.ShapeDtypeStruct((N_DEV, N_DEV, n_ada), F32), jax.ShapeDtypeStruct(wa.shape, wa.dtype)],
        in_specs=[vm, vm, vm, any_spec], out_specs=[vm, vm, vm, any_spec], input_output_aliases={3: 3},
        scratch_shapes=[pltpu.VMEM((N_DEV, n_ada), F32)] + list(_SMALL_GATHER_SEMS) + list(_SMALL_GATHER_SEMS)
        + [pltpu.SemaphoreType.DMA((ng,)), pltpu.SemaphoreType.DMA((ng,))],
        compiler_params=_cparams(),
    )(pre, w_sh, b_sh, wa)


def _place_shard(parts, q_idx, name):
    rows = sum(p.shape[0] for p in parts)
    w = parts[0].shape[1]

    def body(q_ref, *refs):
        o_ref = refs[-1]
        r0 = 0
        for p_ref in refs[:-1]:
            n = p_ref.shape[0]
            o_ref[r0:r0 + n, :] = p_ref[...].astype(BF16)
            r0 += n

    grid_spec = pltpu.PrefetchScalarGridSpec(
        num_scalar_prefetch=1, grid=(1,),
        in_specs=[pl.BlockSpec(p.shape, lambda i, q: (0, 0)) for p in parts],
        out_specs=pl.BlockSpec((None, rows, w), lambda i, q: (q[0], 0, 0)))
    return _pcall(body, name=name, grid_spec=grid_spec, out_shape=jax.ShapeDtypeStruct((N_CHIPS, rows, w), BF16),
                  compiler_params=_seq(1))(q_idx, *parts)


class _GatherShards:
    n_sems = 6

    @staticmethod
    def _half(ref, slot, h):
        rows = ref.shape[1] // 2
        return ref.at[slot, pl.ds(h * rows, rows)]

    @classmethod
    def _copies(cls, in_ref, out_ref, send_sems, recv_sems):
        x, y, c = _my_pos()
        q = 2 * x + y
        sibling = (x, y, 1 - c)
        sends, landed, forwards, passed = [], [], [], []
        for j, chip in enumerate(_other_chips(x, y)):
            qj = 2 * chip[0] + chip[1]

            def copy(src, dst, k, to):
                return pltpu.make_async_remote_copy(src_ref=src, dst_ref=dst, send_sem=send_sems.at[k],
                                                    recv_sem=recv_sems.at[k], device_id=to, device_id_type=MESH)

            mine, theirs = cls._half(out_ref, qj, c), cls._half(out_ref, qj, 1 - c)
            sends.append(copy(cls._half(in_ref, q, c), cls._half(out_ref, q, c), j, (*chip, c)))
            landed.append(copy(mine, mine, j, (*chip, c)))
            forwards.append(copy(mine, mine, 3 + j, sibling))
            passed.append(copy(theirs, theirs, 3 + j, sibling))
        return sends, landed, forwards, passed

    @classmethod
    def start(cls, *refs):
        for cp in cls._copies(*refs)[0]:
            cp.start()

    @classmethod
    def relay(cls, *refs):
        _, landed, forwards, _ = cls._copies(*refs)
        for arrived, onward in zip(landed, forwards):
            arrived.wait_recv()
            onward.start()

    @classmethod
    def finish(cls, *refs):
        sends, _, forwards, passed = cls._copies(*refs)
        for cp in passed:
            cp.wait_recv()
        for cp in sends + forwards:
            cp.wait_send()


class _SiblingHalf:
    n_sems = 1

    @staticmethod
    def out_shape(g):
        return jax.ShapeDtypeStruct((g.shape[0], g.shape[1] // 2, g.shape[2]), g.dtype)

    @staticmethod
    def copies(g_ref, r_ref, send_sems, recv_sems, base):
        x, y, c = _my_pos()
        rows = r_ref.shape[1]
        return [pltpu.make_async_remote_copy(
            src_ref=g_ref.at[:, pl.ds((1 - c) * rows, rows)], dst_ref=r_ref,
            send_sem=send_sems.at[base], recv_sem=recv_sems.at[base], device_id=(x, y, 1 - c), device_id_type=MESH)]


class _ChipBlocks:
    n_sems = 3

    @staticmethod
    def out_shape(p):
        return jax.ShapeDtypeStruct(p.shape, p.dtype)

    @staticmethod
    def copies(p_ref, r_ref, send_sems, recv_sems, base):
        x, y, c = _my_pos()
        return [pltpu.make_async_remote_copy(
            src_ref=p_ref.at[j], dst_ref=r_ref.at[j], send_sem=send_sems.at[base + j], recv_sem=recv_sems.at[base + j],
            device_id=(*chip, c), device_id_type=MESH) for j, chip in enumerate(_other_chips(x, y))]


class _TableToAll:
    n_sems = N_DEV

    @staticmethod
    def out_shape(t):
        return jax.ShapeDtypeStruct((N_DEV,) + t.shape, t.dtype)

    @staticmethod
    def copies(t_ref, all_ref, send_sems, recv_sems, base):
        x, y, c = _my_pos()
        me = 4 * x + 2 * y + c
        cps = [pltpu.make_async_copy(t_ref, all_ref.at[me], send_sems.at[base + N_DEV - 1])]
        for rel in range(1, N_DEV):
            rx, ry, rc = (rel >> 2) & 1, (rel >> 1) & 1, rel & 1
            peer = (1 - x if rx else x, 1 - y if ry else y, 1 - c if rc else c)
            cps.append(_SlotCopy(t_ref, all_ref, me, 4 * peer[0] + 2 * peer[1] + peer[2], peer,
                                 send_sems.at[base + rel - 1], recv_sems.at[base + rel - 1]))
        return cps


class _SlotCopy:
    def __init__(self, src_ref, all_ref, my_slot, peer_slot, peer, send_sem, recv_sem):
        self._send = pltpu.make_async_remote_copy(src_ref=src_ref, dst_ref=all_ref.at[my_slot], send_sem=send_sem,
                                                  recv_sem=recv_sem, device_id=peer, device_id_type=MESH)
        self._recv = pltpu.make_async_remote_copy(src_ref=src_ref, dst_ref=all_ref.at[peer_slot], send_sem=send_sem,
                                                  recv_sem=recv_sem, device_id=peer, device_id_type=MESH)

    def start(self):
        self._send.start()

    def wait(self):
        self._send.wait_send()
        self._recv.wait_recv()


def _plan_copies(plans, src_refs, dst_refs, send_sems, recv_sems):
    cps, base = [], 0
    for plan, s_ref, d_ref in zip(plans, src_refs, dst_refs):
        cps += plan.copies(s_ref, d_ref, send_sems, recv_sems, base)
        base += plan.n_sems
    return cps


def _plan_sems(plans):
    n = sum(p.n_sems for p in plans)
    return [pltpu.SemaphoreType.DMA((n,)), pltpu.SemaphoreType.DMA((n,))]


def _exchange(plan, src, name):
    def body(s_ref, o_ref, send_sems, recv_sems):
        cps = _plan_copies([plan], [s_ref], [o_ref], send_sems, recv_sems)
        for cp in cps:
            cp.start()
        for cp in cps:
            cp.wait()

    any_spec = pl.BlockSpec(memory_space=pl.ANY)
    return _pcall(
        body, name=name, out_shape=plan.out_shape(src), in_specs=[any_spec], out_specs=any_spec,
        scratch_shapes=_plan_sems([plan]), compiler_params=_cparams(),
    )(src)


def _carried_start(plans, first, comm):
    @pl.when(first)
    def _():
        for cp in _plan_copies(plans, *comm):
            cp.start()


def _carried_wait(plans, last, comm):
    @pl.when(last)
    def _():
        for cp in _plan_copies(plans, *comm):
            cp.wait()


def _rs_join_halves(fa, fb):
    bufs = (fa, fb)
    nb = len(bufs)

    def body(a_ref, b_ref, ao_ref, bo_ref, send_sems, recv_sems):
        x, y, c = _my_pos()
        srcs, outs = (a_ref, b_ref), (ao_ref, bo_ref)
        cps = []
        for b in range(nb):
            rows = srcs[b].shape[0] // 2
            cp = pltpu.make_async_remote_copy(
                src_ref=srcs[b].at[pl.ds(c * rows, rows)], dst_ref=outs[b].at[pl.ds(c * rows, rows)],
                send_sem=send_sems.at[b], recv_sem=recv_sems.at[b], device_id=(x, y, 1 - c), device_id_type=MESH)
            cp.start()
            cps.append(cp)
        for b in range(nb):
            rows = srcs[b].shape[0] // 2
            theirs = outs[b].at[pl.ds((1 - c) * rows, rows)]
            pltpu.make_async_remote_copy(
                src_ref=theirs, dst_ref=theirs, send_sem=send_sems.at[b], recv_sem=recv_sems.at[b],
                device_id=(x, y, 1 - c), device_id_type=MESH).wait_recv()
        for cp in cps:
            cp.wait_send()

    any_spec = pl.BlockSpec(memory_space=pl.ANY)
    return _pcall(
        body, name="rs_join_halves",
        out_shape=[jax.ShapeDtypeStruct(b.shape, b.dtype) for b in bufs],
        in_specs=[any_spec] * nb, out_specs=[any_spec] * nb, input_output_aliases={0: 0, 1: 1},
        scratch_shapes=[pltpu.SemaphoreType.DMA((nb,)), pltpu.SemaphoreType.DMA((nb,))],
        compiler_params=_cparams(),
    )(*bufs)


def _add_halves(g, r, idx, tr, name):
    _, rows, w = r.shape
    nt = rows // tr

    def body(i_ref, g_ref, r_ref, o_ref):
        o_ref[...] = (g_ref[...] + r_ref[...]).astype(BF16)

    grid_spec = pltpu.PrefetchScalarGridSpec(
        num_scalar_prefetch=1, grid=(3, nt),
        in_specs=[pl.BlockSpec((None, tr, w), lambda j, i, ix: (ix[1 + j], ix[0] * nt + i, 0)),
                  pl.BlockSpec((None, tr, w), lambda j, i, ix: (ix[1 + j], i, 0))],
        out_specs=pl.BlockSpec((None, tr, w), lambda j, i, ix: (j, i, 0)))
    return _pcall(body, name=name, grid_spec=grid_spec,
                  out_shape=jax.ShapeDtypeStruct((3, rows, w), BF16), compiler_params=_seq(2))(idx, g, r)


def _add_chips(g, r, r3, idx, tr, name):
    _, rows, w = r.shape
    nt = rows // tr

    def body(i_ref, g_ref, r_ref, a_ref, b_ref, c_ref, o_ref):
        own = g_ref[...] + r_ref[...]
        o_ref[...] = ((own + a_ref[...].astype(F32)) + b_ref[...].astype(F32)) + c_ref[...].astype(F32)

    def other(j):
        return pl.BlockSpec((None, tr, w), lambda i, ix: (j, i, 0))

    grid_spec = pltpu.PrefetchScalarGridSpec(
        num_scalar_prefetch=1, grid=(nt,),
        in_specs=[pl.BlockSpec((None, tr, w), lambda i, ix: (ix[0], ix[1] * nt + i, 0)),
                  pl.BlockSpec((None, tr, w), lambda i, ix: (ix[0], i, 0)), other(0), other(1), other(2)],
        out_specs=pl.BlockSpec((tr, w), lambda i, ix: (ix[1] * nt + i, 0)))
    return _pcall(body, name=name, grid_spec=grid_spec,
                  out_shape=jax.ShapeDtypeStruct((2 * rows, w), F32), compiler_params=_seq(1))(idx, g, r, r3, r3, r3)


def _ada_bwd(c_all_t, dmod_sh):
    def body(c_ref, d_ref, o_ref):
        cv = c_ref[...]
        ca = cv * _sig(cv)
        o_ref[...] = jnp.dot(ca, d_ref[...], preferred_element_type=F32, precision=lax.Precision.HIGHEST)

    return _pcall(body, name="ada_bwd", out_shape=jax.ShapeDtypeStruct((c_all_t.shape[0], dmod_sh.shape[1]), F32),
                  compiler_params=_cparams())(c_all_t, dmod_sh)


def _residue_passes(residues, copies):
    passes, current, used = [], [], 0
    for s in residues:
        if s != 0 and used == copies:
            passes.append(current)
            current, used = [], 0
        current.append(s)
        used += s != 0
    return passes + [current] if current else passes


def _conv_causal(ext_ref, sh_ref, w_ref, ntaps, halo, bias, out_ref, tm, d, rc):
    off = halo - (ntaps - 1)
    n = tm + halo - SUBLANES
    taps = {s: [(k, (off + k) // SUBLANES * SUBLANES) for k in range(ntaps) if (off + k) % SUBLANES == s]
            for s in range(SUBLANES)}
    started = False
    for group in _residue_passes([s for s in range(SUBLANES) if taps[s]], sh_ref.shape[0]):
        srcs = {}
        for s in group:
            if s == 0:
                srcs[s] = ext_ref
            else:
                j = len(srcs) - (0 in srcs)
                sh_ref[j, 0:n, :] = ext_ref[s:s + n, :]
                srcs[s] = sh_ref.at[j]
        for r0 in range(0, tm, rc):
            if started:
                acc = out_ref[r0:r0 + rc, :]
            else:
                acc = jnp.zeros((rc, d), F32) if bias is None else jnp.broadcast_to(bias, (rc, d))
            for s, src in srcs.items():
                for k, a in taps[s]:
                    acc = acc + w_ref[k:k + 1, :] * src[r0 + a:r0 + a + rc, :]
            out_ref[r0:r0 + rc, :] = acc
        started = True


def _conv_adjoint(dp_ref, sh_ref, ext_ref, w_ref, dx_ref, dw_ref, ntaps, halo, tm, d, rc):
    off = halo - (ntaps - 1)
    lead = SUBLANES + ntaps - 1
    n = tm + halo
    row = lax.broadcasted_iota(jnp.int32, (SUBLANES, d), 0)
    taps = {s: [(k, (lead - k) // SUBLANES * SUBLANES) for k in range(ntaps) if (lead - k) % SUBLANES == s]
            for s in range(SUBLANES)}
    wtaps = {s: [(k, s + off + k - SUBLANES) for k in range(ntaps) if (-(off + k)) % SUBLANES == s]
             for s in range(SUBLANES)}
    rw = 2 * SUBLANES
    started = False
    for group in _residue_passes([s for s in range(SUBLANES) if taps[s] or wtaps[s]], sh_ref.shape[0]):
        srcs = {}
        for s in group:
            if s == 0:
                srcs[s] = dp_ref
            else:
                j = len(srcs) - (0 in srcs)
                sh_ref[j, 0:n, :] = dp_ref[s:s + n, :]
                srcs[s] = sh_ref.at[j]
        if any(taps[s] for s in srcs):
            for r0 in range(0, tm, rc):
                acc = dx_ref[r0:r0 + rc, :] if started else jnp.zeros((rc, d), F32)
                for s, src in srcs.items():
                    for k, a in taps[s]:
                        acc = acc + w_ref[k:k + 1, :] * src[r0 + a:r0 + a + rc, :]
                dx_ref[r0:r0 + rc, :] = acc
            started = True
        for s, src in srcs.items():
            if not wtaps[s]:
                continue
            sums = [jnp.zeros((SUBLANES, d), F32) for _ in wtaps[s]]
            for r0 in range(0, tm, rw):
                g = src[r0:r0 + rw, :]
                for j, (_, e) in enumerate(wtaps[s]):
                    p = g * ext_ref[r0 + e:r0 + e + rw, :]
                    for r8 in range(0, rw, SUBLANES):
                        sums[j] = sums[j] + p[r8:r8 + SUBLANES, :]
            for j, (k, e) in enumerate(wtaps[s]):
                tail = src[tm:tm + SUBLANES, :] * ext_ref[tm + e:tm + e + SUBLANES, :]
                dw_ref[k:k + 1, :] += _rsum(sums[j] + jnp.where(row < SUBLANES - s, tail, 0.0))


def _fwd_in(x, vecs, wa, b_in, wb, tm):
    s, d = x.shape
    nq, _, nw = wa.shape
    nt = s // tm

    def body(x_ref, v_ref, w_ref, b_ref, wbi_ref, h_ref, z_ref, wbo_ref, send_sems, recv_sems):
        i = pl.program_id(0)
        gather = (wbi_ref, wbo_ref, send_sems, recv_sems)
        pl.when(i == 0)(lambda: _GatherShards.start(*gather))
        pl.when(i == nt // 2)(lambda: _GatherShards.relay(*gather))
        xh, _ = _ln(x_ref[...])
        h = (xh * (1.0 + v_ref[V_SCALE1:V_SCALE1 + 1, :]) + v_ref[V_SHIFT1:V_SHIFT1 + 1, :]).astype(BF16)
        h_ref[...] = h
        for q in range(nq):
            z_ref[:, q * nw:(q + 1) * nw] = _dot(h, w_ref[q]) + b_ref[:, q * nw:(q + 1) * nw]
        pl.when(i == nt - 1)(lambda: _GatherShards.finish(*gather))

    any_spec = pl.BlockSpec(memory_space=pl.ANY)
    n = _GatherShards.n_sems
    return _pcall(
        body, name="fwd_in", grid=(nt,),
        in_specs=[_rows(tm, d), _full(vecs.shape), _full(wa.shape, single=True), _full(b_in.shape), any_spec],
        out_specs=[_rows(tm, d), _rows(tm, nq * nw), any_spec],
        out_shape=[jax.ShapeDtypeStruct((s, d), BF16), jax.ShapeDtypeStruct((s, nq * nw), F32),
                   jax.ShapeDtypeStruct(wb.shape, wb.dtype)],
        input_output_aliases={4: 2},
        scratch_shapes=[pltpu.SemaphoreType.DMA((n,)), pltpu.SemaphoreType.DMA((n,))],
        compiler_params=_seq(1),
    )(x, vecs, wa, b_in, wb)


def _fwd_mix(z, vecs, caw, cbw, wb, tm, rc):
    s = z.shape[0]
    d = vecs.shape[1]
    nq = wb.shape[0]
    kq = d // nq
    base = 2 * d // kq

    def body(z_ref, v_ref, caw_ref, cbw_ref, wao_ref, wbo_ref, wo_ref,
             u1_ref, ya_ref, yb_ref, o1_ref, u3_ref, vv_ref, mg_ref, ext_ref, sh_ref, pext_ref, q_ref):
        @pl.when(pl.program_id(0) == 0)
        def _():
            ext_ref[0:HALO_A, :] = jnp.zeros((HALO_A, d), F32)
            pext_ref[0:HALO_B, :] = jnp.zeros((HALO_B, d), F32)

        ext_ref[HALO_A:HALO_A + tm, :] = z_ref[:, 0:d] * _sig(z_ref[:, d:2 * d])
        _conv_causal(ext_ref, sh_ref, caw_ref, CONV_A, HALO_A, v_ref[V_CAB:V_CAB + 1, :], u1_ref, tm, d, rc)
        ext_ref[0:HALO_A, :] = ext_ref[tm:tm + HALO_A, :]
        xa, _ = _ln(u1_ref[...])
        u2 = xa * v_ref[V_LNAG:V_LNAG + 1, :] + v_ref[V_LNAB:V_LNAB + 1, :]
        u3 = (u2 * _sig(u2)).astype(BF16)
        u3_ref[...] = u3
        ya = jnp.broadcast_to(v_ref[V_BAO:V_BAO + 1, :], (tm, d))
        for q in range(nq):
            ya = ya + _dot(u3[:, q * kq:(q + 1) * kq], wao_ref[q])
        ya_ref[...] = ya

        pext_ref[HALO_B:HALO_B + tm, :] = z_ref[:, 3 * d:4 * d] * z_ref[:, 4 * d:5 * d]
        _conv_causal(pext_ref, sh_ref, cbw_ref, CONV_B, HALO_B, None, q_ref, tm, d, rc)
        pext_ref[0:HALO_B, :] = pext_ref[tm:tm + HALO_B, :]
        vv = (z_ref[:, 2 * d:3 * d] * q_ref[...]).astype(BF16)
        vv_ref[...] = vv
        yb = jnp.zeros((tm, d), F32)
        for q in range(nq):
            yb = yb + _dot(vv[:, q * kq:(q + 1) * kq], wbo_ref[q])
        yb_ref[...] = yb

        mg = (_sig(z_ref[:, 5 * d:6 * d]) * ya + _sig(z_ref[:, 6 * d:7 * d]) * yb).astype(BF16)
        mg_ref[...] = mg
        o1 = jnp.broadcast_to(v_ref[V_BO:V_BO + 1, :], (tm, d))
        for q in range(nq):
            o1 = o1 + _dot(mg[:, q * kq:(q + 1) * kq], wo_ref[q])
        o1_ref[...] = o1

    def wspec(j):
        return pl.BlockSpec((nq, kq, d), lambda i: (0, base + j, 0), pipeline_mode=pl.Buffered(1))

    f32o = jax.ShapeDtypeStruct((s, d), F32)
    b16o = jax.ShapeDtypeStruct((s, d), BF16)
    return _pcall(
        body, name="fwd_mix", grid=(s // tm,),
        in_specs=[_rows(tm, 7 * d), _full(vecs.shape), _full(caw.shape), _full(cbw.shape), wspec(0), wspec(1), wspec(2)],
        out_specs=[_rows(tm, d)] * 7,
        out_shape=[f32o, f32o, f32o, f32o, b16o, b16o, b16o],
        scratch_shapes=[pltpu.VMEM((HALO_A + tm, d), F32), pltpu.VMEM((SHIFTED_COPIES, HALO_A + tm, d), F32),
                        pltpu.VMEM((HALO_B + tm, d), F32), pltpu.VMEM((tm, d), F32)],
        compiler_params=_seq(1),
    )(z, vecs, caw, cbw, wb, wb, wb)


def _mlp_fwd_bwd(x, out1, tgt, vecs, b_up, wb, tm, groups):
    s, d = x.shape
    nq = wb.shape[0]
    dff = nq * d
    hm = tm // groups

    def body(x_ref, o1_ref, t_ref, v_ref, bup_ref, wup_ref, wdn_ref,
             h2_ref, f_ref, df0_ref, do2_ref, do1_ref, dxp_ref, acc_ref, dbup_ref, f0_ref):
        @pl.when(pl.program_id(0) == 0)
        def _():
            acc_ref[...] = jnp.zeros(acc_ref.shape, F32)
            dbup_ref[...] = jnp.zeros(dbup_ref.shape, F32)

        def vec(r):
            return v_ref[r:r + 1, :]

        def accum(r, val):
            acc_ref[r:r + 1, :] += _rsum(val)

        for g in range(groups):
            rs = slice(g * hm, (g + 1) * hm)
            out1v = o1_ref[rs, :]
            r1 = ALPHA * x_ref[rs, :] + (1.0 + vec(V_GATE1)) * out1v
            xh1, rstd1 = _ln(r1)
            x1 = xh1 * vec(V_LN1G) + vec(V_LN1B)
            xn1, rstdn = _ln(x1)
            h2 = (xn1 * (1.0 + vec(V_SCALE2)) + vec(V_SHIFT2)).astype(BF16)
            h2_ref[rs, :] = h2
            out2 = jnp.broadcast_to(vec(V_BDN), (hm, d))
            for q in range(nq):
                f0 = _dot(h2, wup_ref[q]) + bup_ref[:, q * d:(q + 1) * d]
                rl = jnp.maximum(f0, 0.0)
                f0_ref[rs, q * d:(q + 1) * d] = rl
                fb = (rl * rl).astype(BF16)
                f_ref[rs, q * d:(q + 1) * d] = fb
                out2 = out2 + _dot(fb, wdn_ref[q])
            r2 = ALPHA * x1 + (1.0 + vec(V_GATE2)) * out2
            xh2, rstd2 = _ln(r2)
            yv = xh2 * vec(V_LN2G) + vec(V_LN2B)
            err = yv - t_ref[rs, :]
            accum(M_LOSS, err * err)
            dy = err * (1.0 / d)
            accum(M_LN2G, dy * xh2)
            accum(M_LN2B, dy)
            dr2 = _ln_bwd(dy * vec(V_LN2G), xh2, rstd2)
            accum(M_GATE2, dr2 * out2)
            dout2 = (1.0 + vec(V_GATE2)) * dr2
            accum(M_BDN, dout2)
            do2b = dout2.astype(BF16)
            do2_ref[rs, :] = do2b
            dh2 = jnp.zeros((hm, d), F32)
            for q in range(nq):
                df0 = _dot_nt(do2b, wdn_ref[q]) * (2.0 * f0_ref[rs, q * d:(q + 1) * d])
                dbup_ref[q:q + 1, :] += _rsum(df0)
                df0b = df0.astype(BF16)
                df0_ref[rs, q * d:(q + 1) * d] = df0b
                dh2 = dh2 + _dot_nt(df0b, wup_ref[q])
            accum(M_SHIFT2, dh2)
            accum(M_SCALE2, dh2 * xn1)
            dx1 = ALPHA * dr2 + _ln_bwd(dh2 * (1.0 + vec(V_SCALE2)), xn1, rstdn)
            accum(M_LN1G, dx1 * xh1)
            accum(M_LN1B, dx1)
            dr1 = _ln_bwd(dx1 * vec(V_LN1G), xh1, rstd1)
            accum(M_GATE1, dr1 * out1v)
            dout1 = (1.0 + vec(V_GATE1)) * dr1
            accum(M_BO, dout1)
            do1_ref[rs, :] = dout1.astype(BF16)
            dxp_ref[rs, :] = ALPHA * dr1

    def wspec(j):
        return pl.BlockSpec((nq, d, d), lambda i: (0, j, 0), pipeline_mode=pl.Buffered(1))

    b16 = lambda w: jax.ShapeDtypeStruct((s, w), BF16)
    return _pcall(
        body, name="mlp_fwd_bwd", grid=(s // tm,),
        in_specs=[_rows(tm, d), _rows(tm, d), _rows(tm, d), _full(vecs.shape), _full(b_up.shape), wspec(0), wspec(1)],
        out_specs=[_rows(tm, d), _rows(tm, dff), _rows(tm, dff), _rows(tm, d), _rows(tm, d), _rows(tm, d),
                   _full((16, d)), _full((SUBLANES, d))],
        out_shape=[b16(d), b16(dff), b16(dff), b16(d), b16(d), jax.ShapeDtypeStruct((s, d), F32),
                   jax.ShapeDtypeStruct((16, d), F32), jax.ShapeDtypeStruct((SUBLANES, d), F32)],
        scratch_shapes=[pltpu.VMEM((tm, dff), F32)],
        compiler_params=_seq(1),
    )(x, out1, tgt, vecs, b_up, wb, wb)


def _mix_bwd(dout1, z, u1, ya, yb, vecs, caw, cbw, wb, tm, rc):
    s = z.shape[0]
    d = vecs.shape[1]
    nq = wb.shape[0]
    kq = d // nq
    base = 2 * d // kq
    nt = s // tm
    hb = tm // HALO_A

    def body(do1_ref, z_ref, zh_ref, u1_ref, ya_ref, yb_ref, v_ref, caw_ref, cbw_ref, wao_ref, wbo_ref, wo_ref,
             dz_ref, dya_ref, dyb_ref, acc_ref, dcaw_ref, dcbw_ref, dbin_ref,
             ext_ref, du1p_ref, sh_ref, pext_ref, dqp_ref, tmp_ref):
        i = pl.program_id(0)

        @pl.when(i == 0)
        def _():
            acc_ref[...] = jnp.zeros(acc_ref.shape, F32)
            dcaw_ref[...] = jnp.zeros(dcaw_ref.shape, F32)
            dcbw_ref[...] = jnp.zeros(dcbw_ref.shape, F32)
            dbin_ref[...] = jnp.zeros(dbin_ref.shape, F32)
            du1p_ref[0:SUBLANES, :] = jnp.zeros((SUBLANES, d), F32)
            du1p_ref[SUBLANES + tm:SUBLANES + tm + HALO_A, :] = jnp.zeros((HALO_A, d), F32)
            dqp_ref[0:SUBLANES, :] = jnp.zeros((SUBLANES, d), F32)
            dqp_ref[SUBLANES + tm:SUBLANES + tm + HALO_B, :] = jnp.zeros((HALO_B, d), F32)

        def vec(r):
            return v_ref[r:r + 1, :]

        def accum(r, val):
            acc_ref[r:r + 1, :] += _rsum(val)

        def put_dz(j, val):
            vb = val.astype(BF16)
            dz_ref[:, j * d:(j + 1) * d] = vb
            dbin_ref[j:j + 1, :] += _rsum_mxu(vb)

        has_history = i < nt - 1

        do1 = do1_ref[...]
        dmg = jnp.concatenate([_dot_nt(do1, wo_ref[q]) for q in range(nq)], axis=1)
        sga = _sig(z_ref[:, 5 * d:6 * d])
        sgb = _sig(z_ref[:, 6 * d:7 * d])
        dya = dmg * sga
        dyb = dmg * sgb
        accum(X_BAO, dya)
        put_dz(5, dya * ya_ref[...] * (1.0 - sga))
        put_dz(6, dyb * yb_ref[...] * (1.0 - sgb))
        dyab = dya.astype(BF16)
        dybb = dyb.astype(BF16)
        dya_ref[...] = dyab
        dyb_ref[...] = dybb

        du3 = jnp.concatenate([_dot_nt(dyab, wao_ref[q]) for q in range(nq)], axis=1)
        xa, rstda = _ln(u1_ref[...])
        u2 = xa * vec(V_LNAG) + vec(V_LNAB)
        s2 = _sig(u2)
        du2 = du3 * (s2 * (1.0 + u2 * (1.0 - s2)))
        accum(X_LNAG, du2 * xa)
        accum(X_LNAB, du2)
        du1 = _ln_bwd(du2 * vec(V_LNAG), xa, rstda)
        accum(X_CAB, du1)
        du1p_ref[SUBLANES:SUBLANES + tm, :] = du1
        sg = _sig(z_ref[:, d:2 * d])
        aval = z_ref[:, 0:d]
        ext_ref[HALO_A:HALO_A + tm, :] = aval * sg
        ext_ref[0:HALO_A, :] = jnp.where(has_history, zh_ref[:, 0:d] * _sig(zh_ref[:, d:2 * d]), 0.0)
        _conv_adjoint(du1p_ref, sh_ref, ext_ref, caw_ref, tmp_ref, dcaw_ref, CONV_A, HALO_A, tm, d, rc)
        du1p_ref[SUBLANES + tm:SUBLANES + tm + HALO_A, :] = du1p_ref[SUBLANES:SUBLANES + HALO_A, :]
        du0 = tmp_ref[...]
        put_dz(0, du0 * sg)
        put_dz(1, du0 * aval * sg * (1.0 - sg))

        dv = jnp.concatenate([_dot_nt(dybb, wbo_ref[q]) for q in range(nq)], axis=1)
        bgc = z_ref[:, 3 * d:4 * d]
        bx = z_ref[:, 4 * d:5 * d]
        pext_ref[HALO_B:HALO_B + tm, :] = bgc * bx
        pext_ref[0:HALO_B, :] = jnp.where(
            has_history, zh_ref[HALO_A - HALO_B:HALO_A, 3 * d:4 * d] * zh_ref[HALO_A - HALO_B:HALO_A, 4 * d:5 * d], 0.0)
        _conv_causal(pext_ref, sh_ref, cbw_ref, CONV_B, HALO_B, None, tmp_ref, tm, d, rc)
        put_dz(2, dv * tmp_ref[...])
        dqp_ref[SUBLANES:SUBLANES + tm, :] = dv * z_ref[:, 2 * d:3 * d]
        _conv_adjoint(dqp_ref, sh_ref, pext_ref, cbw_ref, tmp_ref, dcbw_ref, CONV_B, HALO_B, tm, d, rc)
        dqp_ref[SUBLANES + tm:SUBLANES + tm + HALO_B, :] = dqp_ref[SUBLANES:SUBLANES + HALO_B, :]
        dp = tmp_ref[...]
        put_dz(3, dp * bx)
        put_dz(4, dp * bgc)

    def rev(width):
        return pl.BlockSpec((tm, width), lambda i: (nt - 1 - i, 0))

    def wspec(j):
        return pl.BlockSpec((nq, kq, d), lambda i: (0, base + j, 0), pipeline_mode=pl.Buffered(1))

    halo = pl.BlockSpec((HALO_A, 7 * d), lambda i: (jnp.maximum((nt - 1 - i) * hb - 1, 0), 0))
    b16 = jax.ShapeDtypeStruct((s, d), BF16)
    acc8 = jax.ShapeDtypeStruct((SUBLANES, d), F32)
    return _pcall(
        body, name="mix_bwd", grid=(nt,),
        in_specs=[rev(d), rev(7 * d), halo, rev(d), rev(d), rev(d), _full(vecs.shape), _full(caw.shape), _full(cbw.shape),
                  wspec(0), wspec(1), wspec(2)],
        out_specs=[rev(7 * d), rev(d), rev(d), _full((SUBLANES, d)), _full((HALO_A, d)), _full((HALO_B, d)),
                   _full((SUBLANES, d))],
        out_shape=[jax.ShapeDtypeStruct((s, 7 * d), BF16), b16, b16, acc8,
                   jax.ShapeDtypeStruct((HALO_A, d), F32), jax.ShapeDtypeStruct((HALO_B, d), F32), acc8],
        scratch_shapes=[pltpu.VMEM((HALO_A + tm, d), F32), pltpu.VMEM((SUBLANES + tm + HALO_A, d), F32),
                        pltpu.VMEM((SHIFTED_COPIES, tm + HALO_A, d), F32), pltpu.VMEM((HALO_B + tm, d), F32),
                        pltpu.VMEM((SUBLANES + tm + HALO_B, d), F32), pltpu.VMEM((tm, d), F32)],
        compiler_params=_seq(1),
    )(dout1, z, z, u1, ya, yb, vecs, caw, cbw, wb, wb, wb)


def _in_bwd(dz, x, dxp, vecs, wa, tm, plans, plan_srcs):
    s, d = x.shape
    nq, _, nw = wa.shape
    nt = s // tm
    nc = len(plan_srcs)

    def body(dz_ref, x_ref, dxp_ref, v_ref, w_ref, *rest):
        src_refs, (gx_ref, acc_ref), dst_refs = rest[:nc], rest[nc:nc + 2], rest[nc + 2:2 * nc + 2]
        send_sems, recv_sems = rest[2 * nc + 2:]
        i = pl.program_id(0)
        comm = (src_refs, dst_refs, send_sems, recv_sems)
        _carried_start(plans, i == 0, comm)

        @pl.when(i == 0)
        def _():
            acc_ref[...] = jnp.zeros(acc_ref.shape, F32)

        dh1 = jnp.zeros((tm, d), F32)
        for q in range(nq):
            dh1 = dh1 + _dot_nt(dz_ref[:, q * nw:(q + 1) * nw], w_ref[q])
        xh, rstd = _ln(x_ref[...])
        acc_ref[I_SHIFT1:I_SHIFT1 + 1, :] += _rsum(dh1)
        acc_ref[I_SCALE1:I_SCALE1 + 1, :] += _rsum(dh1 * xh)
        gx_ref[...] = dxp_ref[...] + _ln_bwd(dh1 * (1.0 + v_ref[V_SCALE1:V_SCALE1 + 1, :]), xh, rstd)
        _carried_wait(plans, i == nt - 1, comm)

    any_spec = pl.BlockSpec(memory_space=pl.ANY)
    return _pcall(
        body, name="in_bwd", grid=(nt,),
        in_specs=[_rows(tm, nq * nw), _rows(tm, d), _rows(tm, d), _full(vecs.shape), _full(wa.shape, single=True)]
        + [any_spec] * nc,
        out_specs=[_rows(tm, d), _full((SUBLANES, d))] + [any_spec] * nc,
        out_shape=[jax.ShapeDtypeStruct((s, d), F32), jax.ShapeDtypeStruct((SUBLANES, d), F32)]
        + [plan.out_shape(p) for plan, p in zip(plans, plan_srcs)],
        scratch_shapes=_plan_sems(plans),
        compiler_params=_seq(1),
    )(dz, x, dxp, vecs, wa, *plan_srcs)


def _dw(a, b, split_a, ts, name, into=None, rows_total=None, row_block=0):
    s = a.shape[0]
    ka = a.shape[1] // N_CHIPS if split_a else a.shape[1]
    nb = b.shape[1] if split_a else b.shape[1] // N_CHIPS
    rows_total = ka if rows_total is None else rows_total

    def body(a_ref, b_ref, *rest):
        o_ref = rest[-1]

        @pl.when(pl.program_id(1) == 0)
        def _():
            o_ref[...] = jnp.zeros(o_ref.shape, F32)

        o_ref[...] += _dot_tn(a_ref[...], b_ref[...])

    a_spec = pl.BlockSpec((ts, ka), (lambda q, i: (i, q)) if split_a else (lambda q, i: (i, 0)))
    b_spec = pl.BlockSpec((ts, nb), (lambda q, i: (i, 0)) if split_a else (lambda q, i: (i, q)))
    extra = {} if into is None else dict(input_output_aliases={2: 0})
    return _pcall(
        body, name=name, grid=(N_CHIPS, s // ts),
        in_specs=[a_spec, b_spec] + ([] if into is None else [pl.BlockSpec(memory_space=pl.ANY)]),
        out_specs=pl.BlockSpec((None, ka, nb), lambda q, i: (q, row_block, 0)),
        out_shape=jax.ShapeDtypeStruct((N_CHIPS, rows_total, nb), F32),
        compiler_params=_seq(2), **extra,
    )(*((a, b) if into is None else (a, b, into)))


def _dw_carrying(a, b, ts, name, plan, plan_src):
    s, k = a.shape
    nb = b.shape[1] // N_CHIPS
    ns = s // ts

    def body(a_ref, b_ref, src_ref, o_ref, dst_ref, send_sems, recv_sems):
        q, i = pl.program_id(0), pl.program_id(1)
        comm = ([src_ref], [dst_ref], send_sems, recv_sems)
        _carried_start([plan], jnp.logical_and(q == 0, i == 0), comm)

        @pl.when(i == 0)
        def _():
            o_ref[...] = jnp.zeros(o_ref.shape, F32)

        o_ref[...] += _dot_tn(a_ref[...], b_ref[...])
        _carried_wait([plan], jnp.logical_and(q == N_CHIPS - 1, i == ns - 1), comm)

    any_spec = pl.BlockSpec(memory_space=pl.ANY)
    return _pcall(
        body, name=name, grid=(N_CHIPS, ns),
        in_specs=[pl.BlockSpec((ts, k), lambda q, i: (i, 0)), pl.BlockSpec((ts, nb), lambda q, i: (i, q)), any_spec],
        out_specs=[pl.BlockSpec((None, k, nb), lambda q, i: (q, 0, 0)), any_spec],
        out_shape=[jax.ShapeDtypeStruct((N_CHIPS, k, nb), F32), plan.out_shape(plan_src)],
        scratch_shapes=_plan_sems([plan]),
        compiler_params=_seq(2),
    )(a, b, plan_src)


def _dw_rows(a, b, ts, name, into, row_block):
    s, k = a.shape
    n = b.shape[1]
    kq = k // N_CHIPS

    def body(a_ref, b_ref, buf_ref, o_ref):
        @pl.when(pl.program_id(0) == 0)
        def _():
            o_ref[...] = jnp.zeros(o_ref.shape, F32)

        res = _dot_tn(a_ref[...], b_ref[...])
        for q in range(N_CHIPS):
            o_ref[q] += res[q * kq:(q + 1) * kq, :]

    return _pcall(
        body, name=name, grid=(s // ts,),
        in_specs=[_rows(ts, k), _rows(ts, n), pl.BlockSpec(memory_space=pl.ANY)],
        out_specs=pl.BlockSpec((N_CHIPS, kq, n), lambda i: (0, row_block, 0)),
        out_shape=jax.ShapeDtypeStruct(into.shape, F32), input_output_aliases={2: 0},
        compiler_params=_seq(1),
    )(a, b, into)


def _adam_math(w, g, m, v):
    m2 = ADAM_B1 * m + (1.0 - ADAM_B1) * g
    v2 = ADAM_B2 * v + (1.0 - ADAM_B2) * (g * g)
    m_hat = m2 / (1.0 - ADAM_B1 ** ADAM_STEP)
    v_hat = v2 / (1.0 - ADAM_B2 ** ADAM_STEP)
    delta = -ADAM_LR * (m_hat / (jnp.sqrt(v_hat) + ADAM_EPS) + ADAM_WD * w)
    return delta, m2, v2


def _adam(w, g, m, v, g_row0, tr, name):
    r, c = w.shape
    blk0 = g_row0 // tr

    def body(w_ref, g_ref, m_ref, v_ref, go_ref, d_ref, mo_ref, vo_ref):
        gv = g_ref[...]
        go_ref[...] = gv
        d_ref[...], mo_ref[...], vo_ref[...] = _adam_math(w_ref[...], gv, m_ref[...], v_ref[...])

    spec = _rows(tr, c)
    g_spec = pl.BlockSpec((tr, c), lambda i: (blk0 + i, 0))
    o = jax.ShapeDtypeStruct((r, c), F32)
    return _pcall(body, name=name, grid=(r // tr,), in_specs=[spec, g_spec, spec, spec], out_specs=[spec] * 4,
                  out_shape=[o, o, o, o], compiler_params=_seq(1))(w, g, m, v)


def _small_update(gathered_head, gathered, q_idx, small_w, small_m, small_v, conv_w, conv_m, conv_v):
    d = gathered.shape[2]
    ns = len(_SMALL)
    cw = conv_w[0].shape[1]
    conv_rows = ((T_CAW, CONV_A), (T_CBW, CONV_B))

    def body(q_ref, h_ref, g_ref, *refs):
        ins, outs = refs[:3 * (ns + 2)], refs[3 * (ns + 2):]
        tot_ref, loss_ref = outs[0], outs[1]
        outs = outs[2:]
        head, tot = h_ref[0], g_ref[0]
        for dev in range(1, N_DEV):
            head = head + h_ref[dev]
            tot = tot + g_ref[dev]
        tot_ref[0:T_M, :] = head
        tot_ref[T_M:T_ROWS, :] = tot
        loss_ref[...] = (0.5 / d) * jnp.sum(tot_ref[T_LOSS:T_LOSS + 1, :], axis=1, keepdims=True)
        for p, (_, rows) in enumerate(_SMALL):
            w_ref, m_ref, v_ref = ins[p], ins[ns + 2 + p], ins[2 * (ns + 2) + p]
            go, do, mo, vo = outs[4 * p:4 * p + 4]
            for j, row in enumerate(rows):
                sl = slice(j * d, (j + 1) * d)
                gv = tot_ref[row:row + 1, :]
                go[:, sl] = gv
                do[:, sl], mo[:, sl], vo[:, sl] = _adam_math(w_ref[:, sl], gv, m_ref[:, sl], v_ref[:, sl])
        for p, (row, taps) in enumerate(conv_rows):
            w_ref, m_ref, v_ref = ins[ns + p], ins[ns + 2 + ns + p], ins[2 * (ns + 2) + ns + p]
            go, do, mo, vo = outs[4 * (ns + p):4 * (ns + p) + 4]
            gv = tot_ref[row:row + taps, 0:cw]
            for qq in range(1, N_CHIPS):
                gv = jnp.where(q_ref[0] == qq, tot_ref[row:row + taps, qq * cw:(qq + 1) * cw], gv)
            go[...] = gv
            do[...], mo[...], vo[...] = _adam_math(w_ref[...], gv, m_ref[...], v_ref[...])

    params = list(small_w) + list(conv_w) + list(small_m) + list(conv_m) + list(small_v) + list(conv_v)
    out_shape = [jax.ShapeDtypeStruct((T_ROWS, d), F32), jax.ShapeDtypeStruct((1, 1), F32)]
    for w in list(small_w) + list(conv_w):
        out_shape += [jax.ShapeDtypeStruct(w.shape, F32)] * 4
    vm = pl.BlockSpec(memory_space=pltpu.VMEM)
    return _pcall(
        body, name="small_update", out_shape=out_shape,
        in_specs=[pl.BlockSpec(memory_space=pltpu.SMEM), vm, vm] + [vm] * len(params),
        out_specs=[vm] * len(out_shape), compiler_params=_cparams(),
    )(q_idx, gathered_head, gathered, *params)


def kernel(x, c, w_ada, b_ada, w_in, b_in, conv_a_w, conv_a_b, ln_a_g, ln_a_b, w_a_out, b_a_out, conv_b_w, w_b_out, w_o, b_o, ln1_g, ln1_b, w_up, b_up, w_down, b_down, ln2_g, ln2_b, loss_target, m_w_ada, m_b_ada, m_w_in, m_b_in, m_conv_a_w, m_conv_a_b, m_ln_a_g, m_ln_a_b, m_w_a_out, m_b_a_out, m_conv_b_w, m_w_b_out, m_w_o, m_b_o, m_ln1_g, m_ln1_b, m_w_up, m_b_up, m_w_down, m_b_down, m_ln2_g, m_ln2_b, v_w_ada, v_b_ada, v_w_in, v_b_in, v_conv_a_w, v_conv_a_b, v_ln_a_g, v_ln_a_b, v_w_a_out, v_b_a_out, v_conv_b_w, v_w_b_out, v_w_o, v_b_o, v_ln1_g, v_ln1_b, v_w_up, v_b_up, v_w_down, v_b_down, v_ln2_g, v_ln2_b):
    given = dict(locals())
    s, d = x.shape[1], x.shape[2]
    xi, yi, ci = _my_pos()
    q = 2 * xi + yi
    me = 4 * xi + 2 * yi + ci
    i32 = jnp.int32
    q_arr = jnp.reshape(q, (1,)).astype(i32)
    others = [2 * ox + oy for ox, oy in _other_chips(xi, yi)]
    halves_idx = jnp.stack([ci] + others).astype(i32)
    chips_idx = jnp.stack([q, ci]).astype(i32)
    kq = d // N_CHIPS
    tm = min(256, s)
    rc = tm

    def sq(a):
        return a.reshape(a.shape[1:])

    x2, tgt = sq(x), sq(loss_target)

    wa = _place_shard([sq(w_in)], q_arr, "place_w_in")
    wb = _place_shard([sq(w_up), sq(w_down), sq(w_a_out), sq(w_b_out), sq(w_o)], q_arr, "place_w_rest")

    n_ada = w_ada.shape[2]
    pre = jnp.concatenate([
        jnp.broadcast_to(c, (SUBLANES, d)),
        jnp.pad(sq(conv_a_w), ((0, HALO_A - CONV_A), (0, d - kq))),
        jnp.pad(sq(conv_b_w), ((0, HALO_B - CONV_B), (0, d - kq)))], axis=0)
    b_ada_sh = lax.dynamic_slice(b_ada, (0, q * n_ada), (1, n_ada))
    pre_all, c_all, mod_all, wa = _prologue(pre, sq(w_ada), b_ada_sh, wa)
    caw = jnp.concatenate([pre_all[2 * p, SUBLANES:SUBLANES + HALO_A, :kq] for p in range(N_CHIPS)], axis=1)
    cbw = jnp.concatenate([pre_all[2 * p, SUBLANES + HALO_A:, :kq] for p in range(N_CHIPS)], axis=1)
    mod_rows = lax.dynamic_slice(mod_all, (0, me, 0), (N_DEV, 1, n_ada))[0::2, 0, :]
    mod = mod_rows.reshape(6, d)
    vecs = jnp.concatenate([mod, conv_a_b, ln_a_g, ln_a_b, b_a_out, b_o, ln1_g, ln1_b, b_down, ln2_g, ln2_b], axis=0)

    h1, z, wb = _fwd_in(x2, vecs, wa, b_in, wb, tm)
    u1, ya, yb, out1, u3, vv, mg = _fwd_mix(z, vecs, caw, cbw, wb, tm, rc)

    ts = min(2048, s)
    rest_rows = wb.shape[1]
    small0 = 2 * d // kq
    h2, fb, df0, do2, do1, dxp, macc, dbup = _mlp_fwd_bwd(x2, out1, tgt, vecs, b_up, wb, tm, 1)
    gb = _dw(h2, df0, False, ts, "dw_up", rows_total=rest_rows)
    gb = _dw(fb, do2, True, ts, "dw_down", into=gb, rows_total=rest_rows, row_block=1)
    dz, dya, dyb, xacc, dcaw, dcbw, dbin = _mix_bwd(do1, z, u1, ya, yb, vecs, caw, cbw, wb, tm, rc)
    gb = _dw_rows(u3, dya, ts, "dw_a_out", gb, small0)
    gb = _dw_rows(vv, dyb, ts, "dw_b_out", gb, small0 + 1)
    gb = _dw_rows(mg, do1, ts, "dw_o", gb, small0 + 2)

    trb, tra = rest_rows // 8, d // 8
    ga, rb = _dw_carrying(h1, dz, ts, "dw_in", _SiblingHalf, gb)
    pb = _add_halves(gb, rb, halves_idx, trb, "rs_add_halves_rest")
    ra = _exchange(_SiblingHalf, ga, "rs_to_sibling")
    pa = _add_halves(ga, ra, halves_idx, tra, "rs_add_halves_in")
    table = jnp.concatenate([macc, dbin, dcaw, xacc, dcbw, dbup], axis=0)
    gx, iacc, r3a, r3b, gathered = _in_bwd(dz, x2, dxp, vecs, wa, tm, [_ChipBlocks, _ChipBlocks, _TableToAll],
                                           [pa, pb, table])
    fa = _add_chips(ga, ra, r3a, chips_idx, tra, "rs_add_chips_in")
    fb_ = _add_chips(gb, rb, r3b, chips_idx, trb, "rs_add_chips_rest")
    g_in, g_b = _rs_join_halves(fa, fb_)

    gathered_head = _all_gather_small(iacc, "gather_ln0_sums")
    names = [n for n, _ in _SMALL]
    res = _small_update(
        gathered_head, gathered, q_arr,
        [given[n] for n in names], [given["m_" + n] for n in names], [given["v_" + n] for n in names],
        [sq(conv_a_w), sq(conv_b_w)], [sq(m_conv_a_w), sq(m_conv_b_w)], [sq(v_conv_a_w), sq(v_conv_b_w)])
    loss = res[1].reshape(())
    upd = {}
    for p, n in enumerate(names + ["conv_a_w", "conv_b_w"]):
        upd[n] = res[2 + 4 * p:6 + 4 * p]

    dmod_all = jnp.stack([gathered_head[:, r, :] if r < T_M else gathered[:, r - T_M, :] for r in _SMALL[0][1]],
                         axis=1).reshape(N_DEV, 6 * d)
    dmod_sh = lax.dynamic_slice(dmod_all, (0, q * n_ada), (N_DEV, n_ada))
    g_ada = _ada_bwd(c_all.T, dmod_sh)
    upd["w_ada"] = _adam(sq(w_ada), g_ada, sq(m_w_ada), sq(v_w_ada), 0, min(256, d), "adam_w_ada")

    upd["w_in"] = _adam(sq(w_in), g_in, sq(m_w_in), sq(v_w_in), 0, min(256, d), "adam_w_in")
    r0 = 0
    for n in ("w_up", "w_down", "w_a_out", "w_b_out", "w_o"):
        w = sq(given[n])
        upd[n] = _adam(w, g_b, sq(given["m_" + n]), sq(given["v_" + n]), r0, min(256, w.shape[0]), "adam_" + n)
        r0 += w.shape[0]

    order = ["w_ada", "b_ada", "w_in", "b_in", "conv_a_w", "conv_a_b", "ln_a_g", "ln_a_b", "w_a_out", "b_a_out", "conv_b_w",
             "w_b_out", "w_o", "b_o", "ln1_g", "ln1_b", "w_up", "b_up", "w_down", "b_down", "ln2_g", "ln2_b"]
    outs = [loss, gx.reshape(x.shape)]
    for k in range(4):
        outs += [upd[n][k].reshape(given[n].shape) for n in order]
    return tuple(outs)
```

```python
import jax
import jax.numpy as jnp
from jax import lax
from jax.experimental import pallas as pl
from jax.experimental.pallas import tpu as pltpu

F32 = jnp.float32
BF16 = jnp.bfloat16
MESH = pl.DeviceIdType.MESH

LN_EPS = 1e-5
DEPTH = 1
ALPHA = (2.0 * DEPTH) ** 0.25
CONV_A = 31
CONV_B = 3
SUBLANES = 8
HALO_A = 32
HALO_B = 8
SHIFTED_COPIES = 4
N_CHIPS = 4
N_DEV = 8
ADAM_LR = 0.001
ADAM_B1 = 0.9
ADAM_B2 = 0.999
ADAM_EPS = 1e-08
ADAM_WD = 0.01
ADAM_STEP = 10
VMEM_LIMIT = 56 * 1024 * 1024

V_SHIFT1, V_SCALE1, V_GATE1, V_SHIFT2, V_SCALE2, V_GATE2 = 0, 1, 2, 3, 4, 5
V_CAB, V_LNAG, V_LNAB, V_BAO, V_BO, V_LN1G, V_LN1B, V_BDN, V_LN2G, V_LN2B = 6, 7, 8, 9, 10, 11, 12, 13, 14, 15

M_LN2G, M_LN2B, M_GATE2, M_BDN, M_SHIFT2, M_SCALE2, M_LN1G, M_LN1B, M_GATE1, M_BO, M_LOSS = range(11)
X_BAO, X_LNAG, X_LNAB, X_CAB = range(4)
I_SHIFT1, I_SCALE1 = 0, 1

T_I, T_M, T_BIN, T_CAW, T_X, T_CBW, T_BUP, T_ROWS = 0, 8, 24, 32, 64, 72, 80, 88
T_LOSS = T_M + M_LOSS
_SMALL = (
    ("b_ada", (T_I + I_SHIFT1, T_I + I_SCALE1, T_M + M_GATE1, T_M + M_SHIFT2, T_M + M_SCALE2, T_M + M_GATE2)),
    ("b_in", tuple(T_BIN + j for j in range(7))),
    ("conv_a_b", (T_X + X_CAB,)), ("ln_a_g", (T_X + X_LNAG,)), ("ln_a_b", (T_X + X_LNAB,)), ("b_a_out", (T_X + X_BAO,)),
    ("b_o", (T_M + M_BO,)), ("ln1_g", (T_M + M_LN1G,)), ("ln1_b", (T_M + M_LN1B,)),
    ("b_up", tuple(T_BUP + j for j in range(4))),
    ("b_down", (T_M + M_BDN,)), ("ln2_g", (T_M + M_LN2G,)), ("ln2_b", (T_M + M_LN2B,)),
)


def _pcall(body, **kw):
    return pl.pallas_call(body, **kw)


def _cparams(**kw):
    return pltpu.CompilerParams(vmem_limit_bytes=VMEM_LIMIT, **kw)


def _seq(n):
    return _cparams(dimension_semantics=("arbitrary",) * n)


def _full(shape, single=False):
    nd = len(shape)
    if single:
        return pl.BlockSpec(shape, lambda *_: (0,) * nd, pipeline_mode=pl.Buffered(1))
    return pl.BlockSpec(shape, lambda *_: (0,) * nd)


def _rows(tm, width):
    return pl.BlockSpec((tm, width), lambda i: (i, 0))


def _sig(x):
    return jax.nn.sigmoid(x)


def _ln(x):
    mu = jnp.mean(x, axis=-1, keepdims=True)
    xc = x - mu
    var = jnp.mean(xc * xc, axis=-1, keepdims=True)
    rstd = lax.rsqrt(var + LN_EPS)
    return xc * rstd, rstd


def _ln_bwd(dxh, xh, rstd):
    m1 = jnp.mean(dxh, axis=-1, keepdims=True)
    m2 = jnp.mean(dxh * xh, axis=-1, keepdims=True)
    return rstd * (dxh - m1 - xh * m2)


def _rsum(v):
    return jnp.sum(v, axis=0, keepdims=True)


def _rsum_mxu(v):
    vb = v if v.dtype == BF16 else v.astype(BF16)
    return _dot(jnp.ones((2 * SUBLANES, v.shape[0]), BF16), vb)[0:1, :]


def _dot(a, b):
    return jnp.dot(a, b, preferred_element_type=F32)


def _dot_nt(a, b):
    return lax.dot_general(a, b, (((1,), (1,)), ((), ())), preferred_element_type=F32)


def _dot_tn(a, b):
    return lax.dot_general(a, b, (((0,), (0,)), ((), ())), preferred_element_type=F32)


def _my_pos():
    return lax.axis_index("x"), lax.axis_index("y"), lax.axis_index("c")


def _other_chips(x, y):
    return [(1 - x, y), (x, 1 - y), (1 - x, 1 - y)]


def _small_gather(v_ref, out_ref, send_sems, recv_sems, local_sem):
    x, y, cc = _my_pos()
    me = 4 * x + 2 * y + cc
    mine = pltpu.make_async_copy(v_ref, out_ref.at[me], local_sem)
    mine.start()
    sends = []
    for rel in range(1, N_DEV):
        rx, ry, rc = (rel >> 2) & 1, (rel >> 1) & 1, rel & 1
        peer = (1 - x if rx else x, 1 - y if ry else y, 1 - cc if rc else cc)
        cp = pltpu.make_async_remote_copy(
            src_ref=v_ref, dst_ref=out_ref.at[me], send_sem=send_sems.at[rel - 1], recv_sem=recv_sems.at[rel - 1],
            device_id=peer, device_id_type=MESH)
        cp.start()
        sends.append(cp)
    for rel in range(1, N_DEV):
        rx, ry, rc = (rel >> 2) & 1, (rel >> 1) & 1, rel & 1
        peer = (1 - x if rx else x, 1 - y if ry else y, 1 - cc if rc else cc)
        slot = 4 * peer[0] + 2 * peer[1] + peer[2]
        pltpu.make_async_remote_copy(
            src_ref=v_ref, dst_ref=out_ref.at[slot], send_sem=send_sems.at[rel - 1], recv_sem=recv_sems.at[rel - 1],
            device_id=peer, device_id_type=MESH).wait_recv()
    for cp in sends:
        cp.wait_send()
    mine.wait()


_SMALL_GATHER_SEMS = [pltpu.SemaphoreType.DMA((N_DEV - 1,)), pltpu.SemaphoreType.DMA((N_DEV - 1,)),
                      pltpu.SemaphoreType.DMA]


def _all_gather_small(v, name):
    r, c = v.shape

    def body(*refs):
        _small_gather(*refs)

    return _pcall(
        body, name=name,
        out_shape=jax.ShapeDtypeStruct((N_DEV, r, c), v.dtype),
        in_specs=[pl.BlockSpec(memory_space=pltpu.VMEM)],
        out_specs=pl.BlockSpec(memory_space=pltpu.VMEM),
        scratch_shapes=list(_SMALL_GATHER_SEMS),
        compiler_params=_cparams(),
    )(v)


def _prologue(pre, w_sh, b_sh, wa):
    r, d = pre.shape
    n_ada = w_sh.shape[1]
    ng = _GatherShards.n_sems

    def body(pre_ref, w_ref, b_ref, wai_ref, pre_all_ref, c_all_ref, mod_all_ref, wao_ref, mod_ref,
             s1, r1, l1, s2, r2, l2, sg, rg):
        gather = (wai_ref, wao_ref, sg, rg)
        _GatherShards.start(*gather)
        _small_gather(pre_ref, pre_all_ref, s1, r1, l1)
        cv = jnp.concatenate([pre_all_ref[dev, 0:1, :] for dev in range(N_DEV)], axis=0)
        c_all_ref[...] = cv
        ca = cv * _sig(cv)
        mod_ref[...] = jnp.dot(ca, w_ref[...], preferred_element_type=F32, precision=lax.Precision.HIGHEST) + b_ref[...]
        _small_gather(mod_ref, mod_all_ref, s2, r2, l2)
        _GatherShards.relay(*gather)
        _GatherShards.finish(*gather)

    vm = pl.BlockSpec(memory_space=pltpu.VMEM)
    any_spec = pl.BlockSpec(memory_space=pl.ANY)
    return _pcall(
        body, name="prologue",
        out_shape=[jax.ShapeDtypeStruct((N_DEV, r, d), F32), jax.ShapeDtypeStruct((N_DEV, d), F32),
                   jax.ShapeDtypeStruct((N_DEV, N_DEV, n_ada), F32), jax.ShapeDtypeStruct(wa.shape, wa.dtype)],
        in_specs=[vm, vm, vm, any_spec], out_specs=[vm, vm, vm, any_spec], input_output_aliases={3: 3},
        scratch_shapes=[pltpu.VMEM((N_DEV, n_ada), F32)] + list(_SMALL_GATHER_SEMS) + list(_SMALL_GATHER_SEMS)
        + [pltpu.SemaphoreType.DMA((ng,)), pltpu.SemaphoreType.DMA((ng,))],
        compiler_params=_cparams(),
    )(pre, w_sh, b_sh, wa)


def _place_shard(parts, q_idx, name):
    rows = sum(p.shape[0] for p in parts)
    w = parts[0].shape[1]

    def body(q_ref, *refs):
        o_ref = refs[-1]
        r0 = 0
        for p_ref in refs[:-1]:
            n = p_ref.shape[0]
            o_ref[r0:r0 + n, :] = p_ref[...].astype(BF16)
            r0 += n

    grid_spec = pltpu.PrefetchScalarGridSpec(
        num_scalar_prefetch=1, grid=(1,),
        in_specs=[pl.BlockSpec(p.shape, lambda i, q: (0, 0)) for p in parts],
        out_specs=pl.BlockSpec((None, rows, w), lambda i, q: (q[0], 0, 0)))
    return _pcall(body, name=name, grid_spec=grid_spec, out_shape=jax.ShapeDtypeStruct((N_CHIPS, rows, w), BF16),
                  compiler_params=_seq(1))(q_idx, *parts)


class _GatherShards:
    n_sems = 6

    @staticmethod
    def _half(ref, slot, h):
        rows = ref.shape[1] // 2
        return ref.at[slot, pl.ds(h * rows, rows)]

    @classmethod
    def _copies(cls, in_ref, out_ref, send_sems, recv_sems):
        x, y, c = _my_pos()
        q = 2 * x + y
        sibling = (x, y, 1 - c)
        sends, landed, forwards, passed = [], [], [], []
        for j, chip in enumerate(_other_chips(x, y)):
            qj = 2 * chip[0] + chip[1]

            def copy(src, dst, k, to):
                return pltpu.make_async_remote_copy(src_ref=src, dst_ref=dst, send_sem=send_sems.at[k],
                                                    recv_sem=recv_sems.at[k], device_id=to, device_id_type=MESH)

            mine, theirs = cls._half(out_ref, qj, c), cls._half(out_ref, qj, 1 - c)
            sends.append(copy(cls._half(in_ref, q, c), cls._half(out_ref, q, c), j, (*chip, c)))
            landed.append(copy(mine, mine, j, (*chip, c)))
            forwards.append(copy(mine, mine, 3 + j, sibling))
            passed.append(copy(theirs, theirs, 3 + j, sibling))
        return sends, landed, forwards, passed

    @classmethod
    def start(cls, *refs):
        for cp in cls._copies(*refs)[0]:
            cp.start()

    @classmethod
    def relay(cls, *refs):
        _, landed, forwards, _ = cls._copies(*refs)
        for arrived, onward in zip(landed, forwards):
            arrived.wait_recv()
            onward.start()

    @classmethod
    def finish(cls, *refs):
        sends, _, forwards, passed = cls._copies(*refs)
        for cp in passed:
            cp.wait_recv()
        for cp in sends + forwards:
            cp.wait_send()


class _SiblingHalf:
    n_sems = 1

    @staticmethod
    def out_shape(g):
        return jax.ShapeDtypeStruct((g.shape[0], g.shape[1] // 2, g.shape[2]), g.dtype)

    @staticmethod
    def copies(g_ref, r_ref, send_sems, recv_sems, base):
        x, y, c = _my_pos()
        rows = r_ref.shape[1]
        return [pltpu.make_async_remote_copy(
            src_ref=g_ref.at[:, pl.ds((1 - c) * rows, rows)], dst_ref=r_ref,
            send_sem=send_sems.at[base], recv_sem=recv_sems.at[base], device_id=(x, y, 1 - c), device_id_type=MESH)]


class _ChipBlocks:
    n_sems = 3

    @staticmethod
    def out_shape(p):
        return jax.ShapeDtypeStruct(p.shape, p.dtype)

    @staticmethod
    def copies(p_ref, r_ref, send_sems, recv_sems, base):
        x, y, c = _my_pos()
        return [pltpu.make_async_remote_copy(
            src_ref=p_ref.at[j], dst_ref=r_ref.at[j], send_sem=send_sems.at[base + j], recv_sem=recv_sems.at[base + j],
            device_id=(*chip, c), device_id_type=MESH) for j, chip in enumerate(_other_chips(x, y))]


class _TableToAll:
    n_sems = N_DEV

    @staticmethod
    def out_shape(t):
        return jax.ShapeDtypeStruct((N_DEV,) + t.shape, t.dtype)

    @staticmethod
    def copies(t_ref, all_ref, send_sems, recv_sems, base):
        x, y, c = _my_pos()
        me = 4 * x + 2 * y + c
        cps = [pltpu.make_async_copy(t_ref, all_ref.at[me], send_sems.at[base + N_DEV - 1])]
        for rel in range(1, N_DEV):
            rx, ry, rc = (rel >> 2) & 1, (rel >> 1) & 1, rel & 1
            peer = (1 - x if rx else x, 1 - y if ry else y, 1 - c if rc else c)
            cps.append(_SlotCopy(t_ref, all_ref, me, 4 * peer[0] + 2 * peer[1] + peer[2], peer,
                                 send_sems.at[base + rel - 1], recv_sems.at[base + rel - 1]))
        return cps


class _SlotCopy:
    def __init__(self, src_ref, all_ref, my_slot, peer_slot, peer, send_sem, recv_sem):
        self._send = pltpu.make_async_remote_copy(src_ref=src_ref, dst_ref=all_ref.at[my_slot], send_sem=send_sem,
                                                  recv_sem=recv_sem, device_id=peer, device_id_type=MESH)
        self._recv = pltpu.make_async_remote_copy(src_ref=src_ref, dst_ref=all_ref.at[peer_slot], send_sem=send_sem,
                                                  recv_sem=recv_sem, device_id=peer, device_id_type=MESH)

    def start(self):
        self._send.start()

    def wait(self):
        self._send.wait_send()
        self._recv.wait_recv()


def _plan_copies(plans, src_refs, dst_refs, send_sems, recv_sems):
    cps, base = [], 0
    for plan, s_ref, d_ref in zip(plans, src_refs, dst_refs):
        cps += plan.copies(s_ref, d_ref, send_sems, recv_sems, base)
        base += plan.n_sems
    return cps


def _plan_sems(plans):
    n = sum(p.n_sems for p in plans)
    return [pltpu.SemaphoreType.DMA((n,)), pltpu.SemaphoreType.DMA((n,))]


def _carried_start(plans, first, comm):
    @pl.when(first)
    def _():
        for cp in _plan_copies(plans, *comm):
            cp.start()


def _carried_wait(plans, last, comm):
    @pl.when(last)
    def _():
        for cp in _plan_copies(plans, *comm):
            cp.wait()


def _rs_join_halves(fa, fb):
    bufs = (fa, fb)
    nb = len(bufs)

    def body(a_ref, b_ref, ao_ref, bo_ref, send_sems, recv_sems):
        x, y, c = _my_pos()
        srcs, outs = (a_ref, b_ref), (ao_ref, bo_ref)
        cps = []
        for b in range(nb):
            rows = srcs[b].shape[0] // 2
            cp = pltpu.make_async_remote_copy(
                src_ref=srcs[b].at[pl.ds(c * rows, rows)], dst_ref=outs[b].at[pl.ds(c * rows, rows)],
                send_sem=send_sems.at[b], recv_sem=recv_sems.at[b], device_id=(x, y, 1 - c), device_id_type=MESH)
            cp.start()
            cps.append(cp)
        for b in range(nb):
            rows = srcs[b].shape[0] // 2
            theirs = outs[b].at[pl.ds((1 - c) * rows, rows)]
            pltpu.make_async_remote_copy(
                src_ref=theirs, dst_ref=theirs, send_sem=send_sems.at[b], recv_sem=recv_sems.at[b],
                device_id=(x, y, 1 - c), device_id_type=MESH).wait_recv()
        for cp in cps:
            cp.wait_send()

    any_spec = pl.BlockSpec(memory_space=pl.ANY)
    return _pcall(
        body, name="rs_join_halves",
        out_shape=[jax.ShapeDtypeStruct(b.shape, b.dtype) for b in bufs],
        in_specs=[any_spec] * nb, out_specs=[any_spec] * nb, input_output_aliases={0: 0, 1: 1},
        scratch_shapes=[pltpu.SemaphoreType.DMA((nb,)), pltpu.SemaphoreType.DMA((nb,))],
        compiler_params=_cparams(),
    )(*bufs)


def _add_halves(g, r, idx, tr, name, plan=None, plan_src=None):
    _, rows, w = r.shape
    nt = rows // tr

    def body(i_ref, g_ref, r_ref, *rest):
        o_ref = rest[0] if plan is None else rest[1]
        if plan is not None:
            comm = ([rest[0]], [rest[2]], rest[3], rest[4])
            first = jnp.logical_and(pl.program_id(0) == 0, pl.program_id(1) == 0)
            _carried_start([plan], first, comm)
        o_ref[...] = (g_ref[...] + r_ref[...]).astype(BF16)
        if plan is not None:
            last = jnp.logical_and(pl.program_id(0) == 2, pl.program_id(1) == nt - 1)
            _carried_wait([plan], last, comm)

    any_spec = pl.BlockSpec(memory_space=pl.ANY)
    carried = plan is not None
    grid_spec = pltpu.PrefetchScalarGridSpec(
        num_scalar_prefetch=1, grid=(3, nt),
        in_specs=[pl.BlockSpec((None, tr, w), lambda j, i, ix: (ix[1 + j], ix[0] * nt + i, 0)),
                  pl.BlockSpec((None, tr, w), lambda j, i, ix: (ix[1 + j], i, 0))] + [any_spec] * carried,
        out_specs=[pl.BlockSpec((None, tr, w), lambda j, i, ix: (j, i, 0))] + [any_spec] * carried,
        scratch_shapes=_plan_sems([plan]) if carried else [])
    out_shape = [jax.ShapeDtypeStruct((3, rows, w), BF16)] + ([plan.out_shape(plan_src)] if carried else [])
    res = _pcall(body, name=name, grid_spec=grid_spec, out_shape=out_shape,
                 compiler_params=_seq(2))(*((idx, g, r, plan_src) if carried else (idx, g, r)))
    return res if carried else res[0]


def _add_chips(g, r, r3, idx, tr, name):
    _, rows, w = r.shape
    nt = rows // tr

    def body(i_ref, g_ref, r_ref, a_ref, b_ref, c_ref, o_ref):
        own = g_ref[...] + r_ref[...]
        o_ref[...] = ((own + a_ref[...].astype(F32)) + b_ref[...].astype(F32)) + c_ref[...].astype(F32)

    def other(j):
        return pl.BlockSpec((None, tr, w), lambda i, ix: (j, i, 0))

    grid_spec = pltpu.PrefetchScalarGridSpec(
        num_scalar_prefetch=1, grid=(nt,),
        in_specs=[pl.BlockSpec((None, tr, w), lambda i, ix: (ix[0], ix[1] * nt + i, 0)),
                  pl.BlockSpec((None, tr, w), lambda i, ix: (ix[0], i, 0)), other(0), other(1), other(2)],
        out_specs=pl.BlockSpec((tr, w), lambda i, ix: (ix[1] * nt + i, 0)))
    return _pcall(body, name=name, grid_spec=grid_spec,
                  out_shape=jax.ShapeDtypeStruct((2 * rows, w), F32), compiler_params=_seq(1))(idx, g, r, r3, r3, r3)


def _ada_bwd(c_all_t, dmod_sh):
    def body(c_ref, d_ref, o_ref):
        cv = c_ref[...]
        ca = cv * _sig(cv)
        o_ref[...] = jnp.dot(ca, d_ref[...], preferred_element_type=F32, precision=lax.Precision.HIGHEST)

    return _pcall(body, name="ada_bwd", out_shape=jax.ShapeDtypeStruct((c_all_t.shape[0], dmod_sh.shape[1]), F32),
                  compiler_params=_cparams())(c_all_t, dmod_sh)


def _residue_passes(residues, copies):
    passes, current, used = [], [], 0
    for s in residues:
        if s != 0 and used == copies:
            passes.append(current)
            current, used = [], 0
        current.append(s)
        used += s != 0
    return passes + [current] if current else passes


def _conv_causal(ext_ref, sh_ref, w_ref, ntaps, halo, bias, out_ref, tm, d, rc):
    off = halo - (ntaps - 1)
    n = tm + halo - SUBLANES
    taps = {s: [(k, (off + k) // SUBLANES * SUBLANES) for k in range(ntaps) if (off + k) % SUBLANES == s]
            for s in range(SUBLANES)}
    started = False
    for group in _residue_passes([s for s in range(SUBLANES) if taps[s]], sh_ref.shape[0]):
        srcs = {}
        for s in group:
            if s == 0:
                srcs[s] = ext_ref
            else:
                j = len(srcs) - (0 in srcs)
                sh_ref[j, 0:n, :] = ext_ref[s:s + n, :]
                srcs[s] = sh_ref.at[j]
        for r0 in range(0, tm, rc):
            if started:
                acc = out_ref[r0:r0 + rc, :]
            else:
                acc = jnp.zeros((rc, d), F32) if bias is None else jnp.broadcast_to(bias, (rc, d))
            for s, src in srcs.items():
                for k, a in taps[s]:
                    acc = acc + w_ref[k:k + 1, :] * src[r0 + a:r0 + a + rc, :]
            out_ref[r0:r0 + rc, :] = acc
        started = True


def _conv_adjoint(dp_ref, sh_ref, ext_ref, w_ref, dx_ref, dw_ref, ntaps, halo, tm, d, rc):
    off = halo - (ntaps - 1)
    lead = SUBLANES + ntaps - 1
    n = tm + halo
    row = lax.broadcasted_iota(jnp.int32, (SUBLANES, d), 0)
    taps = {s: [(k, (lead - k) // SUBLANES * SUBLANES) for k in range(ntaps) if (lead - k) % SUBLANES == s]
            for s in range(SUBLANES)}
    wtaps = {s: [(k, s + off + k - SUBLANES) for k in range(ntaps) if (-(off + k)) % SUBLANES == s]
             for s in range(SUBLANES)}
    rw = 2 * SUBLANES
    started = False
    for group in _residue_passes([s for s in range(SUBLANES) if taps[s] or wtaps[s]], sh_ref.shape[0]):
        srcs = {}
        for s in group:
            if s == 0:
                srcs[s] = dp_ref
            else:
                j = len(srcs) - (0 in srcs)
                sh_ref[j, 0:n, :] = dp_ref[s:s + n, :]
                srcs[s] = sh_ref.at[j]
        if any(taps[s] for s in srcs):
            for r0 in range(0, tm, rc):
                acc = dx_ref[r0:r0 + rc, :] if started else jnp.zeros((rc, d), F32)
                for s, src in srcs.items():
                    for k, a in taps[s]:
                        acc = acc + w_ref[k:k + 1, :] * src[r0 + a:r0 + a + rc, :]
                dx_ref[r0:r0 + rc, :] = acc
            started = True
        for s, src in srcs.items():
            if not wtaps[s]:
                continue
            sums = [jnp.zeros((SUBLANES, d), F32) for _ in wtaps[s]]
            for r0 in range(0, tm, rw):
                g = src[r0:r0 + rw, :]
                for j, (_, e) in enumerate(wtaps[s]):
                    p = g * ext_ref[r0 + e:r0 + e + rw, :]
                    for r8 in range(0, rw, SUBLANES):
                        sums[j] = sums[j] + p[r8:r8 + SUBLANES, :]
            for j, (k, e) in enumerate(wtaps[s]):
                tail = src[tm:tm + SUBLANES, :] * ext_ref[tm + e:tm + e + SUBLANES, :]
                dw_ref[k:k + 1, :] += _rsum(sums[j] + jnp.where(row < SUBLANES - s, tail, 0.0))


def _fwd_in(x, vecs, wa, b_in, wb, tm):
    s, d = x.shape
    nq, _, nw = wa.shape
    nt = s // tm

    def body(x_ref, v_ref, w_ref, b_ref, wbi_ref, h_ref, z_ref, wbo_ref, send_sems, recv_sems):
        i = pl.program_id(0)
        gather = (wbi_ref, wbo_ref, send_sems, recv_sems)
        pl.when(i == 0)(lambda: _GatherShards.start(*gather))
        pl.when(i == nt // 2)(lambda: _GatherShards.relay(*gather))
        xh, _ = _ln(x_ref[...])
        h = (xh * (1.0 + v_ref[V_SCALE1:V_SCALE1 + 1, :]) + v_ref[V_SHIFT1:V_SHIFT1 + 1, :]).astype(BF16)
        h_ref[...] = h
        for q in range(nq):
            z_ref[:, q * nw:(q + 1) * nw] = _dot(h, w_ref[q]) + b_ref[:, q * nw:(q + 1) * nw]
        pl.when(i == nt - 1)(lambda: _GatherShards.finish(*gather))

    any_spec = pl.BlockSpec(memory_space=pl.ANY)
    n = _GatherShards.n_sems
    return _pcall(
        body, name="fwd_in", grid=(nt,),
        in_specs=[_rows(tm, d), _full(vecs.shape), _full(wa.shape, single=True), _full(b_in.shape), any_spec],
        out_specs=[_rows(tm, d), _rows(tm, nq * nw), any_spec],
        out_shape=[jax.ShapeDtypeStruct((s, d), BF16), jax.ShapeDtypeStruct((s, nq * nw), F32),
                   jax.ShapeDtypeStruct(wb.shape, wb.dtype)],
        input_output_aliases={4: 2},
        scratch_shapes=[pltpu.SemaphoreType.DMA((n,)), pltpu.SemaphoreType.DMA((n,))],
        compiler_params=_seq(1),
    )(x, vecs, wa, b_in, wb)


def _fwd_mix(z, vecs, caw, cbw, wb, tm, rc):
    s = z.shape[0]
    d = vecs.shape[1]
    nq = wb.shape[0]
    kq = d // nq
    base = 2 * d // kq

    def body(z_ref, v_ref, caw_ref, cbw_ref, wao_ref, wbo_ref, wo_ref,
             u1_ref, ya_ref, yb_ref, o1_ref, u3_ref, vv_ref, mg_ref, ext_ref, sh_ref, pext_ref, q_ref):
        @pl.when(pl.program_id(0) == 0)
        def _():
            ext_ref[0:HALO_A, :] = jnp.zeros((HALO_A, d), F32)
            pext_ref[0:HALO_B, :] = jnp.zeros((HALO_B, d), F32)

        ext_ref[HALO_A:HALO_A + tm, :] = z_ref[:, 0:d] * _sig(z_ref[:, d:2 * d])
        _conv_causal(ext_ref, sh_ref, caw_ref, CONV_A, HALO_A, v_ref[V_CAB:V_CAB + 1, :], u1_ref, tm, d, rc)
        ext_ref[0:HALO_A, :] = ext_ref[tm:tm + HALO_A, :]
        xa, _ = _ln(u1_ref[...])
        u2 = xa * v_ref[V_LNAG:V_LNAG + 1, :] + v_ref[V_LNAB:V_LNAB + 1, :]
        u3 = (u2 * _sig(u2)).astype(BF16)
        u3_ref[...] = u3
        ya = jnp.broadcast_to(v_ref[V_BAO:V_BAO + 1, :], (tm, d))
        for q in range(nq):
            ya = ya + _dot(u3[:, q * kq:(q + 1) * kq], wao_ref[q])
        ya_ref[...] = ya

        pext_ref[HALO_B:HALO_B + tm, :] = z_ref[:, 3 * d:4 * d] * z_ref[:, 4 * d:5 * d]
        _conv_causal(pext_ref, sh_ref, cbw_ref, CONV_B, HALO_B, None, q_ref, tm, d, rc)
        pext_ref[0:HALO_B, :] = pext_ref[tm:tm + HALO_B, :]
        vv = (z_ref[:, 2 * d:3 * d] * q_ref[...]).astype(BF16)
        vv_ref[...] = vv
        yb = jnp.zeros((tm, d), F32)
        for q in range(nq):
            yb = yb + _dot(vv[:, q * kq:(q + 1) * kq], wbo_ref[q])
        yb_ref[...] = yb

        mg = (_sig(z_ref[:, 5 * d:6 * d]) * ya + _sig(z_ref[:, 6 * d:7 * d]) * yb).astype(BF16)
        mg_ref[...] = mg
        o1 = jnp.broadcast_to(v_ref[V_BO:V_BO + 1, :], (tm, d))
        for q in range(nq):
            o1 = o1 + _dot(mg[:, q * kq:(q + 1) * kq], wo_ref[q])
        o1_ref[...] = o1

    def wspec(j):
        return pl.BlockSpec((nq, kq, d), lambda i: (0, base + j, 0), pipeline_mode=pl.Buffered(1))

    f32o = jax.ShapeDtypeStruct((s, d), F32)
    b16o = jax.ShapeDtypeStruct((s, d), BF16)
    return _pcall(
        body, name="fwd_mix", grid=(s // tm,),
        in_specs=[_rows(tm, 7 * d), _full(vecs.shape), _full(caw.shape), _full(cbw.shape), wspec(0), wspec(1), wspec(2)],
        out_specs=[_rows(tm, d)] * 7,
        out_shape=[f32o, f32o, f32o, f32o, b16o, b16o, b16o],
        scratch_shapes=[pltpu.VMEM((HALO_A + tm, d), F32), pltpu.VMEM((SHIFTED_COPIES, HALO_A + tm, d), F32),
                        pltpu.VMEM((HALO_B + tm, d), F32), pltpu.VMEM((tm, d), F32)],
        compiler_params=_seq(1),
    )(z, vecs, caw, cbw, wb, wb, wb)


def _mlp_fwd_bwd(x, out1, tgt, vecs, b_up, wb, tm):
    s, d = x.shape
    nq = wb.shape[0]
    dff = nq * d

    def body(x_ref, o1_ref, t_ref, v_ref, bup_ref, wup_ref, wdn_ref,
             h2_ref, f_ref, df0_ref, do2_ref, do1_ref, dxp_ref, acc_ref, dbup_ref, f0_ref):
        @pl.when(pl.program_id(0) == 0)
        def _():
            acc_ref[...] = jnp.zeros(acc_ref.shape, F32)
            dbup_ref[...] = jnp.zeros(dbup_ref.shape, F32)

        def vec(r):
            return v_ref[r:r + 1, :]

        def accum(r, val):
            acc_ref[r:r + 1, :] += _rsum(val)

        out1v = o1_ref[...]
        r1 = ALPHA * x_ref[...] + (1.0 + vec(V_GATE1)) * out1v
        xh1, rstd1 = _ln(r1)
        x1 = xh1 * vec(V_LN1G) + vec(V_LN1B)
        xn1, rstdn = _ln(x1)
        h2 = (xn1 * (1.0 + vec(V_SCALE2)) + vec(V_SHIFT2)).astype(BF16)
        h2_ref[...] = h2
        out2 = jnp.broadcast_to(vec(V_BDN), (tm, d))
        for q in range(nq):
            f0 = _dot(h2, wup_ref[q]) + bup_ref[:, q * d:(q + 1) * d]
            rl = jnp.maximum(f0, 0.0)
            f0_ref[:, q * d:(q + 1) * d] = rl
            fb = (rl * rl).astype(BF16)
            f_ref[:, q * d:(q + 1) * d] = fb
            out2 = out2 + _dot(fb, wdn_ref[q])
        r2 = ALPHA * x1 + (1.0 + vec(V_GATE2)) * out2
        xh2, rstd2 = _ln(r2)
        yv = xh2 * vec(V_LN2G) + vec(V_LN2B)
        err = yv - t_ref[...]
        accum(M_LOSS, err * err)
        dy = err * (1.0 / d)
        accum(M_LN2G, dy * xh2)
        accum(M_LN2B, dy)
        dr2 = _ln_bwd(dy * vec(V_LN2G), xh2, rstd2)
        accum(M_GATE2, dr2 * out2)
        dout2 = (1.0 + vec(V_GATE2)) * dr2
        accum(M_BDN, dout2)
        do2b = dout2.astype(BF16)
        do2_ref[...] = do2b
        dh2 = jnp.zeros((tm, d), F32)
        for q in range(nq):
            df0 = _dot_nt(do2b, wdn_ref[q]) * (2.0 * f0_ref[:, q * d:(q + 1) * d])
            dbup_ref[q:q + 1, :] += _rsum(df0)
            df0b = df0.astype(BF16)
            df0_ref[:, q * d:(q + 1) * d] = df0b
            dh2 = dh2 + _dot_nt(df0b, wup_ref[q])
        accum(M_SHIFT2, dh2)
        accum(M_SCALE2, dh2 * xn1)
        dx1 = ALPHA * dr2 + _ln_bwd(dh2 * (1.0 + vec(V_SCALE2)), xn1, rstdn)
        accum(M_LN1G, dx1 * xh1)
        accum(M_LN1B, dx1)
        dr1 = _ln_bwd(dx1 * vec(V_LN1G), xh1, rstd1)
        accum(M_GATE1, dr1 * out1v)
        dout1 = (1.0 + vec(V_GATE1)) * dr1
        accum(M_BO, dout1)
        do1_ref[...] = dout1.astype(BF16)
        dxp_ref[...] = ALPHA * dr1

    def wspec(j):
        return pl.BlockSpec((nq, d, d), lambda i: (0, j, 0), pipeline_mode=pl.Buffered(1))

    b16 = lambda w: jax.ShapeDtypeStruct((s, w), BF16)
    return _pcall(
        body, name="mlp_fwd_bwd", grid=(s // tm,),
        in_specs=[_rows(tm, d), _rows(tm, d), _rows(tm, d), _full(vecs.shape), _full(b_up.shape), wspec(0), wspec(1)],
        out_specs=[_rows(tm, d), _rows(tm, dff), _rows(tm, dff), _rows(tm, d), _rows(tm, d), _rows(tm, d),
                   _full((16, d)), _full((SUBLANES, d))],
        out_shape=[b16(d), b16(dff), b16(dff), b16(d), b16(d), jax.ShapeDtypeStruct((s, d), F32),
                   jax.ShapeDtypeStruct((16, d), F32), jax.ShapeDtypeStruct((SUBLANES, d), F32)],
        scratch_shapes=[pltpu.VMEM((tm, dff), F32)],
        compiler_params=_seq(1),
    )(x, out1, tgt, vecs, b_up, wb, wb)


def _mix_bwd(dout1, z, u1, ya, yb, vecs, caw, cbw, wb, tm, rc):
    s = z.shape[0]
    d = vecs.shape[1]
    nq = wb.shape[0]
    kq = d // nq
    base = 2 * d // kq
    nt = s // tm
    hb = tm // HALO_A

    def body(do1_ref, z_ref, zh_ref, u1_ref, ya_ref, yb_ref, v_ref, caw_ref, cbw_ref, wao_ref, wbo_ref, wo_ref,
             dz_ref, dya_ref, dyb_ref, acc_ref, dcaw_ref, dcbw_ref, dbin_ref,
             ext_ref, du1p_ref, sh_ref, pext_ref, dqp_ref, tmp_ref):
        i = pl.program_id(0)

        @pl.when(i == 0)
        def _():
            acc_ref[...] = jnp.zeros(acc_ref.shape, F32)
            dcaw_ref[...] = jnp.zeros(dcaw_ref.shape, F32)
            dcbw_ref[...] = jnp.zeros(dcbw_ref.shape, F32)
            dbin_ref[...] = jnp.zeros(dbin_ref.shape, F32)
            du1p_ref[0:SUBLANES, :] = jnp.zeros((SUBLANES, d), F32)
            du1p_ref[SUBLANES + tm:SUBLANES + tm + HALO_A, :] = jnp.zeros((HALO_A, d), F32)
            dqp_ref[0:SUBLANES, :] = jnp.zeros((SUBLANES, d), F32)
            dqp_ref[SUBLANES + tm:SUBLANES + tm + HALO_B, :] = jnp.zeros((HALO_B, d), F32)

        def vec(r):
            return v_ref[r:r + 1, :]

        def accum(r, val):
            acc_ref[r:r + 1, :] += _rsum(val)

        def put_dz(j, val):
            vb = val.astype(BF16)
            dz_ref[:, j * d:(j + 1) * d] = vb
            dbin_ref[j:j + 1, :] += _rsum_mxu(vb)

        has_history = i < nt - 1

        do1 = do1_ref[...]
        dmg = jnp.concatenate([_dot_nt(do1, wo_ref[q]) for q in range(nq)], axis=1)
        sga = _sig(z_ref[:, 5 * d:6 * d])
        sgb = _sig(z_ref[:, 6 * d:7 * d])
        dya = dmg * sga
        dyb = dmg * sgb
        accum(X_BAO, dya)
        put_dz(5, dya * ya_ref[...] * (1.0 - sga))
        put_dz(6, dyb * yb_ref[...] * (1.0 - sgb))
        dyab = dya.astype(BF16)
        dybb = dyb.astype(BF16)
        dya_ref[...] = dyab
        dyb_ref[...] = dybb

        du3 = jnp.concatenate([_dot_nt(dyab, wao_ref[q]) for q in range(nq)], axis=1)
        xa, rstda = _ln(u1_ref[...])
        u2 = xa * vec(V_LNAG) + vec(V_LNAB)
        s2 = _sig(u2)
        du2 = du3 * (s2 * (1.0 + u2 * (1.0 - s2)))
        accum(X_LNAG, du2 * xa)
        accum(X_LNAB, du2)
        du1 = _ln_bwd(du2 * vec(V_LNAG), xa, rstda)
        accum(X_CAB, du1)
        du1p_ref[SUBLANES:SUBLANES + tm, :] = du1
        sg = _sig(z_ref[:, d:2 * d])
        aval = z_ref[:, 0:d]
        ext_ref[HALO_A:HALO_A + tm, :] = aval * sg
        ext_ref[0:HALO_A, :] = jnp.where(has_history, zh_ref[:, 0:d] * _sig(zh_ref[:, d:2 * d]), 0.0)
        _conv_adjoint(du1p_ref, sh_ref, ext_ref, caw_ref, tmp_ref, dcaw_ref, CONV_A, HALO_A, tm, d, rc)
        du1p_ref[SUBLANES + tm:SUBLANES + tm + HALO_A, :] = du1p_ref[SUBLANES:SUBLANES + HALO_A, :]
        du0 = tmp_ref[...]
        put_dz(0, du0 * sg)
        put_dz(1, du0 * aval * sg * (1.0 - sg))

        dv = jnp.concatenate([_dot_nt(dybb, wbo_ref[q]) for q in range(nq)], axis=1)
        bgc = z_ref[:, 3 * d:4 * d]
        bx = z_ref[:, 4 * d:5 * d]
        pext_ref[HALO_B:HALO_B + tm, :] = bgc * bx
        pext_ref[0:HALO_B, :] = jnp.where(
            has_history, zh_ref[HALO_A - HALO_B:HALO_A, 3 * d:4 * d] * zh_ref[HALO_A - HALO_B:HALO_A, 4 * d:5 * d], 0.0)
        _conv_causal(pext_ref, sh_ref, cbw_ref, CONV_B, HALO_B, None, tmp_ref, tm, d, rc)
        put_dz(2, dv * tmp_ref[...])
        dqp_ref[SUBLANES:SUBLANES + tm, :] = dv * z_ref[:, 2 * d:3 * d]
        _conv_adjoint(dqp_ref, sh_ref, pext_ref, cbw_ref, tmp_ref, dcbw_ref, CONV_B, HALO_B, tm, d, rc)
        dqp_ref[SUBLANES + tm:SUBLANES + tm + HALO_B, :] = dqp_ref[SUBLANES:SUBLANES + HALO_B, :]
        dp = tmp_ref[...]
        put_dz(3, dp * bx)
        put_dz(4, dp * bgc)

    def rev(width):
        return pl.BlockSpec((tm, width), lambda i: (nt - 1 - i, 0))

    def wspec(j):
        return pl.BlockSpec((nq, kq, d), lambda i: (0, base + j, 0), pipeline_mode=pl.Buffered(1))

    halo = pl.BlockSpec((HALO_A, 7 * d), lambda i: (jnp.maximum((nt - 1 - i) * hb - 1, 0), 0))
    b16 = jax.ShapeDtypeStruct((s, d), BF16)
    acc8 = jax.ShapeDtypeStruct((SUBLANES, d), F32)
    return _pcall(
        body, name="mix_bwd", grid=(nt,),
        in_specs=[rev(d), rev(7 * d), halo, rev(d), rev(d), rev(d), _full(vecs.shape), _full(caw.shape), _full(cbw.shape),
                  wspec(0), wspec(1), wspec(2)],
        out_specs=[rev(7 * d), rev(d), rev(d), _full((SUBLANES, d)), _full((HALO_A, d)), _full((HALO_B, d)),
                   _full((SUBLANES, d))],
        out_shape=[jax.ShapeDtypeStruct((s, 7 * d), BF16), b16, b16, acc8,
                   jax.ShapeDtypeStruct((HALO_A, d), F32), jax.ShapeDtypeStruct((HALO_B, d), F32), acc8],
        scratch_shapes=[pltpu.VMEM((HALO_A + tm, d), F32), pltpu.VMEM((SUBLANES + tm + HALO_A, d), F32),
                        pltpu.VMEM((SHIFTED_COPIES, tm + HALO_A, d), F32), pltpu.VMEM((HALO_B + tm, d), F32),
                        pltpu.VMEM((SUBLANES + tm + HALO_B, d), F32), pltpu.VMEM((tm, d), F32)],
        compiler_params=_seq(1),
    )(dout1, z, z, u1, ya, yb, vecs, caw, cbw, wb, wb, wb)


def _in_bwd(dz, x, dxp, vecs, wa, tm, plans, plan_srcs):
    s, d = x.shape
    nq, _, nw = wa.shape
    nt = s // tm
    nc = len(plan_srcs)

    def body(dz_ref, x_ref, dxp_ref, v_ref, w_ref, *rest):
        src_refs, (gx_ref, acc_ref), dst_refs = rest[:nc], rest[nc:nc + 2], rest[nc + 2:2 * nc + 2]
        send_sems, recv_sems = rest[2 * nc + 2:]
        i = pl.program_id(0)
        comm = (src_refs, dst_refs, send_sems, recv_sems)
        _carried_start(plans, i == 0, comm)

        @pl.when(i == 0)
        def _():
            acc_ref[...] = jnp.zeros(acc_ref.shape, F32)

        dh1 = jnp.zeros((tm, d), F32)
        for q in range(nq):
            dh1 = dh1 + _dot_nt(dz_ref[:, q * nw:(q + 1) * nw], w_ref[q])
        xh, rstd = _ln(x_ref[...])
        acc_ref[I_SHIFT1:I_SHIFT1 + 1, :] += _rsum(dh1)
        acc_ref[I_SCALE1:I_SCALE1 + 1, :] += _rsum(dh1 * xh)
        gx_ref[...] = dxp_ref[...] + _ln_bwd(dh1 * (1.0 + v_ref[V_SCALE1:V_SCALE1 + 1, :]), xh, rstd)
        _carried_wait(plans, i == nt - 1, comm)

    any_spec = pl.BlockSpec(memory_space=pl.ANY)
    return _pcall(
        body, name="in_bwd", grid=(nt,),
        in_specs=[_rows(tm, nq * nw), _rows(tm, d), _rows(tm, d), _full(vecs.shape), _full(wa.shape, single=True)]
        + [any_spec] * nc,
        out_specs=[_rows(tm, d), _full((SUBLANES, d))] + [any_spec] * nc,
        out_shape=[jax.ShapeDtypeStruct((s, d), F32), jax.ShapeDtypeStruct((SUBLANES, d), F32)]
        + [plan.out_shape(p) for plan, p in zip(plans, plan_srcs)],
        scratch_shapes=_plan_sems(plans),
        compiler_params=_seq(1),
    )(dz, x, dxp, vecs, wa, *plan_srcs)


def _dw(a, b, split_a, ts, name, into=None, rows_total=None, row_block=0):
    s = a.shape[0]
    ka = a.shape[1] // N_CHIPS if split_a else a.shape[1]
    nb = b.shape[1] if split_a else b.shape[1] // N_CHIPS
    rows_total = ka if rows_total is None else rows_total

    def body(a_ref, b_ref, *rest):
        o_ref = rest[-1]

        @pl.when(pl.program_id(1) == 0)
        def _():
            o_ref[...] = jnp.zeros(o_ref.shape, F32)

        o_ref[...] += _dot_tn(a_ref[...], b_ref[...])

    a_spec = pl.BlockSpec((ts, ka), (lambda q, i: (i, q)) if split_a else (lambda q, i: (i, 0)))
    b_spec = pl.BlockSpec((ts, nb), (lambda q, i: (i, 0)) if split_a else (lambda q, i: (i, q)))
    extra = {} if into is None else dict(input_output_aliases={2: 0})
    return _pcall(
        body, name=name, grid=(N_CHIPS, s // ts),
        in_specs=[a_spec, b_spec] + ([] if into is None else [pl.BlockSpec(memory_space=pl.ANY)]),
        out_specs=pl.BlockSpec((None, ka, nb), lambda q, i: (q, row_block, 0)),
        out_shape=jax.ShapeDtypeStruct((N_CHIPS, rows_total, nb), F32),
        compiler_params=_seq(2), **extra,
    )(*((a, b) if into is None else (a, b, into)))


def _dw_carrying(a, b, ts, name, plan, plan_src):
    s, k = a.shape
    nb = b.shape[1] // N_CHIPS
    ns = s // ts

    def body(a_ref, b_ref, src_ref, o_ref, dst_ref, send_sems, recv_sems):
        q, i = pl.program_id(0), pl.program_id(1)
        comm = ([src_ref], [dst_ref], send_sems, recv_sems)
        _carried_start([plan], jnp.logical_and(q == 0, i == 0), comm)

        @pl.when(i == 0)
        def _():
            o_ref[...] = jnp.zeros(o_ref.shape, F32)

        o_ref[...] += _dot_tn(a_ref[...], b_ref[...])
        _carried_wait([plan], jnp.logical_and(q == N_CHIPS - 1, i == ns - 1), comm)

    any_spec = pl.BlockSpec(memory_space=pl.ANY)
    return _pcall(
        body, name=name, grid=(N_CHIPS, ns),
        in_specs=[pl.BlockSpec((ts, k), lambda q, i: (i, 0)), pl.BlockSpec((ts, nb), lambda q, i: (i, q)), any_spec],
        out_specs=[pl.BlockSpec((None, k, nb), lambda q, i: (q, 0, 0)), any_spec],
        out_shape=[jax.ShapeDtypeStruct((N_CHIPS, k, nb), F32), plan.out_shape(plan_src)],
        scratch_shapes=_plan_sems([plan]),
        compiler_params=_seq(2),
    )(a, b, plan_src)


def _dw_rows(a, b, ts, name, into, row_block):
    s, k = a.shape
    n = b.shape[1]
    kq = k // N_CHIPS

    def body(a_ref, b_ref, buf_ref, o_ref):
        @pl.when(pl.program_id(0) == 0)
        def _():
            o_ref[...] = jnp.zeros(o_ref.shape, F32)

        res = _dot_tn(a_ref[...], b_ref[...])
        for q in range(N_CHIPS):
            o_ref[q] += res[q * kq:(q + 1) * kq, :]

    return _pcall(
        body, name=name, grid=(s // ts,),
        in_specs=[_rows(ts, k), _rows(ts, n), pl.BlockSpec(memory_space=pl.ANY)],
        out_specs=pl.BlockSpec((N_CHIPS, kq, n), lambda i: (0, row_block, 0)),
        out_shape=jax.ShapeDtypeStruct(into.shape, F32), input_output_aliases={2: 0},
        compiler_params=_seq(1),
    )(a, b, into)


def _adam_math(w, g, m, v):
    m2 = ADAM_B1 * m + (1.0 - ADAM_B1) * g
    v2 = ADAM_B2 * v + (1.0 - ADAM_B2) * (g * g)
    m_hat = m2 / (1.0 - ADAM_B1 ** ADAM_STEP)
    v_hat = v2 / (1.0 - ADAM_B2 ** ADAM_STEP)
    delta = -ADAM_LR * (m_hat / (jnp.sqrt(v_hat) + ADAM_EPS) + ADAM_WD * w)
    return delta, m2, v2


def _adam(w, g, m, v, g_row0, tr, name):
    r, c = w.shape
    blk0 = g_row0 // tr

    def body(w_ref, g_ref, m_ref, v_ref, go_ref, d_ref, mo_ref, vo_ref):
        gv = g_ref[...]
        go_ref[...] = gv
        d_ref[...], mo_ref[...], vo_ref[...] = _adam_math(w_ref[...], gv, m_ref[...], v_ref[...])

    spec = _rows(tr, c)
    g_spec = pl.BlockSpec((tr, c), lambda i: (blk0 + i, 0))
    o = jax.ShapeDtypeStruct((r, c), F32)
    return _pcall(body, name=name, grid=(r // tr,), in_specs=[spec, g_spec, spec, spec], out_specs=[spec] * 4,
                  out_shape=[o, o, o, o], compiler_params=_seq(1))(w, g, m, v)


def _small_update(gathered_head, gathered, q_idx, small_w, small_m, small_v, conv_w, conv_m, conv_v):
    d = gathered.shape[2]
    ns = len(_SMALL)
    cw = conv_w[0].shape[1]
    conv_rows = ((T_CAW, CONV_A), (T_CBW, CONV_B))

    def body(q_ref, h_ref, g_ref, *refs):
        ins, outs = refs[:3 * (ns + 2)], refs[3 * (ns + 2):]
        tot_ref, loss_ref = outs[0], outs[1]
        outs = outs[2:]
        head, tot = h_ref[0], g_ref[0]
        for dev in range(1, N_DEV):
            head = head + h_ref[dev]
            tot = tot + g_ref[dev]
        tot_ref[0:T_M, :] = head
        tot_ref[T_M:T_ROWS, :] = tot
        loss_ref[...] = (0.5 / d) * jnp.sum(tot_ref[T_LOSS:T_LOSS + 1, :], axis=1, keepdims=True)
        for p, (_, rows) in enumerate(_SMALL):
            w_ref, m_ref, v_ref = ins[p], ins[ns + 2 + p], ins[2 * (ns + 2) + p]
            go, do, mo, vo = outs[4 * p:4 * p + 4]
            for j, row in enumerate(rows):
                sl = slice(j * d, (j + 1) * d)
                gv = tot_ref[row:row + 1, :]
                go[:, sl] = gv
                do[:, sl], mo[:, sl], vo[:, sl] = _adam_math(w_ref[:, sl], gv, m_ref[:, sl], v_ref[:, sl])
        for p, (row, taps) in enumerate(conv_rows):
            w_ref, m_ref, v_ref = ins[ns + p], ins[ns + 2 + ns + p], ins[2 * (ns + 2) + ns + p]
            go, do, mo, vo = outs[4 * (ns + p):4 * (ns + p) + 4]
            gv = tot_ref[row:row + taps, 0:cw]
            for qq in range(1, N_CHIPS):
                gv = jnp.where(q_ref[0] == qq, tot_ref[row:row + taps, qq * cw:(qq + 1) * cw], gv)
            go[...] = gv
            do[...], mo[...], vo[...] = _adam_math(w_ref[...], gv, m_ref[...], v_ref[...])

    params = list(small_w) + list(conv_w) + list(small_m) + list(conv_m) + list(small_v) + list(conv_v)
    out_shape = [jax.ShapeDtypeStruct((T_ROWS, d), F32), jax.ShapeDtypeStruct((1, 1), F32)]
    for w in list(small_w) + list(conv_w):
        out_shape += [jax.ShapeDtypeStruct(w.shape, F32)] * 4
    vm = pl.BlockSpec(memory_space=pltpu.VMEM)
    return _pcall(
        body, name="small_update", out_shape=out_shape,
        in_specs=[pl.BlockSpec(memory_space=pltpu.SMEM), vm, vm] + [vm] * len(params),
        out_specs=[vm] * len(out_shape), compiler_params=_cparams(),
    )(q_idx, gathered_head, gathered, *params)


def kernel(x, c, w_ada, b_ada, w_in, b_in, conv_a_w, conv_a_b, ln_a_g, ln_a_b, w_a_out, b_a_out, conv_b_w, w_b_out, w_o, b_o, ln1_g, ln1_b, w_up, b_up, w_down, b_down, ln2_g, ln2_b, loss_target, m_w_ada, m_b_ada, m_w_in, m_b_in, m_conv_a_w, m_conv_a_b, m_ln_a_g, m_ln_a_b, m_w_a_out, m_b_a_out, m_conv_b_w, m_w_b_out, m_w_o, m_b_o, m_ln1_g, m_ln1_b, m_w_up, m_b_up, m_w_down, m_b_down, m_ln2_g, m_ln2_b, v_w_ada, v_b_ada, v_w_in, v_b_in, v_conv_a_w, v_conv_a_b, v_ln_a_g, v_ln_a_b, v_w_a_out, v_b_a_out, v_conv_b_w, v_w_b_out, v_w_o, v_b_o, v_ln1_g, v_ln1_b, v_w_up, v_b_up, v_w_down, v_b_down, v_ln2_g, v_ln2_b):
    given = dict(locals())
    s, d = x.shape[1], x.shape[2]
    xi, yi, ci = _my_pos()
    q = 2 * xi + yi
    me = 4 * xi + 2 * yi + ci
    i32 = jnp.int32
    q_arr = jnp.reshape(q, (1,)).astype(i32)
    others = [2 * ox + oy for ox, oy in _other_chips(xi, yi)]
    halves_idx = jnp.stack([ci] + others).astype(i32)
    chips_idx = jnp.stack([q, ci]).astype(i32)
    kq = d // N_CHIPS
    tm = min(256, s)
    rc = tm

    def sq(a):
        return a.reshape(a.shape[1:])

    x2, tgt = sq(x), sq(loss_target)

    wa = _place_shard([sq(w_in)], q_arr, "place_w_in")
    wb = _place_shard([sq(w_up), sq(w_down), sq(w_a_out), sq(w_b_out), sq(w_o)], q_arr, "place_w_rest")

    n_ada = w_ada.shape[2]
    pre = jnp.concatenate([
        jnp.broadcast_to(c, (SUBLANES, d)),
        jnp.pad(sq(conv_a_w), ((0, HALO_A - CONV_A), (0, d - kq))),
        jnp.pad(sq(conv_b_w), ((0, HALO_B - CONV_B), (0, d - kq)))], axis=0)
    b_ada_sh = lax.dynamic_slice(b_ada, (0, q * n_ada), (1, n_ada))
    pre_all, c_all, mod_all, wa = _prologue(pre, sq(w_ada), b_ada_sh, wa)
    caw = jnp.concatenate([pre_all[2 * p, SUBLANES:SUBLANES + HALO_A, :kq] for p in range(N_CHIPS)], axis=1)
    cbw = jnp.concatenate([pre_all[2 * p, SUBLANES + HALO_A:, :kq] for p in range(N_CHIPS)], axis=1)
    mod_rows = lax.dynamic_slice(mod_all, (0, me, 0), (N_DEV, 1, n_ada))[0::2, 0, :]
    mod = mod_rows.reshape(6, d)
    vecs = jnp.concatenate([mod, conv_a_b, ln_a_g, ln_a_b, b_a_out, b_o, ln1_g, ln1_b, b_down, ln2_g, ln2_b], axis=0)

    h1, z, wb = _fwd_in(x2, vecs, wa, b_in, wb, tm)
    u1, ya, yb, out1, u3, vv, mg = _fwd_mix(z, vecs, caw, cbw, wb, tm, rc)

    ts = min(2048, s)
    rest_rows = wb.shape[1]
    small0 = 2 * d // kq
    h2, fb, df0, do2, do1, dxp, macc, dbup = _mlp_fwd_bwd(x2, out1, tgt, vecs, b_up, wb, tm)
    gb = _dw(h2, df0, False, ts, "dw_up", rows_total=rest_rows)
    gb = _dw(fb, do2, True, ts, "dw_down", into=gb, rows_total=rest_rows, row_block=1)
    dz, dya, dyb, xacc, dcaw, dcbw, dbin = _mix_bwd(do1, z, u1, ya, yb, vecs, caw, cbw, wb, tm, rc)
    gb = _dw_rows(u3, dya, ts, "dw_a_out", gb, small0)
    gb = _dw_rows(vv, dyb, ts, "dw_b_out", gb, small0 + 1)
    gb = _dw_rows(mg, do1, ts, "dw_o", gb, small0 + 2)

    trb, tra = rest_rows // 8, d // 8
    ga, rb = _dw_carrying(h1, dz, ts, "dw_in", _SiblingHalf, gb)
    pb, ra = _add_halves(gb, rb, halves_idx, trb, "rs_add_halves_rest", _SiblingHalf, ga)
    pa = _add_halves(ga, ra, halves_idx, tra, "rs_add_halves_in")
    table = jnp.concatenate([macc, dbin, dcaw, xacc, dcbw, dbup], axis=0)
    gx, iacc, r3a, r3b, gathered = _in_bwd(dz, x2, dxp, vecs, wa, tm, [_ChipBlocks, _ChipBlocks, _TableToAll],
                                           [pa, pb, table])
    fa = _add_chips(ga, ra, r3a, chips_idx, tra, "rs_add_chips_in")
    fb_ = _add_chips(gb, rb, r3b, chips_idx, trb, "rs_add_chips_rest")
    g_in, g_b = _rs_join_halves(fa, fb_)

    gathered_head = _all_gather_small(iacc, "gather_ln0_sums")
    names = [n for n, _ in _SMALL]
    res = _small_update(
        gathered_head, gathered, q_arr,
        [given[n] for n in names], [given["m_" + n] for n in names], [given["v_" + n] for n in names],
        [sq(conv_a_w), sq(conv_b_w)], [sq(m_conv_a_w), sq(m_conv_b_w)], [sq(v_conv_a_w), sq(v_conv_b_w)])
    loss = res[1].reshape(())
    upd = {}
    for p, n in enumerate(names + ["conv_a_w", "conv_b_w"]):
        upd[n] = res[2 + 4 * p:6 + 4 * p]

    dmod_all = jnp.stack([gathered_head[:, r, :] if r < T_M else gathered[:, r - T_M, :] for r in _SMALL[0][1]],
                         axis=1).reshape(N_DEV, 6 * d)
    dmod_sh = lax.dynamic_slice(dmod_all, (0, q * n_ada), (N_DEV, n_ada))
    g_ada = _ada_bwd(c_all.T, dmod_sh)
    upd["w_ada"] = _adam(sq(w_ada), g_ada, sq(m_w_ada), sq(v_w_ada), 0, min(256, d), "adam_w_ada")

    upd["w_in"] = _adam(sq(w_in), g_in, sq(m_w_in), sq(v_w_in), 0, min(256, d), "adam_w_in")
    r0 = 0
    for n in ("w_up", "w_down", "w_a_out", "w_b_out", "w_o"):
        w = sq(given[n])
        upd[n] = _adam(w, g_b, sq(given["m_" + n]), sq(given["v_" + n]), r0, min(256, w.shape[0]), "adam_" + n)
        r0 += w.shape[0]

    order = ["w_ada", "b_ada", "w_in", "b_in", "conv_a_w", "conv_a_b", "ln_a_g", "ln_a_b", "w_a_out", "b_a_out", "conv_b_w",
             "w_b_out", "w_o", "b_o", "ln1_g", "ln1_b", "w_up", "b_up", "w_down", "b_down", "ln2_g", "ln2_b"]
    outs = [loss, gx.reshape(x.shape)]
    for k in range(4):
        outs += [upd[n][k].reshape(given[n].shape) for n in order]
    return tuple(outs)
```

```python
import jax
import jax.numpy as jnp
from jax import lax
from jax.experimental import pallas as pl
from jax.experimental.pallas import tpu as pltpu

F32 = jnp.float32
BF16 = jnp.bfloat16
MESH = pl.DeviceIdType.MESH

LN_EPS = 1e-5
DEPTH = 1
ALPHA = (2.0 * DEPTH) ** 0.25
CONV_A = 31
CONV_B = 3
SUBLANES = 8
HALO_A = 32
HALO_B = 8
Z_RING = 3
SHIFTED_COPIES = 4
N_CHIPS = 4
N_DEV = 8
ADAM_LR = 0.001
ADAM_B1 = 0.9
ADAM_B2 = 0.999
ADAM_EPS = 1e-08
ADAM_WD = 0.01
ADAM_STEP = 10
VMEM_LIMIT = 56 * 1024 * 1024

V_SHIFT1, V_SCALE1, V_GATE1, V_SHIFT2, V_SCALE2, V_GATE2 = 0, 1, 2, 3, 4, 5
V_CAB, V_LNAG, V_LNAB, V_BAO, V_BO, V_LN1G, V_LN1B, V_BDN, V_LN2G, V_LN2B = 6, 7, 8, 9, 10, 11, 12, 13, 14, 15

M_LN2G, M_LN2B, M_GATE2, M_BDN, M_SHIFT2, M_SCALE2, M_LN1G, M_LN1B, M_GATE1, M_BO, M_LOSS = range(11)
X_BAO, X_LNAG, X_LNAB, X_CAB = range(4)
I_SHIFT1, I_SCALE1 = 0, 1

T_I, T_M, T_BIN, T_CAW, T_X, T_CBW, T_BUP, T_ROWS = 0, 8, 24, 32, 64, 72, 80, 88
T_LOSS = T_M + M_LOSS
_SMALL = (
    ("b_ada", (T_I + I_SHIFT1, T_I + I_SCALE1, T_M + M_GATE1, T_M + M_SHIFT2, T_M + M_SCALE2, T_M + M_GATE2)),
    ("b_in", tuple(T_BIN + j for j in range(7))),
    ("conv_a_b", (T_X + X_CAB,)), ("ln_a_g", (T_X + X_LNAG,)), ("ln_a_b", (T_X + X_LNAB,)), ("b_a_out", (T_X + X_BAO,)),
    ("b_o", (T_M + M_BO,)), ("ln1_g", (T_M + M_LN1G,)), ("ln1_b", (T_M + M_LN1B,)),
    ("b_up", tuple(T_BUP + j for j in range(4))),
    ("b_down", (T_M + M_BDN,)), ("ln2_g", (T_M + M_LN2G,)), ("ln2_b", (T_M + M_LN2B,)),
)


def _pcall(body, **kw):
    return pl.pallas_call(body, **kw)


def _cparams(**kw):
    return pltpu.CompilerParams(vmem_limit_bytes=VMEM_LIMIT, **kw)


def _seq(n):
    return _cparams(dimension_semantics=("arbitrary",) * n)


def _full(shape, single=False):
    nd = len(shape)
    if single:
        return pl.BlockSpec(shape, lambda *_: (0,) * nd, pipeline_mode=pl.Buffered(1))
    return pl.BlockSpec(shape, lambda *_: (0,) * nd)


def _rows(tm, width):
    return pl.BlockSpec((tm, width), lambda i: (i, 0))


def _sig(x):
    return jax.nn.sigmoid(x)


def _ln(x):
    mu = jnp.mean(x, axis=-1, keepdims=True)
    xc = x - mu
    var = jnp.mean(xc * xc, axis=-1, keepdims=True)
    rstd = lax.rsqrt(var + LN_EPS)
    return xc * rstd, rstd


def _ln_bwd(dxh, xh, rstd):
    m1 = jnp.mean(dxh, axis=-1, keepdims=True)
    m2 = jnp.mean(dxh * xh, axis=-1, keepdims=True)
    return rstd * (dxh - m1 - xh * m2)


def _rsum(v):
    return jnp.sum(v, axis=0, keepdims=True)


def _rsum_mxu(v):
    vb = v if v.dtype == BF16 else v.astype(BF16)
    return _dot(jnp.ones((2 * SUBLANES, v.shape[0]), BF16), vb)[0:1, :]


def _dot(a, b):
    return jnp.dot(a, b, preferred_element_type=F32)


def _dot_nt(a, b):
    return lax.dot_general(a, b, (((1,), (1,)), ((), ())), preferred_element_type=F32)


def _dot_tn(a, b):
    return lax.dot_general(a, b, (((0,), (0,)), ((), ())), preferred_element_type=F32)


def _my_pos():
    return lax.axis_index("x"), lax.axis_index("y"), lax.axis_index("c")


def _other_chips(x, y):
    return [(1 - x, y), (x, 1 - y), (1 - x, 1 - y)]


def _small_gather(v_ref, out_ref, send_sems, recv_sems, local_sem):
    x, y, cc = _my_pos()
    me = 4 * x + 2 * y + cc
    mine = pltpu.make_async_copy(v_ref, out_ref.at[me], local_sem)
    mine.start()
    sends = []
    for rel in range(1, N_DEV):
        rx, ry, rc = (rel >> 2) & 1, (rel >> 1) & 1, rel & 1
        peer = (1 - x if rx else x, 1 - y if ry else y, 1 - cc if rc else cc)
        cp = pltpu.make_async_remote_copy(
            src_ref=v_ref, dst_ref=out_ref.at[me], send_sem=send_sems.at[rel - 1], recv_sem=recv_sems.at[rel - 1],
            device_id=peer, device_id_type=MESH)
        cp.start()
        sends.append(cp)
    for rel in range(1, N_DEV):
        rx, ry, rc = (rel >> 2) & 1, (rel >> 1) & 1, rel & 1
        peer = (1 - x if rx else x, 1 - y if ry else y, 1 - cc if rc else cc)
        slot = 4 * peer[0] + 2 * peer[1] + peer[2]
        pltpu.make_async_remote_copy(
            src_ref=v_ref, dst_ref=out_ref.at[slot], send_sem=send_sems.at[rel - 1], recv_sem=recv_sems.at[rel - 1],
            device_id=peer, device_id_type=MESH).wait_recv()
    for cp in sends:
        cp.wait_send()
    mine.wait()


_SMALL_GATHER_SEMS = [pltpu.SemaphoreType.DMA((N_DEV - 1,)), pltpu.SemaphoreType.DMA((N_DEV - 1,)),
                      pltpu.SemaphoreType.DMA]


def _all_gather_small(v, name):
    r, c = v.shape

    def body(*refs):
        _small_gather(*refs)

    return _pcall(
        body, name=name,
        out_shape=jax.ShapeDtypeStruct((N_DEV, r, c), v.dtype),
        in_specs=[pl.BlockSpec(memory_space=pltpu.VMEM)],
        out_specs=pl.BlockSpec(memory_space=pltpu.VMEM),
        scratch_shapes=list(_SMALL_GATHER_SEMS),
        compiler_params=_cparams(),
    )(v)


def _prologue(pre, w_sh, b_sh, wa):
    r, d = pre.shape
    n_ada = w_sh.shape[1]
    ng = _GatherShards.n_sems

    def body(pre_ref, w_ref, b_ref, wai_ref, pre_all_ref, c_all_ref, mod_all_ref, wao_ref, mod_ref,
             s1, r1, l1, s2, r2, l2, sg, rg):
        gather = (wai_ref, wao_ref, sg, rg)
        _GatherShards.start(*gather)
        _small_gather(pre_ref, pre_all_ref, s1, r1, l1)
        cv = jnp.concatenate([pre_all_ref[dev, 0:1, :] for dev in range(N_DEV)], axis=0)
        c_all_ref[...] = cv
        ca = cv * _sig(cv)
        mod_ref[...] = jnp.dot(ca, w_ref[...], preferred_element_type=F32, precision=lax.Precision.HIGHEST) + b_ref[...]
        _small_gather(mod_ref, mod_all_ref, s2, r2, l2)
        _GatherShards.relay(*gather)
        _GatherShards.finish(*gather)

    vm = pl.BlockSpec(memory_space=pltpu.VMEM)
    any_spec = pl.BlockSpec(memory_space=pl.ANY)
    return _pcall(
        body, name="prologue",
        out_shape=[jax.ShapeDtypeStruct((N_DEV, r, d), F32), jax.ShapeDtypeStruct((N_DEV, d), F32),
                   jax.ShapeDtypeStruct((N_DEV, N_DEV, n_ada), F32), jax.ShapeDtypeStruct(wa.shape, wa.dtype)],
        in_specs=[vm, vm, vm, any_spec], out_specs=[vm, vm, vm, any_spec], input_output_aliases={3: 3},
        scratch_shapes=[pltpu.VMEM((N_DEV, n_ada), F32)] + list(_SMALL_GATHER_SEMS) + list(_SMALL_GATHER_SEMS)
        + [pltpu.SemaphoreType.DMA((ng,)), pltpu.SemaphoreType.DMA((ng,))],
        compiler_params=_cparams(),
    )(pre, w_sh, b_sh, wa)


def _place_shard(parts, q_idx, name):
    rows = sum(p.shape[0] for p in parts)
    w = parts[0].shape[1]

    def body(q_ref, *refs):
        o_ref = refs[-1]
        r0 = 0
        for p_ref in refs[:-1]:
            n = p_ref.shape[0]
            o_ref[r0:r0 + n, :] = p_ref[...].astype(BF16)
            r0 += n

    grid_spec = pltpu.PrefetchScalarGridSpec(
        num_scalar_prefetch=1, grid=(1,),
        in_specs=[pl.BlockSpec(p.shape, lambda i, q: (0, 0)) for p in parts],
        out_specs=pl.BlockSpec((None, rows, w), lambda i, q: (q[0], 0, 0)))
    return _pcall(body, name=name, grid_spec=grid_spec, out_shape=jax.ShapeDtypeStruct((N_CHIPS, rows, w), BF16),
                  compiler_params=_seq(1))(q_idx, *parts)


class _GatherShards:
    n_sems = 6

    @staticmethod
    def _half(ref, slot, h):
        rows = ref.shape[1] // 2
        return ref.at[slot, pl.ds(h * rows, rows)]

    @classmethod
    def _copies(cls, in_ref, out_ref, send_sems, recv_sems):
        x, y, c = _my_pos()
        q = 2 * x + y
        sibling = (x, y, 1 - c)
        sends, landed, forwards, passed = [], [], [], []
        for j, chip in enumerate(_other_chips(x, y)):
            qj = 2 * chip[0] + chip[1]

            def copy(src, dst, k, to):
                return pltpu.make_async_remote_copy(src_ref=src, dst_ref=dst, send_sem=send_sems.at[k],
                                                    recv_sem=recv_sems.at[k], device_id=to, device_id_type=MESH)

            mine, theirs = cls._half(out_ref, qj, c), cls._half(out_ref, qj, 1 - c)
            sends.append(copy(cls._half(in_ref, q, c), cls._half(out_ref, q, c), j, (*chip, c)))
            landed.append(copy(mine, mine, j, (*chip, c)))
            forwards.append(copy(mine, mine, 3 + j, sibling))
            passed.append(copy(theirs, theirs, 3 + j, sibling))
        return sends, landed, forwards, passed

    @classmethod
    def start(cls, *refs):
        for cp in cls._copies(*refs)[0]:
            cp.start()

    @classmethod
    def relay(cls, *refs):
        _, landed, forwards, _ = cls._copies(*refs)
        for arrived, onward in zip(landed, forwards):
            arrived.wait_recv()
            onward.start()

    @classmethod
    def finish(cls, *refs):
        sends, _, forwards, passed = cls._copies(*refs)
        for cp in passed:
            cp.wait_recv()
        for cp in sends + forwards:
            cp.wait_send()


class _SiblingHalf:
    n_sems = 1

    @staticmethod
    def out_shape(g):
        return jax.ShapeDtypeStruct((g.shape[0], g.shape[1] // 2, g.shape[2]), g.dtype)

    @staticmethod
    def copies(g_ref, r_ref, send_sems, recv_sems, base):
        x, y, c = _my_pos()
        rows = r_ref.shape[1]
        return [pltpu.make_async_remote_copy(
            src_ref=g_ref.at[:, pl.ds((1 - c) * rows, rows)], dst_ref=r_ref,
            send_sem=send_sems.at[base], recv_sem=recv_sems.at[base], device_id=(x, y, 1 - c), device_id_type=MESH)]


class _ChipBlocks:
    n_sems = 3

    @staticmethod
    def out_shape(p):
        return jax.ShapeDtypeStruct(p.shape, p.dtype)

    @staticmethod
    def copies(p_ref, r_ref, send_sems, recv_sems, base):
        x, y, c = _my_pos()
        return [pltpu.make_async_remote_copy(
            src_ref=p_ref.at[j], dst_ref=r_ref.at[j], send_sem=send_sems.at[base + j], recv_sem=recv_sems.at[base + j],
            device_id=(*chip, c), device_id_type=MESH) for j, chip in enumerate(_other_chips(x, y))]


class _TableToAll:
    n_sems = N_DEV

    @staticmethod
    def out_shape(t):
        return jax.ShapeDtypeStruct((N_DEV,) + t.shape, t.dtype)

    @staticmethod
    def copies(t_ref, all_ref, send_sems, recv_sems, base):
        x, y, c = _my_pos()
        me = 4 * x + 2 * y + c
        cps = [pltpu.make_async_copy(t_ref, all_ref.at[me], send_sems.at[base + N_DEV - 1])]
        for rel in range(1, N_DEV):
            rx, ry, rc = (rel >> 2) & 1, (rel >> 1) & 1, rel & 1
            peer = (1 - x if rx else x, 1 - y if ry else y, 1 - c if rc else c)
            cps.append(_SlotCopy(t_ref, all_ref, me, 4 * peer[0] + 2 * peer[1] + peer[2], peer,
                                 send_sems.at[base + rel - 1], recv_sems.at[base + rel - 1]))
        return cps


class _SlotCopy:
    def __init__(self, src_ref, all_ref, my_slot, peer_slot, peer, send_sem, recv_sem):
        self._send = pltpu.make_async_remote_copy(src_ref=src_ref, dst_ref=all_ref.at[my_slot], send_sem=send_sem,
                                                  recv_sem=recv_sem, device_id=peer, device_id_type=MESH)
        self._recv = pltpu.make_async_remote_copy(src_ref=src_ref, dst_ref=all_ref.at[peer_slot], send_sem=send_sem,
                                                  recv_sem=recv_sem, device_id=peer, device_id_type=MESH)

    def start(self):
        self._send.start()

    def wait(self):
        self._send.wait_send()
        self._recv.wait_recv()


def _plan_copies(plans, src_refs, dst_refs, send_sems, recv_sems):
    cps, base = [], 0
    for plan, s_ref, d_ref in zip(plans, src_refs, dst_refs):
        cps += plan.copies(s_ref, d_ref, send_sems, recv_sems, base)
        base += plan.n_sems
    return cps


def _plan_sems(plans):
    n = sum(p.n_sems for p in plans)
    return [pltpu.SemaphoreType.DMA((n,)), pltpu.SemaphoreType.DMA((n,))]


def _carried_start(plans, first, comm):
    @pl.when(first)
    def _():
        for cp in _plan_copies(plans, *comm):
            cp.start()


def _carried_wait(plans, last, comm):
    @pl.when(last)
    def _():
        for cp in _plan_copies(plans, *comm):
            cp.wait()


def _rs_join_halves(fa, fb):
    bufs = (fa, fb)
    nb = len(bufs)

    def body(a_ref, b_ref, ao_ref, bo_ref, send_sems, recv_sems):
        x, y, c = _my_pos()
        srcs, outs = (a_ref, b_ref), (ao_ref, bo_ref)
        cps = []
        for b in range(nb):
            rows = srcs[b].shape[0] // 2
            cp = pltpu.make_async_remote_copy(
                src_ref=srcs[b].at[pl.ds(c * rows, rows)], dst_ref=outs[b].at[pl.ds(c * rows, rows)],
                send_sem=send_sems.at[b], recv_sem=recv_sems.at[b], device_id=(x, y, 1 - c), device_id_type=MESH)
            cp.start()
            cps.append(cp)
        for b in range(nb):
            rows = srcs[b].shape[0] // 2
            theirs = outs[b].at[pl.ds((1 - c) * rows, rows)]
            pltpu.make_async_remote_copy(
                src_ref=theirs, dst_ref=theirs, send_sem=send_sems.at[b], recv_sem=recv_sems.at[b],
                device_id=(x, y, 1 - c), device_id_type=MESH).wait_recv()
        for cp in cps:
            cp.wait_send()

    any_spec = pl.BlockSpec(memory_space=pl.ANY)
    return _pcall(
        body, name="rs_join_halves",
        out_shape=[jax.ShapeDtypeStruct(b.shape, b.dtype) for b in bufs],
        in_specs=[any_spec] * nb, out_specs=[any_spec] * nb, input_output_aliases={0: 0, 1: 1},
        scratch_shapes=[pltpu.SemaphoreType.DMA((nb,)), pltpu.SemaphoreType.DMA((nb,))],
        compiler_params=_cparams(),
    )(*bufs)


def _add_halves(g, r, idx, tr, name, plan=None, plan_src=None):
    _, rows, w = r.shape
    nt = rows // tr

    def body(i_ref, g_ref, r_ref, *rest):
        o_ref = rest[0] if plan is None else rest[1]
        if plan is not None:
            comm = ([rest[0]], [rest[2]], rest[3], rest[4])
            first = jnp.logical_and(pl.program_id(0) == 0, pl.program_id(1) == 0)
            _carried_start([plan], first, comm)
        o_ref[...] = (g_ref[...] + r_ref[...]).astype(BF16)
        if plan is not None:
            last = jnp.logical_and(pl.program_id(0) == 2, pl.program_id(1) == nt - 1)
            _carried_wait([plan], last, comm)

    any_spec = pl.BlockSpec(memory_space=pl.ANY)
    carried = plan is not None
    grid_spec = pltpu.PrefetchScalarGridSpec(
        num_scalar_prefetch=1, grid=(3, nt),
        in_specs=[pl.BlockSpec((None, tr, w), lambda j, i, ix: (ix[1 + j], ix[0] * nt + i, 0)),
                  pl.BlockSpec((None, tr, w), lambda j, i, ix: (ix[1 + j], i, 0))] + [any_spec] * carried,
        out_specs=[pl.BlockSpec((None, tr, w), lambda j, i, ix: (j, i, 0))] + [any_spec] * carried,
        scratch_shapes=_plan_sems([plan]) if carried else [])
    out_shape = [jax.ShapeDtypeStruct((3, rows, w), BF16)] + ([plan.out_shape(plan_src)] if carried else [])
    res = _pcall(body, name=name, grid_spec=grid_spec, out_shape=out_shape,
                 compiler_params=_seq(2))(*((idx, g, r, plan_src) if carried else (idx, g, r)))
    return res if carried else res[0]


def _add_chips(g, r, r3, idx, tr, name):
    _, rows, w = r.shape
    nt = rows // tr

    def body(i_ref, g_ref, r_ref, a_ref, b_ref, c_ref, o_ref):
        own = g_ref[...] + r_ref[...]
        o_ref[...] = ((own + a_ref[...].astype(F32)) + b_ref[...].astype(F32)) + c_ref[...].astype(F32)

    def other(j):
        return pl.BlockSpec((None, tr, w), lambda i, ix: (j, i, 0))

    grid_spec = pltpu.PrefetchScalarGridSpec(
        num_scalar_prefetch=1, grid=(nt,),
        in_specs=[pl.BlockSpec((None, tr, w), lambda i, ix: (ix[0], ix[1] * nt + i, 0)),
                  pl.BlockSpec((None, tr, w), lambda i, ix: (ix[0], i, 0)), other(0), other(1), other(2)],
        out_specs=pl.BlockSpec((tr, w), lambda i, ix: (ix[1] * nt + i, 0)))
    return _pcall(body, name=name, grid_spec=grid_spec,
                  out_shape=jax.ShapeDtypeStruct((2 * rows, w), F32), compiler_params=_seq(1))(idx, g, r, r3, r3, r3)


def _ada_bwd(c_all_t, dmod_sh):
    def body(c_ref, d_ref, o_ref):
        cv = c_ref[...]
        ca = cv * _sig(cv)
        o_ref[...] = jnp.dot(ca, d_ref[...], preferred_element_type=F32, precision=lax.Precision.HIGHEST)

    return _pcall(body, name="ada_bwd", out_shape=jax.ShapeDtypeStruct((c_all_t.shape[0], dmod_sh.shape[1]), F32),
                  compiler_params=_cparams())(c_all_t, dmod_sh)


def _residue_passes(residues, copies):
    passes, current, used = [], [], 0
    for s in residues:
        if s != 0 and used == copies:
            passes.append(current)
            current, used = [], 0
        current.append(s)
        used += s != 0
    return passes + [current] if current else passes


def _conv_causal(ext_ref, sh_ref, w_ref, ntaps, halo, bias, out_ref, tm, d, rc):
    off = halo - (ntaps - 1)
    n = tm + halo - SUBLANES
    taps = {s: [(k, (off + k) // SUBLANES * SUBLANES) for k in range(ntaps) if (off + k) % SUBLANES == s]
            for s in range(SUBLANES)}
    started = False
    for group in _residue_passes([s for s in range(SUBLANES) if taps[s]], sh_ref.shape[0]):
        srcs = {}
        for s in group:
            if s == 0:
                srcs[s] = ext_ref
            else:
                j = len(srcs) - (0 in srcs)
                sh_ref[j, 0:n, :] = ext_ref[s:s + n, :]
                srcs[s] = sh_ref.at[j]
        for r0 in range(0, tm, rc):
            if started:
                acc = out_ref[r0:r0 + rc, :]
            else:
                acc = jnp.zeros((rc, d), F32) if bias is None else jnp.broadcast_to(bias, (rc, d))
            for s, src in srcs.items():
                for k, a in taps[s]:
                    acc = acc + w_ref[k:k + 1, :] * src[r0 + a:r0 + a + rc, :]
            out_ref[r0:r0 + rc, :] = acc
        started = True


def _conv_adjoint(dp_ref, sh_ref, ext_ref, w_ref, dx_ref, dw_ref, ntaps, halo, tm, d, rc):
    off = halo - (ntaps - 1)
    lead = SUBLANES + ntaps - 1
    n = tm + halo
    row = lax.broadcasted_iota(jnp.int32, (SUBLANES, d), 0)
    taps = {s: [(k, (lead - k) // SUBLANES * SUBLANES) for k in range(ntaps) if (lead - k) % SUBLANES == s]
            for s in range(SUBLANES)}
    wtaps = {s: [(k, s + off + k - SUBLANES) for k in range(ntaps) if (-(off + k)) % SUBLANES == s]
             for s in range(SUBLANES)}
    rw = 2 * SUBLANES
    started = False
    for group in _residue_passes([s for s in range(SUBLANES) if taps[s] or wtaps[s]], sh_ref.shape[0]):
        srcs = {}
        for s in group:
            if s == 0:
                srcs[s] = dp_ref
            else:
                j = len(srcs) - (0 in srcs)
                sh_ref[j, 0:n, :] = dp_ref[s:s + n, :]
                srcs[s] = sh_ref.at[j]
        if any(taps[s] for s in srcs):
            for r0 in range(0, tm, rc):
                acc = dx_ref[r0:r0 + rc, :] if started else jnp.zeros((rc, d), F32)
                for s, src in srcs.items():
                    for k, a in taps[s]:
                        acc = acc + w_ref[k:k + 1, :] * src[r0 + a:r0 + a + rc, :]
                dx_ref[r0:r0 + rc, :] = acc
            started = True
        for s, src in srcs.items():
            if not wtaps[s]:
                continue
            sums = [jnp.zeros((SUBLANES, d), F32) for _ in wtaps[s]]
            for r0 in range(0, tm, rw):
                g = src[r0:r0 + rw, :]
                for j, (_, e) in enumerate(wtaps[s]):
                    p = g * ext_ref[r0 + e:r0 + e + rw, :]
                    for r8 in range(0, rw, SUBLANES):
                        sums[j] = sums[j] + p[r8:r8 + SUBLANES, :]
            for j, (k, e) in enumerate(wtaps[s]):
                tail = src[tm:tm + SUBLANES, :] * ext_ref[tm + e:tm + e + SUBLANES, :]
                dw_ref[k:k + 1, :] += _rsum(sums[j] + jnp.where(row < SUBLANES - s, tail, 0.0))


def _fwd_in(x, vecs, wa, b_in, wb, tm):
    s, d = x.shape
    nq, _, nw = wa.shape
    nt = s // tm

    def body(x_ref, v_ref, w_ref, b_ref, wbi_ref, h_ref, z_ref, wbo_ref, send_sems, recv_sems):
        i = pl.program_id(0)
        gather = (wbi_ref, wbo_ref, send_sems, recv_sems)
        pl.when(i == 0)(lambda: _GatherShards.start(*gather))
        pl.when(i == nt // 2)(lambda: _GatherShards.relay(*gather))
        xh, _ = _ln(x_ref[...])
        h = (xh * (1.0 + v_ref[V_SCALE1:V_SCALE1 + 1, :]) + v_ref[V_SHIFT1:V_SHIFT1 + 1, :]).astype(BF16)
        h_ref[...] = h
        for q in range(nq):
            z_ref[:, q * nw:(q + 1) * nw] = _dot(h, w_ref[q]) + b_ref[:, q * nw:(q + 1) * nw]
        pl.when(i == nt - 1)(lambda: _GatherShards.finish(*gather))

    any_spec = pl.BlockSpec(memory_space=pl.ANY)
    n = _GatherShards.n_sems
    return _pcall(
        body, name="fwd_in", grid=(nt,),
        in_specs=[_rows(tm, d), _full(vecs.shape), _full(wa.shape, single=True), _full(b_in.shape), any_spec],
        out_specs=[_rows(tm, d), _rows(tm, nq * nw), any_spec],
        out_shape=[jax.ShapeDtypeStruct((s, d), BF16), jax.ShapeDtypeStruct((s, nq * nw), F32),
                   jax.ShapeDtypeStruct(wb.shape, wb.dtype)],
        input_output_aliases={4: 2},
        scratch_shapes=[pltpu.SemaphoreType.DMA((n,)), pltpu.SemaphoreType.DMA((n,))],
        compiler_params=_seq(1),
    )(x, vecs, wa, b_in, wb)


def _fwd_mix(z, vecs, caw, cbw, wb, tm, rc):
    s = z.shape[0]
    d = vecs.shape[1]
    nq = wb.shape[0]
    kq = d // nq
    base = 2 * d // kq

    nt = s // tm

    def body(z_hbm, v_ref, caw_ref, cbw_ref, wao_ref, wbo_ref, wo_ref,
             u1_ref, ya_ref, yb_ref, o1_ref, u3_ref, vv_ref, mg_ref, ext_ref, sh_ref, pext_ref, q_ref, zbuf, zsem):
        i = pl.program_id(0)

        def fetch(t, slot):
            return pltpu.make_async_copy(z_hbm.at[pl.ds(pl.multiple_of(t * tm, tm), tm)], zbuf.at[slot], zsem.at[slot])

        @pl.when(i == 0)
        def _():
            ext_ref[0:HALO_A, :] = jnp.zeros((HALO_A, d), F32)
            pext_ref[0:HALO_B, :] = jnp.zeros((HALO_B, d), F32)
            for t in range(min(Z_RING - 1, nt)):
                fetch(t, t).start()

        @pl.when(i + Z_RING - 1 < nt)
        def _():
            fetch(i + Z_RING - 1, lax.rem(i + Z_RING - 1, Z_RING)).start()

        slot = lax.rem(i, Z_RING)
        fetch(i, slot).wait()
        z_ref = zbuf.at[slot]

        ext_ref[HALO_A:HALO_A + tm, :] = z_ref[:, 0:d] * _sig(z_ref[:, d:2 * d])
        _conv_causal(ext_ref, sh_ref, caw_ref, CONV_A, HALO_A, v_ref[V_CAB:V_CAB + 1, :], u1_ref, tm, d, rc)
        ext_ref[0:HALO_A, :] = ext_ref[tm:tm + HALO_A, :]
        xa, _ = _ln(u1_ref[...])
        u2 = xa * v_ref[V_LNAG:V_LNAG + 1, :] + v_ref[V_LNAB:V_LNAB + 1, :]
        u3 = (u2 * _sig(u2)).astype(BF16)
        u3_ref[...] = u3
        ya = jnp.broadcast_to(v_ref[V_BAO:V_BAO + 1, :], (tm, d))
        for q in range(nq):
            ya = ya + _dot(u3[:, q * kq:(q + 1) * kq], wao_ref[q])
        ya_ref[...] = ya

        pext_ref[HALO_B:HALO_B + tm, :] = z_ref[:, 3 * d:4 * d] * z_ref[:, 4 * d:5 * d]
        _conv_causal(pext_ref, sh_ref, cbw_ref, CONV_B, HALO_B, None, q_ref, tm, d, rc)
        pext_ref[0:HALO_B, :] = pext_ref[tm:tm + HALO_B, :]
        vv = (z_ref[:, 2 * d:3 * d] * q_ref[...]).astype(BF16)
        vv_ref[...] = vv
        yb = jnp.zeros((tm, d), F32)
        for q in range(nq):
            yb = yb + _dot(vv[:, q * kq:(q + 1) * kq], wbo_ref[q])
        yb_ref[...] = yb

        mg = (_sig(z_ref[:, 5 * d:6 * d]) * ya + _sig(z_ref[:, 6 * d:7 * d]) * yb).astype(BF16)
        mg_ref[...] = mg
        o1 = jnp.broadcast_to(v_ref[V_BO:V_BO + 1, :], (tm, d))
        for q in range(nq):
            o1 = o1 + _dot(mg[:, q * kq:(q + 1) * kq], wo_ref[q])
        o1_ref[...] = o1

    def wspec(j):
        return pl.BlockSpec((nq, kq, d), lambda i: (0, base + j, 0), pipeline_mode=pl.Buffered(1))

    f32o = jax.ShapeDtypeStruct((s, d), F32)
    b16o = jax.ShapeDtypeStruct((s, d), BF16)
    return _pcall(
        body, name="fwd_mix", grid=(s // tm,),
        in_specs=[pl.BlockSpec(memory_space=pl.ANY), _full(vecs.shape), _full(caw.shape), _full(cbw.shape),
                  wspec(0), wspec(1), wspec(2)],
        out_specs=[_rows(tm, d)] * 7,
        out_shape=[f32o, f32o, f32o, f32o, b16o, b16o, b16o],
        scratch_shapes=[pltpu.VMEM((HALO_A + tm, d), F32), pltpu.VMEM((SHIFTED_COPIES, HALO_A + tm, d), F32),
                        pltpu.VMEM((HALO_B + tm, d), F32), pltpu.VMEM((tm, d), F32),
                        pltpu.VMEM((Z_RING, tm, 7 * d), F32), pltpu.SemaphoreType.DMA((Z_RING,))],
        compiler_params=_seq(1),
    )(z, vecs, caw, cbw, wb, wb, wb)


def _mlp_fwd_bwd(x, out1, tgt, vecs, b_up, wb, tm):
    s, d = x.shape
    nq = wb.shape[0]
    dff = nq * d

    def body(x_ref, o1_ref, t_ref, v_ref, bup_ref, wup_ref, wdn_ref,
             h2_ref, f_ref, df0_ref, do2_ref, do1_ref, dxp_ref, acc_ref, dbup_ref, f0_ref):
        @pl.when(pl.program_id(0) == 0)
        def _():
            acc_ref[...] = jnp.zeros(acc_ref.shape, F32)
            dbup_ref[...] = jnp.zeros(dbup_ref.shape, F32)

        def vec(r):
            return v_ref[r:r + 1, :]

        def accum(r, val):
            acc_ref[r:r + 1, :] += _rsum(val)

        out1v = o1_ref[...]
        r1 = ALPHA * x_ref[...] + (1.0 + vec(V_GATE1)) * out1v
        xh1, rstd1 = _ln(r1)
        x1 = xh1 * vec(V_LN1G) + vec(V_LN1B)
        xn1, rstdn = _ln(x1)
        h2 = (xn1 * (1.0 + vec(V_SCALE2)) + vec(V_SHIFT2)).astype(BF16)
        h2_ref[...] = h2
        out2 = jnp.broadcast_to(vec(V_BDN), (tm, d))
        for q in range(nq):
            f0 = _dot(h2, wup_ref[q]) + bup_ref[:, q * d:(q + 1) * d]
            rl = jnp.maximum(f0, 0.0)
            f0_ref[:, q * d:(q + 1) * d] = rl
            fb = (rl * rl).astype(BF16)
            f_ref[:, q * d:(q + 1) * d] = fb
            out2 = out2 + _dot(fb, wdn_ref[q])
        r2 = ALPHA * x1 + (1.0 + vec(V_GATE2)) * out2
        xh2, rstd2 = _ln(r2)
        yv = xh2 * vec(V_LN2G) + vec(V_LN2B)
        err = yv - t_ref[...]
        accum(M_LOSS, err * err)
        dy = err * (1.0 / d)
        accum(M_LN2G, dy * xh2)
        accum(M_LN2B, dy)
        dr2 = _ln_bwd(dy * vec(V_LN2G), xh2, rstd2)
        accum(M_GATE2, dr2 * out2)
        dout2 = (1.0 + vec(V_GATE2)) * dr2
        accum(M_BDN, dout2)
        do2b = dout2.astype(BF16)
        do2_ref[...] = do2b
        dh2 = jnp.zeros((tm, d), F32)
        for q in range(nq):
            df0 = _dot_nt(do2b, wdn_ref[q]) * (2.0 * f0_ref[:, q * d:(q + 1) * d])
            dbup_ref[q:q + 1, :] += _rsum(df0)
            df0b = df0.astype(BF16)
            df0_ref[:, q * d:(q + 1) * d] = df0b
            dh2 = dh2 + _dot_nt(df0b, wup_ref[q])
        accum(M_SHIFT2, dh2)
        accum(M_SCALE2, dh2 * xn1)
        dx1 = ALPHA * dr2 + _ln_bwd(dh2 * (1.0 + vec(V_SCALE2)), xn1, rstdn)
        accum(M_LN1G, dx1 * xh1)
        accum(M_LN1B, dx1)
        dr1 = _ln_bwd(dx1 * vec(V_LN1G), xh1, rstd1)
        accum(M_GATE1, dr1 * out1v)
        dout1 = (1.0 + vec(V_GATE1)) * dr1
        accum(M_BO, dout1)
        do1_ref[...] = dout1.astype(BF16)
        dxp_ref[...] = ALPHA * dr1

    def wspec(j):
        return pl.BlockSpec((nq, d, d), lambda i: (0, j, 0), pipeline_mode=pl.Buffered(1))

    b16 = lambda w: jax.ShapeDtypeStruct((s, w), BF16)
    return _pcall(
        body, name="mlp_fwd_bwd", grid=(s // tm,),
        in_specs=[_rows(tm, d), _rows(tm, d), _rows(tm, d), _full(vecs.shape), _full(b_up.shape), wspec(0), wspec(1)],
        out_specs=[_rows(tm, d), _rows(tm, dff), _rows(tm, dff), _rows(tm, d), _rows(tm, d), _rows(tm, d),
                   _full((16, d)), _full((SUBLANES, d))],
        out_shape=[b16(d), b16(dff), b16(dff), b16(d), b16(d), jax.ShapeDtypeStruct((s, d), F32),
                   jax.ShapeDtypeStruct((16, d), F32), jax.ShapeDtypeStruct((SUBLANES, d), F32)],
        scratch_shapes=[pltpu.VMEM((tm, dff), F32)],
        compiler_params=_seq(1),
    )(x, out1, tgt, vecs, b_up, wb, wb)


def _mix_bwd(dout1, z, u1, ya, yb, vecs, caw, cbw, wb, tm, rc):
    s = z.shape[0]
    d = vecs.shape[1]
    nq = wb.shape[0]
    kq = d // nq
    base = 2 * d // kq
    nt = s // tm
    hb = tm // HALO_A

    def body(do1_ref, z_ref, zh_ref, u1_ref, ya_ref, yb_ref, v_ref, caw_ref, cbw_ref, wao_ref, wbo_ref, wo_ref,
             dz_ref, dya_ref, dyb_ref, acc_ref, dcaw_ref, dcbw_ref, dbin_ref,
             ext_ref, du1p_ref, sh_ref, pext_ref, dqp_ref, tmp_ref):
        i = pl.program_id(0)

        @pl.when(i == 0)
        def _():
            acc_ref[...] = jnp.zeros(acc_ref.shape, F32)
            dcaw_ref[...] = jnp.zeros(dcaw_ref.shape, F32)
            dcbw_ref[...] = jnp.zeros(dcbw_ref.shape, F32)
            dbin_ref[...] = jnp.zeros(dbin_ref.shape, F32)
            du1p_ref[0:SUBLANES, :] = jnp.zeros((SUBLANES, d), F32)
            du1p_ref[SUBLANES + tm:SUBLANES + tm + HALO_A, :] = jnp.zeros((HALO_A, d), F32)
            dqp_ref[0:SUBLANES, :] = jnp.zeros((SUBLANES, d), F32)
            dqp_ref[SUBLANES + tm:SUBLANES + tm + HALO_B, :] = jnp.zeros((HALO_B, d), F32)

        def vec(r):
            return v_ref[r:r + 1, :]

        def accum(r, val):
            acc_ref[r:r + 1, :] += _rsum(val)

        def put_dz(j, val):
            vb = val.astype(BF16)
            dz_ref[:, j * d:(j + 1) * d] = vb
            dbin_ref[j:j + 1, :] += _rsum_mxu(vb)

        has_history = i < nt - 1

        do1 = do1_ref[...]
        dmg = jnp.concatenate([_dot_nt(do1, wo_ref[q]) for q in range(nq)], axis=1)
        sga = _sig(z_ref[:, 5 * d:6 * d])
        sgb = _sig(z_ref[:, 6 * d:7 * d])
        dya = dmg * sga
        dyb = dmg * sgb
        accum(X_BAO, dya)
        put_dz(5, dya * ya_ref[...] * (1.0 - sga))
        put_dz(6, dyb * yb_ref[...] * (1.0 - sgb))
        dyab = dya.astype(BF16)
        dybb = dyb.astype(BF16)
        dya_ref[...] = dyab
        dyb_ref[...] = dybb

        du3 = jnp.concatenate([_dot_nt(dyab, wao_ref[q]) for q in range(nq)], axis=1)
        xa, rstda = _ln(u1_ref[...])
        u2 = xa * vec(V_LNAG) + vec(V_LNAB)
        s2 = _sig(u2)
        du2 = du3 * (s2 * (1.0 + u2 * (1.0 - s2)))
        accum(X_LNAG, du2 * xa)
        accum(X_LNAB, du2)
        du1 = _ln_bwd(du2 * vec(V_LNAG), xa, rstda)
        accum(X_CAB, du1)
        du1p_ref[SUBLANES:SUBLANES + tm, :] = du1
        sg = _sig(z_ref[:, d:2 * d])
        aval = z_ref[:, 0:d]
        ext_ref[HALO_A:HALO_A + tm, :] = aval * sg
        ext_ref[0:HALO_A, :] = jnp.where(has_history, zh_ref[:, 0:d] * _sig(zh_ref[:, d:2 * d]), 0.0)
        _conv_adjoint(du1p_ref, sh_ref, ext_ref, caw_ref, tmp_ref, dcaw_ref, CONV_A, HALO_A, tm, d, rc)
        du1p_ref[SUBLANES + tm:SUBLANES + tm + HALO_A, :] = du1p_ref[SUBLANES:SUBLANES + HALO_A, :]
        du0 = tmp_ref[...]
        put_dz(0, du0 * sg)
        put_dz(1, du0 * aval * sg * (1.0 - sg))

        dv = jnp.concatenate([_dot_nt(dybb, wbo_ref[q]) for q in range(nq)], axis=1)
        bgc = z_ref[:, 3 * d:4 * d]
        bx = z_ref[:, 4 * d:5 * d]
        pext_ref[HALO_B:HALO_B + tm, :] = bgc * bx
        pext_ref[0:HALO_B, :] = jnp.where(
            has_history, zh_ref[HALO_A - HALO_B:HALO_A, 3 * d:4 * d] * zh_ref[HALO_A - HALO_B:HALO_A, 4 * d:5 * d], 0.0)
        _conv_causal(pext_ref, sh_ref, cbw_ref, CONV_B, HALO_B, None, tmp_ref, tm, d, rc)
        put_dz(2, dv * tmp_ref[...])
        dqp_ref[SUBLANES:SUBLANES + tm, :] = dv * z_ref[:, 2 * d:3 * d]
        _conv_adjoint(dqp_ref, sh_ref, pext_ref, cbw_ref, tmp_ref, dcbw_ref, CONV_B, HALO_B, tm, d, rc)
        dqp_ref[SUBLANES + tm:SUBLANES + tm + HALO_B, :] = dqp_ref[SUBLANES:SUBLANES + HALO_B, :]
        dp = tmp_ref[...]
        put_dz(3, dp * bx)
        put_dz(4, dp * bgc)

    def rev(width):
        return pl.BlockSpec((tm, width), lambda i: (nt - 1 - i, 0))

    def wspec(j):
        return pl.BlockSpec((nq, kq, d), lambda i: (0, base + j, 0), pipeline_mode=pl.Buffered(1))

    halo = pl.BlockSpec((HALO_A, 7 * d), lambda i: (jnp.maximum((nt - 1 - i) * hb - 1, 0), 0))
    b16 = jax.ShapeDtypeStruct((s, d), BF16)
    acc8 = jax.ShapeDtypeStruct((SUBLANES, d), F32)
    return _pcall(
        body, name="mix_bwd", grid=(nt,),
        in_specs=[rev(d), rev(7 * d), halo, rev(d), rev(d), rev(d), _full(vecs.shape), _full(caw.shape), _full(cbw.shape),
                  wspec(0), wspec(1), wspec(2)],
        out_specs=[rev(7 * d), rev(d), rev(d), _full((SUBLANES, d)), _full((HALO_A, d)), _full((HALO_B, d)),
                   _full((SUBLANES, d))],
        out_shape=[jax.ShapeDtypeStruct((s, 7 * d), BF16), b16, b16, acc8,
                   jax.ShapeDtypeStruct((HALO_A, d), F32), jax.ShapeDtypeStruct((HALO_B, d), F32), acc8],
        scratch_shapes=[pltpu.VMEM((HALO_A + tm, d), F32), pltpu.VMEM((SUBLANES + tm + HALO_A, d), F32),
                        pltpu.VMEM((SHIFTED_COPIES, tm + HALO_A, d), F32), pltpu.VMEM((HALO_B + tm, d), F32),
                        pltpu.VMEM((SUBLANES + tm + HALO_B, d), F32), pltpu.VMEM((tm, d), F32)],
        compiler_params=_seq(1),
    )(dout1, z, z, u1, ya, yb, vecs, caw, cbw, wb, wb, wb)


def _in_bwd(dz, x, dxp, vecs, wa, tm, plans, plan_srcs):
    s, d = x.shape
    nq, _, nw = wa.shape
    nt = s // tm
    nc = len(plan_srcs)

    def body(dz_ref, x_ref, dxp_ref, v_ref, w_ref, *rest):
        src_refs, (gx_ref, acc_ref), dst_refs = rest[:nc], rest[nc:nc + 2], rest[nc + 2:2 * nc + 2]
        send_sems, recv_sems = rest[2 * nc + 2:]
        i = pl.program_id(0)
        comm = (src_refs, dst_refs, send_sems, recv_sems)
        _carried_start(plans, i == 0, comm)

        @pl.when(i == 0)
        def _():
            acc_ref[...] = jnp.zeros(acc_ref.shape, F32)

        dh1 = jnp.zeros((tm, d), F32)
        for q in range(nq):
            dh1 = dh1 + _dot_nt(dz_ref[:, q * nw:(q + 1) * nw], w_ref[q])
        xh, rstd = _ln(x_ref[...])
        acc_ref[I_SHIFT1:I_SHIFT1 + 1, :] += _rsum(dh1)
        acc_ref[I_SCALE1:I_SCALE1 + 1, :] += _rsum(dh1 * xh)
        gx_ref[...] = dxp_ref[...] + _ln_bwd(dh1 * (1.0 + v_ref[V_SCALE1:V_SCALE1 + 1, :]), xh, rstd)
        _carried_wait(plans, i == nt - 1, comm)

    any_spec = pl.BlockSpec(memory_space=pl.ANY)
    return _pcall(
        body, name="in_bwd", grid=(nt,),
        in_specs=[_rows(tm, nq * nw), _rows(tm, d), _rows(tm, d), _full(vecs.shape), _full(wa.shape, single=True)]
        + [any_spec] * nc,
        out_specs=[_rows(tm, d), _full((SUBLANES, d))] + [any_spec] * nc,
        out_shape=[jax.ShapeDtypeStruct((s, d), F32), jax.ShapeDtypeStruct((SUBLANES, d), F32)]
        + [plan.out_shape(p) for plan, p in zip(plans, plan_srcs)],
        scratch_shapes=_plan_sems(plans),
        compiler_params=_seq(1),
    )(dz, x, dxp, vecs, wa, *plan_srcs)


def _dw(a, b, split_a, ts, name, into=None, rows_total=None, row_block=0):
    s = a.shape[0]
    ka = a.shape[1] // N_CHIPS if split_a else a.shape[1]
    nb = b.shape[1] if split_a else b.shape[1] // N_CHIPS
    rows_total = ka if rows_total is None else rows_total

    def body(a_ref, b_ref, *rest):
        o_ref = rest[-1]

        @pl.when(pl.program_id(1) == 0)
        def _():
            o_ref[...] = jnp.zeros(o_ref.shape, F32)

        o_ref[...] += _dot_tn(a_ref[...], b_ref[...])

    a_spec = pl.BlockSpec((ts, ka), (lambda q, i: (i, q)) if split_a else (lambda q, i: (i, 0)))
    b_spec = pl.BlockSpec((ts, nb), (lambda q, i: (i, 0)) if split_a else (lambda q, i: (i, q)))
    extra = {} if into is None else dict(input_output_aliases={2: 0})
    return _pcall(
        body, name=name, grid=(N_CHIPS, s // ts),
        in_specs=[a_spec, b_spec] + ([] if into is None else [pl.BlockSpec(memory_space=pl.ANY)]),
        out_specs=pl.BlockSpec((None, ka, nb), lambda q, i: (q, row_block, 0)),
        out_shape=jax.ShapeDtypeStruct((N_CHIPS, rows_total, nb), F32),
        compiler_params=_seq(2), **extra,
    )(*((a, b) if into is None else (a, b, into)))


def _dw_carrying(a, b, ts, name, plan, plan_src):
    s, k = a.shape
    nb = b.shape[1] // N_CHIPS
    ns = s // ts

    def body(a_ref, b_ref, src_ref, o_ref, dst_ref, send_sems, recv_sems):
        q, i = pl.program_id(0), pl.program_id(1)
        comm = ([src_ref], [dst_ref], send_sems, recv_sems)
        _carried_start([plan], jnp.logical_and(q == 0, i == 0), comm)

        @pl.when(i == 0)
        def _():
            o_ref[...] = jnp.zeros(o_ref.shape, F32)

        o_ref[...] += _dot_tn(a_ref[...], b_ref[...])
        _carried_wait([plan], jnp.logical_and(q == N_CHIPS - 1, i == ns - 1), comm)

    any_spec = pl.BlockSpec(memory_space=pl.ANY)
    return _pcall(
        body, name=name, grid=(N_CHIPS, ns),
        in_specs=[pl.BlockSpec((ts, k), lambda q, i: (i, 0)), pl.BlockSpec((ts, nb), lambda q, i: (i, q)), any_spec],
        out_specs=[pl.BlockSpec((None, k, nb), lambda q, i: (q, 0, 0)), any_spec],
        out_shape=[jax.ShapeDtypeStruct((N_CHIPS, k, nb), F32), plan.out_shape(plan_src)],
        scratch_shapes=_plan_sems([plan]),
        compiler_params=_seq(2),
    )(a, b, plan_src)


def _dw_rows(a, b, ts, name, into, row_block):
    s, k = a.shape
    n = b.shape[1]
    kq = k // N_CHIPS

    def body(a_ref, b_ref, buf_ref, o_ref):
        @pl.when(pl.program_id(0) == 0)
        def _():
            o_ref[...] = jnp.zeros(o_ref.shape, F32)

        res = _dot_tn(a_ref[...], b_ref[...])
        for q in range(N_CHIPS):
            o_ref[q] += res[q * kq:(q + 1) * kq, :]

    return _pcall(
        body, name=name, grid=(s // ts,),
        in_specs=[_rows(ts, k), _rows(ts, n), pl.BlockSpec(memory_space=pl.ANY)],
        out_specs=pl.BlockSpec((N_CHIPS, kq, n), lambda i: (0, row_block, 0)),
        out_shape=jax.ShapeDtypeStruct(into.shape, F32), input_output_aliases={2: 0},
        compiler_params=_seq(1),
    )(a, b, into)


def _adam_math(w, g, m, v):
    m2 = ADAM_B1 * m + (1.0 - ADAM_B1) * g
    v2 = ADAM_B2 * v + (1.0 - ADAM_B2) * (g * g)
    m_hat = m2 / (1.0 - ADAM_B1 ** ADAM_STEP)
    v_hat = v2 / (1.0 - ADAM_B2 ** ADAM_STEP)
    delta = -ADAM_LR * (m_hat / (jnp.sqrt(v_hat) + ADAM_EPS) + ADAM_WD * w)
    return delta, m2, v2


def _adam(w, g, m, v, g_row0, tr, name):
    r, c = w.shape
    blk0 = g_row0 // tr

    def body(w_ref, g_ref, m_ref, v_ref, go_ref, d_ref, mo_ref, vo_ref):
        gv = g_ref[...]
        go_ref[...] = gv
        d_ref[...], mo_ref[...], vo_ref[...] = _adam_math(w_ref[...], gv, m_ref[...], v_ref[...])

    spec = _rows(tr, c)
    g_spec = pl.BlockSpec((tr, c), lambda i: (blk0 + i, 0))
    o = jax.ShapeDtypeStruct((r, c), F32)
    return _pcall(body, name=name, grid=(r // tr,), in_specs=[spec, g_spec, spec, spec], out_specs=[spec] * 4,
                  out_shape=[o, o, o, o], compiler_params=_seq(1))(w, g, m, v)


def _small_update(gathered_head, gathered, q_idx, small_w, small_m, small_v, conv_w, conv_m, conv_v):
    d = gathered.shape[2]
    ns = len(_SMALL)
    cw = conv_w[0].shape[1]
    conv_rows = ((T_CAW, CONV_A), (T_CBW, CONV_B))

    def body(q_ref, h_ref, g_ref, *refs):
        ins, outs = refs[:3 * (ns + 2)], refs[3 * (ns + 2):]
        tot_ref, loss_ref = outs[0], outs[1]
        outs = outs[2:]
        head, tot = h_ref[0], g_ref[0]
        for dev in range(1, N_DEV):
            head = head + h_ref[dev]
            tot = tot + g_ref[dev]
        tot_ref[0:T_M, :] = head
        tot_ref[T_M:T_ROWS, :] = tot
        loss_ref[...] = (0.5 / d) * jnp.sum(tot_ref[T_LOSS:T_LOSS + 1, :], axis=1, keepdims=True)
        for p, (_, rows) in enumerate(_SMALL):
            w_ref, m_ref, v_ref = ins[p], ins[ns + 2 + p], ins[2 * (ns + 2) + p]
            go, do, mo, vo = outs[4 * p:4 * p + 4]
            for j, row in enumerate(rows):
                sl = slice(j * d, (j + 1) * d)
                gv = tot_ref[row:row + 1, :]
                go[:, sl] = gv
                do[:, sl], mo[:, sl], vo[:, sl] = _adam_math(w_ref[:, sl], gv, m_ref[:, sl], v_ref[:, sl])
        for p, (row, taps) in enumerate(conv_rows):
            w_ref, m_ref, v_ref = ins[ns + p], ins[ns + 2 + ns + p], ins[2 * (ns + 2) + ns + p]
            go, do, mo, vo = outs[4 * (ns + p):4 * (ns + p) + 4]
            gv = tot_ref[row:row + taps, 0:cw]
            for qq in range(1, N_CHIPS):
                gv = jnp.where(q_ref[0] == qq, tot_ref[row:row + taps, qq * cw:(qq + 1) * cw], gv)
            go[...] = gv
            do[...], mo[...], vo[...] = _adam_math(w_ref[...], gv, m_ref[...], v_ref[...])

    params = list(small_w) + list(conv_w) + list(small_m) + list(conv_m) + list(small_v) + list(conv_v)
    out_shape = [jax.ShapeDtypeStruct((T_ROWS, d), F32), jax.ShapeDtypeStruct((1, 1), F32)]
    for w in list(small_w) + list(conv_w):
        out_shape += [jax.ShapeDtypeStruct(w.shape, F32)] * 4
    vm = pl.BlockSpec(memory_space=pltpu.VMEM)
    return _pcall(
        body, name="small_update", out_shape=out_shape,
        in_specs=[pl.BlockSpec(memory_space=pltpu.SMEM), vm, vm] + [vm] * len(params),
        out_specs=[vm] * len(out_shape), compiler_params=_cparams(),
    )(q_idx, gathered_head, gathered, *params)


def kernel(x, c, w_ada, b_ada, w_in, b_in, conv_a_w, conv_a_b, ln_a_g, ln_a_b, w_a_out, b_a_out, conv_b_w, w_b_out, w_o, b_o, ln1_g, ln1_b, w_up, b_up, w_down, b_down, ln2_g, ln2_b, loss_target, m_w_ada, m_b_ada, m_w_in, m_b_in, m_conv_a_w, m_conv_a_b, m_ln_a_g, m_ln_a_b, m_w_a_out, m_b_a_out, m_conv_b_w, m_w_b_out, m_w_o, m_b_o, m_ln1_g, m_ln1_b, m_w_up, m_b_up, m_w_down, m_b_down, m_ln2_g, m_ln2_b, v_w_ada, v_b_ada, v_w_in, v_b_in, v_conv_a_w, v_conv_a_b, v_ln_a_g, v_ln_a_b, v_w_a_out, v_b_a_out, v_conv_b_w, v_w_b_out, v_w_o, v_b_o, v_ln1_g, v_ln1_b, v_w_up, v_b_up, v_w_down, v_b_down, v_ln2_g, v_ln2_b):
    given = dict(locals())
    s, d = x.shape[1], x.shape[2]
    xi, yi, ci = _my_pos()
    q = 2 * xi + yi
    me = 4 * xi + 2 * yi + ci
    i32 = jnp.int32
    q_arr = jnp.reshape(q, (1,)).astype(i32)
    others = [2 * ox + oy for ox, oy in _other_chips(xi, yi)]
    halves_idx = jnp.stack([ci] + others).astype(i32)
    chips_idx = jnp.stack([q, ci]).astype(i32)
    kq = d // N_CHIPS
    tm = min(256, s)
    rc = tm

    def sq(a):
        return a.reshape(a.shape[1:])

    x2, tgt = sq(x), sq(loss_target)

    wa = _place_shard([sq(w_in)], q_arr, "place_w_in")
    wb = _place_shard([sq(w_up), sq(w_down), sq(w_a_out), sq(w_b_out), sq(w_o)], q_arr, "place_w_rest")

    n_ada = w_ada.shape[2]
    pre = jnp.concatenate([
        jnp.broadcast_to(c, (SUBLANES, d)),
        jnp.pad(sq(conv_a_w), ((0, HALO_A - CONV_A), (0, d - kq))),
        jnp.pad(sq(conv_b_w), ((0, HALO_B - CONV_B), (0, d - kq)))], axis=0)
    b_ada_sh = lax.dynamic_slice(b_ada, (0, q * n_ada), (1, n_ada))
    pre_all, c_all, mod_all, wa = _prologue(pre, sq(w_ada), b_ada_sh, wa)
    caw = jnp.concatenate([pre_all[2 * p, SUBLANES:SUBLANES + HALO_A, :kq] for p in range(N_CHIPS)], axis=1)
    cbw = jnp.concatenate([pre_all[2 * p, SUBLANES + HALO_A:, :kq] for p in range(N_CHIPS)], axis=1)
    mod_rows = lax.dynamic_slice(mod_all, (0, me, 0), (N_DEV, 1, n_ada))[0::2, 0, :]
    mod = mod_rows.reshape(6, d)
    vecs = jnp.concatenate([mod, conv_a_b, ln_a_g, ln_a_b, b_a_out, b_o, ln1_g, ln1_b, b_down, ln2_g, ln2_b], axis=0)

    h1, z, wb = _fwd_in(x2, vecs, wa, b_in, wb, tm)
    u1, ya, yb, out1, u3, vv, mg = _fwd_mix(z, vecs, caw, cbw, wb, tm, rc)

    ts = min(2048, s)
    rest_rows = wb.shape[1]
    small0 = 2 * d // kq
    h2, fb, df0, do2, do1, dxp, macc, dbup = _mlp_fwd_bwd(x2, out1, tgt, vecs, b_up, wb, tm)
    gb = _dw(h2, df0, False, ts, "dw_up", rows_total=rest_rows)
    gb = _dw(fb, do2, True, ts, "dw_down", into=gb, rows_total=rest_rows, row_block=1)
    dz, dya, dyb, xacc, dcaw, dcbw, dbin = _mix_bwd(do1, z, u1, ya, yb, vecs, caw, cbw, wb, tm, rc)
    gb = _dw_rows(u3, dya, ts, "dw_a_out", gb, small0)
    gb = _dw_rows(vv, dyb, ts, "dw_b_out", gb, small0 + 1)
    gb = _dw_rows(mg, do1, ts, "dw_o", gb, small0 + 2)

    trb, tra = rest_rows // 8, d // 8
    ga, rb = _dw_carrying(h1, dz, ts, "dw_in", _SiblingHalf, gb)
    pb, ra = _add_halves(gb, rb, halves_idx, trb, "rs_add_halves_rest", _SiblingHalf, ga)
    pa = _add_halves(ga, ra, halves_idx, tra, "rs_add_halves_in")
    table = jnp.concatenate([macc, dbin, dcaw, xacc, dcbw, dbup], axis=0)
    gx, iacc, r3a, r3b, gathered = _in_bwd(dz, x2, dxp, vecs, wa, tm, [_ChipBlocks, _ChipBlocks, _TableToAll],
                                           [pa, pb, table])
    fa = _add_chips(ga, ra, r3a, chips_idx, tra, "rs_add_chips_in")
    fb_ = _add_chips(gb, rb, r3b, chips_idx, trb, "rs_add_chips_rest")
    g_in, g_b = _rs_join_halves(fa, fb_)

    gathered_head = _all_gather_small(iacc, "gather_ln0_sums")
    names = [n for n, _ in _SMALL]
    res = _small_update(
        gathered_head, gathered, q_arr,
        [given[n] for n in names], [given["m_" + n] for n in names], [given["v_" + n] for n in names],
        [sq(conv_a_w), sq(conv_b_w)], [sq(m_conv_a_w), sq(m_conv_b_w)], [sq(v_conv_a_w), sq(v_conv_b_w)])
    loss = res[1].reshape(())
    upd = {}
    for p, n in enumerate(names + ["conv_a_w", "conv_b_w"]):
        upd[n] = res[2 + 4 * p:6 + 4 * p]

    dmod_all = jnp.stack([gathered_head[:, r, :] if r < T_M else gathered[:, r - T_M, :] for r in _SMALL[0][1]],
                         axis=1).reshape(N_DEV, 6 * d)
    dmod_sh = lax.dynamic_slice(dmod_all, (0, q * n_ada), (N_DEV, n_ada))
    g_ada = _ada_bwd(c_all.T, dmod_sh)
    upd["w_ada"] = _adam(sq(w_ada), g_ada, sq(m_w_ada), sq(v_w_ada), 0, min(256, d), "adam_w_ada")

    upd["w_in"] = _adam(sq(w_in), g_in, sq(m_w_in), sq(v_w_in), 0, min(256, d), "adam_w_in")
    r0 = 0
    for n in ("w_up", "w_down", "w_a_out", "w_b_out", "w_o"):
        w = sq(given[n])
        upd[n] = _adam(w, g_b, sq(given["m_" + n]), sq(given["v_" + n]), r0, min(256, w.shape[0]), "adam_" + n)
        r0 += w.shape[0]

    order = ["w_ada", "b_ada", "w_in", "b_in", "conv_a_w", "conv_a_b", "ln_a_g", "ln_a_b", "w_a_out", "b_a_out", "conv_b_w",
             "w_b_out", "w_o", "b_o", "ln1_g", "ln1_b", "w_up", "b_up", "w_down", "b_down", "ln2_g", "ln2_b"]
    outs = [loss, gx.reshape(x.shape)]
    for k in range(4):
        outs += [upd[n][k].reshape(given[n].shape) for n in order]
    return tuple(outs)
```

```python
import jax
import jax.numpy as jnp
from jax import lax
from jax.experimental import pallas as pl
from jax.experimental.pallas import tpu as pltpu

F32 = jnp.float32
BF16 = jnp.bfloat16
MESH = pl.DeviceIdType.MESH

LN_EPS = 1e-5
DEPTH = 1
ALPHA = (2.0 * DEPTH) ** 0.25
CONV_A = 31
CONV_B = 3
SUBLANES = 8
HALO_A = 32
HALO_B = 8
SHIFTED_COPIES = 4
N_CHIPS = 4
N_DEV = 8
ADAM_LR = 0.001
ADAM_B1 = 0.9
ADAM_B2 = 0.999
ADAM_EPS = 1e-08
ADAM_WD = 0.01
ADAM_STEP = 10
VMEM_LIMIT = 56 * 1024 * 1024

V_SHIFT1, V_SCALE1, V_GATE1, V_SHIFT2, V_SCALE2, V_GATE2 = 0, 1, 2, 3, 4, 5
V_CAB, V_LNAG, V_LNAB, V_BAO, V_BO, V_LN1G, V_LN1B, V_BDN, V_LN2G, V_LN2B = 6, 7, 8, 9, 10, 11, 12, 13, 14, 15

M_LN2G, M_LN2B, M_GATE2, M_BDN, M_SHIFT2, M_SCALE2, M_LN1G, M_LN1B, M_GATE1, M_BO, M_LOSS = range(11)
X_BAO, X_LNAG, X_LNAB, X_CAB = range(4)
I_SHIFT1, I_SCALE1 = 0, 1

T_I, T_M, T_BIN, T_CAW, T_X, T_CBW, T_BUP, T_ROWS = 0, 8, 24, 32, 64, 72, 80, 88
T_LOSS = T_M + M_LOSS
_SMALL = (
    ("b_ada", (T_I + I_SHIFT1, T_I + I_SCALE1, T_M + M_GATE1, T_M + M_SHIFT2, T_M + M_SCALE2, T_M + M_GATE2)),
    ("b_in", tuple(T_BIN + j for j in range(7))),
    ("conv_a_b", (T_X + X_CAB,)), ("ln_a_g", (T_X + X_LNAG,)), ("ln_a_b", (T_X + X_LNAB,)), ("b_a_out", (T_X + X_BAO,)),
    ("b_o", (T_M + M_BO,)), ("ln1_g", (T_M + M_LN1G,)), ("ln1_b", (T_M + M_LN1B,)),
    ("b_up", tuple(T_BUP + j for j in range(4))),
    ("b_down", (T_M + M_BDN,)), ("ln2_g", (T_M + M_LN2G,)), ("ln2_b", (T_M + M_LN2B,)),
)


def _pcall(body, **kw):
    return pl.pallas_call(body, **kw)


def _cparams(**kw):
    return pltpu.CompilerParams(vmem_limit_bytes=VMEM_LIMIT, **kw)


def _seq(n):
    return _cparams(dimension_semantics=("arbitrary",) * n)


def _full(shape, single=False):
    nd = len(shape)
    if single:
        return pl.BlockSpec(shape, lambda *_: (0,) * nd, pipeline_mode=pl.Buffered(1))
    return pl.BlockSpec(shape, lambda *_: (0,) * nd)


def _rows(tm, width):
    return pl.BlockSpec((tm, width), lambda i: (i, 0))


def _sig(x):
    return jax.nn.sigmoid(x)


def _ln(x):
    mu = jnp.mean(x, axis=-1, keepdims=True)
    xc = x - mu
    var = jnp.mean(xc * xc, axis=-1, keepdims=True)
    rstd = lax.rsqrt(var + LN_EPS)
    return xc * rstd, rstd


def _ln_bwd(dxh, xh, rstd):
    m1 = jnp.mean(dxh, axis=-1, keepdims=True)
    m2 = jnp.mean(dxh * xh, axis=-1, keepdims=True)
    return rstd * (dxh - m1 - xh * m2)


def _rsum(v):
    return jnp.sum(v, axis=0, keepdims=True)


def _rsum_mxu(v):
    vb = v if v.dtype == BF16 else v.astype(BF16)
    return _dot(jnp.ones((2 * SUBLANES, v.shape[0]), BF16), vb)[0:1, :]


def _dot(a, b):
    return jnp.dot(a, b, preferred_element_type=F32)


def _dot_nt(a, b):
    return lax.dot_general(a, b, (((1,), (1,)), ((), ())), preferred_element_type=F32)


def _dot_tn(a, b):
    return lax.dot_general(a, b, (((0,), (0,)), ((), ())), preferred_element_type=F32)


def _my_pos():
    return lax.axis_index("x"), lax.axis_index("y"), lax.axis_index("c")


def _other_chips(x, y):
    return [(1 - x, y), (x, 1 - y), (1 - x, 1 - y)]


def _small_gather(v_ref, out_ref, send_sems, recv_sems, local_sem):
    x, y, cc = _my_pos()
    me = 4 * x + 2 * y + cc
    mine = pltpu.make_async_copy(v_ref, out_ref.at[me], local_sem)
    mine.start()
    sends = []
    for rel in range(1, N_DEV):
        rx, ry, rc = (rel >> 2) & 1, (rel >> 1) & 1, rel & 1
        peer = (1 - x if rx else x, 1 - y if ry else y, 1 - cc if rc else cc)
        cp = pltpu.make_async_remote_copy(
            src_ref=v_ref, dst_ref=out_ref.at[me], send_sem=send_sems.at[rel - 1], recv_sem=recv_sems.at[rel - 1],
            device_id=peer, device_id_type=MESH)
        cp.start()
        sends.append(cp)
    for rel in range(1, N_DEV):
        rx, ry, rc = (rel >> 2) & 1, (rel >> 1) & 1, rel & 1
        peer = (1 - x if rx else x, 1 - y if ry else y, 1 - cc if rc else cc)
        slot = 4 * peer[0] + 2 * peer[1] + peer[2]
        pltpu.make_async_remote_copy(
            src_ref=v_ref, dst_ref=out_ref.at[slot], send_sem=send_sems.at[rel - 1], recv_sem=recv_sems.at[rel - 1],
            device_id=peer, device_id_type=MESH).wait_recv()
    for cp in sends:
        cp.wait_send()
    mine.wait()


_SMALL_GATHER_SEMS = [pltpu.SemaphoreType.DMA((N_DEV - 1,)), pltpu.SemaphoreType.DMA((N_DEV - 1,)),
                      pltpu.SemaphoreType.DMA]


def _all_gather_small(v, name):
    r, c = v.shape

    def body(*refs):
        _small_gather(*refs)

    return _pcall(
        body, name=name,
        out_shape=jax.ShapeDtypeStruct((N_DEV, r, c), v.dtype),
        in_specs=[pl.BlockSpec(memory_space=pltpu.VMEM)],
        out_specs=pl.BlockSpec(memory_space=pltpu.VMEM),
        scratch_shapes=list(_SMALL_GATHER_SEMS),
        compiler_params=_cparams(),
    )(v)


def _prologue(pre, w_sh, b_sh, wa):
    r, d = pre.shape
    n_ada = w_sh.shape[1]
    ng = _GatherShards.n_sems

    def body(pre_ref, w_ref, b_ref, wai_ref, pre_all_ref, c_all_ref, mod_all_ref, wao_ref, mod_ref,
             s1, r1, l1, s2, r2, l2, sg, rg):
        gather = (wai_ref, wao_ref, sg, rg)
        _GatherShards.start(*gather)
        _small_gather(pre_ref, pre_all_ref, s1, r1, l1)
        cv = jnp.concatenate([pre_all_ref[dev, 0:1, :] for dev in range(N_DEV)], axis=0)
        c_all_ref[...] = cv
        ca = cv * _sig(cv)
        mod_ref[...] = jnp.dot(ca, w_ref[...], preferred_element_type=F32, precision=lax.Precision.HIGHEST) + b_ref[...]
        _small_gather(mod_ref, mod_all_ref, s2, r2, l2)
        _GatherShards.relay(*gather)
        _GatherShards.finish(*gather)

    vm = pl.BlockSpec(memory_space=pltpu.VMEM)
    any_spec = pl.BlockSpec(memory_space=pl.ANY)
    return _pcall(
        body, name="prologue",
        out_shape=[jax.ShapeDtypeStruct((N_DEV, r, d), F32), jax.ShapeDtypeStruct((N_DEV, d), F32),
                   jax.ShapeDtypeStruct((N_DEV, N_DEV, n_ada), F32), jax.ShapeDtypeStruct(wa.shape, wa.dtype)],
        in_specs=[vm, vm, vm, any_spec], out_specs=[vm, vm, vm, any_spec], input_output_aliases={3: 3},
        scratch_shapes=[pltpu.VMEM((N_DEV, n_ada), F32)] + list(_SMALL_GATHER_SEMS) + list(_SMALL_GATHER_SEMS)
        + [pltpu.SemaphoreType.DMA((ng,)), pltpu.SemaphoreType.DMA((ng,))],
        compiler_params=_cparams(),
    )(pre, w_sh, b_sh, wa)


def _place_shard(parts, q_idx, name):
    rows = sum(p.shape[0] for p in parts)
    w = parts[0].shape[1]

    def body(q_ref, *refs):
        o_ref = refs[-1]
        r0 = 0
        for p_ref in refs[:-1]:
            n = p_ref.shape[0]
            o_ref[r0:r0 + n, :] = p_ref[...].astype(BF16)
            r0 += n

    grid_spec = pltpu.PrefetchScalarGridSpec(
        num_scalar_prefetch=1, grid=(1,),
        in_specs=[pl.BlockSpec(p.shape, lambda i, q: (0, 0)) for p in parts],
        out_specs=pl.BlockSpec((None, rows, w), lambda i, q: (q[0], 0, 0)))
    return _pcall(body, name=name, grid_spec=grid_spec, out_shape=jax.ShapeDtypeStruct((N_CHIPS, rows, w), BF16),
                  compiler_params=_seq(1))(q_idx, *parts)


class _GatherShards:
    n_sems = 6

    @staticmethod
    def _half(ref, slot, h):
        rows = ref.shape[1] // 2
        return ref.at[slot, pl.ds(h * rows, rows)]

    @classmethod
    def _copies(cls, in_ref, out_ref, send_sems, recv_sems):
        x, y, c = _my_pos()
        q = 2 * x + y
        sibling = (x, y, 1 - c)
        sends, landed, forwards, passed = [], [], [], []
        for j, chip in enumerate(_other_chips(x, y)):
            qj = 2 * chip[0] + chip[1]

            def copy(src, dst, k, to):
                return pltpu.make_async_remote_copy(src_ref=src, dst_ref=dst, send_sem=send_sems.at[k],
                                                    recv_sem=recv_sems.at[k], device_id=to, device_id_type=MESH)

            mine, theirs = cls._half(out_ref, qj, c), cls._half(out_ref, qj, 1 - c)
            sends.append(copy(cls._half(in_ref, q, c), cls._half(out_ref, q, c), j, (*chip, c)))
            landed.append(copy(mine, mine, j, (*chip, c)))
            forwards.append(copy(mine, mine, 3 + j, sibling))
            passed.append(copy(theirs, theirs, 3 + j, sibling))
        return sends, landed, forwards, passed

    @classmethod
    def start(cls, *refs):
        for cp in cls._copies(*refs)[0]:
            cp.start()

    @classmethod
    def relay(cls, *refs):
        _, landed, forwards, _ = cls._copies(*refs)
        for arrived, onward in zip(landed, forwards):
            arrived.wait_recv()
            onward.start()

    @classmethod
    def finish(cls, *refs):
        sends, _, forwards, passed = cls._copies(*refs)
        for cp in passed:
            cp.wait_recv()
        for cp in sends + forwards:
            cp.wait_send()


class _SiblingHalf:
    n_sems = 1

    @staticmethod
    def out_shape(g):
        return jax.ShapeDtypeStruct((g.shape[0], g.shape[1] // 2, g.shape[2]), g.dtype)

    @staticmethod
    def copies(g_ref, r_ref, send_sems, recv_sems, base):
        x, y, c = _my_pos()
        rows = r_ref.shape[1]
        return [pltpu.make_async_remote_copy(
            src_ref=g_ref.at[:, pl.ds((1 - c) * rows, rows)], dst_ref=r_ref,
            send_sem=send_sems.at[base], recv_sem=recv_sems.at[base], device_id=(x, y, 1 - c), device_id_type=MESH)]


class _ChipBlocks:
    n_sems = 3

    @staticmethod
    def out_shape(p):
        return jax.ShapeDtypeStruct(p.shape, p.dtype)

    @staticmethod
    def copies(p_ref, r_ref, send_sems, recv_sems, base):
        x, y, c = _my_pos()
        return [pltpu.make_async_remote_copy(
            src_ref=p_ref.at[j], dst_ref=r_ref.at[j], send_sem=send_sems.at[base + j], recv_sem=recv_sems.at[base + j],
            device_id=(*chip, c), device_id_type=MESH) for j, chip in enumerate(_other_chips(x, y))]


class _TableToAll:
    n_sems = N_DEV

    @staticmethod
    def out_shape(t):
        return jax.ShapeDtypeStruct((N_DEV,) + t.shape, t.dtype)

    @staticmethod
    def copies(t_ref, all_ref, send_sems, recv_sems, base):
        x, y, c = _my_pos()
        me = 4 * x + 2 * y + c
        cps = [pltpu.make_async_copy(t_ref, all_ref.at[me], send_sems.at[base + N_DEV - 1])]
        for rel in range(1, N_DEV):
            rx, ry, rc = (rel >> 2) & 1, (rel >> 1) & 1, rel & 1
            peer = (1 - x if rx else x, 1 - y if ry else y, 1 - c if rc else c)
            cps.append(_SlotCopy(t_ref, all_ref, me, 4 * peer[0] + 2 * peer[1] + peer[2], peer,
                                 send_sems.at[base + rel - 1], recv_sems.at[base + rel - 1]))
        return cps


class _SlotCopy:
    def __init__(self, src_ref, all_ref, my_slot, peer_slot, peer, send_sem, recv_sem):
        self._send = pltpu.make_async_remote_copy(src_ref=src_ref, dst_ref=all_ref.at[my_slot], send_sem=send_sem,
                                                  recv_sem=recv_sem, device_id=peer, device_id_type=MESH)
        self._recv = pltpu.make_async_remote_copy(src_ref=src_ref, dst_ref=all_ref.at[peer_slot], send_sem=send_sem,
                                                  recv_sem=recv_sem, device_id=peer, device_id_type=MESH)

    def start(self):
        self._send.start()

    def wait(self):
        self._send.wait_send()
        self._recv.wait_recv()


def _plan_copies(plans, src_refs, dst_refs, send_sems, recv_sems):
    cps, base = [], 0
    for plan, s_ref, d_ref in zip(plans, src_refs, dst_refs):
        cps += plan.copies(s_ref, d_ref, send_sems, recv_sems, base)
        base += plan.n_sems
    return cps


def _plan_sems(plans):
    n = sum(p.n_sems for p in plans)
    return [pltpu.SemaphoreType.DMA((n,)), pltpu.SemaphoreType.DMA((n,))]


def _carried_start(plans, first, comm):
    @pl.when(first)
    def _():
        for cp in _plan_copies(plans, *comm):
            cp.start()


def _carried_wait(plans, last, comm):
    @pl.when(last)
    def _():
        for cp in _plan_copies(plans, *comm):
            cp.wait()


def _rs_join_halves(fa, fb):
    bufs = (fa, fb)
    nb = len(bufs)

    def body(a_ref, b_ref, ao_ref, bo_ref, send_sems, recv_sems):
        x, y, c = _my_pos()
        srcs, outs = (a_ref, b_ref), (ao_ref, bo_ref)
        cps = []
        for b in range(nb):
            rows = srcs[b].shape[0] // 2
            cp = pltpu.make_async_remote_copy(
                src_ref=srcs[b].at[pl.ds(c * rows, rows)], dst_ref=outs[b].at[pl.ds(c * rows, rows)],
                send_sem=send_sems.at[b], recv_sem=recv_sems.at[b], device_id=(x, y, 1 - c), device_id_type=MESH)
            cp.start()
            cps.append(cp)
        for b in range(nb):
            rows = srcs[b].shape[0] // 2
            theirs = outs[b].at[pl.ds((1 - c) * rows, rows)]
            pltpu.make_async_remote_copy(
                src_ref=theirs, dst_ref=theirs, send_sem=send_sems.at[b], recv_sem=recv_sems.at[b],
                device_id=(x, y, 1 - c), device_id_type=MESH).wait_recv()
        for cp in cps:
            cp.wait_send()

    any_spec = pl.BlockSpec(memory_space=pl.ANY)
    return _pcall(
        body, name="rs_join_halves",
        out_shape=[jax.ShapeDtypeStruct(b.shape, b.dtype) for b in bufs],
        in_specs=[any_spec] * nb, out_specs=[any_spec] * nb, input_output_aliases={0: 0, 1: 1},
        scratch_shapes=[pltpu.SemaphoreType.DMA((nb,)), pltpu.SemaphoreType.DMA((nb,))],
        compiler_params=_cparams(),
    )(*bufs)


def _add_halves(g, r, idx, tr, name, plan=None, plan_src=None):
    _, rows, w = r.shape
    nt = rows // tr

    def body(i_ref, g_ref, r_ref, *rest):
        o_ref = rest[0] if plan is None else rest[1]
        if plan is not None:
            comm = ([rest[0]], [rest[2]], rest[3], rest[4])
            first = jnp.logical_and(pl.program_id(0) == 0, pl.program_id(1) == 0)
            _carried_start([plan], first, comm)
        o_ref[...] = (g_ref[...] + r_ref[...]).astype(BF16)
        if plan is not None:
            last = jnp.logical_and(pl.program_id(0) == 2, pl.program_id(1) == nt - 1)
            _carried_wait([plan], last, comm)

    any_spec = pl.BlockSpec(memory_space=pl.ANY)
    carried = plan is not None
    grid_spec = pltpu.PrefetchScalarGridSpec(
        num_scalar_prefetch=1, grid=(3, nt),
        in_specs=[pl.BlockSpec((None, tr, w), lambda j, i, ix: (ix[1 + j], ix[0] * nt + i, 0)),
                  pl.BlockSpec((None, tr, w), lambda j, i, ix: (ix[1 + j], i, 0))] + [any_spec] * carried,
        out_specs=[pl.BlockSpec((None, tr, w), lambda j, i, ix: (j, i, 0))] + [any_spec] * carried,
        scratch_shapes=_plan_sems([plan]) if carried else [])
    out_shape = [jax.ShapeDtypeStruct((3, rows, w), BF16)] + ([plan.out_shape(plan_src)] if carried else [])
    res = _pcall(body, name=name, grid_spec=grid_spec, out_shape=out_shape,
                 compiler_params=_seq(2))(*((idx, g, r, plan_src) if carried else (idx, g, r)))
    return res if carried else res[0]


def _add_chips(g, r, r3, idx, tr, name):
    _, rows, w = r.shape
    nt = rows // tr

    def body(i_ref, g_ref, r_ref, a_ref, b_ref, c_ref, o_ref):
        own = g_ref[...] + r_ref[...]
        o_ref[...] = ((own + a_ref[...].astype(F32)) + b_ref[...].astype(F32)) + c_ref[...].astype(F32)

    def other(j):
        return pl.BlockSpec((None, tr, w), lambda i, ix: (j, i, 0))

    grid_spec = pltpu.PrefetchScalarGridSpec(
        num_scalar_prefetch=1, grid=(nt,),
        in_specs=[pl.BlockSpec((None, tr, w), lambda i, ix: (ix[0], ix[1] * nt + i, 0)),
                  pl.BlockSpec((None, tr, w), lambda i, ix: (ix[0], i, 0)), other(0), other(1), other(2)],
        out_specs=pl.BlockSpec((tr, w), lambda i, ix: (ix[1] * nt + i, 0)))
    return _pcall(body, name=name, grid_spec=grid_spec,
                  out_shape=jax.ShapeDtypeStruct((2 * rows, w), F32), compiler_params=_seq(1))(idx, g, r, r3, r3, r3)


def _ada_bwd(c_all_t, dmod_sh):
    def body(c_ref, d_ref, o_ref):
        cv = c_ref[...]
        ca = cv * _sig(cv)
        o_ref[...] = jnp.dot(ca, d_ref[...], preferred_element_type=F32, precision=lax.Precision.HIGHEST)

    return _pcall(body, name="ada_bwd", out_shape=jax.ShapeDtypeStruct((c_all_t.shape[0], dmod_sh.shape[1]), F32),
                  compiler_params=_cparams())(c_all_t, dmod_sh)


def _residue_passes(residues, copies):
    passes, current, used = [], [], 0
    for s in residues:
        if s != 0 and used == copies:
            passes.append(current)
            current, used = [], 0
        current.append(s)
        used += s != 0
    return passes + [current] if current else passes


def _conv_causal(ext_ref, sh_ref, w_ref, ntaps, halo, bias, out_ref, tm, d, rc):
    off = halo - (ntaps - 1)
    n = tm + halo - SUBLANES
    taps = {s: [(k, (off + k) // SUBLANES * SUBLANES) for k in range(ntaps) if (off + k) % SUBLANES == s]
            for s in range(SUBLANES)}
    started = False
    for group in _residue_passes([s for s in range(SUBLANES) if taps[s]], sh_ref.shape[0]):
        srcs = {}
        for s in group:
            if s == 0:
                srcs[s] = ext_ref
            else:
                j = len(srcs) - (0 in srcs)
                sh_ref[j, 0:n, :] = ext_ref[s:s + n, :]
                srcs[s] = sh_ref.at[j]
        for r0 in range(0, tm, rc):
            if started:
                acc = out_ref[r0:r0 + rc, :]
            else:
                acc = jnp.zeros((rc, d), F32) if bias is None else jnp.broadcast_to(bias, (rc, d))
            for s, src in srcs.items():
                for k, a in taps[s]:
                    acc = acc + w_ref[k:k + 1, :] * src[r0 + a:r0 + a + rc, :]
            out_ref[r0:r0 + rc, :] = acc
        started = True


def _conv_adjoint(dp_ref, sh_ref, ext_ref, w_ref, dx_ref, dw_ref, ntaps, halo, tm, d, rc):
    off = halo - (ntaps - 1)
    lead = SUBLANES + ntaps - 1
    n = tm + halo
    row = lax.broadcasted_iota(jnp.int32, (SUBLANES, d), 0)
    taps = {s: [(k, (lead - k) // SUBLANES * SUBLANES) for k in range(ntaps) if (lead - k) % SUBLANES == s]
            for s in range(SUBLANES)}
    wtaps = {s: [(k, s + off + k - SUBLANES) for k in range(ntaps) if (-(off + k)) % SUBLANES == s]
             for s in range(SUBLANES)}
    rw = 2 * SUBLANES
    started = False
    for group in _residue_passes([s for s in range(SUBLANES) if taps[s] or wtaps[s]], sh_ref.shape[0]):
        srcs = {}
        for s in group:
            if s == 0:
                srcs[s] = dp_ref
            else:
                j = len(srcs) - (0 in srcs)
                sh_ref[j, 0:n, :] = dp_ref[s:s + n, :]
                srcs[s] = sh_ref.at[j]
        if any(taps[s] for s in srcs):
            for r0 in range(0, tm, rc):
                acc = dx_ref[r0:r0 + rc, :] if started else jnp.zeros((rc, d), F32)
                for s, src in srcs.items():
                    for k, a in taps[s]:
                        acc = acc + w_ref[k:k + 1, :] * src[r0 + a:r0 + a + rc, :]
                dx_ref[r0:r0 + rc, :] = acc
            started = True
        for s, src in srcs.items():
            if not wtaps[s]:
                continue
            sums = [jnp.zeros((SUBLANES, d), F32) for _ in wtaps[s]]
            for r0 in range(0, tm, rw):
                g = src[r0:r0 + rw, :]
                for j, (_, e) in enumerate(wtaps[s]):
                    p = g * ext_ref[r0 + e:r0 + e + rw, :]
                    for r8 in range(0, rw, SUBLANES):
                        sums[j] = sums[j] + p[r8:r8 + SUBLANES, :]
            for j, (k, e) in enumerate(wtaps[s]):
                tail = src[tm:tm + SUBLANES, :] * ext_ref[tm + e:tm + e + SUBLANES, :]
                dw_ref[k:k + 1, :] += _rsum(sums[j] + jnp.where(row < SUBLANES - s, tail, 0.0))


def _fwd_in(x, vecs, wa, b_in, wb, tm):
    s, d = x.shape
    nq, _, nw = wa.shape
    nt = s // tm

    def body(x_ref, v_ref, w_ref, b_ref, wbi_ref, h_ref, z_ref, wbo_ref, send_sems, recv_sems):
        i = pl.program_id(0)
        gather = (wbi_ref, wbo_ref, send_sems, recv_sems)
        pl.when(i == 0)(lambda: _GatherShards.start(*gather))
        pl.when(i == nt // 2)(lambda: _GatherShards.relay(*gather))
        xh, _ = _ln(x_ref[...])
        h = (xh * (1.0 + v_ref[V_SCALE1:V_SCALE1 + 1, :]) + v_ref[V_SHIFT1:V_SHIFT1 + 1, :]).astype(BF16)
        h_ref[...] = h
        for q in range(nq):
            z_ref[:, q * nw:(q + 1) * nw] = _dot(h, w_ref[q]) + b_ref[:, q * nw:(q + 1) * nw]
        pl.when(i == nt - 1)(lambda: _GatherShards.finish(*gather))

    any_spec = pl.BlockSpec(memory_space=pl.ANY)
    n = _GatherShards.n_sems
    return _pcall(
        body, name="fwd_in", grid=(nt,),
        in_specs=[_rows(tm, d), _full(vecs.shape), _full(wa.shape, single=True), _full(b_in.shape), any_spec],
        out_specs=[_rows(tm, d), _rows(tm, nq * nw), any_spec],
        out_shape=[jax.ShapeDtypeStruct((s, d), BF16), jax.ShapeDtypeStruct((s, nq * nw), F32),
                   jax.ShapeDtypeStruct(wb.shape, wb.dtype)],
        input_output_aliases={4: 2},
        scratch_shapes=[pltpu.SemaphoreType.DMA((n,)), pltpu.SemaphoreType.DMA((n,))],
        compiler_params=_seq(1),
    )(x, vecs, wa, b_in, wb)


def _fwd_mix(z, vecs, caw, cbw, wb, tm, rc):
    s = z.shape[0]
    d = vecs.shape[1]
    nq = wb.shape[0]
    kq = d // nq
    base = 2 * d // kq

    def body(z_ref, v_ref, caw_ref, cbw_ref, wao_ref, wbo_ref, wo_ref,
             u1_ref, ya_ref, yb_ref, o1_ref, u3_ref, vv_ref, mg_ref, ext_ref, sh_ref, pext_ref, q_ref):
        @pl.when(pl.program_id(0) == 0)
        def _():
            ext_ref[0:HALO_A, :] = jnp.zeros((HALO_A, d), F32)
            pext_ref[0:HALO_B, :] = jnp.zeros((HALO_B, d), F32)

        ext_ref[HALO_A:HALO_A + tm, :] = z_ref[:, 0:d] * _sig(z_ref[:, d:2 * d])
        _conv_causal(ext_ref, sh_ref, caw_ref, CONV_A, HALO_A, v_ref[V_CAB:V_CAB + 1, :], u1_ref, tm, d, rc)
        ext_ref[0:HALO_A, :] = ext_ref[tm:tm + HALO_A, :]
        xa, _ = _ln(u1_ref[...])
        u2 = xa * v_ref[V_LNAG:V_LNAG + 1, :] + v_ref[V_LNAB:V_LNAB + 1, :]
        u3 = (u2 * _sig(u2)).astype(BF16)
        u3_ref[...] = u3
        ya = jnp.broadcast_to(v_ref[V_BAO:V_BAO + 1, :], (tm, d))
        for q in range(nq):
            ya = ya + _dot(u3[:, q * kq:(q + 1) * kq], wao_ref[q])
        ya_ref[...] = ya

        pext_ref[HALO_B:HALO_B + tm, :] = z_ref[:, 3 * d:4 * d] * z_ref[:, 4 * d:5 * d]
        _conv_causal(pext_ref, sh_ref, cbw_ref, CONV_B, HALO_B, None, q_ref, tm, d, rc)
        pext_ref[0:HALO_B, :] = pext_ref[tm:tm + HALO_B, :]
        vv = (z_ref[:, 2 * d:3 * d] * q_ref[...]).astype(BF16)
        vv_ref[...] = vv
        yb = jnp.zeros((tm, d), F32)
        for q in range(nq):
            yb = yb + _dot(vv[:, q * kq:(q + 1) * kq], wbo_ref[q])
        yb_ref[...] = yb

        mg = (_sig(z_ref[:, 5 * d:6 * d]) * ya + _sig(z_ref[:, 6 * d:7 * d]) * yb).astype(BF16)
        mg_ref[...] = mg
        o1 = jnp.broadcast_to(v_ref[V_BO:V_BO + 1, :], (tm, d))
        for q in range(nq):
            o1 = o1 + _dot(mg[:, q * kq:(q + 1) * kq], wo_ref[q])
        o1_ref[...] = o1

    def wspec(j):
        return pl.BlockSpec((nq, kq, d), lambda i: (0, base + j, 0), pipeline_mode=pl.Buffered(1))

    f32o = jax.ShapeDtypeStruct((s, d), F32)
    b16o = jax.ShapeDtypeStruct((s, d), BF16)
    return _pcall(
        body, name="fwd_mix", grid=(s // tm,),
        in_specs=[_rows(tm, 7 * d), _full(vecs.shape), _full(caw.shape), _full(cbw.shape), wspec(0), wspec(1), wspec(2)],
        out_specs=[_rows(tm, d)] * 7,
        out_shape=[f32o, f32o, f32o, f32o, b16o, b16o, b16o],
        scratch_shapes=[pltpu.VMEM((HALO_A + tm, d), F32), pltpu.VMEM((SHIFTED_COPIES, HALO_A + tm, d), F32),
                        pltpu.VMEM((HALO_B + tm, d), F32), pltpu.VMEM((tm, d), F32)],
        compiler_params=_seq(1),
    )(z, vecs, caw, cbw, wb, wb, wb)


def _mlp_fwd_bwd(x, out1, tgt, vecs, b_up, wb, tm):
    s, d = x.shape
    nq = wb.shape[0]
    dff = nq * d

    def body(x_ref, o1_ref, t_ref, v_ref, bup_ref, wup_ref, wdn_ref,
             h2_ref, f_ref, df0_ref, do2_ref, do1_ref, dxp_ref, acc_ref, dbup_ref, f0_ref):
        @pl.when(pl.program_id(0) == 0)
        def _():
            acc_ref[...] = jnp.zeros(acc_ref.shape, F32)
            dbup_ref[...] = jnp.zeros(dbup_ref.shape, F32)

        def vec(r):
            return v_ref[r:r + 1, :]

        def accum(r, val):
            acc_ref[r:r + 1, :] += _rsum(val)

        out1v = o1_ref[...]
        r1 = ALPHA * x_ref[...] + (1.0 + vec(V_GATE1)) * out1v
        xh1, rstd1 = _ln(r1)
        x1 = xh1 * vec(V_LN1G) + vec(V_LN1B)
        xn1, rstdn = _ln(x1)
        h2 = (xn1 * (1.0 + vec(V_SCALE2)) + vec(V_SHIFT2)).astype(BF16)
        h2_ref[...] = h2
        out2 = jnp.broadcast_to(vec(V_BDN), (tm, d))
        for q in range(nq):
            f0 = _dot(h2, wup_ref[q]) + bup_ref[:, q * d:(q + 1) * d]
            rl = jnp.maximum(f0, 0.0)
            f0_ref[:, q * d:(q + 1) * d] = rl
            fb = (rl * rl).astype(BF16)
            f_ref[:, q * d:(q + 1) * d] = fb
            out2 = out2 + _dot(fb, wdn_ref[q])
        r2 = ALPHA * x1 + (1.0 + vec(V_GATE2)) * out2
        xh2, rstd2 = _ln(r2)
        yv = xh2 * vec(V_LN2G) + vec(V_LN2B)
        err = yv - t_ref[...]
        accum(M_LOSS, err * err)
        dy = err * (1.0 / d)
        accum(M_LN2G, dy * xh2)
        accum(M_LN2B, dy)
        dr2 = _ln_bwd(dy * vec(V_LN2G), xh2, rstd2)
        accum(M_GATE2, dr2 * out2)
        dout2 = (1.0 + vec(V_GATE2)) * dr2
        accum(M_BDN, dout2)
        do2b = dout2.astype(BF16)
        do2_ref[...] = do2b
        dh2 = jnp.zeros((tm, d), F32)
        for q in range(nq):
            df0 = _dot_nt(do2b, wdn_ref[q]) * (2.0 * f0_ref[:, q * d:(q + 1) * d])
            dbup_ref[q:q + 1, :] += _rsum(df0)
            df0b = df0.astype(BF16)
            df0_ref[:, q * d:(q + 1) * d] = df0b
            dh2 = dh2 + _dot_nt(df0b, wup_ref[q])
        accum(M_SHIFT2, dh2)
        accum(M_SCALE2, dh2 * xn1)
        dx1 = ALPHA * dr2 + _ln_bwd(dh2 * (1.0 + vec(V_SCALE2)), xn1, rstdn)
        accum(M_LN1G, dx1 * xh1)
        accum(M_LN1B, dx1)
        dr1 = _ln_bwd(dx1 * vec(V_LN1G), xh1, rstd1)
        accum(M_GATE1, dr1 * out1v)
        dout1 = (1.0 + vec(V_GATE1)) * dr1
        accum(M_BO, dout1)
        do1_ref[...] = dout1.astype(BF16)
        dxp_ref[...] = ALPHA * dr1

    def wspec(j):
        return pl.BlockSpec((nq, d, d), lambda i: (0, j, 0), pipeline_mode=pl.Buffered(1))

    b16 = lambda w: jax.ShapeDtypeStruct((s, w), BF16)
    return _pcall(
        body, name="mlp_fwd_bwd", grid=(s // tm,),
        in_specs=[_rows(tm, d), _rows(tm, d), _rows(tm, d), _full(vecs.shape), _full(b_up.shape), wspec(0), wspec(1)],
        out_specs=[_rows(tm, d), _rows(tm, dff), _rows(tm, dff), _rows(tm, d), _rows(tm, d), _rows(tm, d),
                   _full((16, d)), _full((SUBLANES, d))],
        out_shape=[b16(d), b16(dff), b16(dff), b16(d), b16(d), jax.ShapeDtypeStruct((s, d), F32),
                   jax.ShapeDtypeStruct((16, d), F32), jax.ShapeDtypeStruct((SUBLANES, d), F32)],
        scratch_shapes=[pltpu.VMEM((tm, dff), F32)],
        compiler_params=_seq(1),
    )(x, out1, tgt, vecs, b_up, wb, wb)


def _mix_bwd(dout1, z, u1, ya, yb, vecs, caw, cbw, wb, tm, rc):
    s = z.shape[0]
    d = vecs.shape[1]
    nq = wb.shape[0]
    kq = d // nq
    base = 2 * d // kq
    nt = s // tm
    hb = tm // HALO_A

    def body(do1_ref, z_ref, zh_ref, u1_ref, ya_ref, yb_ref, v_ref, caw_ref, cbw_ref, wao_ref, wbo_ref, wo_ref,
             dz_ref, dya_ref, dyb_ref, acc_ref, dcaw_ref, dcbw_ref, dbin_ref,
             ext_ref, du1p_ref, sh_ref, pext_ref, dqp_ref, tmp_ref):
        i = pl.program_id(0)

        @pl.when(i == 0)
        def _():
            acc_ref[...] = jnp.zeros(acc_ref.shape, F32)
            dcaw_ref[...] = jnp.zeros(dcaw_ref.shape, F32)
            dcbw_ref[...] = jnp.zeros(dcbw_ref.shape, F32)
            dbin_ref[...] = jnp.zeros(dbin_ref.shape, F32)
            du1p_ref[0:SUBLANES, :] = jnp.zeros((SUBLANES, d), F32)
            du1p_ref[SUBLANES + tm:SUBLANES + tm + HALO_A, :] = jnp.zeros((HALO_A, d), F32)
            dqp_ref[0:SUBLANES, :] = jnp.zeros((SUBLANES, d), F32)
            dqp_ref[SUBLANES + tm:SUBLANES + tm + HALO_B, :] = jnp.zeros((HALO_B, d), F32)

        def vec(r):
            return v_ref[r:r + 1, :]

        def accum(r, val):
            acc_ref[r:r + 1, :] += _rsum(val)

        def put_dz(j, val):
            vb = val.astype(BF16)
            dz_ref[:, j * d:(j + 1) * d] = vb
            dbin_ref[j:j + 1, :] += _rsum_mxu(vb)

        has_history = i < nt - 1

        do1 = do1_ref[...]
        dmg = jnp.concatenate([_dot_nt(do1, wo_ref[q]) for q in range(nq)], axis=1)
        sga = _sig(z_ref[:, 5 * d:6 * d])
        sgb = _sig(z_ref[:, 6 * d:7 * d])
        dya = dmg * sga
        dyb = dmg * sgb
        accum(X_BAO, dya)
        put_dz(5, dya * ya_ref[...] * (1.0 - sga))
        put_dz(6, dyb * yb_ref[...] * (1.0 - sgb))
        dyab = dya.astype(BF16)
        dybb = dyb.astype(BF16)
        dya_ref[...] = dyab
        dyb_ref[...] = dybb

        du3 = jnp.concatenate([_dot_nt(dyab, wao_ref[q]) for q in range(nq)], axis=1)
        xa, rstda = _ln(u1_ref[...])
        u2 = xa * vec(V_LNAG) + vec(V_LNAB)
        s2 = _sig(u2)
        du2 = du3 * (s2 * (1.0 + u2 * (1.0 - s2)))
        accum(X_LNAG, du2 * xa)
        accum(X_LNAB, du2)
        du1 = _ln_bwd(du2 * vec(V_LNAG), xa, rstda)
        accum(X_CAB, du1)
        du1p_ref[SUBLANES:SUBLANES + tm, :] = du1
        sg = _sig(z_ref[:, d:2 * d])
        aval = z_ref[:, 0:d]
        ext_ref[HALO_A:HALO_A + tm, :] = aval * sg
        ext_ref[0:HALO_A, :] = jnp.where(has_history, zh_ref[:, 0:d] * _sig(zh_ref[:, d:2 * d]), 0.0)
        _conv_adjoint(du1p_ref, sh_ref, ext_ref, caw_ref, tmp_ref, dcaw_ref, CONV_A, HALO_A, tm, d, rc)
        du1p_ref[SUBLANES + tm:SUBLANES + tm + HALO_A, :] = du1p_ref[SUBLANES:SUBLANES + HALO_A, :]
        du0 = tmp_ref[...]
        put_dz(0, du0 * sg)
        put_dz(1, du0 * aval * sg * (1.0 - sg))

        dv = jnp.concatenate([_dot_nt(dybb, wbo_ref[q]) for q in range(nq)], axis=1)
        bgc = z_ref[:, 3 * d:4 * d]
        bx = z_ref[:, 4 * d:5 * d]
        pext_ref[HALO_B:HALO_B + tm, :] = bgc * bx
        pext_ref[0:HALO_B, :] = jnp.where(
            has_history, zh_ref[HALO_A - HALO_B:HALO_A, 3 * d:4 * d] * zh_ref[HALO_A - HALO_B:HALO_A, 4 * d:5 * d], 0.0)
        _conv_causal(pext_ref, sh_ref, cbw_ref, CONV_B, HALO_B, None, tmp_ref, tm, d, rc)
        put_dz(2, dv * tmp_ref[...])
        dqp_ref[SUBLANES:SUBLANES + tm, :] = dv * z_ref[:, 2 * d:3 * d]
        _conv_adjoint(dqp_ref, sh_ref, pext_ref, cbw_ref, tmp_ref, dcbw_ref, CONV_B, HALO_B, tm, d, rc)
        dqp_ref[SUBLANES + tm:SUBLANES + tm + HALO_B, :] = dqp_ref[SUBLANES:SUBLANES + HALO_B, :]
        dp = tmp_ref[...]
        put_dz(3, dp * bx)
        put_dz(4, dp * bgc)

    def rev(width):
        return pl.BlockSpec((tm, width), lambda i: (nt - 1 - i, 0))

    def wspec(j):
        return pl.BlockSpec((nq, kq, d), lambda i: (0, base + j, 0), pipeline_mode=pl.Buffered(1))

    halo = pl.BlockSpec((HALO_A, 7 * d), lambda i: (jnp.maximum((nt - 1 - i) * hb - 1, 0), 0))
    b16 = jax.ShapeDtypeStruct((s, d), BF16)
    acc8 = jax.ShapeDtypeStruct((SUBLANES, d), F32)
    return _pcall(
        body, name="mix_bwd", grid=(nt,),
        in_specs=[rev(d), rev(7 * d), halo, rev(d), rev(d), rev(d), _full(vecs.shape), _full(caw.shape), _full(cbw.shape),
                  wspec(0), wspec(1), wspec(2)],
        out_specs=[rev(7 * d), rev(d), rev(d), _full((SUBLANES, d)), _full((HALO_A, d)), _full((HALO_B, d)),
                   _full((SUBLANES, d))],
        out_shape=[jax.ShapeDtypeStruct((s, 7 * d), BF16), b16, b16, acc8,
                   jax.ShapeDtypeStruct((HALO_A, d), F32), jax.ShapeDtypeStruct((HALO_B, d), F32), acc8],
        scratch_shapes=[pltpu.VMEM((HALO_A + tm, d), F32), pltpu.VMEM((SUBLANES + tm + HALO_A, d), F32),
                        pltpu.VMEM((SHIFTED_COPIES, tm + HALO_A, d), F32), pltpu.VMEM((HALO_B + tm, d), F32),
                        pltpu.VMEM((SUBLANES + tm + HALO_B, d), F32), pltpu.VMEM((tm, d), F32)],
        compiler_params=_seq(1),
    )(dout1, z, z, u1, ya, yb, vecs, caw, cbw, wb, wb, wb)


def _in_bwd(dz, x, dxp, vecs, wa, tm, plans, plan_srcs):
    s, d = x.shape
    nq, _, nw = wa.shape
    nt = s // tm
    nc = len(plan_srcs)

    def body(dz_ref, x_ref, dxp_ref, v_ref, w_ref, *rest):
        src_refs, (gx_ref, acc_ref), dst_refs = rest[:nc], rest[nc:nc + 2], rest[nc + 2:2 * nc + 2]
        send_sems, recv_sems = rest[2 * nc + 2:]
        i = pl.program_id(0)
        comm = (src_refs, dst_refs, send_sems, recv_sems)
        _carried_start(plans, i == 0, comm)

        @pl.when(i == 0)
        def _():
            acc_ref[...] = jnp.zeros(acc_ref.shape, F32)

        dh1 = jnp.zeros((tm, d), F32)
        for q in range(nq):
            dh1 = dh1 + _dot_nt(dz_ref[:, q * nw:(q + 1) * nw], w_ref[q])
        xh, rstd = _ln(x_ref[...])
        acc_ref[I_SHIFT1:I_SHIFT1 + 1, :] += _rsum(dh1)
        acc_ref[I_SCALE1:I_SCALE1 + 1, :] += _rsum(dh1 * xh)
        gx_ref[...] = dxp_ref[...] + _ln_bwd(dh1 * (1.0 + v_ref[V_SCALE1:V_SCALE1 + 1, :]), xh, rstd)
        _carried_wait(plans, i == nt - 1, comm)

    any_spec = pl.BlockSpec(memory_space=pl.ANY)
    return _pcall(
        body, name="in_bwd", grid=(nt,),
        in_specs=[_rows(tm, nq * nw), _rows(tm, d), _rows(tm, d), _full(vecs.shape), _full(wa.shape, single=True)]
        + [any_spec] * nc,
        out_specs=[_rows(tm, d), _full((SUBLANES, d))] + [any_spec] * nc,
        out_shape=[jax.ShapeDtypeStruct((s, d), F32), jax.ShapeDtypeStruct((SUBLANES, d), F32)]
        + [plan.out_shape(p) for plan, p in zip(plans, plan_srcs)],
        scratch_shapes=_plan_sems(plans),
        compiler_params=_seq(1),
    )(dz, x, dxp, vecs, wa, *plan_srcs)


def _dw(a, b, split_a, ts, name, into=None, rows_total=None, row_block=0):
    s = a.shape[0]
    ka = a.shape[1] // N_CHIPS if split_a else a.shape[1]
    nb = b.shape[1] if split_a else b.shape[1] // N_CHIPS
    rows_total = ka if rows_total is None else rows_total

    def body(a_ref, b_ref, *rest):
        o_ref = rest[-1]

        @pl.when(pl.program_id(1) == 0)
        def _():
            o_ref[...] = jnp.zeros(o_ref.shape, F32)

        o_ref[...] += _dot_tn(a_ref[...], b_ref[...])

    a_spec = pl.BlockSpec((ts, ka), (lambda q, i: (i, q)) if split_a else (lambda q, i: (i, 0)))
    b_spec = pl.BlockSpec((ts, nb), (lambda q, i: (i, 0)) if split_a else (lambda q, i: (i, q)))
    extra = {} if into is None else dict(input_output_aliases={2: 0})
    return _pcall(
        body, name=name, grid=(N_CHIPS, s // ts),
        in_specs=[a_spec, b_spec] + ([] if into is None else [pl.BlockSpec(memory_space=pl.ANY)]),
        out_specs=pl.BlockSpec((None, ka, nb), lambda q, i: (q, row_block, 0)),
        out_shape=jax.ShapeDtypeStruct((N_CHIPS, rows_total, nb), F32),
        compiler_params=_seq(2), **extra,
    )(*((a, b) if into is None else (a, b, into)))


def _dw_carrying(a, b, ts, name, plan, plan_src):
    s, k = a.shape
    nb = b.shape[1] // N_CHIPS
    ns = s // ts

    def body(a_ref, b_ref, src_ref, o_ref, dst_ref, send_sems, recv_sems):
        q, i = pl.program_id(0), pl.program_id(1)
        comm = ([src_ref], [dst_ref], send_sems, recv_sems)
        _carried_start([plan], jnp.logical_and(q == 0, i == 0), comm)

        @pl.when(i == 0)
        def _():
            o_ref[...] = jnp.zeros(o_ref.shape, F32)

        o_ref[...] += _dot_tn(a_ref[...], b_ref[...])
        _carried_wait([plan], jnp.logical_and(q == N_CHIPS - 1, i == ns - 1), comm)

    any_spec = pl.BlockSpec(memory_space=pl.ANY)
    return _pcall(
        body, name=name, grid=(N_CHIPS, ns),
        in_specs=[pl.BlockSpec((ts, k), lambda q, i: (i, 0)), pl.BlockSpec((ts, nb), lambda q, i: (i, q)), any_spec],
        out_specs=[pl.BlockSpec((None, k, nb), lambda q, i: (q, 0, 0)), any_spec],
        out_shape=[jax.ShapeDtypeStruct((N_CHIPS, k, nb), F32), plan.out_shape(plan_src)],
        scratch_shapes=_plan_sems([plan]),
        compiler_params=_seq(2),
    )(a, b, plan_src)


def _dw_rows(a, b, ts, name, into, row_block):
    s, k = a.shape
    n = b.shape[1]
    kq = k // N_CHIPS

    def body(a_ref, b_ref, buf_ref, o_ref):
        @pl.when(pl.program_id(0) == 0)
        def _():
            o_ref[...] = jnp.zeros(o_ref.shape, F32)

        res = _dot_tn(a_ref[...], b_ref[...])
        for q in range(N_CHIPS):
            o_ref[q] += res[q * kq:(q + 1) * kq, :]

    return _pcall(
        body, name=name, grid=(s // ts,),
        in_specs=[_rows(ts, k), _rows(ts, n), pl.BlockSpec(memory_space=pl.ANY)],
        out_specs=pl.BlockSpec((N_CHIPS, kq, n), lambda i: (0, row_block, 0)),
        out_shape=jax.ShapeDtypeStruct(into.shape, F32), input_output_aliases={2: 0},
        compiler_params=_seq(1),
    )(a, b, into)


def _adam_math(w, g, m, v):
    m2 = ADAM_B1 * m + (1.0 - ADAM_B1) * g
    v2 = ADAM_B2 * v + (1.0 - ADAM_B2) * (g * g)
    m_hat = m2 / (1.0 - ADAM_B1 ** ADAM_STEP)
    v_hat = v2 / (1.0 - ADAM_B2 ** ADAM_STEP)
    delta = -ADAM_LR * (m_hat / (jnp.sqrt(v_hat) + ADAM_EPS) + ADAM_WD * w)
    return delta, m2, v2


def _adam(w, g, m, v, g_row0, tr, name):
    r, c = w.shape
    blk0 = g_row0 // tr

    def body(w_ref, g_ref, m_ref, v_ref, go_ref, d_ref, mo_ref, vo_ref):
        gv = g_ref[...]
        go_ref[...] = gv
        d_ref[...], mo_ref[...], vo_ref[...] = _adam_math(w_ref[...], gv, m_ref[...], v_ref[...])

    spec = _rows(tr, c)
    g_spec = pl.BlockSpec((tr, c), lambda i: (blk0 + i, 0))
    o = jax.ShapeDtypeStruct((r, c), F32)
    return _pcall(body, name=name, grid=(r // tr,), in_specs=[spec, g_spec, spec, spec], out_specs=[spec] * 4,
                  out_shape=[o, o, o, o], compiler_params=_seq(1))(w, g, m, v)


def _small_update(gathered_head, gathered, q_idx, small_w, small_m, small_v, conv_w, conv_m, conv_v):
    d = gathered.shape[2]
    ns = len(_SMALL)
    cw = conv_w[0].shape[1]
    conv_rows = ((T_CAW, CONV_A), (T_CBW, CONV_B))

    def body(q_ref, h_ref, g_ref, *refs):
        ins, outs = refs[:3 * (ns + 2)], refs[3 * (ns + 2):]
        tot_ref, loss_ref = outs[0], outs[1]
        outs = outs[2:]
        head, tot = h_ref[0], g_ref[0]
        for dev in range(1, N_DEV):
            head = head + h_ref[dev]
            tot = tot + g_ref[dev]
        tot_ref[0:T_M, :] = head
        tot_ref[T_M:T_ROWS, :] = tot
        loss_ref[...] = (0.5 / d) * jnp.sum(tot_ref[T_LOSS:T_LOSS + 1, :], axis=1, keepdims=True)
        for p, (_, rows) in enumerate(_SMALL):
            w_ref, m_ref, v_ref = ins[p], ins[ns + 2 + p], ins[2 * (ns + 2) + p]
            go, do, mo, vo = outs[4 * p:4 * p + 4]
            for j, row in enumerate(rows):
                sl = slice(j * d, (j + 1) * d)
                gv = tot_ref[row:row + 1, :]
                go[:, sl] = gv
                do[:, sl], mo[:, sl], vo[:, sl] = _adam_math(w_ref[:, sl], gv, m_ref[:, sl], v_ref[:, sl])
        for p, (row, taps) in enumerate(conv_rows):
            w_ref, m_ref, v_ref = ins[ns + p], ins[ns + 2 + ns + p], ins[2 * (ns + 2) + ns + p]
            go, do, mo, vo = outs[4 * (ns + p):4 * (ns + p) + 4]
            gv = tot_ref[row:row + taps, 0:cw]
            for qq in range(1, N_CHIPS):
                gv = jnp.where(q_ref[0] == qq, tot_ref[row:row + taps, qq * cw:(qq + 1) * cw], gv)
            go[...] = gv
            do[...], mo[...], vo[...] = _adam_math(w_ref[...], gv, m_ref[...], v_ref[...])

    params = list(small_w) + list(conv_w) + list(small_m) + list(conv_m) + list(small_v) + list(conv_v)
    out_shape = [jax.ShapeDtypeStruct((T_ROWS, d), F32), jax.ShapeDtypeStruct((1, 1), F32)]
    for w in list(small_w) + list(conv_w):
        out_shape += [jax.ShapeDtypeStruct(w.shape, F32)] * 4
    vm = pl.BlockSpec(memory_space=pltpu.VMEM)
    return _pcall(
        body, name="small_update", out_shape=out_shape,
        in_specs=[pl.BlockSpec(memory_space=pltpu.SMEM), vm, vm] + [vm] * len(params),
        out_specs=[vm] * len(out_shape), compiler_params=_cparams(),
    )(q_idx, gathered_head, gathered, *params)


def kernel(x, c, w_ada, b_ada, w_in, b_in, conv_a_w, conv_a_b, ln_a_g, ln_a_b, w_a_out, b_a_out, conv_b_w, w_b_out, w_o, b_o, ln1_g, ln1_b, w_up, b_up, w_down, b_down, ln2_g, ln2_b, loss_target, m_w_ada, m_b_ada, m_w_in, m_b_in, m_conv_a_w, m_conv_a_b, m_ln_a_g, m_ln_a_b, m_w_a_out, m_b_a_out, m_conv_b_w, m_w_b_out, m_w_o, m_b_o, m_ln1_g, m_ln1_b, m_w_up, m_b_up, m_w_down, m_b_down, m_ln2_g, m_ln2_b, v_w_ada, v_b_ada, v_w_in, v_b_in, v_conv_a_w, v_conv_a_b, v_ln_a_g, v_ln_a_b, v_w_a_out, v_b_a_out, v_conv_b_w, v_w_b_out, v_w_o, v_b_o, v_ln1_g, v_ln1_b, v_w_up, v_b_up, v_w_down, v_b_down, v_ln2_g, v_ln2_b):
    given = dict(locals())
    s, d = x.shape[1], x.shape[2]
    xi, yi, ci = _my_pos()
    q = 2 * xi + yi
    me = 4 * xi + 2 * yi + ci
    i32 = jnp.int32
    q_arr = jnp.reshape(q, (1,)).astype(i32)
    others = [2 * ox + oy for ox, oy in _other_chips(xi, yi)]
    halves_idx = jnp.stack([ci] + others).astype(i32)
    chips_idx = jnp.stack([q, ci]).astype(i32)
    kq = d // N_CHIPS
    tm = min(256, s)
    rc = tm

    def sq(a):
        return a.reshape(a.shape[1:])

    x2, tgt = sq(x), sq(loss_target)

    wa = _place_shard([sq(w_in)], q_arr, "place_w_in")
    wb = _place_shard([sq(w_up), sq(w_down), sq(w_a_out), sq(w_b_out), sq(w_o)], q_arr, "place_w_rest")

    n_ada = w_ada.shape[2]
    pre = jnp.concatenate([
        jnp.broadcast_to(c, (SUBLANES, d)),
        jnp.pad(sq(conv_a_w), ((0, HALO_A - CONV_A), (0, d - kq))),
        jnp.pad(sq(conv_b_w), ((0, HALO_B - CONV_B), (0, d - kq)))], axis=0)
    b_ada_sh = lax.dynamic_slice(b_ada, (0, q * n_ada), (1, n_ada))
    pre_all, c_all, mod_all, wa = _prologue(pre, sq(w_ada), b_ada_sh, wa)
    caw = jnp.concatenate([pre_all[2 * p, SUBLANES:SUBLANES + HALO_A, :kq] for p in range(N_CHIPS)], axis=1)
    cbw = jnp.concatenate([pre_all[2 * p, SUBLANES + HALO_A:, :kq] for p in range(N_CHIPS)], axis=1)
    mod_rows = lax.dynamic_slice(mod_all, (0, me, 0), (N_DEV, 1, n_ada))[0::2, 0, :]
    mod = mod_rows.reshape(6, d)
    vecs = jnp.concatenate([mod, conv_a_b, ln_a_g, ln_a_b, b_a_out, b_o, ln1_g, ln1_b, b_down, ln2_g, ln2_b], axis=0)

    h1, z, wb = _fwd_in(x2, vecs, wa, b_in, wb, tm)
    u1, ya, yb, out1, u3, vv, mg = _fwd_mix(z, vecs, caw, cbw, wb, tm, rc)

    ts = min(2048, s)
    rest_rows = wb.shape[1]
    small0 = 2 * d // kq
    h2, fb, df0, do2, do1, dxp, macc, dbup = _mlp_fwd_bwd(x2, out1, tgt, vecs, b_up, wb, tm)
    gb = _dw(h2, df0, False, ts, "dw_up", rows_total=rest_rows)
    gb = _dw(fb, do2, True, ts, "dw_down", into=gb, rows_total=rest_rows, row_block=1)
    dz, dya, dyb, xacc, dcaw, dcbw, dbin = _mix_bwd(do1, z, u1, ya, yb, vecs, caw, cbw, wb, tm, rc)
    gb = _dw_rows(u3, dya, ts, "dw_a_out", gb, small0)
    gb = _dw_rows(vv, dyb, ts, "dw_b_out", gb, small0 + 1)
    gb = _dw_rows(mg, do1, ts, "dw_o", gb, small0 + 2)

    trb, tra = rest_rows // 8, d // 8
    ga, rb = _dw_carrying(h1, dz, ts, "dw_in", _SiblingHalf, gb)
    pb, ra = _add_halves(gb, rb, halves_idx, trb, "rs_add_halves_rest", _SiblingHalf, ga)
    pa = _add_halves(ga, ra, halves_idx, tra, "rs_add_halves_in")
    table = jnp.concatenate([macc, dbin, dcaw, xacc, dcbw, dbup], axis=0)
    tm_in = min(2 * tm, s)
    gx, iacc, r3a, r3b, gathered = _in_bwd(dz, x2, dxp, vecs, wa, tm_in, [_ChipBlocks, _ChipBlocks, _TableToAll],
                                           [pa, pb, table])
    fa = _add_chips(ga, ra, r3a, chips_idx, tra, "rs_add_chips_in")
    fb_ = _add_chips(gb, rb, r3b, chips_idx, trb, "rs_add_chips_rest")
    g_in, g_b = _rs_join_halves(fa, fb_)

    gathered_head = _all_gather_small(iacc, "gather_ln0_sums")
    names = [n for n, _ in _SMALL]
    res = _small_update(
        gathered_head, gathered, q_arr,
        [given[n] for n in names], [given["m_" + n] for n in names], [given["v_" + n] for n in names],
        [sq(conv_a_w), sq(conv_b_w)], [sq(m_conv_a_w), sq(m_conv_b_w)], [sq(v_conv_a_w), sq(v_conv_b_w)])
    loss = res[1].reshape(())
    upd = {}
    for p, n in enumerate(names + ["conv_a_w", "conv_b_w"]):
        upd[n] = res[2 + 4 * p:6 + 4 * p]

    dmod_all = jnp.stack([gathered_head[:, r, :] if r < T_M else gathered[:, r - T_M, :] for r in _SMALL[0][1]],
                         axis=1).reshape(N_DEV, 6 * d)
    dmod_sh = lax.dynamic_slice(dmod_all, (0, q * n_ada), (N_DEV, n_ada))
    g_ada = _ada_bwd(c_all.T, dmod_sh)
    upd["w_ada"] = _adam(sq(w_ada), g_ada, sq(m_w_ada), sq(v_w_ada), 0, min(256, d), "adam_w_ada")

    upd["w_in"] = _adam(sq(w_in), g_in, sq(m_w_in), sq(v_w_in), 0, min(256, d), "adam_w_in")
    r0 = 0
    for n in ("w_up", "w_down", "w_a_out", "w_b_out", "w_o"):
        w = sq(given[n])
        upd[n] = _adam(w, g_b, sq(given["m_" + n]), sq(given["v_" + n]), r0, min(256, w.shape[0]), "adam_" + n)
        r0 += w.shape[0]

    order = ["w_ada", "b_ada", "w_in", "b_in", "conv_a_w", "conv_a_b", "ln_a_g", "ln_a_b", "w_a_out", "b_a_out", "conv_b_w",
             "w_b_out", "w_o", "b_o", "ln1_g", "ln1_b", "w_up", "b_up", "w_down", "b_down", "ln2_g", "ln2_b"]
    outs = [loss, gx.reshape(x.shape)]
    for k in range(4):
        outs += [upd[n][k].reshape(given[n].shape) for n in order]
    return tuple(outs)
```

```python
import jax
import jax.numpy as jnp
from jax import lax
from jax.experimental import pallas as pl
from jax.experimental.pallas import tpu as pltpu

F32 = jnp.float32
BF16 = jnp.bfloat16
MESH = pl.DeviceIdType.MESH

LN_EPS = 1e-5
DEPTH = 1
ALPHA = (2.0 * DEPTH) ** 0.25
CONV_A = 31
CONV_B = 3
SUBLANES = 8
HALO_A = 32
HALO_B = 8
SHIFTED_COPIES = 4
N_CHIPS = 4
N_DEV = 8
ADAM_LR = 0.001
ADAM_B1 = 0.9
ADAM_B2 = 0.999
ADAM_EPS = 1e-08
ADAM_WD = 0.01
ADAM_STEP = 10
VMEM_LIMIT = 56 * 1024 * 1024

V_SHIFT1, V_SCALE1, V_GATE1, V_SHIFT2, V_SCALE2, V_GATE2 = 0, 1, 2, 3, 4, 5
V_CAB, V_LNAG, V_LNAB, V_BAO, V_BO, V_LN1G, V_LN1B, V_BDN, V_LN2G, V_LN2B = 6, 7, 8, 9, 10, 11, 12, 13, 14, 15

M_LN2G, M_LN2B, M_GATE2, M_BDN, M_SHIFT2, M_SCALE2, M_LN1G, M_LN1B, M_GATE1, M_BO, M_LOSS = range(11)
X_BAO, X_LNAG, X_LNAB, X_CAB = range(4)
I_SHIFT1, I_SCALE1 = 0, 1

T_I, T_M, T_BIN, T_CAW, T_X, T_CBW, T_BUP, T_ROWS = 0, 8, 24, 32, 64, 72, 80, 88
T_LOSS = T_M + M_LOSS
_SMALL = (
    ("b_ada", (T_I + I_SHIFT1, T_I + I_SCALE1, T_M + M_GATE1, T_M + M_SHIFT2, T_M + M_SCALE2, T_M + M_GATE2)),
    ("b_in", tuple(T_BIN + j for j in range(7))),
    ("conv_a_b", (T_X + X_CAB,)), ("ln_a_g", (T_X + X_LNAG,)), ("ln_a_b", (T_X + X_LNAB,)), ("b_a_out", (T_X + X_BAO,)),
    ("b_o", (T_M + M_BO,)), ("ln1_g", (T_M + M_LN1G,)), ("ln1_b", (T_M + M_LN1B,)),
    ("b_up", tuple(T_BUP + j for j in range(4))),
    ("b_down", (T_M + M_BDN,)), ("ln2_g", (T_M + M_LN2G,)), ("ln2_b", (T_M + M_LN2B,)),
)


def _pcall(body, **kw):
    return pl.pallas_call(body, **kw)


def _cparams(**kw):
    return pltpu.CompilerParams(vmem_limit_bytes=VMEM_LIMIT, **kw)


def _seq(n):
    return _cparams(dimension_semantics=("arbitrary",) * n)


def _full(shape, single=False):
    nd = len(shape)
    if single:
        return pl.BlockSpec(shape, lambda *_: (0,) * nd, pipeline_mode=pl.Buffered(1))
    return pl.BlockSpec(shape, lambda *_: (0,) * nd)


def _rows(tm, width):
    return pl.BlockSpec((tm, width), lambda i: (i, 0))


def _sig(x):
    return jax.nn.sigmoid(x)


def _ln(x):
    mu = jnp.mean(x, axis=-1, keepdims=True)
    xc = x - mu
    var = jnp.mean(xc * xc, axis=-1, keepdims=True)
    rstd = lax.rsqrt(var + LN_EPS)
    return xc * rstd, rstd


def _ln_bwd(dxh, xh, rstd):
    m1 = jnp.mean(dxh, axis=-1, keepdims=True)
    m2 = jnp.mean(dxh * xh, axis=-1, keepdims=True)
    return rstd * (dxh - m1 - xh * m2)


def _rsum(v):
    return jnp.sum(v, axis=0, keepdims=True)


def _rsum_mxu(v):
    vb = v if v.dtype == BF16 else v.astype(BF16)
    return _dot(jnp.ones((2 * SUBLANES, v.shape[0]), BF16), vb)[0:1, :]


def _dot(a, b):
    return jnp.dot(a, b, preferred_element_type=F32)


def _dot_nt(a, b):
    return lax.dot_general(a, b, (((1,), (1,)), ((), ())), preferred_element_type=F32)


def _dot_tn(a, b):
    return lax.dot_general(a, b, (((0,), (0,)), ((), ())), preferred_element_type=F32)


def _my_pos():
    return lax.axis_index("x"), lax.axis_index("y"), lax.axis_index("c")


def _other_chips(x, y):
    return [(1 - x, y), (x, 1 - y), (1 - x, 1 - y)]


def _small_gather(v_ref, out_ref, send_sems, recv_sems, local_sem):
    x, y, cc = _my_pos()
    me = 4 * x + 2 * y + cc
    mine = pltpu.make_async_copy(v_ref, out_ref.at[me], local_sem)
    mine.start()
    sends = []
    for rel in range(1, N_DEV):
        rx, ry, rc = (rel >> 2) & 1, (rel >> 1) & 1, rel & 1
        peer = (1 - x if rx else x, 1 - y if ry else y, 1 - cc if rc else cc)
        cp = pltpu.make_async_remote_copy(
            src_ref=v_ref, dst_ref=out_ref.at[me], send_sem=send_sems.at[rel - 1], recv_sem=recv_sems.at[rel - 1],
            device_id=peer, device_id_type=MESH)
        cp.start()
        sends.append(cp)
    for rel in range(1, N_DEV):
        rx, ry, rc = (rel >> 2) & 1, (rel >> 1) & 1, rel & 1
        peer = (1 - x if rx else x, 1 - y if ry else y, 1 - cc if rc else cc)
        slot = 4 * peer[0] + 2 * peer[1] + peer[2]
        pltpu.make_async_remote_copy(
            src_ref=v_ref, dst_ref=out_ref.at[slot], send_sem=send_sems.at[rel - 1], recv_sem=recv_sems.at[rel - 1],
            device_id=peer, device_id_type=MESH).wait_recv()
    for cp in sends:
        cp.wait_send()
    mine.wait()


_SMALL_GATHER_SEMS = [pltpu.SemaphoreType.DMA((N_DEV - 1,)), pltpu.SemaphoreType.DMA((N_DEV - 1,)),
                      pltpu.SemaphoreType.DMA]


def _all_gather_small(v, name):
    r, c = v.shape

    def body(*refs):
        _small_gather(*refs)

    return _pcall(
        body, name=name,
        out_shape=jax.ShapeDtypeStruct((N_DEV, r, c), v.dtype),
        in_specs=[pl.BlockSpec(memory_space=pltpu.VMEM)],
        out_specs=pl.BlockSpec(memory_space=pltpu.VMEM),
        scratch_shapes=list(_SMALL_GATHER_SEMS),
        compiler_params=_cparams(),
    )(v)


def _prologue(pre, w_sh, b_sh, wa):
    r, d = pre.shape
    n_ada = w_sh.shape[1]
    ng = _GatherShards.n_sems

    def body(pre_ref, w_ref, b_ref, wai_ref, pre_all_ref, c_all_ref, mod_all_ref, wao_ref, mod_ref,
             s1, r1, l1, s2, r2, l2, sg, rg):
        gather = (wai_ref, wao_ref, sg, rg)
        _GatherShards.start(*gather)
        _small_gather(pre_ref, pre_all_ref, s1, r1, l1)
        cv = jnp.concatenate([pre_all_ref[dev, 0:1, :] for dev in range(N_DEV)], axis=0)
        c_all_ref[...] = cv
        ca = cv * _sig(cv)
        mod_ref[...] = jnp.dot(ca, w_ref[...], preferred_element_type=F32, precision=lax.Precision.HIGHEST) + b_ref[...]
        _small_gather(mod_ref, mod_all_ref, s2, r2, l2)
        _GatherShards.relay(*gather)
        _GatherShards.finish(*gather)

    vm = pl.BlockSpec(memory_space=pltpu.VMEM)
    any_spec = pl.BlockSpec(memory_space=pl.ANY)
    return _pcall(
        body, name="prologue",
        out_shape=[jax.ShapeDtypeStruct((N_DEV, r, d), F32), jax.ShapeDtypeStruct((N_DEV, d), F32),
                   jax.ShapeDtypeStruct((N_DEV, N_DEV, n_ada), F32), jax.ShapeDtypeStruct(wa.shape, wa.dtype)],
        in_specs=[vm, vm, vm, any_spec], out_specs=[vm, vm, vm, any_spec], input_output_aliases={3: 3},
        scratch_shapes=[pltpu.VMEM((N_DEV, n_ada), F32)] + list(_SMALL_GATHER_SEMS) + list(_SMALL_GATHER_SEMS)
        + [pltpu.SemaphoreType.DMA((ng,)), pltpu.SemaphoreType.DMA((ng,))],
        compiler_params=_cparams(),
    )(pre, w_sh, b_sh, wa)


def _place_shard(parts, q_idx, name):
    rows = sum(p.shape[0] for p in parts)
    w = parts[0].shape[1]

    def body(q_ref, *refs):
        o_ref = refs[-1]
        r0 = 0
        for p_ref in refs[:-1]:
            n = p_ref.shape[0]
            o_ref[r0:r0 + n, :] = p_ref[...].astype(BF16)
            r0 += n

    grid_spec = pltpu.PrefetchScalarGridSpec(
        num_scalar_prefetch=1, grid=(1,),
        in_specs=[pl.BlockSpec(p.shape, lambda i, q: (0, 0)) for p in parts],
        out_specs=pl.BlockSpec((None, rows, w), lambda i, q: (q[0], 0, 0)))
    return _pcall(body, name=name, grid_spec=grid_spec, out_shape=jax.ShapeDtypeStruct((N_CHIPS, rows, w), BF16),
                  compiler_params=_seq(1))(q_idx, *parts)


class _GatherShards:
    n_sems = 6

    @staticmethod
    def _half(ref, slot, h):
        rows = ref.shape[1] // 2
        return ref.at[slot, pl.ds(h * rows, rows)]

    @classmethod
    def _copies(cls, in_ref, out_ref, send_sems, recv_sems):
        x, y, c = _my_pos()
        q = 2 * x + y
        sibling = (x, y, 1 - c)
        sends, landed, forwards, passed = [], [], [], []
        for j, chip in enumerate(_other_chips(x, y)):
            qj = 2 * chip[0] + chip[1]

            def copy(src, dst, k, to):
                return pltpu.make_async_remote_copy(src_ref=src, dst_ref=dst, send_sem=send_sems.at[k],
                                                    recv_sem=recv_sems.at[k], device_id=to, device_id_type=MESH)

            mine, theirs = cls._half(out_ref, qj, c), cls._half(out_ref, qj, 1 - c)
            sends.append(copy(cls._half(in_ref, q, c), cls._half(out_ref, q, c), j, (*chip, c)))
            landed.append(copy(mine, mine, j, (*chip, c)))
            forwards.append(copy(mine, mine, 3 + j, sibling))
            passed.append(copy(theirs, theirs, 3 + j, sibling))
        return sends, landed, forwards, passed

    @classmethod
    def start(cls, *refs):
        for cp in cls._copies(*refs)[0]:
            cp.start()

    @classmethod
    def relay(cls, *refs):
        _, landed, forwards, _ = cls._copies(*refs)
        for arrived, onward in zip(landed, forwards):
            arrived.wait_recv()
            onward.start()

    @classmethod
    def finish(cls, *refs):
        sends, _, forwards, passed = cls._copies(*refs)
        for cp in passed:
            cp.wait_recv()
        for cp in sends + forwards:
            cp.wait_send()


class _SiblingHalf:
    n_sems = 1

    @staticmethod
    def out_shape(g):
        return jax.ShapeDtypeStruct((g.shape[0], g.shape[1] // 2, g.shape[2]), g.dtype)

    @staticmethod
    def copies(g_ref, r_ref, send_sems, recv_sems, base):
        x, y, c = _my_pos()
        rows = r_ref.shape[1]
        return [pltpu.make_async_remote_copy(
            src_ref=g_ref.at[:, pl.ds((1 - c) * rows, rows)], dst_ref=r_ref,
            send_sem=send_sems.at[base], recv_sem=recv_sems.at[base], device_id=(x, y, 1 - c), device_id_type=MESH)]


class _ChipBlocks:
    n_sems = 3

    @staticmethod
    def out_shape(p):
        return jax.ShapeDtypeStruct(p.shape, p.dtype)

    @staticmethod
    def copies(p_ref, r_ref, send_sems, recv_sems, base):
        x, y, c = _my_pos()
        return [pltpu.make_async_remote_copy(
            src_ref=p_ref.at[j], dst_ref=r_ref.at[j], send_sem=send_sems.at[base + j], recv_sem=recv_sems.at[base + j],
            device_id=(*chip, c), device_id_type=MESH) for j, chip in enumerate(_other_chips(x, y))]


class _TableToAll:
    n_sems = N_DEV

    @staticmethod
    def out_shape(t):
        return jax.ShapeDtypeStruct((N_DEV,) + t.shape, t.dtype)

    @staticmethod
    def copies(t_ref, all_ref, send_sems, recv_sems, base):
        x, y, c = _my_pos()
        me = 4 * x + 2 * y + c
        cps = [pltpu.make_async_copy(t_ref, all_ref.at[me], send_sems.at[base + N_DEV - 1])]
        for rel in range(1, N_DEV):
            rx, ry, rc = (rel >> 2) & 1, (rel >> 1) & 1, rel & 1
            peer = (1 - x if rx else x, 1 - y if ry else y, 1 - c if rc else c)
            cps.append(_SlotCopy(t_ref, all_ref, me, 4 * peer[0] + 2 * peer[1] + peer[2], peer,
                                 send_sems.at[base + rel - 1], recv_sems.at[base + rel - 1]))
        return cps


class _SlotCopy:
    def __init__(self, src_ref, all_ref, my_slot, peer_slot, peer, send_sem, recv_sem):
        self._send = pltpu.make_async_remote_copy(src_ref=src_ref, dst_ref=all_ref.at[my_slot], send_sem=send_sem,
                                                  recv_sem=recv_sem, device_id=peer, device_id_type=MESH)
        self._recv = pltpu.make_async_remote_copy(src_ref=src_ref, dst_ref=all_ref.at[peer_slot], send_sem=send_sem,
                                                  recv_sem=recv_sem, device_id=peer, device_id_type=MESH)

    def start(self):
        self._send.start()

    def wait(self):
        self._send.wait_send()
        self._recv.wait_recv()


def _plan_copies(plans, src_refs, dst_refs, send_sems, recv_sems):
    cps, base = [], 0
    for plan, s_ref, d_ref in zip(plans, src_refs, dst_refs):
        cps += plan.copies(s_ref, d_ref, send_sems, recv_sems, base)
        base += plan.n_sems
    return cps


def _plan_sems(plans):
    n = sum(p.n_sems for p in plans)
    return [pltpu.SemaphoreType.DMA((n,)), pltpu.SemaphoreType.DMA((n,))]


def _carried_start(plans, first, comm):
    @pl.when(first)
    def _():
        for cp in _plan_copies(plans, *comm):
            cp.start()


def _carried_wait(plans, last, comm):
    @pl.when(last)
    def _():
        for cp in _plan_copies(plans, *comm):
            cp.wait()


def _rs_join_halves(fa, fb):
    bufs = (fa, fb)
    nb = len(bufs)

    def body(a_ref, b_ref, ao_ref, bo_ref, send_sems, recv_sems):
        x, y, c = _my_pos()
        srcs, outs = (a_ref, b_ref), (ao_ref, bo_ref)
        cps = []
        for b in range(nb):
            rows = srcs[b].shape[0] // 2
            cp = pltpu.make_async_remote_copy(
                src_ref=srcs[b].at[pl.ds(c * rows, rows)], dst_ref=outs[b].at[pl.ds(c * rows, rows)],
                send_sem=send_sems.at[b], recv_sem=recv_sems.at[b], device_id=(x, y, 1 - c), device_id_type=MESH)
            cp.start()
            cps.append(cp)
        for b in range(nb):
            rows = srcs[b].shape[0] // 2
            theirs = outs[b].at[pl.ds((1 - c) * rows, rows)]
            pltpu.make_async_remote_copy(
                src_ref=theirs, dst_ref=theirs, send_sem=send_sems.at[b], recv_sem=recv_sems.at[b],
                device_id=(x, y, 1 - c), device_id_type=MESH).wait_recv()
        for cp in cps:
            cp.wait_send()

    any_spec = pl.BlockSpec(memory_space=pl.ANY)
    return _pcall(
        body, name="rs_join_halves",
        out_shape=[jax.ShapeDtypeStruct(b.shape, b.dtype) for b in bufs],
        in_specs=[any_spec] * nb, out_specs=[any_spec] * nb, input_output_aliases={0: 0, 1: 1},
        scratch_shapes=[pltpu.SemaphoreType.DMA((nb,)), pltpu.SemaphoreType.DMA((nb,))],
        compiler_params=_cparams(),
    )(*bufs)


def _add_halves(g, r, idx, tr, name, plan=None, plan_src=None):
    _, rows, w = r.shape
    nt = rows // tr

    def body(i_ref, g_ref, r_ref, *rest):
        o_ref = rest[0] if plan is None else rest[1]
        if plan is not None:
            comm = ([rest[0]], [rest[2]], rest[3], rest[4])
            first = jnp.logical_and(pl.program_id(0) == 0, pl.program_id(1) == 0)
            _carried_start([plan], first, comm)
        o_ref[...] = (g_ref[...] + r_ref[...]).astype(BF16)
        if plan is not None:
            last = jnp.logical_and(pl.program_id(0) == 2, pl.program_id(1) == nt - 1)
            _carried_wait([plan], last, comm)

    any_spec = pl.BlockSpec(memory_space=pl.ANY)
    carried = plan is not None
    grid_spec = pltpu.PrefetchScalarGridSpec(
        num_scalar_prefetch=1, grid=(3, nt),
        in_specs=[pl.BlockSpec((None, tr, w), lambda j, i, ix: (ix[1 + j], ix[0] * nt + i, 0)),
                  pl.BlockSpec((None, tr, w), lambda j, i, ix: (ix[1 + j], i, 0))] + [any_spec] * carried,
        out_specs=[pl.BlockSpec((None, tr, w), lambda j, i, ix: (j, i, 0))] + [any_spec] * carried,
        scratch_shapes=_plan_sems([plan]) if carried else [])
    out_shape = [jax.ShapeDtypeStruct((3, rows, w), BF16)] + ([plan.out_shape(plan_src)] if carried else [])
    res = _pcall(body, name=name, grid_spec=grid_spec, out_shape=out_shape,
                 compiler_params=_seq(2))(*((idx, g, r, plan_src) if carried else (idx, g, r)))
    return res if carried else res[0]


def _add_chips(g, r, r3, idx, tr, name):
    _, rows, w = r.shape
    nt = rows // tr

    def body(i_ref, g_ref, r_ref, a_ref, b_ref, c_ref, o_ref):
        own = g_ref[...] + r_ref[...]
        o_ref[...] = ((own + a_ref[...].astype(F32)) + b_ref[...].astype(F32)) + c_ref[...].astype(F32)

    def other(j):
        return pl.BlockSpec((None, tr, w), lambda i, ix: (j, i, 0))

    grid_spec = pltpu.PrefetchScalarGridSpec(
        num_scalar_prefetch=1, grid=(nt,),
        in_specs=[pl.BlockSpec((None, tr, w), lambda i, ix: (ix[0], ix[1] * nt + i, 0)),
                  pl.BlockSpec((None, tr, w), lambda i, ix: (ix[0], i, 0)), other(0), other(1), other(2)],
        out_specs=pl.BlockSpec((tr, w), lambda i, ix: (ix[1] * nt + i, 0)))
    return _pcall(body, name=name, grid_spec=grid_spec,
                  out_shape=jax.ShapeDtypeStruct((2 * rows, w), F32), compiler_params=_seq(1))(idx, g, r, r3, r3, r3)


def _ada_bwd(c_all_t, dmod_sh):
    def body(c_ref, d_ref, o_ref):
        cv = c_ref[...]
        ca = cv * _sig(cv)
        o_ref[...] = jnp.dot(ca, d_ref[...], preferred_element_type=F32, precision=lax.Precision.HIGHEST)

    return _pcall(body, name="ada_bwd", out_shape=jax.ShapeDtypeStruct((c_all_t.shape[0], dmod_sh.shape[1]), F32),
                  compiler_params=_cparams())(c_all_t, dmod_sh)


def _residue_passes(residues, copies):
    passes, current, used = [], [], 0
    for s in residues:
        if s != 0 and used == copies:
            passes.append(current)
            current, used = [], 0
        current.append(s)
        used += s != 0
    return passes + [current] if current else passes


def _conv_causal(ext_ref, sh_ref, w_ref, ntaps, halo, bias, out_ref, tm, d, rc):
    off = halo - (ntaps - 1)
    n = tm + halo - SUBLANES
    taps = {s: [(k, (off + k) // SUBLANES * SUBLANES) for k in range(ntaps) if (off + k) % SUBLANES == s]
            for s in range(SUBLANES)}
    started = False
    for group in _residue_passes([s for s in range(SUBLANES) if taps[s]], sh_ref.shape[0]):
        srcs = {}
        for s in group:
            if s == 0:
                srcs[s] = ext_ref
            else:
                j = len(srcs) - (0 in srcs)
                sh_ref[j, 0:n, :] = ext_ref[s:s + n, :]
                srcs[s] = sh_ref.at[j]
        for r0 in range(0, tm, rc):
            if started:
                acc = out_ref[r0:r0 + rc, :]
            else:
                acc = jnp.zeros((rc, d), F32) if bias is None else jnp.broadcast_to(bias, (rc, d))
            for s, src in srcs.items():
                for k, a in taps[s]:
                    acc = acc + w_ref[k:k + 1, :] * src[r0 + a:r0 + a + rc, :]
            out_ref[r0:r0 + rc, :] = acc
        started = True


def _conv_adjoint(dp_ref, sh_ref, ext_ref, w_ref, dx_ref, dw_ref, ntaps, halo, tm, d, rc):
    off = halo - (ntaps - 1)
    lead = SUBLANES + ntaps - 1
    n = tm + halo
    row = lax.broadcasted_iota(jnp.int32, (SUBLANES, d), 0)
    taps = {s: [(k, (lead - k) // SUBLANES * SUBLANES) for k in range(ntaps) if (lead - k) % SUBLANES == s]
            for s in range(SUBLANES)}
    wtaps = {s: [(k, s + off + k - SUBLANES) for k in range(ntaps) if (-(off + k)) % SUBLANES == s]
             for s in range(SUBLANES)}
    rw = 2 * SUBLANES
    started = False
    for group in _residue_passes([s for s in range(SUBLANES) if taps[s] or wtaps[s]], sh_ref.shape[0]):
        srcs = {}
        for s in group:
            if s == 0:
                srcs[s] = dp_ref
            else:
                j = len(srcs) - (0 in srcs)
                sh_ref[j, 0:n, :] = dp_ref[s:s + n, :]
                srcs[s] = sh_ref.at[j]
        if any(taps[s] for s in srcs):
            for r0 in range(0, tm, rc):
                acc = dx_ref[r0:r0 + rc, :] if started else jnp.zeros((rc, d), F32)
                for s, src in srcs.items():
                    for k, a in taps[s]:
                        acc = acc + w_ref[k:k + 1, :] * src[r0 + a:r0 + a + rc, :]
                dx_ref[r0:r0 + rc, :] = acc
            started = True
        for s, src in srcs.items():
            if not wtaps[s]:
                continue
            sums = [jnp.zeros((SUBLANES, d), F32) for _ in wtaps[s]]
            for r0 in range(0, tm, rw):
                g = src[r0:r0 + rw, :]
                for j, (_, e) in enumerate(wtaps[s]):
                    p = g * ext_ref[r0 + e:r0 + e + rw, :]
                    for r8 in range(0, rw, SUBLANES):
                        sums[j] = sums[j] + p[r8:r8 + SUBLANES, :]
            for j, (k, e) in enumerate(wtaps[s]):
                tail = src[tm:tm + SUBLANES, :] * ext_ref[tm + e:tm + e + SUBLANES, :]
                dw_ref[k:k + 1, :] += _rsum(sums[j] + jnp.where(row < SUBLANES - s, tail, 0.0))


def _fwd_in(x, vecs, wa, b_in, wb, tm):
    s, d = x.shape
    nq, _, nw = wa.shape
    nt = s // tm

    def body(x_ref, v_ref, w_ref, b_ref, wbi_ref, h_ref, z_ref, wbo_ref, send_sems, recv_sems):
        i = pl.program_id(0)
        gather = (wbi_ref, wbo_ref, send_sems, recv_sems)
        pl.when(i == 0)(lambda: _GatherShards.start(*gather))
        pl.when(i == nt // 2)(lambda: _GatherShards.relay(*gather))
        xh, _ = _ln(x_ref[...])
        h = (xh * (1.0 + v_ref[V_SCALE1:V_SCALE1 + 1, :]) + v_ref[V_SHIFT1:V_SHIFT1 + 1, :]).astype(BF16)
        h_ref[...] = h
        for q in range(nq):
            z_ref[:, q * nw:(q + 1) * nw] = _dot(h, w_ref[q]) + b_ref[:, q * nw:(q + 1) * nw]
        pl.when(i == nt - 1)(lambda: _GatherShards.finish(*gather))

    any_spec = pl.BlockSpec(memory_space=pl.ANY)
    n = _GatherShards.n_sems
    return _pcall(
        body, name="fwd_in", grid=(nt,),
        in_specs=[_rows(tm, d), _full(vecs.shape), _full(wa.shape, single=True), _full(b_in.shape), any_spec],
        out_specs=[_rows(tm, d), _rows(tm, nq * nw), any_spec],
        out_shape=[jax.ShapeDtypeStruct((s, d), BF16), jax.ShapeDtypeStruct((s, nq * nw), F32),
                   jax.ShapeDtypeStruct(wb.shape, wb.dtype)],
        input_output_aliases={4: 2},
        scratch_shapes=[pltpu.SemaphoreType.DMA((n,)), pltpu.SemaphoreType.DMA((n,))],
        compiler_params=_seq(1),
    )(x, vecs, wa, b_in, wb)


def _fwd_mix(z, vecs, caw, cbw, wb, tm, rc):
    s = z.shape[0]
    d = vecs.shape[1]
    nq = wb.shape[0]
    kq = d // nq
    base = 2 * d // kq

    def body(z_ref, v_ref, caw_ref, cbw_ref, wao_ref, wbo_ref, wo_ref,
             u1_ref, ya_ref, yb_ref, o1_ref, u3_ref, vv_ref, mg_ref, ext_ref, sh_ref, pext_ref, q_ref):
        @pl.when(pl.program_id(0) == 0)
        def _():
            ext_ref[0:HALO_A, :] = jnp.zeros((HALO_A, d), F32)
            pext_ref[0:HALO_B, :] = jnp.zeros((HALO_B, d), F32)

        ext_ref[HALO_A:HALO_A + tm, :] = z_ref[:, 0:d] * _sig(z_ref[:, d:2 * d])
        _conv_causal(ext_ref, sh_ref, caw_ref, CONV_A, HALO_A, v_ref[V_CAB:V_CAB + 1, :], u1_ref, tm, d, rc)
        ext_ref[0:HALO_A, :] = ext_ref[tm:tm + HALO_A, :]
        xa, _ = _ln(u1_ref[...])
        u2 = xa * v_ref[V_LNAG:V_LNAG + 1, :] + v_ref[V_LNAB:V_LNAB + 1, :]
        u3 = (u2 * _sig(u2)).astype(BF16)
        u3_ref[...] = u3
        ya = jnp.broadcast_to(v_ref[V_BAO:V_BAO + 1, :], (tm, d))
        for q in range(nq):
            ya = ya + _dot(u3[:, q * kq:(q + 1) * kq], wao_ref[q])
        ya_ref[...] = ya

        pext_ref[HALO_B:HALO_B + tm, :] = z_ref[:, 3 * d:4 * d] * z_ref[:, 4 * d:5 * d]
        _conv_causal(pext_ref, sh_ref, cbw_ref, CONV_B, HALO_B, None, q_ref, tm, d, rc)
        pext_ref[0:HALO_B, :] = pext_ref[tm:tm + HALO_B, :]
        vv = (z_ref[:, 2 * d:3 * d] * q_ref[...]).astype(BF16)
        vv_ref[...] = vv
        yb = jnp.zeros((tm, d), F32)
        for q in range(nq):
            yb = yb + _dot(vv[:, q * kq:(q + 1) * kq], wbo_ref[q])
        yb_ref[...] = yb

        mg = (_sig(z_ref[:, 5 * d:6 * d]) * ya + _sig(z_ref[:, 6 * d:7 * d]) * yb).astype(BF16)
        mg_ref[...] = mg
        o1 = jnp.broadcast_to(v_ref[V_BO:V_BO + 1, :], (tm, d))
        for q in range(nq):
            o1 = o1 + _dot(mg[:, q * kq:(q + 1) * kq], wo_ref[q])
        o1_ref[...] = o1

    def wspec(j):
        return pl.BlockSpec((nq, kq, d), lambda i: (0, base + j, 0), pipeline_mode=pl.Buffered(1))

    f32o = jax.ShapeDtypeStruct((s, d), F32)
    b16o = jax.ShapeDtypeStruct((s, d), BF16)
    return _pcall(
        body, name="fwd_mix", grid=(s // tm,),
        in_specs=[_rows(tm, 7 * d), _full(vecs.shape), _full(caw.shape), _full(cbw.shape), wspec(0), wspec(1), wspec(2)],
        out_specs=[_rows(tm, d)] * 7,
        out_shape=[f32o, f32o, f32o, f32o, b16o, b16o, b16o],
        scratch_shapes=[pltpu.VMEM((HALO_A + tm, d), F32), pltpu.VMEM((SHIFTED_COPIES, HALO_A + tm, d), F32),
                        pltpu.VMEM((HALO_B + tm, d), F32), pltpu.VMEM((tm, d), F32)],
        compiler_params=_seq(1),
    )(z, vecs, caw, cbw, wb, wb, wb)


def _mlp_fwd_bwd(x, out1, tgt, vecs, b_up, wb, tm):
    s, d = x.shape
    nq = wb.shape[0]
    dff = nq * d

    def body(x_ref, o1_ref, t_ref, v_ref, bup_ref, wup_ref, wdn_ref,
             h2_ref, f_ref, df0_ref, do2_ref, do1_ref, dxp_ref, acc_ref, dbup_ref, f0_ref):
        @pl.when(pl.program_id(0) == 0)
        def _():
            acc_ref[...] = jnp.zeros(acc_ref.shape, F32)
            dbup_ref[...] = jnp.zeros(dbup_ref.shape, F32)

        def vec(r):
            return v_ref[r:r + 1, :]

        def accum(r, val):
            acc_ref[r:r + 1, :] += _rsum(val)

        out1v = o1_ref[...]
        r1 = ALPHA * x_ref[...] + (1.0 + vec(V_GATE1)) * out1v
        xh1, rstd1 = _ln(r1)
        x1 = xh1 * vec(V_LN1G) + vec(V_LN1B)
        xn1, rstdn = _ln(x1)
        h2 = (xn1 * (1.0 + vec(V_SCALE2)) + vec(V_SHIFT2)).astype(BF16)
        h2_ref[...] = h2
        out2 = jnp.broadcast_to(vec(V_BDN), (tm, d))
        for q in range(nq):
            f0 = _dot(h2, wup_ref[q]) + bup_ref[:, q * d:(q + 1) * d]
            rl = jnp.maximum(f0, 0.0)
            f0_ref[:, q * d:(q + 1) * d] = rl
            fb = (rl * rl).astype(BF16)
            f_ref[:, q * d:(q + 1) * d] = fb
            out2 = out2 + _dot(fb, wdn_ref[q])
        r2 = ALPHA * x1 + (1.0 + vec(V_GATE2)) * out2
        xh2, rstd2 = _ln(r2)
        yv = xh2 * vec(V_LN2G) + vec(V_LN2B)
        err = yv - t_ref[...]
        accum(M_LOSS, err * err)
        dy = err * (1.0 / d)
        accum(M_LN2G, dy * xh2)
        accum(M_LN2B, dy)
        dr2 = _ln_bwd(dy * vec(V_LN2G), xh2, rstd2)
        accum(M_GATE2, dr2 * out2)
        dout2 = (1.0 + vec(V_GATE2)) * dr2
        accum(M_BDN, dout2)
        do2b = dout2.astype(BF16)
        do2_ref[...] = do2b
        dh2 = jnp.zeros((tm, d), F32)
        for q in range(nq):
            df0 = _dot_nt(do2b, wdn_ref[q]) * (2.0 * f0_ref[:, q * d:(q + 1) * d])
            dbup_ref[q:q + 1, :] += _rsum(df0)
            df0b = df0.astype(BF16)
            df0_ref[:, q * d:(q + 1) * d] = df0b
            dh2 = dh2 + _dot_nt(df0b, wup_ref[q])
        accum(M_SHIFT2, dh2)
        accum(M_SCALE2, dh2 * xn1)
        dx1 = ALPHA * dr2 + _ln_bwd(dh2 * (1.0 + vec(V_SCALE2)), xn1, rstdn)
        accum(M_LN1G, dx1 * xh1)
        accum(M_LN1B, dx1)
        dr1 = _ln_bwd(dx1 * vec(V_LN1G), xh1, rstd1)
        accum(M_GATE1, dr1 * out1v)
        dout1 = (1.0 + vec(V_GATE1)) * dr1
        accum(M_BO, dout1)
        do1_ref[...] = dout1.astype(BF16)
        dxp_ref[...] = ALPHA * dr1

    def wspec(j):
        return pl.BlockSpec((nq, d, d), lambda i: (0, j, 0), pipeline_mode=pl.Buffered(1))

    b16 = lambda w: jax.ShapeDtypeStruct((s, w), BF16)
    return _pcall(
        body, name="mlp_fwd_bwd", grid=(s // tm,),
        in_specs=[_rows(tm, d), _rows(tm, d), _rows(tm, d), _full(vecs.shape), _full(b_up.shape), wspec(0), wspec(1)],
        out_specs=[_rows(tm, d), _rows(tm, dff), _rows(tm, dff), _rows(tm, d), _rows(tm, d), _rows(tm, d),
                   _full((16, d)), _full((SUBLANES, d))],
        out_shape=[b16(d), b16(dff), b16(dff), b16(d), b16(d), jax.ShapeDtypeStruct((s, d), F32),
                   jax.ShapeDtypeStruct((16, d), F32), jax.ShapeDtypeStruct((SUBLANES, d), F32)],
        scratch_shapes=[pltpu.VMEM((tm, dff), F32)],
        compiler_params=_seq(1),
    )(x, out1, tgt, vecs, b_up, wb, wb)


def _mix_bwd(dout1, z, u1, ya, yb, vecs, caw, cbw, wb, tm, rc):
    s = z.shape[0]
    d = vecs.shape[1]
    nq = wb.shape[0]
    kq = d // nq
    base = 2 * d // kq
    nt = s // tm
    hb = tm // HALO_A

    def body(do1_ref, z_ref, zh_ref, u1_ref, ya_ref, yb_ref, v_ref, caw_ref, cbw_ref, wao_ref, wbo_ref, wo_ref,
             dz_ref, dya_ref, dyb_ref, acc_ref, dcaw_ref, dcbw_ref, dbin_ref,
             ext_ref, du1p_ref, sh_ref, pext_ref, dqp_ref, tmp_ref):
        i = pl.program_id(0)

        @pl.when(i == 0)
        def _():
            acc_ref[...] = jnp.zeros(acc_ref.shape, F32)
            dcaw_ref[...] = jnp.zeros(dcaw_ref.shape, F32)
            dcbw_ref[...] = jnp.zeros(dcbw_ref.shape, F32)
            dbin_ref[...] = jnp.zeros(dbin_ref.shape, F32)
            du1p_ref[0:SUBLANES, :] = jnp.zeros((SUBLANES, d), F32)
            du1p_ref[SUBLANES + tm:SUBLANES + tm + HALO_A, :] = jnp.zeros((HALO_A, d), F32)
            dqp_ref[0:SUBLANES, :] = jnp.zeros((SUBLANES, d), F32)
            dqp_ref[SUBLANES + tm:SUBLANES + tm + HALO_B, :] = jnp.zeros((HALO_B, d), F32)

        def vec(r):
            return v_ref[r:r + 1, :]

        def accum(r, val):
            acc_ref[r:r + 1, :] += _rsum(val)

        def put_dz(j, val):
            vb = val.astype(BF16)
            dz_ref[:, j * d:(j + 1) * d] = vb
            dbin_ref[j:j + 1, :] += _rsum_mxu(vb)

        has_history = i < nt - 1

        do1 = do1_ref[...]
        dmg = jnp.concatenate([_dot_nt(do1, wo_ref[q]) for q in range(nq)], axis=1)
        sga = _sig(z_ref[:, 5 * d:6 * d])
        sgb = _sig(z_ref[:, 6 * d:7 * d])
        dya = dmg * sga
        dyb = dmg * sgb
        accum(X_BAO, dya)
        put_dz(5, dya * ya_ref[...] * (1.0 - sga))
        put_dz(6, dyb * yb_ref[...] * (1.0 - sgb))
        dyab = dya.astype(BF16)
        dybb = dyb.astype(BF16)
        dya_ref[...] = dyab
        dyb_ref[...] = dybb

        du3 = jnp.concatenate([_dot_nt(dyab, wao_ref[q]) for q in range(nq)], axis=1)
        xa, rstda = _ln(u1_ref[...])
        u2 = xa * vec(V_LNAG) + vec(V_LNAB)
        s2 = _sig(u2)
        du2 = du3 * (s2 * (1.0 + u2 * (1.0 - s2)))
        accum(X_LNAG, du2 * xa)
        accum(X_LNAB, du2)
        du1 = _ln_bwd(du2 * vec(V_LNAG), xa, rstda)
        accum(X_CAB, du1)
        du1p_ref[SUBLANES:SUBLANES + tm, :] = du1
        sg = _sig(z_ref[:, d:2 * d])
        aval = z_ref[:, 0:d]
        ext_ref[HALO_A:HALO_A + tm, :] = aval * sg
        ext_ref[0:HALO_A, :] = jnp.where(has_history, zh_ref[:, 0:d] * _sig(zh_ref[:, d:2 * d]), 0.0)
        _conv_adjoint(du1p_ref, sh_ref, ext_ref, caw_ref, tmp_ref, dcaw_ref, CONV_A, HALO_A, tm, d, rc)
        du1p_ref[SUBLANES + tm:SUBLANES + tm + HALO_A, :] = du1p_ref[SUBLANES:SUBLANES + HALO_A, :]
        du0 = tmp_ref[...]
        put_dz(0, du0 * sg)
        put_dz(1, du0 * aval * sg * (1.0 - sg))

        dv = jnp.concatenate([_dot_nt(dybb, wbo_ref[q]) for q in range(nq)], axis=1)
        bgc = z_ref[:, 3 * d:4 * d]
        bx = z_ref[:, 4 * d:5 * d]
        pext_ref[HALO_B:HALO_B + tm, :] = bgc * bx
        pext_ref[0:HALO_B, :] = jnp.where(
            has_history, zh_ref[HALO_A - HALO_B:HALO_A, 3 * d:4 * d] * zh_ref[HALO_A - HALO_B:HALO_A, 4 * d:5 * d], 0.0)
        _conv_causal(pext_ref, sh_ref, cbw_ref, CONV_B, HALO_B, None, tmp_ref, tm, d, rc)
        put_dz(2, dv * tmp_ref[...])
        dqp_ref[SUBLANES:SUBLANES + tm, :] = dv * z_ref[:, 2 * d:3 * d]
        _conv_adjoint(dqp_ref, sh_ref, pext_ref, cbw_ref, tmp_ref, dcbw_ref, CONV_B, HALO_B, tm, d, rc)
        dqp_ref[SUBLANES + tm:SUBLANES + tm + HALO_B, :] = dqp_ref[SUBLANES:SUBLANES + HALO_B, :]
        dp = tmp_ref[...]
        put_dz(3, dp * bx)
        put_dz(4, dp * bgc)

    def rev(width):
        return pl.BlockSpec((tm, width), lambda i: (nt - 1 - i, 0))

    def wspec(j):
        return pl.BlockSpec((nq, kq, d), lambda i: (0, base + j, 0), pipeline_mode=pl.Buffered(1))

    halo = pl.BlockSpec((HALO_A, 7 * d), lambda i: (jnp.maximum((nt - 1 - i) * hb - 1, 0), 0))
    b16 = jax.ShapeDtypeStruct((s, d), BF16)
    acc8 = jax.ShapeDtypeStruct((SUBLANES, d), F32)
    return _pcall(
        body, name="mix_bwd", grid=(nt,),
        in_specs=[rev(d), rev(7 * d), halo, rev(d), rev(d), rev(d), _full(vecs.shape), _full(caw.shape), _full(cbw.shape),
                  wspec(0), wspec(1), wspec(2)],
        out_specs=[rev(7 * d), rev(d), rev(d), _full((SUBLANES, d)), _full((HALO_A, d)), _full((HALO_B, d)),
                   _full((SUBLANES, d))],
        out_shape=[jax.ShapeDtypeStruct((s, 7 * d), BF16), b16, b16, acc8,
                   jax.ShapeDtypeStruct((HALO_A, d), F32), jax.ShapeDtypeStruct((HALO_B, d), F32), acc8],
        scratch_shapes=[pltpu.VMEM((HALO_A + tm, d), F32), pltpu.VMEM((SUBLANES + tm + HALO_A, d), F32),
                        pltpu.VMEM((SHIFTED_COPIES, tm + HALO_A, d), F32), pltpu.VMEM((HALO_B + tm, d), F32),
                        pltpu.VMEM((SUBLANES + tm + HALO_B, d), F32), pltpu.VMEM((tm, d), F32)],
        compiler_params=_seq(1),
    )(dout1, z, z, u1, ya, yb, vecs, caw, cbw, wb, wb, wb)


def _in_bwd(dz, x, dxp, vecs, wa, tm, plans, plan_srcs):
    s, d = x.shape
    nq, _, nw = wa.shape
    nt = s // tm
    nc = len(plan_srcs)

    def body(dz_ref, x_ref, dxp_ref, v_ref, w_ref, *rest):
        src_refs, (gx_ref, acc_ref), dst_refs = rest[:nc], rest[nc:nc + 2], rest[nc + 2:2 * nc + 2]
        send_sems, recv_sems = rest[2 * nc + 2:]
        i = pl.program_id(0)
        comm = (src_refs, dst_refs, send_sems, recv_sems)
        _carried_start(plans, i == 0, comm)

        @pl.when(i == 0)
        def _():
            acc_ref[...] = jnp.zeros(acc_ref.shape, F32)

        dh1 = jnp.zeros((tm, d), F32)
        for q in range(nq):
            dh1 = dh1 + _dot_nt(dz_ref[:, q * nw:(q + 1) * nw], w_ref[q])
        xh, rstd = _ln(x_ref[...])
        acc_ref[I_SHIFT1:I_SHIFT1 + 1, :] += _rsum(dh1)
        acc_ref[I_SCALE1:I_SCALE1 + 1, :] += _rsum(dh1 * xh)
        gx_ref[...] = dxp_ref[...] + _ln_bwd(dh1 * (1.0 + v_ref[V_SCALE1:V_SCALE1 + 1, :]), xh, rstd)
        _carried_wait(plans, i == nt - 1, comm)

    any_spec = pl.BlockSpec(memory_space=pl.ANY)
    return _pcall(
        body, name="in_bwd", grid=(nt,),
        in_specs=[_rows(tm, nq * nw), _rows(tm, d), _rows(tm, d), _full(vecs.shape), _full(wa.shape, single=True)]
        + [any_spec] * nc,
        out_specs=[_rows(tm, d), _full((SUBLANES, d))] + [any_spec] * nc,
        out_shape=[jax.ShapeDtypeStruct((s, d), F32), jax.ShapeDtypeStruct((SUBLANES, d), F32)]
        + [plan.out_shape(p) for plan, p in zip(plans, plan_srcs)],
        scratch_shapes=_plan_sems(plans),
        compiler_params=_seq(1),
    )(dz, x, dxp, vecs, wa, *plan_srcs)


def _dw(a, b, split_a, ts, name, into=None, rows_total=None, row_block=0):
    s = a.shape[0]
    ka = a.shape[1] // N_CHIPS if split_a else a.shape[1]
    nb = b.shape[1] if split_a else b.shape[1] // N_CHIPS
    rows_total = ka if rows_total is None else rows_total

    def body(a_ref, b_ref, *rest):
        o_ref = rest[-1]

        @pl.when(pl.program_id(1) == 0)
        def _():
            o_ref[...] = jnp.zeros(o_ref.shape, F32)

        o_ref[...] += _dot_tn(a_ref[...], b_ref[...])

    a_spec = pl.BlockSpec((ts, ka), (lambda q, i: (i, q)) if split_a else (lambda q, i: (i, 0)))
    b_spec = pl.BlockSpec((ts, nb), (lambda q, i: (i, 0)) if split_a else (lambda q, i: (i, q)))
    extra = {} if into is None else dict(input_output_aliases={2: 0})
    return _pcall(
        body, name=name, grid=(N_CHIPS, s // ts),
        in_specs=[a_spec, b_spec] + ([] if into is None else [pl.BlockSpec(memory_space=pl.ANY)]),
        out_specs=pl.BlockSpec((None, ka, nb), lambda q, i: (q, row_block, 0)),
        out_shape=jax.ShapeDtypeStruct((N_CHIPS, rows_total, nb), F32),
        compiler_params=_seq(2), **extra,
    )(*((a, b) if into is None else (a, b, into)))


def _dw_carrying(a, b, ts, name, plan, plan_src):
    s, k = a.shape
    nb = b.shape[1] // N_CHIPS
    ns = s // ts

    def body(a_ref, b_ref, src_ref, o_ref, dst_ref, send_sems, recv_sems):
        q, i = pl.program_id(0), pl.program_id(1)
        comm = ([src_ref], [dst_ref], send_sems, recv_sems)
        _carried_start([plan], jnp.logical_and(q == 0, i == 0), comm)

        @pl.when(i == 0)
        def _():
            o_ref[...] = jnp.zeros(o_ref.shape, F32)

        o_ref[...] += _dot_tn(a_ref[...], b_ref[...])
        _carried_wait([plan], jnp.logical_and(q == N_CHIPS - 1, i == ns - 1), comm)

    any_spec = pl.BlockSpec(memory_space=pl.ANY)
    return _pcall(
        body, name=name, grid=(N_CHIPS, ns),
        in_specs=[pl.BlockSpec((ts, k), lambda q, i: (i, 0)), pl.BlockSpec((ts, nb), lambda q, i: (i, q)), any_spec],
        out_specs=[pl.BlockSpec((None, k, nb), lambda q, i: (q, 0, 0)), any_spec],
        out_shape=[jax.ShapeDtypeStruct((N_CHIPS, k, nb), F32), plan.out_shape(plan_src)],
        scratch_shapes=_plan_sems([plan]),
        compiler_params=_seq(2),
    )(a, b, plan_src)


def _dw_rows(a, b, ts, name, into, row_block):
    s, k = a.shape
    n = b.shape[1]
    kq = k // N_CHIPS

    def body(a_ref, b_ref, buf_ref, o_ref):
        @pl.when(pl.program_id(0) == 0)
        def _():
            o_ref[...] = jnp.zeros(o_ref.shape, F32)

        res = _dot_tn(a_ref[...], b_ref[...])
        for q in range(N_CHIPS):
            o_ref[q] += res[q * kq:(q + 1) * kq, :]

    return _pcall(
        body, name=name, grid=(s // ts,),
        in_specs=[_rows(ts, k), _rows(ts, n), pl.BlockSpec(memory_space=pl.ANY)],
        out_specs=pl.BlockSpec((N_CHIPS, kq, n), lambda i: (0, row_block, 0)),
        out_shape=jax.ShapeDtypeStruct(into.shape, F32), input_output_aliases={2: 0},
        compiler_params=_seq(1),
    )(a, b, into)


def _adam_math(w, g, m, v):
    m2 = ADAM_B1 * m + (1.0 - ADAM_B1) * g
    v2 = ADAM_B2 * v + (1.0 - ADAM_B2) * (g * g)
    m_hat = m2 / (1.0 - ADAM_B1 ** ADAM_STEP)
    v_hat = v2 / (1.0 - ADAM_B2 ** ADAM_STEP)
    delta = -ADAM_LR * (m_hat / (jnp.sqrt(v_hat) + ADAM_EPS) + ADAM_WD * w)
    return delta, m2, v2


def _adam(w, g, m, v, g_row0, tr, name):
    r, c = w.shape
    blk0 = g_row0 // tr

    def body(w_ref, g_ref, m_ref, v_ref, go_ref, d_ref, mo_ref, vo_ref):
        gv = g_ref[...]
        go_ref[...] = gv
        d_ref[...], mo_ref[...], vo_ref[...] = _adam_math(w_ref[...], gv, m_ref[...], v_ref[...])

    spec = _rows(tr, c)
    g_spec = pl.BlockSpec((tr, c), lambda i: (blk0 + i, 0))
    o = jax.ShapeDtypeStruct((r, c), F32)
    return _pcall(body, name=name, grid=(r // tr,), in_specs=[spec, g_spec, spec, spec], out_specs=[spec] * 4,
                  out_shape=[o, o, o, o], compiler_params=_seq(1))(w, g, m, v)


def _small_update(gathered_head, gathered, q_idx, small_w, small_m, small_v, conv_w, conv_m, conv_v):
    d = gathered.shape[2]
    ns = len(_SMALL)
    cw = conv_w[0].shape[1]
    conv_rows = ((T_CAW, CONV_A), (T_CBW, CONV_B))

    def body(q_ref, h_ref, g_ref, *refs):
        ins, outs = refs[:3 * (ns + 2)], refs[3 * (ns + 2):]
        tot_ref, loss_ref = outs[0], outs[1]
        outs = outs[2:]
        head, tot = h_ref[0], g_ref[0]
        for dev in range(1, N_DEV):
            head = head + h_ref[dev]
            tot = tot + g_ref[dev]
        tot_ref[0:T_M, :] = head
        tot_ref[T_M:T_ROWS, :] = tot
        loss_ref[...] = (0.5 / d) * jnp.sum(tot_ref[T_LOSS:T_LOSS + 1, :], axis=1, keepdims=True)
        for p, (_, rows) in enumerate(_SMALL):
            w_ref, m_ref, v_ref = ins[p], ins[ns + 2 + p], ins[2 * (ns + 2) + p]
            go, do, mo, vo = outs[4 * p:4 * p + 4]
            for j, row in enumerate(rows):
                sl = slice(j * d, (j + 1) * d)
                gv = tot_ref[row:row + 1, :]
                go[:, sl] = gv
                do[:, sl], mo[:, sl], vo[:, sl] = _adam_math(w_ref[:, sl], gv, m_ref[:, sl], v_ref[:, sl])
        for p, (row, taps) in enumerate(conv_rows):
            w_ref, m_ref, v_ref = ins[ns + p], ins[ns + 2 + ns + p], ins[2 * (ns + 2) + ns + p]
            go, do, mo, vo = outs[4 * (ns + p):4 * (ns + p) + 4]
            gv = tot_ref[row:row + taps, 0:cw]
            for qq in range(1, N_CHIPS):
                gv = jnp.where(q_ref[0] == qq, tot_ref[row:row + taps, qq * cw:(qq + 1) * cw], gv)
            go[...] = gv
            do[...], mo[...], vo[...] = _adam_math(w_ref[...], gv, m_ref[...], v_ref[...])

    params = list(small_w) + list(conv_w) + list(small_m) + list(conv_m) + list(small_v) + list(conv_v)
    out_shape = [jax.ShapeDtypeStruct((T_ROWS, d), F32), jax.ShapeDtypeStruct((1, 1), F32)]
    for w in list(small_w) + list(conv_w):
        out_shape += [jax.ShapeDtypeStruct(w.shape, F32)] * 4
    vm = pl.BlockSpec(memory_space=pltpu.VMEM)
    return _pcall(
        body, name="small_update", out_shape=out_shape,
        in_specs=[pl.BlockSpec(memory_space=pltpu.SMEM), vm, vm] + [vm] * len(params),
        out_specs=[vm] * len(out_shape), compiler_params=_cparams(),
    )(q_idx, gathered_head, gathered, *params)


def kernel(x, c, w_ada, b_ada, w_in, b_in, conv_a_w, conv_a_b, ln_a_g, ln_a_b, w_a_out, b_a_out, conv_b_w, w_b_out, w_o, b_o, ln1_g, ln1_b, w_up, b_up, w_down, b_down, ln2_g, ln2_b, loss_target, m_w_ada, m_b_ada, m_w_in, m_b_in, m_conv_a_w, m_conv_a_b, m_ln_a_g, m_ln_a_b, m_w_a_out, m_b_a_out, m_conv_b_w, m_w_b_out, m_w_o, m_b_o, m_ln1_g, m_ln1_b, m_w_up, m_b_up, m_w_down, m_b_down, m_ln2_g, m_ln2_b, v_w_ada, v_b_ada, v_w_in, v_b_in, v_conv_a_w, v_conv_a_b, v_ln_a_g, v_ln_a_b, v_w_a_out, v_b_a_out, v_conv_b_w, v_w_b_out, v_w_o, v_b_o, v_ln1_g, v_ln1_b, v_w_up, v_b_up, v_w_down, v_b_down, v_ln2_g, v_ln2_b):
    given = dict(locals())
    s, d = x.shape[1], x.shape[2]
    xi, yi, ci = _my_pos()
    q = 2 * xi + yi
    me = 4 * xi + 2 * yi + ci
    i32 = jnp.int32
    q_arr = jnp.reshape(q, (1,)).astype(i32)
    others = [2 * ox + oy for ox, oy in _other_chips(xi, yi)]
    halves_idx = jnp.stack([ci] + others).astype(i32)
    chips_idx = jnp.stack([q, ci]).astype(i32)
    kq = d // N_CHIPS
    tm = min(256, s)
    rc = tm

    def sq(a):
        return a.reshape(a.shape[1:])

    x2, tgt = sq(x), sq(loss_target)

    wa = _place_shard([sq(w_in)], q_arr, "place_w_in")
    wb = _place_shard([sq(w_up), sq(w_down), sq(w_a_out), sq(w_b_out), sq(w_o)], q_arr, "place_w_rest")

    n_ada = w_ada.shape[2]
    pre = jnp.concatenate([
        jnp.broadcast_to(c, (SUBLANES, d)),
        jnp.pad(sq(conv_a_w), ((0, HALO_A - CONV_A), (0, d - kq))),
        jnp.pad(sq(conv_b_w), ((0, HALO_B - CONV_B), (0, d - kq)))], axis=0)
    b_ada_sh = lax.dynamic_slice(b_ada, (0, q * n_ada), (1, n_ada))
    pre_all, c_all, mod_all, wa = _prologue(pre, sq(w_ada), b_ada_sh, wa)
    caw = jnp.concatenate([pre_all[2 * p, SUBLANES:SUBLANES + HALO_A, :kq] for p in range(N_CHIPS)], axis=1)
    cbw = jnp.concatenate([pre_all[2 * p, SUBLANES + HALO_A:, :kq] for p in range(N_CHIPS)], axis=1)
    mod_rows = lax.dynamic_slice(mod_all, (0, me, 0), (N_DEV, 1, n_ada))[0::2, 0, :]
    mod = mod_rows.reshape(6, d)
    vecs = jnp.concatenate([mod, conv_a_b, ln_a_g, ln_a_b, b_a_out, b_o, ln1_g, ln1_b, b_down, ln2_g, ln2_b], axis=0)

    h1, z, wb = _fwd_in(x2, vecs, wa, b_in, wb, min(2 * tm, s))
    u1, ya, yb, out1, u3, vv, mg = _fwd_mix(z, vecs, caw, cbw, wb, tm, rc)

    ts = min(2048, s)
    rest_rows = wb.shape[1]
    small0 = 2 * d // kq
    h2, fb, df0, do2, do1, dxp, macc, dbup = _mlp_fwd_bwd(x2, out1, tgt, vecs, b_up, wb, tm)
    gb = _dw(h2, df0, False, ts, "dw_up", rows_total=rest_rows)
    gb = _dw(fb, do2, True, ts, "dw_down", into=gb, rows_total=rest_rows, row_block=1)
    dz, dya, dyb, xacc, dcaw, dcbw, dbin = _mix_bwd(do1, z, u1, ya, yb, vecs, caw, cbw, wb, tm, rc)
    gb = _dw_rows(u3, dya, ts, "dw_a_out", gb, small0)
    gb = _dw_rows(vv, dyb, ts, "dw_b_out", gb, small0 + 1)
    gb = _dw_rows(mg, do1, ts, "dw_o", gb, small0 + 2)

    trb, tra = rest_rows // 8, d // 8
    ga, rb = _dw_carrying(h1, dz, ts, "dw_in", _SiblingHalf, gb)
    pb, ra = _add_halves(gb, rb, halves_idx, trb, "rs_add_halves_rest", _SiblingHalf, ga)
    pa = _add_halves(ga, ra, halves_idx, tra, "rs_add_halves_in")
    table = jnp.concatenate([macc, dbin, dcaw, xacc, dcbw, dbup], axis=0)
    tm_in = min(2 * tm, s)
    gx, iacc, r3a, r3b, gathered = _in_bwd(dz, x2, dxp, vecs, wa, tm_in, [_ChipBlocks, _ChipBlocks, _TableToAll],
                                           [pa, pb, table])
    fa = _add_chips(ga, ra, r3a, chips_idx, tra, "rs_add_chips_in")
    fb_ = _add_chips(gb, rb, r3b, chips_idx, trb, "rs_add_chips_rest")
    g_in, g_b = _rs_join_halves(fa, fb_)

    gathered_head = _all_gather_small(iacc, "gather_ln0_sums")
    names = [n for n, _ in _SMALL]
    res = _small_update(
        gathered_head, gathered, q_arr,
        [given[n] for n in names], [given["m_" + n] for n in names], [given["v_" + n] for n in names],
        [sq(conv_a_w), sq(conv_b_w)], [sq(m_conv_a_w), sq(m_conv_b_w)], [sq(v_conv_a_w), sq(v_conv_b_w)])
    loss = res[1].reshape(())
    upd = {}
    for p, n in enumerate(names + ["conv_a_w", "conv_b_w"]):
        upd[n] = res[2 + 4 * p:6 + 4 * p]

    dmod_all = jnp.stack([gathered_head[:, r, :] if r < T_M else gathered[:, r - T_M, :] for r in _SMALL[0][1]],
                         axis=1).reshape(N_DEV, 6 * d)
    dmod_sh = lax.dynamic_slice(dmod_all, (0, q * n_ada), (N_DEV, n_ada))
    g_ada = _ada_bwd(c_all.T, dmod_sh)
    upd["w_ada"] = _adam(sq(w_ada), g_ada, sq(m_w_ada), sq(v_w_ada), 0, min(256, d), "adam_w_ada")

    upd["w_in"] = _adam(sq(w_in), g_in, sq(m_w_in), sq(v_w_in), 0, min(256, d), "adam_w_in")
    r0 = 0
    for n in ("w_up", "w_down", "w_a_out", "w_b_out", "w_o"):
        w = sq(given[n])
        upd[n] = _adam(w, g_b, sq(given["m_" + n]), sq(given["v_" + n]), r0, min(256, w.shape[0]), "adam_" + n)
        r0 += w.shape[0]

    order = ["w_ada", "b_ada", "w_in", "b_in", "conv_a_w", "conv_a_b", "ln_a_g", "ln_a_b", "w_a_out", "b_a_out", "conv_b_w",
             "w_b_out", "w_o", "b_o", "ln1_g", "ln1_b", "w_up", "b_up", "w_down", "b_down", "ln2_g", "ln2_b"]
    outs = [loss, gx.reshape(x.shape)]
    for k in range(4):
        outs += [upd[n][k].reshape(given[n].shape) for n in order]
    return tuple(outs)
```
